```python
import math
import jax
import jax.numpy as jnp
from jax import lax
import numpy as np

D_MODEL = 1024
BATCH = 16
SEQ = 256
DEPTH = 2
DEC_BATCH = 8
DEC_SEQ = 1024
PAST_LEN = 256

GRID_W = 64
N_BRANCH = 4
MIX_W = 256
H_A = 4
NOPE_A = 64
ROPE_A = 32
VH_A = 64
Q_LORA = 256
KV_LORA = 128
D_B = 256
G_B = 4
CHUNK_B = 128
H_C = 4
DH_C = 64
D_C = H_C * DH_C
CHUNK_C = 64
H_D = 4
DH_D = 32
D_D = H_D * 2 * DH_D
D_FF = 4 * D_MODEL
IN_A = Q_LORA + KV_LORA + ROPE_A
IN_B = 2 * D_B
IN_C = 4 * D_C + 4 * H_C
IN_D = 3 * D_D
IN_W = IN_A + IN_B + IN_C + IN_D
ROPE_BASE = 10000.0
EPS = 1e-6
MLA_SCALE = (NOPE_A + ROPE_A) ** -0.5
DIFF_SCALE = DH_D ** -0.5
DENSE_MAX_KEYS = 2048
Q_BLOCK = 128

kernel_name = "hybrid_diffusion_mla_gmlp_mlstm_diffattn_step"


def _rms(x, g):
    xf = x.astype(jnp.float32)
    y = xf * lax.rsqrt(jnp.mean(xf * xf, axis=-1, keepdims=True) + EPS)
    return (y * g.astype(jnp.float32)).astype(x.dtype)


def _axial_tables(rows, dim):
    row = jnp.repeat(jnp.arange(rows), GRID_W).astype(jnp.float32)
    col = jnp.tile(jnp.arange(GRID_W), rows).astype(jnp.float32)
    nf = dim // 4
    inv = jnp.exp(-math.log(ROPE_BASE) * jnp.arange(nf, dtype=jnp.float32) / nf)
    ang = jnp.concatenate([row[:, None] * inv, col[:, None] * inv], axis=-1)
    return jnp.cos(ang), jnp.sin(ang)


def _rope(x, cos, sin):
    half = x.shape[-1] // 2
    xf = x.astype(jnp.float32)
    x1, x2 = xf[..., :half], xf[..., half:]
    c = cos[None, :, None, :]
    s = sin[None, :, None, :]
    return jnp.concatenate([x1 * c - x2 * s, x1 * s + x2 * c], axis=-1).astype(x.dtype)


def _attend(q, k, v, scale):
    def block(qb):
        s = jnp.einsum('bqhd,bkhd->bhqk', qb, k).astype(jnp.float32) * scale
        p = jax.nn.softmax(s, axis=-1).astype(v.dtype)
        return jnp.einsum('bhqk,bkhe->bqhe', p, v)
    B, Lq = q.shape[0], q.shape[1]
    if k.shape[1] < DENSE_MAX_KEYS or Lq % Q_BLOCK != 0:
        return block(q)
    qb = q.reshape(B, Lq // Q_BLOCK, Q_BLOCK, *q.shape[2:]).swapaxes(0, 1)
    out = lax.map(block, qb)
    return out.swapaxes(0, 1).reshape(B, Lq, *out.shape[3:])


def _mla(z, q_norm, w_uq, kv_norm, w_ukv, rope_cs, ctx_ckv, ctx_kr):
    B, L, _ = z.shape
    cq = _rms(z[..., :Q_LORA], q_norm)
    ckv = _rms(z[..., Q_LORA:Q_LORA + KV_LORA], kv_norm)
    kr = z[..., Q_LORA + KV_LORA:IN_A]
    qh = (cq @ w_uq).reshape(B, L, H_A, NOPE_A + ROPE_A)
    q_nope, q_rope = qh[..., :NOPE_A], qh[..., NOPE_A:]
    if rope_cs is None:
        ckv_all, kr_all = ckv, kr
    else:
        cos, sin = rope_cs
        q_rope = _rope(q_rope, cos, sin)
        kr_lat = _rope(kr[:, :, None, :], cos, sin)[:, :, 0]
        ckv_all = jnp.concatenate([ctx_ckv.astype(ckv.dtype), ckv], axis=1)
        kr_all = jnp.concatenate([ctx_kr.astype(kr.dtype), kr_lat], axis=1)
    Lk = ckv_all.shape[1]
    kv = (ckv_all @ w_ukv).reshape(B, Lk, H_A, NOPE_A + VH_A)
    k = jnp.concatenate([kv[..., :NOPE_A], jnp.broadcast_to(kr_all[:, :, None, :], (B, Lk, H_A, ROPE_A))], axis=-1)
    q = jnp.concatenate([q_nope, q_rope], axis=-1)
    out = _attend(q, k, kv[..., NOPE_A:], MLA_SCALE)
    return out.reshape(B, L, H_A * VH_A), ckv, kr


def _chunk_mlp(z, v_norm, w_s, b_s):
    B, L, _ = z.shape
    u = z[..., :D_B]
    v = _rms(z[..., D_B:], v_norm)
    vc = v.reshape(B, L // CHUNK_B, CHUNK_B, G_B, D_B // G_B)
    mixed = jnp.einsum('gpq,bnqgd->bnpgd', w_s, vc) + b_s.T[None, None, :, :, None]
    return u * mixed.reshape(B, L, D_B)


def _mlstm_dir(q, k, v, i_pre, f_pre, state):
    B, H, L, _ = q.shape
    nc = L // CHUNK_C

    def chunks(a):
        return jnp.moveaxis(a.reshape(B, H, nc, CHUNK_C, *a.shape[3:]), 2, 0)

    tril = jnp.tril(jnp.ones((CHUNK_C, CHUNK_C), dtype=bool))

    def step(carry, xs):
        C, n, m = carry
        qc, kc, vc, ic, lfc = xs
        b = jnp.cumsum(lfc, axis=-1)
        d_log = jnp.where(tril, b[..., :, None] - b[..., None, :] + ic[..., None, :], -jnp.inf)
        inter = b + m[..., None]
        mt = jnp.maximum(inter, jnp.max(d_log, axis=-1))
        w_inter = jnp.exp(inter - mt)
        s = jnp.einsum('bhtd,bhsd->bhts', qc, kc) * jnp.exp(d_log - mt[..., None])
        num = w_inter[..., None] * jnp.einsum('bhtd,bhde->bhte', qc, C) + jnp.einsum('bhts,bhse->bhte', s, vc)
        den = w_inter * jnp.einsum('bhtd,bhd->bht', qc, n) + jnp.sum(s, axis=-1)
        hc = num / jnp.maximum(jnp.abs(den), jnp.exp(-mt))[..., None]
        bT = b[..., -1]
        a_s = bT[..., None] - b + ic
        m_new = jnp.maximum(bT + m, jnp.max(a_s, axis=-1))
        w_old = jnp.exp(bT + m - m_new)
        w_s = jnp.exp(a_s - m_new[..., None])
        C_new = w_old[..., None, None] * C + jnp.einsum('bhs,bhsd,bhse->bhde', w_s, kc, vc)
        n_new = w_old[..., None] * n + jnp.einsum('bhs,bhsd->bhd', w_s, kc)
        return (C_new, n_new, m_new), hc

    xs = (chunks(q), chunks(k), chunks(v), chunks(i_pre), chunks(jax.nn.log_sigmoid(f_pre)))
    state, hs = lax.scan(step, state, xs)
    h = jnp.moveaxis(hs, 0, 2).reshape(B, H, L, v.shape[-1])
    return h, state


def _mlstm(z, gate_bias, head_norm, state):
    B, L, _ = z.shape

    def heads(a):
        return a.reshape(B, L, H_C, DH_C).transpose(0, 2, 1, 3).astype(jnp.float32)

    q = heads(z[..., :D_C])
    k = heads(z[..., D_C:2 * D_C]) * (DH_C ** -0.5)
    v = heads(z[..., 2 * D_C:3 * D_C])
    o = z[..., 3 * D_C:4 * D_C]
    g = (z[..., 4 * D_C:IN_C].reshape(B, L, 2, 2, H_C) + gate_bias).astype(jnp.float32)
    g = g.transpose(2, 3, 0, 4, 1)
    C0, n0, m0 = state
    st0_fw = (C0[:, 0].astype(jnp.float32), n0[:, 0].astype(jnp.float32), m0[:, 0].astype(jnp.float32))
    st0_bw = (C0[:, 1].astype(jnp.float32), n0[:, 1].astype(jnp.float32), m0[:, 1].astype(jnp.float32))
    flip = lambda a: jnp.flip(a, axis=2)
    h_fw, st_fw = _mlstm_dir(q, k, v, g[0, 0], g[0, 1], st0_fw)
    h_bw, st_bw = _mlstm_dir(flip(q), flip(k), flip(v), flip(g[1, 0]), flip(g[1, 1]), st0_bw)
    h = (h_fw + flip(h_bw)).transpose(0, 2, 1, 3)
    h = _rms(h, head_norm.reshape(H_C, DH_C)).reshape(B, L, D_C)
    out = (jax.nn.sigmoid(o.astype(jnp.float32)) * h).astype(z.dtype)
    new_C = jnp.stack([st_fw[0], st_bw[0]], axis=1)
    new_n = jnp.stack([st_fw[1], st_bw[1]], axis=1)
    new_m = jnp.stack([st_fw[2], st_bw[2]], axis=1)
    return out, (new_C, new_n, new_m)


def _diff_attn(z, lam, sub_norm, lam_init, rope_cs, ctx_k, ctx_v):
    B, L, _ = z.shape
    q = z[..., :D_D].reshape(B, L, H_D, 2, DH_D)
    k = z[..., D_D:2 * D_D].reshape(B, L, H_D, 2, DH_D)
    v = z[..., 2 * D_D:3 * D_D].reshape(B, L, H_D, 2 * DH_D)
    if rope_cs is None:
        q_use, k_all, v_all = q, k, v
    else:
        cos, sin = rope_cs
        q_use = _rope(q.reshape(B, L, 2 * H_D, DH_D), cos, sin).reshape(B, L, H_D, 2, DH_D)
        k_rot = _rope(k.reshape(B, L, 2 * H_D, DH_D), cos, sin).reshape(B, L, H_D, 2, DH_D)
        k_all = jnp.concatenate([ctx_k.astype(k.dtype), k_rot], axis=1)
        v_all = jnp.concatenate([ctx_v.astype(v.dtype), v], axis=1)
    lf = lam.astype(jnp.float32)
    lam_val = jnp.exp(jnp.sum(lf[0] * lf[1])) - jnp.exp(jnp.sum(lf[2] * lf[3])) + lam_init
    o1 = _attend(q_use[:, :, :, 0], k_all[:, :, :, 0], v_all, DIFF_SCALE).astype(jnp.float32)
    o2 = _attend(q_use[:, :, :, 1], k_all[:, :, :, 1], v_all, DIFF_SCALE).astype(jnp.float32)
    o = _rms(o1 - lam_val * o2, sub_norm) * (1.0 - lam_init)
    return o.reshape(B, L, D_D).astype(z.dtype), k, v


def _trunk_layer(x, cond, rope_cs, ctx, l, w_mod, b_mod, norm_g, w_in, mla_q_norm, w_uq, mla_kv_norm, w_ukv,
                 gmlp_v_norm, gmlp_w_s, gmlp_b_s, mlstm_gate_bias, mlstm_head_norm, diff_lambda, diff_sub_norm,
                 w_branch, w_merge, b_merge, w_out, w_ff1, w_ff2):
    B, L, _ = x.shape
    mod = (jax.nn.silu(cond) @ w_mod[l] + b_mod[l])[:, None, :]
    sh1, sc1, g1, sh2, sc2, g2 = jnp.split(mod, 6, axis=-1)
    h = _rms(x, norm_g[l, 0]) * (1.0 + sc1) + sh1
    z = h @ w_in[l]
    za = z[..., :IN_A]
    zb = z[..., IN_A:IN_A + IN_B]
    zc = z[..., IN_A + IN_B:IN_A + IN_B + IN_C]
    zd = z[..., IN_A + IN_B + IN_C:]
    if ctx is None:
        ctx_ckv = ctx_kr = ctx_k = ctx_v = None
        st0 = (jnp.zeros((B, 2, H_C, DH_C, DH_C), jnp.float32), jnp.zeros((B, 2, H_C, DH_C), jnp.float32),
               jnp.zeros((B, 2, H_C), jnp.float32))
        rope_a = rope_d = None
    else:
        ctx_ckv, ctx_kr, ctx_k, ctx_v, sC, sn, sm = ctx
        st0 = (sC, sn, sm)
        rope_a, rope_d = rope_cs
    lam_init = 0.8 - 0.6 * math.exp(-0.3 * l)
    ya, ckv, kr = _mla(za, mla_q_norm[l], w_uq[l], mla_kv_norm[l], w_ukv[l], rope_a, ctx_ckv, ctx_kr)
    yb = _chunk_mlp(zb, gmlp_v_norm[l], gmlp_w_s[l], gmlp_b_s[l])
    yc, st = _mlstm(zc, mlstm_gate_bias[l], mlstm_head_norm[l], st0)
    yd, kd, vd = _diff_attn(zd, diff_lambda[l], diff_sub_norm[l], lam_init, rope_d, ctx_k, ctx_v)
    br = jnp.einsum('blnw,nwd->blnd', jnp.stack([ya, yb, yc, yd], axis=2), w_branch[l])
    gates = jax.nn.sigmoid(h @ w_merge[l] + b_merge[l]).reshape(B, L, N_BRANCH, D_MODEL)
    y = jnp.sum(gates * br, axis=2) @ w_out[l]
    x = x + g1 * _rms(y, norm_g[l, 1])
    h2 = _rms(x, norm_g[l, 2]) * (1.0 + sc2) + sh2
    f = jnp.square(jax.nn.relu(h2 @ w_ff1[l])) @ w_ff2[l]
    x = x + g2 * _rms(f, norm_g[l, 3])
    if ctx is None:
        return x, (ckv, kr, kd, vd, st[0], st[1], st[2])
    return x, None


def setup_inputs(seed: int = 0) -> dict:
    key = jax.random.key(seed)
    ks = jax.random.split(key, 40)
    f32 = jnp.float32

    def nrm(i, shape, scale):
        return jax.random.normal(ks[i], shape, f32) * scale

    def gain(i, shape):
        return 1.0 + nrm(i, shape, 0.05)

    gate_bias = nrm(16, (DEPTH, 2, 2, H_C), 0.1) + jnp.array([0.0, 3.0], f32)[None, None, :, None]
    return {
        "x_prompt": nrm(0, (BATCH, SEQ, D_MODEL), 1.0),
        "x_sample": nrm(1, (DEC_BATCH, DEC_SEQ, D_MODEL), 1.0),
        "cache_mla_ckv": nrm(2, (DEC_BATCH, DEPTH, PAST_LEN, KV_LORA), 1.0),
        "cache_mla_krope": nrm(3, (DEC_BATCH, DEPTH, PAST_LEN, ROPE_A), 1.0),
        "cache_diff_k": nrm(4, (DEC_BATCH, DEPTH, PAST_LEN, H_D, 2, DH_D), 1.0),
        "cache_diff_v": nrm(5, (DEC_BATCH, DEPTH, PAST_LEN, H_D, 2 * DH_D), 1.0),
        "state_mlstm_C": nrm(6, (DEC_BATCH, DEPTH, 2, H_C, DH_C, DH_C), 0.5),
        "state_mlstm_n": nrm(7, (DEC_BATCH, DEPTH, 2, H_C, DH_C), 0.5),
        "state_mlstm_m": nrm(8, (DEC_BATCH, DEPTH, 2, H_C), 0.5),
        "c": nrm(9, (DEC_BATCH, D_MODEL), 1.0),
        "c_ctx": nrm(10, (D_MODEL,), 1.0),
        "w_mod": nrm(11, (DEPTH, D_MODEL, 6 * D_MODEL), 0.5 * D_MODEL ** -0.5),
        "b_mod": nrm(12, (DEPTH, 6 * D_MODEL), 0.02),
        "norm_g": gain(13, (DEPTH, 4, D_MODEL)),
        "w_in": nrm(14, (DEPTH, D_MODEL, IN_W), D_MODEL ** -0.5),
        "mla_q_norm": gain(15, (DEPTH, Q_LORA)),
        "w_uq": nrm(17, (DEPTH, Q_LORA, H_A * (NOPE_A + ROPE_A)), Q_LORA ** -0.5),
        "mla_kv_norm": gain(18, (DEPTH, KV_LORA)),
        "w_ukv": nrm(19, (DEPTH, KV_LORA, H_A * (NOPE_A + VH_A)), KV_LORA ** -0.5),
        "gmlp_v_norm": gain(20, (DEPTH, D_B)),
        "gmlp_w_s": nrm(21, (DEPTH, G_B, CHUNK_B, CHUNK_B), CHUNK_B ** -0.5),
        "gmlp_b_s": 1.0 + nrm(22, (DEPTH, G_B, CHUNK_B), 0.05),
        "mlstm_gate_bias": gate_bias,
        "mlstm_head_norm": gain(23, (DEPTH, D_C)),
        "diff_lambda": nrm(24, (DEPTH, 4, DH_D), 0.1),
        "diff_sub_norm": gain(25, (DEPTH, 2 * DH_D)),
        "w_branch": nrm(26, (DEPTH, N_BRANCH, MIX_W, D_MODEL), MIX_W ** -0.5),
        "w_merge": nrm(27, (DEPTH, D_MODEL, N_BRANCH * D_MODEL), D_MODEL ** -0.5),
        "b_merge": nrm(28, (DEPTH, N_BRANCH * D_MODEL), 0.02),
        "w_out": nrm(29, (DEPTH, D_MODEL, D_MODEL), D_MODEL ** -0.5),
        "w_ff1": nrm(30, (DEPTH, D_MODEL, D_FF), D_MODEL ** -0.5),
        "w_ff2": nrm(31, (DEPTH, D_FF, D_MODEL), D_FF ** -0.5),
    }


def reference(x_prompt, x_sample, cache_mla_ckv, cache_mla_krope, cache_diff_k, cache_diff_v,
              state_mlstm_C, state_mlstm_n, state_mlstm_m, c, c_ctx, w_mod, b_mod, norm_g, w_in,
              mla_q_norm, w_uq, mla_kv_norm, w_ukv, gmlp_v_norm, gmlp_w_s, gmlp_b_s, mlstm_gate_bias,
              mlstm_head_norm, diff_lambda, diff_sub_norm, w_branch, w_merge, b_merge, w_out, w_ff1, w_ff2):
    weights = (w_mod, b_mod, norm_g, w_in, mla_q_norm, w_uq, mla_kv_norm, w_ukv, gmlp_v_norm, gmlp_w_s,
               gmlp_b_s, mlstm_gate_bias, mlstm_head_norm, diff_lambda, diff_sub_norm, w_branch, w_merge,
               b_merge, w_out, w_ff1, w_ff2)
    xp = x_prompt
    ents = []
    for l in range(DEPTH):
        xp, ent = _trunk_layer(xp, c_ctx[None, :], None, None, l, *weights)
        ents.append(ent)
    new_mla_ckv = jnp.stack([e[0] for e in ents], axis=1)
    new_mla_krope = jnp.stack([e[1] for e in ents], axis=1)
    new_diff_k = jnp.stack([e[2] for e in ents], axis=1)
    new_diff_v = jnp.stack([e[3] for e in ents], axis=1)
    new_mlstm_C = jnp.stack([e[4] for e in ents], axis=1)
    new_mlstm_n = jnp.stack([e[5] for e in ents], axis=1)
    new_mlstm_m = jnp.stack([e[6] for e in ents], axis=1)
    rows = x_sample.shape[1] // GRID_W
    rope_cs = (_axial_tables(rows, ROPE_A), _axial_tables(rows, DH_D))
    xs = x_sample
    for l in range(DEPTH):
        ctx = (cache_mla_ckv[:, l], cache_mla_krope[:, l], cache_diff_k[:, l], cache_diff_v[:, l],
               state_mlstm_C[:, l], state_mlstm_n[:, l], state_mlstm_m[:, l])
        xs, _ = _trunk_layer(xs, c, rope_cs, ctx, l, *weights)
    return (xp, xs, new_mla_ckv, new_mla_krope, new_diff_k, new_diff_v, new_mlstm_C, new_mlstm_n, new_mlstm_m)
```

```python
import functools
import math

import jax
import jax.numpy as jnp
from jax import lax
from jax.experimental import pallas as pl
from jax.experimental.pallas import tpu as pltpu

F32 = jnp.float32
BF16 = jnp.bfloat16

D_MODEL = 1024
DEPTH = 2
GRID_W = 64
N_BRANCH = 4
MIX_W = 256
H_A, NOPE_A, ROPE_A, VH_A = 4, 64, 32, 64
Q_LORA, KV_LORA = 256, 128
D_B, G_B, CHUNK_B = 256, 4, 128
H_C, DH_C = 4, 64
D_C = H_C * DH_C
H_D, DH_D = 4, 32
D_D = H_D * 2 * DH_D
D_FF = 4 * D_MODEL
IN_A = Q_LORA + KV_LORA + ROPE_A
IN_B = 2 * D_B
IN_C = 4 * D_C + 4 * H_C
IN_D = 3 * D_D
ROPE_BASE = 10000.0
EPS = 1e-6
MLA_SCALE = (NOPE_A + ROPE_A) ** -0.5
DIFF_SCALE = DH_D ** -0.5

LANES = 128
MLSTM_CHUNK = 128
VMEM_LIMIT = 56 * 1024 * 1024
NEG_INF = float("-inf")

ZA_W = 512
ZG_W = 128
Z_OFF_A = 0
Z_OFF_B = Z_OFF_A + ZA_W
Z_OFF_C = Z_OFF_B + IN_B
Z_OFF_G = Z_OFF_C + 4 * D_C
Z_OFF_D = Z_OFF_G + ZG_W
Z_W = Z_OFF_D + IN_D


def _rms(x, g):
    return x * lax.rsqrt(jnp.mean(x * x, axis=-1, keepdims=True) + EPS) * g


def _sigmoid(x):
    return 1.0 / (1.0 + jnp.exp(-x))


def _log_sigmoid(x):
    return jnp.minimum(x, 0.0) - jnp.log1p(jnp.exp(-jnp.abs(x)))


def _dot(a, b):
    return jnp.dot(a, b, preferred_element_type=F32)


def _dot_nt(a, b):
    return lax.dot_general(a, b, (((1,), (1,)), ((), ())), preferred_element_type=F32)


def _lane_group_mask(width, group, index):
    lane = lax.broadcasted_iota(jnp.int32, (1, width), 1)
    return (lane >= index * group) & (lane < (index + 1) * group)


def _mod_kernel(cond_ref, w_ref, b_ref, o_ref):
    c = cond_ref[...]
    s = c * _sigmoid(c)
    o_ref[...] = _dot(s.astype(BF16), w_ref[...].astype(BF16)) + b_ref[...]


def _modulation(cond, w_mod, b_mod):
    rows = cond.shape[0]
    nb = 1024
    return pl.pallas_call(
        _mod_kernel,
        grid=(DEPTH, 6 * D_MODEL // nb),
        in_specs=[
            pl.BlockSpec((rows, D_MODEL), lambda l, j: (0, 0)),
            pl.BlockSpec((None, D_MODEL, nb), lambda l, j: (l, 0, j)),
            pl.BlockSpec((None, 1, nb), lambda l, j: (l, 0, j)),
        ],
        out_specs=pl.BlockSpec((None, rows, nb), lambda l, j: (l, 0, j)),
        out_shape=jax.ShapeDtypeStruct((DEPTH, rows, 6 * D_MODEL), F32),
        name="modulation",
    )(cond, w_mod, b_mod.reshape(DEPTH, 1, 6 * D_MODEL))


def _mod_index(i, *, tm, seq, rows):
    return ((i * tm) // seq if rows > 1 else 0, 0, 0)


def _in_kernel(x_ref, mod_ref, g_ref, w_ref, za_ref, zb_ref, zc_ref, zg_ref, zd_ref):
    x = x_ref[...]
    h = _rms(x, g_ref[...]) * (1.0 + mod_ref[1:2, :]) + mod_ref[0:1, :]
    hb = h.astype(BF16)
    za_ref[...] = _dot(hb, w_ref[:, Z_OFF_A:Z_OFF_B])
    zb_ref[...] = _dot(hb, w_ref[:, Z_OFF_B:Z_OFF_C])
    zc_ref[...] = _dot(hb, w_ref[:, Z_OFF_C:Z_OFF_G])
    zg_ref[...] = _dot(hb, w_ref[:, Z_OFF_G:Z_OFF_D])
    zd_ref[...] = _dot(hb, w_ref[:, Z_OFF_D:Z_W])


def _in_proj(x, mod, g, w, seq):
    t = x.shape[0]
    tm = 256
    widths = (ZA_W, IN_B, 4 * D_C, ZG_W, IN_D)
    return pl.pallas_call(
        _in_kernel,
        grid=(t // tm,),
        in_specs=[
            pl.BlockSpec((tm, D_MODEL), lambda i: (i, 0)),
            pl.BlockSpec((None, 6, D_MODEL), functools.partial(_mod_index, tm=tm, seq=seq, rows=mod.shape[0])),
            pl.BlockSpec((1, D_MODEL), lambda i: (0, 0)),
            pl.BlockSpec((D_MODEL, Z_W), lambda i: (0, 0)),
        ],
        out_specs=[pl.BlockSpec((tm, w_), lambda i: (i, 0)) for w_ in widths],
        out_shape=[jax.ShapeDtypeStruct((t, w_), F32) for w_ in widths],
        compiler_params=pltpu.CompilerParams(vmem_limit_bytes=VMEM_LIMIT),
        name="in_proj",
    )(x, mod, g, w)


def _softmax_pv(q, k, v, scale):
    s = _dot_nt(q, k) * scale
    m = jnp.max(s, axis=-1, keepdims=True)
    p = jnp.exp(s - m)
    l = jnp.sum(p, axis=-1, keepdims=True)
    return _dot(p.astype(BF16), v) / l


def _rope128(x, tc, ts1, ts2):
    return x * tc + pltpu.roll(x, LANES - 16, 1) * ts1 + pltpu.roll(x, 16, 1) * ts2


def _mla_kernel(*refs, seq, has_ctx, tq):
    if has_ctx:
        (za_ref, qn_ref, kvn_ref, wuq_ref, wukv_ref, tc_ref, ts1_ref, ts2_ref, cckv_ref, ckr_ref,
         ya_ref, q_s, k_s, v_s, ckv_s, kr_s) = refs
        past = cckv_ref.shape[0]
    else:
        (za_ref, qn_ref, kvn_ref, wuq_ref, wukv_ref, ya_ref, ckv_ref, q_s, k_s, v_s, ckv_s, kr_s) = refs
        past = 0
    za = za_ref[...]
    cq = _rms(za[:, :Q_LORA], qn_ref[...])
    qh = _dot(cq.astype(BF16), wuq_ref[...])
    ckv = _rms(za[:, Q_LORA:Q_LORA + KV_LORA], kvn_ref[...])
    kr = za[:, Q_LORA + KV_LORA:ZA_W]
    if has_ctx:
        tc, ts1, ts2 = tc_ref[...], ts1_ref[...], ts2_ref[...]
        kr = _rope128(kr, tc, ts1, ts2)
        ckv_s[0:past, :] = cckv_ref[...]
        kr_s[0:past, :] = ckr_ref[...]
    else:
        ckv_ref[...] = ckv
    ckv_s[past:past + seq, :] = ckv
    kr_s[past:past + seq, :] = kr
    kv = _dot(ckv_s[...].astype(BF16), wukv_ref[...])
    kr_all = kr_s[...]
    for h in range(H_A):
        qg = qh[:, h * LANES:(h + 1) * LANES]
        if has_ctx:
            qg = _rope128(qg, tc, ts1, ts2)
        q_s[h] = qg.astype(BF16)
        k_s[h] = (kv[:, h * LANES:(h + 1) * LANES] + kr_all).astype(BF16)
        v_s[h] = kv[:, (H_A + h) * LANES:(H_A + h + 1) * LANES].astype(BF16)

    def body(i, carry):
        rows = pl.ds(pl.multiple_of(i * tq, tq), tq)
        for pair in range(H_A // 2):
            acc = jnp.zeros((tq, LANES), F32)
            for h in (2 * pair, 2 * pair + 1):
                acc = acc + _softmax_pv(q_s[h, rows, :], k_s[h], v_s[h], MLA_SCALE)
            ya_ref[rows, pair * LANES:(pair + 1) * LANES] = acc
        return carry

    lax.fori_loop(0, seq // tq, body, 0)


def _mla(za, qn, kvn, wuq, wukv, batch, seq, rope=None, ctx=None):
    has_ctx = ctx is not None
    past = ctx[0].shape[1] if has_ctx else 0
    lk = past + seq
    tq = min(seq, 256)
    full = lambda shape: pl.BlockSpec(shape, lambda b: (0,) * len(shape))
    in_specs = [
        pl.BlockSpec((seq, ZA_W), lambda b: (b, 0)),
        full((1, Q_LORA)), full((1, KV_LORA)), full((Q_LORA, H_A * LANES)), full((KV_LORA, 2 * H_A * LANES)),
    ]
    args = [za, qn, kvn, wuq, wukv]
    out_specs = [pl.BlockSpec((seq, MIX_W), lambda b: (b, 0))]
    out_shape = [jax.ShapeDtypeStruct((batch * seq, MIX_W), F32)]
    if has_ctx:
        in_specs += [full((seq, LANES))] * 3
        in_specs += [pl.BlockSpec((None, past, LANES), lambda b: (b, 0, 0))] * 2
        args += list(rope) + list(ctx)
    else:
        out_specs.append(pl.BlockSpec((seq, KV_LORA), lambda b: (b, 0)))
        out_shape.append(jax.ShapeDtypeStruct((batch * seq, KV_LORA), F32))
    return pl.pallas_call(
        functools.partial(_mla_kernel, seq=seq, has_ctx=has_ctx, tq=tq),
        grid=(batch,),
        in_specs=in_specs,
        out_specs=out_specs,
        out_shape=out_shape,
        scratch_shapes=[
            pltpu.VMEM((H_A, seq, LANES), BF16),
            pltpu.VMEM((H_A, lk, LANES), BF16),
            pltpu.VMEM((H_A, lk, LANES), BF16),
            pltpu.VMEM((lk, LANES), F32),
            pltpu.VMEM((lk, LANES), F32),
        ],
        compiler_params=pltpu.CompilerParams(vmem_limit_bytes=VMEM_LIMIT),
        name="mla",
    )(*args)


def _diff_kernel(*refs, seq, has_ctx, tq, lam_init):
    if has_ctx:
        (zd_ref, lam_ref, sn_ref, tc_ref, ts1_ref, ts2_ref, ck_ref, cv_ref,
         yd_ref, q_s, k_s, v_s) = refs
        past = ck_ref.shape[0]
    else:
        (zd_ref, lam_ref, sn_ref, yd_ref, q_s, k_s, v_s) = refs
        past = 0
    lo = _lane_group_mask(LANES, 64, 0)
    if has_ctx:
        tc, ts1, ts2 = tc_ref[...], ts1_ref[...], ts2_ref[...]
        k_s[0:past, :] = ck_ref[...].astype(BF16)
        cv = cv_ref[...]
    for half in range(2):
        cols = slice(half * LANES, (half + 1) * LANES)
        q = zd_ref[:, half * LANES:(half + 1) * LANES]
        k = zd_ref[:, D_D + half * LANES:D_D + (half + 1) * LANES]
        v = zd_ref[:, 2 * D_D + half * LANES:2 * D_D + (half + 1) * LANES]
        if has_ctx:
            q = _rope128(q, tc, ts1, ts2)
            k = _rope128(k, tc, ts1, ts2)
            v_lo = jnp.where(lo, cv[:, cols], 0.0)
            v_s[2 * half, 0:past, :] = v_lo.astype(BF16)
            v_s[2 * half + 1, 0:past, :] = (cv[:, cols] - v_lo).astype(BF16)
        q_s[:, cols] = q.astype(BF16)
        k_s[past:past + seq, cols] = k.astype(BF16)
        v_lo = jnp.where(lo, v, 0.0)
        v_s[2 * half, past:past + seq, :] = v_lo.astype(BF16)
        v_s[2 * half + 1, past:past + seq, :] = (v - v_lo).astype(BF16)

    lam = lam_ref[...]
    lam_val = (jnp.exp(jnp.sum(lam[0:1] * lam[1:2], axis=-1, keepdims=True))
               - jnp.exp(jnp.sum(lam[2:3] * lam[3:4], axis=-1, keepdims=True)) + lam_init)
    comp_masks = [jnp.where(_lane_group_mask(D_D, DH_D, j), 1.0, 0.0).astype(BF16) for j in range(2 * H_D)]
    sn = sn_ref[...]

    def body(i, carry):
        rows = pl.ds(pl.multiple_of(i * tq, tq), tq)
        qb = q_s[rows, :]
        k_all = k_s[...]
        for pair in range(H_D // 2):
            acc = jnp.zeros((tq, LANES), F32)
            for h in (2 * pair, 2 * pair + 1):
                o1 = _softmax_pv(qb * comp_masks[2 * h], k_all, v_s[h], DIFF_SCALE)
                o2 = _softmax_pv(qb * comp_masks[2 * h + 1], k_all, v_s[h], DIFF_SCALE)
                acc = acc + (o1 - lam_val * o2)
            sq = acc * acc
            ss_lo = jnp.sum(jnp.where(lo, sq, 0.0), axis=-1, keepdims=True)
            ss_hi = jnp.sum(jnp.where(lo, 0.0, sq), axis=-1, keepdims=True)
            r = jnp.where(lo, lax.rsqrt(ss_lo / (2 * DH_D) + EPS), lax.rsqrt(ss_hi / (2 * DH_D) + EPS))
            yd_ref[rows, pair * LANES:(pair + 1) * LANES] = acc * r * sn * (1.0 - lam_init)
        return carry

    lax.fori_loop(0, seq // tq, body, 0)


def _diff(zd, lam, sn, batch, seq, lam_init, rope=None, ctx=None):
    has_ctx = ctx is not None
    past = ctx[0].shape[1] if has_ctx else 0
    lk = past + seq
    tq = min(seq, 256)
    full = lambda shape: pl.BlockSpec(shape, lambda b: (0,) * len(shape))
    in_specs = [pl.BlockSpec((seq, IN_D), lambda b: (b, 0)), full((4, DH_D)), full((1, LANES))]
    args = [zd, lam, sn]
    if has_ctx:
        in_specs += [full((seq, LANES))] * 3
        in_specs += [pl.BlockSpec((None, past, D_D), lambda b: (b, 0, 0))] * 2
        args += list(rope) + list(ctx)
    return pl.pallas_call(
        functools.partial(_diff_kernel, seq=seq, has_ctx=has_ctx, tq=tq, lam_init=lam_init),
        grid=(batch,),
        in_specs=in_specs,
        out_specs=pl.BlockSpec((seq, MIX_W), lambda b: (b, 0)),
        out_shape=jax.ShapeDtypeStruct((batch * seq, MIX_W), F32),
        scratch_shapes=[
            pltpu.VMEM((seq, D_D), BF16),
            pltpu.VMEM((lk, D_D), BF16),
            pltpu.VMEM((H_D, lk, LANES), BF16),
        ],
        compiler_params=pltpu.CompilerParams(vmem_limit_bytes=VMEM_LIMIT),
        name="diff_attn",
    )(*args)


def _gmlp_kernel(zb_ref, vn_ref, ws_ref, bias_ref, yb_ref):
    tm = zb_ref.shape[0]
    v = _rms(zb_ref[:, D_B:], vn_ref[...]).astype(BF16)
    bias = bias_ref[...]
    gmasks = [_lane_group_mask(D_B, D_B // G_B, g) for g in range(G_B)]
    for ch in range(tm // CHUNK_B):
        rows = slice(ch * CHUNK_B, (ch + 1) * CHUNK_B)
        vc = v[rows, :]
        mixed = bias
        for g in range(G_B):
            mixed = mixed + jnp.where(gmasks[g], _dot(ws_ref[g], vc), 0.0)
        yb_ref[rows, :] = zb_ref[rows, :D_B] * mixed


def _gmlp(zb, vn, ws, bias):
    t = zb.shape[0]
    tm = 512
    full = lambda shape: pl.BlockSpec(shape, lambda i: (0,) * len(shape))
    return pl.pallas_call(
        _gmlp_kernel,
        grid=(t // tm,),
        in_specs=[pl.BlockSpec((tm, IN_B), lambda i: (i, 0)), full((1, D_B)),
                  full((G_B, CHUNK_B, CHUNK_B)), full((CHUNK_B, D_B))],
        out_specs=pl.BlockSpec((tm, MIX_W), lambda i: (i, 0)),
        out_shape=jax.ShapeDtypeStruct((t, MIX_W), F32),
        name="gmlp",
    )(zb, vn, ws, bias)


def _split3(x):
    hi = x.astype(BF16)
    r1 = x - hi.astype(F32)
    mid = r1.astype(BF16)
    lo = (r1 - mid.astype(F32)).astype(BF16)
    return hi, mid, lo


def _mlstm_kernel(*refs, seq, has_state):
    ck = MLSTM_CHUNK
    nc = seq // ck
    if has_state:
        (zc_ref, zg_ref, gb_ref, hn_ref, c0_ref, n0_ref, m0_ref,
         yc_ref, q_s, k_s, v_s, g_s, b_s, gt_s, bt_s, c_s, h_s) = refs
    else:
        (zc_ref, zg_ref, gb_ref, hn_ref,
         yc_ref, cout_ref, nout_ref, mout_ref, q_s, k_s, v_s, g_s, b_s, gt_s, bt_s, c_s, h_s) = refs

    q_s[...] = zc_ref[:, 0:D_C].astype(BF16)
    k_s[...] = (zc_ref[:, D_C:2 * D_C] * (DH_C ** -0.5)).astype(BF16)
    v_s[...] = zc_ref[:, 2 * D_C:3 * D_C].astype(BF16)

    gates = zg_ref[...] + gb_ref[...]
    g_s[...] = gates
    lf = _log_sigmoid(gates)
    r_i = lax.broadcasted_iota(jnp.int32, (ck, ck), 0)
    c_i = lax.broadcasted_iota(jnp.int32, (ck, ck), 1)
    causal = (c_i <= r_i, c_i >= r_i)
    tri = [jnp.where(mk, 1.0, 0.0).astype(BF16) for mk in causal]
    for c in range(nc):
        rows = slice(c * ck, (c + 1) * ck)
        parts = _split3(lf[rows, :])
        for d in range(2):
            b = _dot(tri[d], parts[0]) + _dot(tri[d], parts[1]) + _dot(tri[d], parts[2])
            b_s[d, rows, :] = b
            bt_s[d, c] = b.T
        gt_s[c] = gates[rows, :].T

    hmask = [_lane_group_mask(D_C, DH_C, h) for h in range(H_C)]
    hmask_bf = [jnp.where(mk, 1.0, 0.0).astype(BF16) for mk in hmask]
    rr = lax.broadcasted_iota(jnp.int32, (D_C, D_C), 0) // DH_C
    cc = lax.broadcasted_iota(jnp.int32, (D_C, D_C), 1) // DH_C
    blockdiag = rr == cc

    if has_state:
        c_s[...] = c0_ref[...]
        n_init = tuple(n0_ref[d] for d in range(2))
        m_init = tuple(m0_ref[d, h:h + 1, 0:1] for d in range(2) for h in range(H_C))
    else:
        c_s[...] = jnp.zeros_like(c_s)
        n_init = tuple(jnp.zeros((1, D_C), F32) for _ in range(2))
        m_init = tuple(jnp.zeros((1, 1), F32) for _ in range(2 * H_C))

    def body(i, carry):
        n_st, m_st = carry
        n_new, m_new = [], []
        for d in range(2):
            c = i if d == 0 else nc - 1 - i
            rows = pl.ds(pl.multiple_of(c * ck, ck), ck)
            qc, kc, vc = q_s[rows, :], k_s[rows, :], v_s[rows, :]
            gc, bc = g_s[rows, :], b_s[d, rows, :]
            gtc, btc = gt_s[c], bt_s[d, c]
            n_row = n_st[d]
            qf, kf = qc.astype(F32), kc.astype(F32)
            qcm = _dot(qc, c_s[d].astype(BF16))
            qn_all = qf * n_row
            hc = jnp.zeros((ck, D_C), F32)
            kw = jnp.zeros((ck, D_C), F32)
            w_old_row = jnp.zeros((1, D_C), F32)
            for h in range(H_C):
                li, lfo = d * 8 + h, d * 8 + 4 + h
                m_h = m_st[d * H_C + h]
                bcol, icol = bc[:, lfo:lfo + 1], gc[:, li:li + 1]
                brow, irow = btc[lfo:lfo + 1, :], gtc[li:li + 1, :]
                dlog = jnp.where(causal[d], bcol - brow + irow, NEG_INF)
                inter = bcol + m_h
                mt = jnp.maximum(inter, jnp.max(dlog, axis=-1, keepdims=True))
                w_inter = jnp.exp(inter - mt)
                s = _dot_nt(qc * hmask_bf[h], kc) * jnp.exp(dlog - mt)
                num = w_inter * qcm + _dot(s.astype(BF16), vc)
                qn = jnp.sum(jnp.where(hmask[h], qn_all, 0.0), axis=-1, keepdims=True)
                den = w_inter * qn + jnp.sum(s, axis=-1, keepdims=True)
                hc = hc + jnp.where(hmask[h], num / jnp.maximum(jnp.abs(den), jnp.exp(-mt)), 0.0)
                b_t = bcol[ck - 1:ck, :] if d == 0 else bcol[0:1, :]
                a_s = b_t - bcol + icol
                m_n = jnp.maximum(b_t + m_h, jnp.max(a_s, axis=0, keepdims=True))
                w_old = jnp.exp(b_t + m_h - m_n)
                kw = kw + jnp.where(hmask[h], kf * jnp.exp(a_s - m_n), 0.0)
                w_old_row = w_old_row + jnp.where(hmask[h], w_old, 0.0)
                m_new.append(m_n)
            h_s[d, rows, :] = hc
            outer = _dot(kw.T.astype(BF16), vc)
            c_s[d] = w_old_row * c_s[d] + jnp.where(blockdiag, outer, 0.0)
            n_new.append(w_old_row * n_row + jnp.sum(kw, axis=0, keepdims=True))
        return tuple(n_new), tuple(m_new)

    n_fin, m_fin = lax.fori_loop(0, nc, body, (n_init, m_init))

    hn = hn_ref[...]
    blk = 256
    for rb in range(seq // blk):
        rows = slice(rb * blk, (rb + 1) * blk)
        hsum = h_s[0, rows, :] + h_s[1, rows, :]
        sq = hsum * hsum
        r = jnp.zeros((blk, D_C), F32)
        for h in range(H_C):
            ms = jnp.sum(jnp.where(hmask[h], sq, 0.0), axis=-1, keepdims=True) / DH_C
            r = r + jnp.where(hmask[h], lax.rsqrt(ms + EPS), 0.0)
        yc_ref[rows, :] = _sigmoid(zc_ref[rows, 3 * D_C:4 * D_C]) * (hsum * r * hn)

    if not has_state:
        cout_ref[...] = c_s[...]
        for d in range(2):
            nout_ref[d] = n_fin[d]
            for h in range(H_C):
                mout_ref[d, h:h + 1, :] = jnp.broadcast_to(m_fin[d * H_C + h], (1, LANES))


def _mlstm(zc, zg, gb, hn, batch, seq, state=None):
    has_state = state is not None
    nc = seq // MLSTM_CHUNK
    full = lambda shape: pl.BlockSpec(shape, lambda b: (0,) * len(shape))
    in_specs = [pl.BlockSpec((seq, 4 * D_C), lambda b: (b, 0)), pl.BlockSpec((seq, ZG_W), lambda b: (b, 0)),
                full((1, ZG_W)), full((1, D_C))]
    args = [zc, zg, gb, hn]
    out_specs = [pl.BlockSpec((seq, MIX_W), lambda b: (b, 0))]
    out_shape = [jax.ShapeDtypeStruct((batch * seq, MIX_W), F32)]
    if has_state:
        in_specs += [pl.BlockSpec((None, 2, D_C, D_C), lambda b: (b, 0, 0, 0)),
                     pl.BlockSpec((None, 2, 1, D_C), lambda b: (b, 0, 0, 0)),
                     pl.BlockSpec((None, 2, 8, LANES), lambda b: (b, 0, 0, 0))]
        args += list(state)
    else:
        out_specs += [pl.BlockSpec((None, 2, D_C, D_C), lambda b: (b, 0, 0, 0)),
                      pl.BlockSpec((None, 2, 1, D_C), lambda b: (b, 0, 0, 0)),
                      pl.BlockSpec((None, 2, 8, LANES), lambda b: (b, 0, 0, 0))]
        out_shape += [jax.ShapeDtypeStruct((batch, 2, D_C, D_C), F32),
                      jax.ShapeDtypeStruct((batch, 2, 1, D_C), F32),
                      jax.ShapeDtypeStruct((batch, 2, 8, LANES), F32)]
    return pl.pallas_call(
        functools.partial(_mlstm_kernel, seq=seq, has_state=has_state),
        grid=(batch,),
        in_specs=in_specs,
        out_specs=out_specs,
        out_shape=out_shape,
        scratch_shapes=[
            pltpu.VMEM((seq, D_C), BF16), pltpu.VMEM((seq, D_C), BF16), pltpu.VMEM((seq, D_C), BF16),
            pltpu.VMEM((seq, ZG_W), F32),
            pltpu.VMEM((2, seq, ZG_W), F32),
            pltpu.VMEM((nc, ZG_W, MLSTM_CHUNK), F32),
            pltpu.VMEM((2, nc, ZG_W, MLSTM_CHUNK), F32),
            pltpu.VMEM((2, D_C, D_C), F32),
            pltpu.VMEM((2, seq, D_C), F32),
        ],
        compiler_params=pltpu.CompilerParams(vmem_limit_bytes=VMEM_LIMIT),
        name="mlstm",
    )(*args)


def _merge_kernel(x_ref, mod_ref, g_ref, ya_ref, yb_ref, yc_ref, yd_ref, wm_ref, bm_ref, wb_ref, wo_ref, o_ref):
    x = x_ref[...]
    h = (_rms(x, g_ref[0:1, :]) * (1.0 + mod_ref[1:2, :]) + mod_ref[0:1, :]).astype(BF16)
    acc = jnp.zeros(x.shape, F32)
    for n, y_ref in enumerate((ya_ref, yb_ref, yc_ref, yd_ref)):
        cols = slice(n * D_MODEL, (n + 1) * D_MODEL)
        gate = _sigmoid(_dot(h, wm_ref[:, cols]) + bm_ref[:, cols])
        acc = acc + gate * _dot(y_ref[...].astype(BF16), wb_ref[n])
    y = _dot(acc.astype(BF16), wo_ref[...])
    o_ref[...] = x + mod_ref[2:3, :] * _rms(y, g_ref[1:2, :])


def _merge(x, mod, g, ys, wm, bm, wb, wo, seq):
    t = x.shape[0]
    tm = 256
    full = lambda shape: pl.BlockSpec(shape, lambda i: (0,) * len(shape))
    tok = lambda w_: pl.BlockSpec((tm, w_), lambda i: (i, 0))
    return pl.pallas_call(
        _merge_kernel,
        grid=(t // tm,),
        in_specs=[tok(D_MODEL), pl.BlockSpec((None, 6, D_MODEL), functools.partial(_mod_index, tm=tm, seq=seq, rows=mod.shape[0])),
                  full((4, D_MODEL)), tok(MIX_W), tok(MIX_W), tok(MIX_W), tok(MIX_W),
                  full((D_MODEL, N_BRANCH * D_MODEL)), full((1, N_BRANCH * D_MODEL)),
                  full((N_BRANCH, MIX_W, D_MODEL)), full((D_MODEL, D_MODEL))],
        out_specs=tok(D_MODEL),
        out_shape=jax.ShapeDtypeStruct((t, D_MODEL), F32),
        compiler_params=pltpu.CompilerParams(vmem_limit_bytes=VMEM_LIMIT),
        name="merge",
    )(x, mod, g, *ys, wm, bm, wb, wo)


def _ffn_kernel(x_ref, mod_ref, g_ref, w1_ref, w2_ref, o_ref):
    x = x_ref[...]
    h = (_rms(x, g_ref[2:3, :]) * (1.0 + mod_ref[4:5, :]) + mod_ref[3:4, :]).astype(BF16)
    f = jnp.zeros(x.shape, F32)
    for j in range(D_FF // D_MODEL):
        cols = slice(j * D_MODEL, (j + 1) * D_MODEL)
        a = jnp.maximum(_dot(h, w1_ref[:, cols]), 0.0)
        f = f + _dot((a * a).astype(BF16), w2_ref[cols, :])
    o_ref[...] = x + mod_ref[5:6, :] * _rms(f, g_ref[3:4, :])


def _ffn(x, mod, g, w1, w2, seq):
    t = x.shape[0]
    tm = 256
    full = lambda shape: pl.BlockSpec(shape, lambda i: (0,) * len(shape))
    tok = lambda w_: pl.BlockSpec((tm, w_), lambda i: (i, 0))
    return pl.pallas_call(
        _ffn_kernel,
        grid=(t // tm,),
        in_specs=[tok(D_MODEL), pl.BlockSpec((None, 6, D_MODEL), functools.partial(_mod_index, tm=tm, seq=seq, rows=mod.shape[0])),
                  full((4, D_MODEL)), full((D_MODEL, D_FF)), full((D_FF, D_MODEL))],
        out_specs=tok(D_MODEL),
        out_shape=jax.ShapeDtypeStruct((t, D_MODEL), F32),
        compiler_params=pltpu.CompilerParams(vmem_limit_bytes=VMEM_LIMIT),
        name="ffn",
    )(x, mod, g, w1, w2)


def _arrange_w_in(w):
    z = lambda n: jnp.zeros((D_MODEL, n), w.dtype)
    o_b, o_c = IN_A, IN_A + IN_B
    o_g, o_d = o_c + 4 * D_C, o_c + IN_C
    return jnp.concatenate([
        w[:, :Q_LORA + KV_LORA], z(64), w[:, Q_LORA + KV_LORA:IN_A], z(32),
        w[:, o_b:o_g], w[:, o_g:o_d], z(ZG_W - 4 * H_C), w[:, o_d:]], axis=1).astype(BF16)


def _arrange_w_uq(w):
    w = w.reshape(Q_LORA, H_A, NOPE_A + ROPE_A)
    return jnp.pad(w, ((0, 0), (0, 0), (0, LANES - NOPE_A - ROPE_A))).reshape(Q_LORA, H_A * LANES).astype(BF16)


def _arrange_w_ukv(w):
    w = w.reshape(KV_LORA, H_A, NOPE_A + VH_A)
    wk = jnp.pad(w[:, :, :NOPE_A], ((0, 0), (0, 0), (0, LANES - NOPE_A)))
    v = w[:, :, NOPE_A:]
    zero = jnp.zeros_like(v)
    even = jnp.concatenate([v, zero], axis=-1)
    odd = jnp.concatenate([zero, v], axis=-1)
    wv = jnp.where((jnp.arange(H_A) % 2 == 0)[None, :, None], even, odd)
    return jnp.concatenate([wk.reshape(KV_LORA, -1), wv.reshape(KV_LORA, -1)], axis=1).astype(BF16)


def _rope_tables(rows):
    row = jnp.repeat(jnp.arange(rows), GRID_W).astype(F32)
    col = jnp.tile(jnp.arange(GRID_W), rows).astype(F32)
    nf = ROPE_A // 4
    inv = jnp.exp(-math.log(ROPE_BASE) * jnp.arange(nf, dtype=F32) / nf)
    ang = jnp.concatenate([row[:, None] * inv, col[:, None] * inv], axis=-1)
    cos, sin = jnp.cos(ang), jnp.sin(ang)
    n = cos.shape[0]
    one, zero = jnp.ones((n, 1), F32), jnp.zeros((n, 1), F32)
    rep = lambda a, k: jnp.broadcast_to(a, (n, k))
    mla = (jnp.concatenate([rep(one, 64), cos, cos, rep(one, 32)], axis=1),
           jnp.concatenate([rep(zero, 64), -sin, rep(zero, 48)], axis=1),
           jnp.concatenate([rep(zero, 80), sin, rep(zero, 32)], axis=1))
    z16 = rep(zero, 16)
    dif = (jnp.tile(jnp.concatenate([cos, cos], axis=1), (1, 4)),
           jnp.tile(jnp.concatenate([-sin, z16], axis=1), (1, 4)),
           jnp.tile(jnp.concatenate([z16, sin], axis=1), (1, 4)))
    return mla, dif


def _block_diag_state(c):
    b = c.shape[0]
    eye = jnp.eye(H_C, dtype=c.dtype)
    out = c[:, :, :, :, None, :] * eye[None, None, :, None, :, None]
    return out.reshape(b, 2, D_C, D_C)


def _diag_blocks(c):
    b = c.shape[0]
    c = c.reshape(b, 2, H_C, DH_C, H_C, DH_C)
    return jnp.stack([c[:, :, h, :, h, :] for h in range(H_C)], axis=2)


def kernel(x_prompt, x_sample, cache_mla_ckv, cache_mla_krope, cache_diff_k, cache_diff_v, state_mlstm_C,
           state_mlstm_n, state_mlstm_m, c, c_ctx, w_mod, b_mod, norm_g, w_in, mla_q_norm, w_uq, mla_kv_norm,
           w_ukv, gmlp_v_norm, gmlp_w_s, gmlp_b_s, mlstm_gate_bias, mlstm_head_norm, diff_lambda, diff_sub_norm,
           w_branch, w_merge, b_merge, w_out, w_ff1, w_ff2):
    bp, lp, _ = x_prompt.shape
    bs, ls, _ = x_sample.shape
    past = cache_mla_ckv.shape[2]

    cond = jnp.concatenate([c_ctx[None, :], c, jnp.zeros((16 - 1 - bs, D_MODEL), F32)], axis=0)
    mod = _modulation(cond, w_mod, b_mod).reshape(DEPTH, 16, 6, D_MODEL)
    rope_mla, rope_dif = _rope_tables(ls // GRID_W)

    xp = x_prompt.reshape(bp * lp, D_MODEL)
    xs = x_sample.reshape(bs * ls, D_MODEL)
    ents = []
    for l in range(DEPTH):
        lam_init = 0.8 - 0.6 * math.exp(-0.3 * l)
        w_in_l = _arrange_w_in(w_in[l])
        wuq_l = _arrange_w_uq(w_uq[l])
        wukv_l = _arrange_w_ukv(w_ukv[l])
        qn_l = mla_q_norm[l][None, :]
        kvn_l = mla_kv_norm[l][None, :]
        vn_l = gmlp_v_norm[l][None, :]
        ws_l = gmlp_w_s[l].astype(BF16)
        bias_l = jnp.repeat(gmlp_b_s[l].T, D_B // G_B, axis=1)
        gb_l = jnp.pad(mlstm_gate_bias[l].reshape(1, 4 * H_C), ((0, 0), (0, ZG_W - 4 * H_C)))
        hn_l = mlstm_head_norm[l][None, :]
        sn_l = jnp.tile(diff_sub_norm[l], 2)[None, :]
        wm_l = w_merge[l].astype(BF16)
        bm_l = b_merge[l][None, :]
        wb_l = w_branch[l].astype(BF16)
        wo_l = w_out[l].astype(BF16)
        w1_l = w_ff1[l].astype(BF16)
        w2_l = w_ff2[l].astype(BF16)

        for is_sample in (False, True):
            x = xs if is_sample else xp
            batch, seq = (bs, ls) if is_sample else (bp, lp)
            mod_l = mod[l, 1:1 + bs] if is_sample else mod[l, 0:1]
            za, zb, zc, zg, zd = _in_proj(x, mod_l, norm_g[l, 0:1], w_in_l, seq)
            if is_sample:
                ctx_kr = jnp.pad(cache_mla_krope[:, l], ((0, 0), (0, 0), (64, LANES - 64 - ROPE_A)))
                (ya,) = _mla(za, qn_l, kvn_l, wuq_l, wukv_l, batch, seq, rope=rope_mla,
                             ctx=(cache_mla_ckv[:, l], ctx_kr))
                yd = _diff(zd, diff_lambda[l], sn_l, batch, seq, lam_init, rope=rope_dif,
                           ctx=(cache_diff_k[:, l].reshape(bs, past, D_D), cache_diff_v[:, l].reshape(bs, past, D_D)))
                m0 = jnp.broadcast_to(
                    jnp.pad(state_mlstm_m[:, l], ((0, 0), (0, 0), (0, 8 - H_C)))[..., None], (bs, 2, 8, LANES))
                (yc,) = _mlstm(zc, zg, gb_l, hn_l, batch, seq,
                               state=(_block_diag_state(state_mlstm_C[:, l]),
                                      state_mlstm_n[:, l].reshape(bs, 2, 1, D_C), m0))
            else:
                ya, ckv = _mla(za, qn_l, kvn_l, wuq_l, wukv_l, batch, seq)
                yd = _diff(zd, diff_lambda[l], sn_l, batch, seq, lam_init)
                yc, c_new, n_new, m_new = _mlstm(zc, zg, gb_l, hn_l, batch, seq)
                ents.append((
                    ckv.reshape(bp, lp, KV_LORA),
                    za[:, Q_LORA + KV_LORA + 64:Q_LORA + KV_LORA + 64 + ROPE_A].reshape(bp, lp, ROPE_A),
                    zd[:, D_D:2 * D_D].reshape(bp, lp, H_D, 2, DH_D),
                    zd[:, 2 * D_D:].reshape(bp, lp, H_D, 2 * DH_D),
                    _diag_blocks(c_new),
                    n_new.reshape(bp, 2, H_C, DH_C),
                    m_new[:, :, :H_C, 0]))
            yb = _gmlp(zb, vn_l, ws_l, bias_l)
            x = _merge(x, mod_l, norm_g[l], (ya, yb, yc, yd), wm_l, bm_l, wb_l, wo_l, seq)
            x = _ffn(x, mod_l, norm_g[l], w1_l, w2_l, seq)
            if is_sample:
                xs = x
            else:
                xp = x

    stack = lambda j: jnp.stack([e[j] for e in ents], axis=1)
    return (xp.reshape(bp, lp, D_MODEL), xs.reshape(bs, ls, D_MODEL),
            stack(0), stack(1), stack(2), stack(3), stack(4), stack(5), stack(6))
```

```python
import functools
import math

import jax
import jax.numpy as jnp
from jax import lax
from jax.experimental import pallas as pl
from jax.experimental.pallas import tpu as pltpu

F32 = jnp.float32
BF16 = jnp.bfloat16

D_MODEL = 1024
DEPTH = 2
GRID_W = 64
N_BRANCH = 4
MIX_W = 256
H_A, NOPE_A, ROPE_A, VH_A = 4, 64, 32, 64
Q_LORA, KV_LORA = 256, 128
D_B, G_B, CHUNK_B = 256, 4, 128
H_C, DH_C = 4, 64
D_C = H_C * DH_C
H_D, DH_D = 4, 32
D_D = H_D * 2 * DH_D
D_FF = 4 * D_MODEL
IN_A = Q_LORA + KV_LORA + ROPE_A
IN_B = 2 * D_B
IN_C = 4 * D_C + 4 * H_C
IN_D = 3 * D_D
ROPE_BASE = 10000.0
EPS = 1e-6
MLA_SCALE = (NOPE_A + ROPE_A) ** -0.5
DIFF_SCALE = DH_D ** -0.5

LANES = 128
MLSTM_CHUNK = 128
VMEM_LIMIT = 56 * 1024 * 1024
NEG_INF = float("-inf")

ZA_W = 512
ZG_W = 128
Z_OFF_A = 0
Z_OFF_B = Z_OFF_A + ZA_W
Z_OFF_C = Z_OFF_B + IN_B
Z_OFF_G = Z_OFF_C + 4 * D_C
Z_OFF_D = Z_OFF_G + ZG_W
Z_W = Z_OFF_D + IN_D


def _rms(x, g):
    return x * lax.rsqrt(jnp.mean(x * x, axis=-1, keepdims=True) + EPS) * g


def _sigmoid(x):
    return 1.0 / (1.0 + jnp.exp(-x))


def _log_sigmoid(x):
    return jnp.minimum(x, 0.0) - jnp.log1p(jnp.exp(-jnp.abs(x)))


def _dot(a, b):
    return jnp.dot(a, b, preferred_element_type=F32)


def _dot_nt(a, b):
    return lax.dot_general(a, b, (((1,), (1,)), ((), ())), preferred_element_type=F32)


def _lane_group_mask(width, group, index):
    lane = lax.broadcasted_iota(jnp.int32, (1, width), 1)
    return (lane >= index * group) & (lane < (index + 1) * group)


def _mod_kernel(cond_ref, w_ref, b_ref, o_ref):
    c = cond_ref[...]
    s = c * _sigmoid(c)
    o_ref[...] = _dot(s.astype(BF16), w_ref[...].astype(BF16)) + b_ref[...]


def _modulation(cond, w_mod, b_mod):
    rows = cond.shape[0]
    nb = 1024
    return pl.pallas_call(
        _mod_kernel,
        grid=(DEPTH, 6 * D_MODEL // nb),
        in_specs=[
            pl.BlockSpec((rows, D_MODEL), lambda l, j: (0, 0)),
            pl.BlockSpec((None, D_MODEL, nb), lambda l, j: (l, 0, j)),
            pl.BlockSpec((None, 1, nb), lambda l, j: (l, 0, j)),
        ],
        out_specs=pl.BlockSpec((None, rows, nb), lambda l, j: (l, 0, j)),
        out_shape=jax.ShapeDtypeStruct((DEPTH, rows, 6 * D_MODEL), F32),
        name="modulation",
    )(cond, w_mod, b_mod.reshape(DEPTH, 1, 6 * D_MODEL))


TOKEN_TILE = 512


def _resident(shape):
    return pl.BlockSpec(shape, lambda i: (0,) * len(shape), pipeline_mode=pl.Buffered(1))


def _mod_index(i, *, tm, seq, rows):
    return ((i * tm) // seq if rows > 1 else 0, 0, 0)


def _in_kernel(x_ref, mod_ref, g_ref, w_ref, za_ref, zb_ref, zc_ref, zg_ref, zd_ref):
    x = x_ref[...]
    h = _rms(x, g_ref[...]) * (1.0 + mod_ref[1:2, :]) + mod_ref[0:1, :]
    hb = h.astype(BF16)
    za_ref[...] = _dot(hb, w_ref[:, Z_OFF_A:Z_OFF_B])
    zb_ref[...] = _dot(hb, w_ref[:, Z_OFF_B:Z_OFF_C])
    zc_ref[...] = _dot(hb, w_ref[:, Z_OFF_C:Z_OFF_G])
    zg_ref[...] = _dot(hb, w_ref[:, Z_OFF_G:Z_OFF_D])
    zd_ref[...] = _dot(hb, w_ref[:, Z_OFF_D:Z_W])


def _in_proj(x, mod, g, w, seq):
    t = x.shape[0]
    tm = TOKEN_TILE
    widths = (ZA_W, IN_B, 4 * D_C, ZG_W, IN_D)
    return pl.pallas_call(
        _in_kernel,
        grid=(t // tm,),
        in_specs=[
            pl.BlockSpec((tm, D_MODEL), lambda i: (i, 0)),
            pl.BlockSpec((None, 6, D_MODEL), functools.partial(_mod_index, tm=tm, seq=seq, rows=mod.shape[0])),
            _resident((1, D_MODEL)),
            _resident((D_MODEL, Z_W)),
        ],
        out_specs=[pl.BlockSpec((tm, w_), lambda i: (i, 0)) for w_ in widths],
        out_shape=[jax.ShapeDtypeStruct((t, w_), F32) for w_ in widths],
        compiler_params=pltpu.CompilerParams(vmem_limit_bytes=VMEM_LIMIT),
        name="in_proj",
    )(x, mod, g, w)


def _softmax_pv(q, k, v, scale):
    s = _dot_nt(q, k) * scale
    m = jnp.max(s, axis=-1, keepdims=True)
    p = jnp.exp(s - m)
    l = jnp.sum(p, axis=-1, keepdims=True)
    return _dot(p.astype(BF16), v) / l


def _rope128(x, tc, ts1, ts2):
    return x * tc + pltpu.roll(x, LANES - 16, 1) * ts1 + pltpu.roll(x, 16, 1) * ts2


def _mla_kernel(*refs, seq, has_ctx, tq):
    if has_ctx:
        (za_ref, qn_ref, kvn_ref, wuq_ref, wukv_ref, tc_ref, ts1_ref, ts2_ref, cckv_ref, ckr_ref,
         ya_ref, q_s, k_s, v_s, ckv_s, kr_s) = refs
        past = cckv_ref.shape[0]
    else:
        (za_ref, qn_ref, kvn_ref, wuq_ref, wukv_ref, ya_ref, ckv_ref, q_s, k_s, v_s, ckv_s, kr_s) = refs
        past = 0
    za = za_ref[...]
    cq = _rms(za[:, :Q_LORA], qn_ref[...])
    qh = _dot(cq.astype(BF16), wuq_ref[...])
    ckv = _rms(za[:, Q_LORA:Q_LORA + KV_LORA], kvn_ref[...])
    kr = za[:, Q_LORA + KV_LORA:ZA_W]
    if has_ctx:
        tc, ts1, ts2 = tc_ref[...], ts1_ref[...], ts2_ref[...]
        kr = _rope128(kr, tc, ts1, ts2)
        ckv_s[0:past, :] = cckv_ref[...]
        kr_s[0:past, :] = ckr_ref[...]
    else:
        ckv_ref[...] = ckv
    ckv_s[past:past + seq, :] = ckv
    kr_s[past:past + seq, :] = kr
    kv = _dot(ckv_s[...].astype(BF16), wukv_ref[...])
    kr_all = kr_s[...]
    for h in range(H_A):
        qg = qh[:, h * LANES:(h + 1) * LANES]
        if has_ctx:
            qg = _rope128(qg, tc, ts1, ts2)
        q_s[h] = qg.astype(BF16)
        k_s[h] = (kv[:, h * LANES:(h + 1) * LANES] + kr_all).astype(BF16)
        v_s[h] = kv[:, (H_A + h) * LANES:(H_A + h + 1) * LANES].astype(BF16)

    def body(i, carry):
        rows = pl.ds(pl.multiple_of(i * tq, tq), tq)
        for pair in range(H_A // 2):
            acc = jnp.zeros((tq, LANES), F32)
            for h in (2 * pair, 2 * pair + 1):
                acc = acc + _softmax_pv(q_s[h, rows, :], k_s[h], v_s[h], MLA_SCALE)
            ya_ref[rows, pair * LANES:(pair + 1) * LANES] = acc
        return carry

    lax.fori_loop(0, seq // tq, body, 0)


def _mla(za, qn, kvn, wuq, wukv, batch, seq, rope=None, ctx=None):
    has_ctx = ctx is not None
    past = ctx[0].shape[1] if has_ctx else 0
    lk = past + seq
    tq = min(seq, 256)
    full = lambda shape: pl.BlockSpec(shape, lambda b: (0,) * len(shape))
    in_specs = [
        pl.BlockSpec((seq, ZA_W), lambda b: (b, 0)),
        full((1, Q_LORA)), full((1, KV_LORA)), full((Q_LORA, H_A * LANES)), full((KV_LORA, 2 * H_A * LANES)),
    ]
    args = [za, qn, kvn, wuq, wukv]
    out_specs = [pl.BlockSpec((seq, MIX_W), lambda b: (b, 0))]
    out_shape = [jax.ShapeDtypeStruct((batch * seq, MIX_W), F32)]
    if has_ctx:
        in_specs += [full((seq, LANES))] * 3
        in_specs += [pl.BlockSpec((None, past, LANES), lambda b: (b, 0, 0))] * 2
        args += list(rope) + list(ctx)
    else:
        out_specs.append(pl.BlockSpec((seq, KV_LORA), lambda b: (b, 0)))
        out_shape.append(jax.ShapeDtypeStruct((batch * seq, KV_LORA), F32))
    return pl.pallas_call(
        functools.partial(_mla_kernel, seq=seq, has_ctx=has_ctx, tq=tq),
        grid=(batch,),
        in_specs=in_specs,
        out_specs=out_specs,
        out_shape=out_shape,
        scratch_shapes=[
            pltpu.VMEM((H_A, seq, LANES), BF16),
            pltpu.VMEM((H_A, lk, LANES), BF16),
            pltpu.VMEM((H_A, lk, LANES), BF16),
            pltpu.VMEM((lk, LANES), F32),
            pltpu.VMEM((lk, LANES), F32),
        ],
        compiler_params=pltpu.CompilerParams(vmem_limit_bytes=VMEM_LIMIT),
        name="mla",
    )(*args)


def _diff_kernel(*refs, seq, has_ctx, tq, lam_init):
    if has_ctx:
        (zd_ref, lam_ref, sn_ref, tc_ref, ts1_ref, ts2_ref, ck_ref, cv_ref,
         yd_ref, q_s, k_s, v_s) = refs
        past = ck_ref.shape[0]
    else:
        (zd_ref, lam_ref, sn_ref, yd_ref, q_s, k_s, v_s) = refs
        past = 0
    lo = _lane_group_mask(LANES, 64, 0)
    if has_ctx:
        tc, ts1, ts2 = tc_ref[...], ts1_ref[...], ts2_ref[...]
        k_s[0:past, :] = ck_ref[...].astype(BF16)
        cv = cv_ref[...]
    for half in range(2):
        cols = slice(half * LANES, (half + 1) * LANES)
        q = zd_ref[:, half * LANES:(half + 1) * LANES]
        k = zd_ref[:, D_D + half * LANES:D_D + (half + 1) * LANES]
        v = zd_ref[:, 2 * D_D + half * LANES:2 * D_D + (half + 1) * LANES]
        if has_ctx:
            q = _rope128(q, tc, ts1, ts2)
            k = _rope128(k, tc, ts1, ts2)
            v_lo = jnp.where(lo, cv[:, cols], 0.0)
            v_s[2 * half, 0:past, :] = v_lo.astype(BF16)
            v_s[2 * half + 1, 0:past, :] = (cv[:, cols] - v_lo).astype(BF16)
        q_s[:, cols] = q.astype(BF16)
        k_s[past:past + seq, cols] = k.astype(BF16)
        v_lo = jnp.where(lo, v, 0.0)
        v_s[2 * half, past:past + seq, :] = v_lo.astype(BF16)
        v_s[2 * half + 1, past:past + seq, :] = (v - v_lo).astype(BF16)

    lam = lam_ref[...]
    lam_val = (jnp.exp(jnp.sum(lam[0:1] * lam[1:2], axis=-1, keepdims=True))
               - jnp.exp(jnp.sum(lam[2:3] * lam[3:4], axis=-1, keepdims=True)) + lam_init)
    comp_masks = [jnp.where(_lane_group_mask(D_D, DH_D, j), 1.0, 0.0).astype(BF16) for j in range(2 * H_D)]
    sn = sn_ref[...]

    def body(i, carry):
        rows = pl.ds(pl.multiple_of(i * tq, tq), tq)
        qb = q_s[rows, :]
        k_all = k_s[...]
        for pair in range(H_D // 2):
            acc = jnp.zeros((tq, LANES), F32)
            for h in (2 * pair, 2 * pair + 1):
                o1 = _softmax_pv(qb * comp_masks[2 * h], k_all, v_s[h], DIFF_SCALE)
                o2 = _softmax_pv(qb * comp_masks[2 * h + 1], k_all, v_s[h], DIFF_SCALE)
                acc = acc + (o1 - lam_val * o2)
            sq = acc * acc
            ss_lo = jnp.sum(jnp.where(lo, sq, 0.0), axis=-1, keepdims=True)
            ss_hi = jnp.sum(jnp.where(lo, 0.0, sq), axis=-1, keepdims=True)
            r = jnp.where(lo, lax.rsqrt(ss_lo / (2 * DH_D) + EPS), lax.rsqrt(ss_hi / (2 * DH_D) + EPS))
            yd_ref[rows, pair * LANES:(pair + 1) * LANES] = acc * r * sn * (1.0 - lam_init)
        return carry

    lax.fori_loop(0, seq // tq, body, 0)


def _diff(zd, lam, sn, batch, seq, lam_init, rope=None, ctx=None):
    has_ctx = ctx is not None
    past = ctx[0].shape[1] if has_ctx else 0
    lk = past + seq
    tq = min(seq, 256)
    full = lambda shape: pl.BlockSpec(shape, lambda b: (0,) * len(shape))
    in_specs = [pl.BlockSpec((seq, IN_D), lambda b: (b, 0)), full((4, DH_D)), full((1, LANES))]
    args = [zd, lam, sn]
    if has_ctx:
        in_specs += [full((seq, LANES))] * 3
        in_specs += [pl.BlockSpec((None, past, D_D), lambda b: (b, 0, 0))] * 2
        args += list(rope) + list(ctx)
    return pl.pallas_call(
        functools.partial(_diff_kernel, seq=seq, has_ctx=has_ctx, tq=tq, lam_init=lam_init),
        grid=(batch,),
        in_specs=in_specs,
        out_specs=pl.BlockSpec((seq, MIX_W), lambda b: (b, 0)),
        out_shape=jax.ShapeDtypeStruct((batch * seq, MIX_W), F32),
        scratch_shapes=[
            pltpu.VMEM((seq, D_D), BF16),
            pltpu.VMEM((lk, D_D), BF16),
            pltpu.VMEM((H_D, lk, LANES), BF16),
        ],
        compiler_params=pltpu.CompilerParams(vmem_limit_bytes=VMEM_LIMIT),
        name="diff_attn",
    )(*args)


def _gmlp_kernel(zb_ref, vn_ref, ws_ref, bias_ref, yb_ref):
    tm = zb_ref.shape[0]
    v = _rms(zb_ref[:, D_B:], vn_ref[...]).astype(BF16)
    bias = bias_ref[...]
    gmasks = [_lane_group_mask(D_B, D_B // G_B, g) for g in range(G_B)]
    for ch in range(tm // CHUNK_B):
        rows = slice(ch * CHUNK_B, (ch + 1) * CHUNK_B)
        vc = v[rows, :]
        mixed = bias
        for g in range(G_B):
            mixed = mixed + jnp.where(gmasks[g], _dot(ws_ref[g], vc), 0.0)
        yb_ref[rows, :] = zb_ref[rows, :D_B] * mixed


def _gmlp(zb, vn, ws, bias):
    t = zb.shape[0]
    tm = 512
    full = lambda shape: pl.BlockSpec(shape, lambda i: (0,) * len(shape))
    return pl.pallas_call(
        _gmlp_kernel,
        grid=(t // tm,),
        in_specs=[pl.BlockSpec((tm, IN_B), lambda i: (i, 0)), full((1, D_B)),
                  full((G_B, CHUNK_B, CHUNK_B)), full((CHUNK_B, D_B))],
        out_specs=pl.BlockSpec((tm, MIX_W), lambda i: (i, 0)),
        out_shape=jax.ShapeDtypeStruct((t, MIX_W), F32),
        name="gmlp",
    )(zb, vn, ws, bias)


def _split3(x):
    hi = x.astype(BF16)
    r1 = x - hi.astype(F32)
    mid = r1.astype(BF16)
    lo = (r1 - mid.astype(F32)).astype(BF16)
    return hi, mid, lo


ST_ROWS = DH_C + 8
N_UNITS = 2 * H_C


def _mlstm_kernel(*refs, seq, has_state):
    ck = MLSTM_CHUNK
    nc = seq // ck
    if has_state:
        (zc_ref, zg_ref, gb_ref, hn_ref, st0_ref, m0_ref,
         yc_ref, k_s, qt_s, kt_s, vt_s, rcc_s, ca_s, wi_s, en_s, ws_s, wo_s, st_s, ht_s) = refs
    else:
        (zc_ref, zg_ref, gb_ref, hn_ref,
         yc_ref, stout_ref, mout_ref, k_s, qt_s, kt_s, vt_s, rcc_s, ca_s, wi_s, en_s, ws_s, wo_s, st_s, ht_s) = refs

    for c in range(nc):
        rows = slice(c * ck, (c + 1) * ck)
        qt_s[c] = zc_ref[rows, 0:D_C].T.astype(BF16)
        kk = zc_ref[rows, D_C:2 * D_C] * (DH_C ** -0.5)
        k_s[rows, :] = kk.astype(BF16)
        kt_s[c] = kk.T.astype(BF16)
        vt_s[c] = zc_ref[rows, 2 * D_C:3 * D_C].T.astype(BF16)

    gates = zg_ref[...] + gb_ref[...]
    lf = _log_sigmoid(gates)
    r_i = lax.broadcasted_iota(jnp.int32, (ck, ck), 0)
    c_i = lax.broadcasted_iota(jnp.int32, (ck, ck), 1)
    tri_fw = jnp.where(c_i <= r_i, 1.0, 0.0).astype(BF16)
    tri_bw = jnp.where(c_i >= r_i, 1.0, 0.0).astype(BF16)
    fw_lane = (lax.broadcasted_iota(jnp.int32, (1, ZG_W), 1) % N_UNITS) < H_C
    i_rows, b_rows = [], []
    for c in range(nc):
        rows = slice(c * ck, (c + 1) * ck)
        p0, p1, p2 = _split3(lf[rows, :])
        b_fw = _dot(tri_fw, p0) + _dot(tri_fw, p1) + _dot(tri_fw, p2)
        b_bw = _dot(tri_bw, p0) + _dot(tri_bw, p1) + _dot(tri_bw, p2)
        b = jnp.where(fw_lane, b_fw, b_bw)
        g_c = gates[rows, :]
        rcc_s[rows, :] = g_c - pltpu.roll(b, ZG_W - N_UNITS, 1)
        i_rows.append(g_c.T[0:N_UNITS, :])
        b_rows.append(b.T[N_UNITS:2 * N_UNITS, :])

    row8 = lax.broadcasted_iota(jnp.int32, (N_UNITS, ck), 0)
    lane8 = lax.broadcasted_iota(jnp.int32, (N_UNITS, ck), 1)
    fw8 = row8 < H_C
    fw81 = fw8[:, 0:1]
    m_prev = m0_ref[:, 0:1] if has_state else jnp.zeros((N_UNITS, 1), F32)
    for i in range(nc):
        i8 = jnp.where(fw8, i_rows[i], i_rows[nc - 1 - i])
        b8 = jnp.where(fw8, b_rows[i], b_rows[nc - 1 - i])
        r8 = i8 - b8
        pm, sm = r8, r8
        sh = 1
        while sh < ck:
            pm = jnp.maximum(pm, jnp.where(lane8 >= sh, pltpu.roll(pm, sh, 1), NEG_INF))
            sm = jnp.maximum(sm, jnp.where(lane8 < ck - sh, pltpu.roll(sm, ck - sh, 1), NEG_INF))
            sh *= 2
        cm = jnp.where(fw8, pm, sm)
        b_t = jnp.where(fw81, b8[:, ck - 1:ck], b8[:, 0:1])
        cm_end = jnp.where(fw81, pm[:, ck - 1:ck], sm[:, 0:1])
        inter = b8 + m_prev
        mt = jnp.maximum(inter, b8 + cm)
        ca_s[i] = b8 - mt
        wi_s[i] = jnp.exp(inter - mt)
        en_s[i] = jnp.exp(-mt)
        m_new = b_t + jnp.maximum(m_prev, cm_end)
        ws_s[i] = jnp.exp(b_t + r8 - m_new)
        wo_s[i] = jnp.broadcast_to(jnp.exp(b_t + m_prev - m_new), (N_UNITS, ck))
        m_prev = m_new

    if has_state:
        st_s[...] = st0_ref[...]
    else:
        st_s[...] = jnp.zeros_like(st_s)

    hmask_bf = [jnp.where(_lane_group_mask(D_C, DH_C, h), 1.0, 0.0).astype(BF16) for h in range(H_C)]
    src_ok = (r_i <= c_i, r_i >= c_i)

    def body(i, carry):
        ca, wi, en, ws, wo = ca_s[i], wi_s[i], en_s[i], ws_s[i], wo_s[i]
        for d in range(2):
            c = i if d == 0 else nc - 1 - i
            rows = pl.ds(pl.multiple_of(c * ck, ck), ck)
            kc = k_s[rows, :]
            q_t, k_t, v_t = qt_s[c], kt_s[c], vt_s[c]
            rc = rcc_s[rows, :]
            for h in range(H_C):
                u = d * H_C + h
                hs = slice(h * DH_C, (h + 1) * DH_C)
                w_i, w_s = wi[u:u + 1, :], ws[u:u + 1, :]
                st = _dot(kc * hmask_bf[h], q_t)
                pt = st * jnp.where(src_ok[d], jnp.exp(rc[:, u:u + 1] + ca[u:u + 1, :]), 0.0)
                state = st_s[u]
                qct = _dot(state.astype(BF16), q_t[hs, :])
                den = w_i * qct[DH_C:DH_C + 1, :] + jnp.sum(pt, axis=0, keepdims=True)
                num = w_i * qct[0:DH_C, :] + _dot(v_t[hs, :], pt.astype(BF16))
                ht_s[d, c, hs, :] = num / jnp.maximum(jnp.abs(den), en[u:u + 1, :])
                vw = jnp.concatenate([v_t[hs, :].astype(F32) * w_s, jnp.broadcast_to(w_s, (8, ck))], axis=0)
                st_s[u] = wo[u:u + 1, 0:DH_C] * state + _dot_nt(vw.astype(BF16), k_t[hs, :])
        return carry

    lax.fori_loop(0, nc, body, 0)

    hn = hn_ref[...]
    for c in range(nc):
        rows = slice(c * ck, (c + 1) * ck)
        h_t = ht_s[0, c] + ht_s[1, c]
        parts = []
        for h in range(H_C):
            x = h_t[h * DH_C:(h + 1) * DH_C, :]
            parts.append(x * lax.rsqrt(jnp.mean(x * x, axis=0, keepdims=True) + EPS))
        h_n = jnp.concatenate(parts, axis=0).T
        yc_ref[rows, :] = _sigmoid(zc_ref[rows, 3 * D_C:4 * D_C]) * (h_n * hn)

    if not has_state:
        stout_ref[...] = st_s[...]
        mout_ref[...] = jnp.broadcast_to(m_prev, (N_UNITS, LANES))


def _mlstm(zc, zg, gb, hn, batch, seq, state=None):
    has_state = state is not None
    nc = seq // MLSTM_CHUNK
    full = lambda shape: pl.BlockSpec(shape, lambda b: (0,) * len(shape))
    st_spec = pl.BlockSpec((None, N_UNITS, ST_ROWS, DH_C), lambda b: (b, 0, 0, 0))
    m_spec = pl.BlockSpec((None, N_UNITS, LANES), lambda b: (b, 0, 0))
    in_specs = [pl.BlockSpec((seq, 4 * D_C), lambda b: (b, 0)), pl.BlockSpec((seq, ZG_W), lambda b: (b, 0)),
                full((1, ZG_W)), full((1, D_C))]
    args = [zc, zg, gb, hn]
    out_specs = [pl.BlockSpec((seq, MIX_W), lambda b: (b, 0))]
    out_shape = [jax.ShapeDtypeStruct((batch * seq, MIX_W), F32)]
    if has_state:
        in_specs += [st_spec, m_spec]
        args += list(state)
    else:
        out_specs += [st_spec, m_spec]
        out_shape += [jax.ShapeDtypeStruct((batch, N_UNITS, ST_ROWS, DH_C), F32),
                      jax.ShapeDtypeStruct((batch, N_UNITS, LANES), F32)]
    step = lambda: pltpu.VMEM((nc, N_UNITS, MLSTM_CHUNK), F32)
    return pl.pallas_call(
        functools.partial(_mlstm_kernel, seq=seq, has_state=has_state),
        grid=(batch,),
        in_specs=in_specs,
        out_specs=out_specs,
        out_shape=out_shape,
        scratch_shapes=[
            pltpu.VMEM((seq, D_C), BF16),
            pltpu.VMEM((nc, D_C, MLSTM_CHUNK), BF16), pltpu.VMEM((nc, D_C, MLSTM_CHUNK), BF16),
            pltpu.VMEM((nc, D_C, MLSTM_CHUNK), BF16),
            pltpu.VMEM((seq, ZG_W), F32),
            step(), step(), step(), step(), step(),
            pltpu.VMEM((N_UNITS, ST_ROWS, DH_C), F32),
            pltpu.VMEM((2, nc, D_C, MLSTM_CHUNK), F32),
        ],
        compiler_params=pltpu.CompilerParams(vmem_limit_bytes=VMEM_LIMIT),
        name="mlstm",
    )(*args)


def _merge_kernel(x_ref, mod_ref, g_ref, ya_ref, yb_ref, yc_ref, yd_ref, wm_ref, bm_ref, wb_ref, wo_ref, o_ref):
    x = x_ref[...]
    h = (_rms(x, g_ref[0:1, :]) * (1.0 + mod_ref[1:2, :]) + mod_ref[0:1, :]).astype(BF16)
    acc = jnp.zeros(x.shape, F32)
    for n, y_ref in enumerate((ya_ref, yb_ref, yc_ref, yd_ref)):
        cols = slice(n * D_MODEL, (n + 1) * D_MODEL)
        gate = _sigmoid(_dot(h, wm_ref[:, cols]) + bm_ref[:, cols])
        acc = acc + gate * _dot(y_ref[...].astype(BF16), wb_ref[n])
    y = _dot(acc.astype(BF16), wo_ref[...])
    o_ref[...] = x + mod_ref[2:3, :] * _rms(y, g_ref[1:2, :])


def _merge(x, mod, g, ys, wm, bm, wb, wo, seq):
    t = x.shape[0]
    tm = TOKEN_TILE
    full = _resident
    tok = lambda w_: pl.BlockSpec((tm, w_), lambda i: (i, 0))
    return pl.pallas_call(
        _merge_kernel,
        grid=(t // tm,),
        in_specs=[tok(D_MODEL), pl.BlockSpec((None, 6, D_MODEL), functools.partial(_mod_index, tm=tm, seq=seq, rows=mod.shape[0])),
                  full((4, D_MODEL)), tok(MIX_W), tok(MIX_W), tok(MIX_W), tok(MIX_W),
                  full((D_MODEL, N_BRANCH * D_MODEL)), full((1, N_BRANCH * D_MODEL)),
                  full((N_BRANCH, MIX_W, D_MODEL)), full((D_MODEL, D_MODEL))],
        out_specs=tok(D_MODEL),
        out_shape=jax.ShapeDtypeStruct((t, D_MODEL), F32),
        compiler_params=pltpu.CompilerParams(vmem_limit_bytes=VMEM_LIMIT),
        name="merge",
    )(x, mod, g, *ys, wm, bm, wb, wo)


def _ffn_kernel(x_ref, mod_ref, g_ref, w1_ref, w2_ref, o_ref):
    x = x_ref[...]
    h = (_rms(x, g_ref[2:3, :]) * (1.0 + mod_ref[4:5, :]) + mod_ref[3:4, :]).astype(BF16)
    f = jnp.zeros(x.shape, F32)
    for j in range(D_FF // D_MODEL):
        cols = slice(j * D_MODEL, (j + 1) * D_MODEL)
        a = jnp.maximum(_dot(h, w1_ref[:, cols]), 0.0)
        f = f + _dot((a * a).astype(BF16), w2_ref[cols, :])
    o_ref[...] = x + mod_ref[5:6, :] * _rms(f, g_ref[3:4, :])


def _ffn(x, mod, g, w1, w2, seq):
    t = x.shape[0]
    tm = TOKEN_TILE
    full = _resident
    tok = lambda w_: pl.BlockSpec((tm, w_), lambda i: (i, 0))
    return pl.pallas_call(
        _ffn_kernel,
        grid=(t // tm,),
        in_specs=[tok(D_MODEL), pl.BlockSpec((None, 6, D_MODEL), functools.partial(_mod_index, tm=tm, seq=seq, rows=mod.shape[0])),
                  full((4, D_MODEL)), full((D_MODEL, D_FF)), full((D_FF, D_MODEL))],
        out_specs=tok(D_MODEL),
        out_shape=jax.ShapeDtypeStruct((t, D_MODEL), F32),
        compiler_params=pltpu.CompilerParams(vmem_limit_bytes=VMEM_LIMIT),
        name="ffn",
    )(x, mod, g, w1, w2)


def _arrange_w_in(w):
    z = lambda n: jnp.zeros((D_MODEL, n), w.dtype)
    o_b, o_c = IN_A, IN_A + IN_B
    o_g, o_d = o_c + 4 * D_C, o_c + IN_C
    gates = w[:, o_g:o_d].reshape(D_MODEL, 2, 2, H_C).transpose(0, 2, 1, 3).reshape(D_MODEL, 4 * H_C)
    return jnp.concatenate([
        w[:, :Q_LORA + KV_LORA], z(64), w[:, Q_LORA + KV_LORA:IN_A], z(32),
        w[:, o_b:o_g], gates, z(ZG_W - 4 * H_C), w[:, o_d:]], axis=1).astype(BF16)


def _arrange_w_uq(w):
    w = w.reshape(Q_LORA, H_A, NOPE_A + ROPE_A)
    return jnp.pad(w, ((0, 0), (0, 0), (0, LANES - NOPE_A - ROPE_A))).reshape(Q_LORA, H_A * LANES).astype(BF16)


def _arrange_w_ukv(w):
    w = w.reshape(KV_LORA, H_A, NOPE_A + VH_A)
    wk = jnp.pad(w[:, :, :NOPE_A], ((0, 0), (0, 0), (0, LANES - NOPE_A)))
    v = w[:, :, NOPE_A:]
    zero = jnp.zeros_like(v)
    even = jnp.concatenate([v, zero], axis=-1)
    odd = jnp.concatenate([zero, v], axis=-1)
    wv = jnp.where((jnp.arange(H_A) % 2 == 0)[None, :, None], even, odd)
    return jnp.concatenate([wk.reshape(KV_LORA, -1), wv.reshape(KV_LORA, -1)], axis=1).astype(BF16)


def _rope_tables(rows):
    row = jnp.repeat(jnp.arange(rows), GRID_W).astype(F32)
    col = jnp.tile(jnp.arange(GRID_W), rows).astype(F32)
    nf = ROPE_A // 4
    inv = jnp.exp(-math.log(ROPE_BASE) * jnp.arange(nf, dtype=F32) / nf)
    ang = jnp.concatenate([row[:, None] * inv, col[:, None] * inv], axis=-1)
    cos, sin = jnp.cos(ang), jnp.sin(ang)
    n = cos.shape[0]
    one, zero = jnp.ones((n, 1), F32), jnp.zeros((n, 1), F32)
    rep = lambda a, k: jnp.broadcast_to(a, (n, k))
    mla = (jnp.concatenate([rep(one, 64), cos, cos, rep(one, 32)], axis=1),
           jnp.concatenate([rep(zero, 64), -sin, rep(zero, 48)], axis=1),
           jnp.concatenate([rep(zero, 80), sin, rep(zero, 32)], axis=1))
    z16 = rep(zero, 16)
    dif = (jnp.tile(jnp.concatenate([cos, cos], axis=1), (1, 4)),
           jnp.tile(jnp.concatenate([-sin, z16], axis=1), (1, 4)),
           jnp.tile(jnp.concatenate([z16, sin], axis=1), (1, 4)))
    return mla, dif


def _pack_state(c, n):
    b = c.shape[0]
    ct = jnp.swapaxes(c, -1, -2)
    pad = jnp.zeros((b, 2, H_C, ST_ROWS - DH_C - 1, DH_C), c.dtype)
    return jnp.concatenate([ct, n[:, :, :, None, :], pad], axis=3).reshape(b, N_UNITS, ST_ROWS, DH_C)


def _unpack_state(st):
    b = st.shape[0]
    st = st.reshape(b, 2, H_C, ST_ROWS, DH_C)
    return jnp.swapaxes(st[:, :, :, :DH_C, :], -1, -2), st[:, :, :, DH_C, :]


def kernel(x_prompt, x_sample, cache_mla_ckv, cache_mla_krope, cache_diff_k, cache_diff_v, state_mlstm_C,
           state_mlstm_n, state_mlstm_m, c, c_ctx, w_mod, b_mod, norm_g, w_in, mla_q_norm, w_uq, mla_kv_norm,
           w_ukv, gmlp_v_norm, gmlp_w_s, gmlp_b_s, mlstm_gate_bias, mlstm_head_norm, diff_lambda, diff_sub_norm,
           w_branch, w_merge, b_merge, w_out, w_ff1, w_ff2):
    bp, lp, _ = x_prompt.shape
    bs, ls, _ = x_sample.shape
    past = cache_mla_ckv.shape[2]

    cond = jnp.concatenate([c_ctx[None, :], c, jnp.zeros((16 - 1 - bs, D_MODEL), F32)], axis=0)
    mod = _modulation(cond, w_mod, b_mod).reshape(DEPTH, 16, 6, D_MODEL)
    rope_mla, rope_dif = _rope_tables(ls // GRID_W)

    xp = x_prompt.reshape(bp * lp, D_MODEL)
    xs = x_sample.reshape(bs * ls, D_MODEL)
    ents = []
    for l in range(DEPTH):
        lam_init = 0.8 - 0.6 * math.exp(-0.3 * l)
        w_in_l = _arrange_w_in(w_in[l])
        wuq_l = _arrange_w_uq(w_uq[l])
        wukv_l = _arrange_w_ukv(w_ukv[l])
        qn_l = mla_q_norm[l][None, :]
        kvn_l = mla_kv_norm[l][None, :]
        vn_l = gmlp_v_norm[l][None, :]
        ws_l = gmlp_w_s[l].astype(BF16)
        bias_l = jnp.repeat(gmlp_b_s[l].T, D_B // G_B, axis=1)
        gb_l = jnp.pad(mlstm_gate_bias[l].transpose(1, 0, 2).reshape(1, 4 * H_C), ((0, 0), (0, ZG_W - 4 * H_C)))
        hn_l = mlstm_head_norm[l][None, :]
        sn_l = jnp.tile(diff_sub_norm[l], 2)[None, :]
        wm_l = w_merge[l].astype(BF16)
        bm_l = b_merge[l][None, :]
        wb_l = w_branch[l].astype(BF16)
        wo_l = w_out[l].astype(BF16)
        w1_l = w_ff1[l].astype(BF16)
        w2_l = w_ff2[l].astype(BF16)

        for is_sample in (False, True):
            x = xs if is_sample else xp
            batch, seq = (bs, ls) if is_sample else (bp, lp)
            mod_l = mod[l, 1:1 + bs] if is_sample else mod[l, 0:1]
            za, zb, zc, zg, zd = _in_proj(x, mod_l, norm_g[l, 0:1], w_in_l, seq)
            if is_sample:
                ctx_kr = jnp.pad(cache_mla_krope[:, l], ((0, 0), (0, 0), (64, LANES - 64 - ROPE_A)))
                (ya,) = _mla(za, qn_l, kvn_l, wuq_l, wukv_l, batch, seq, rope=rope_mla,
                             ctx=(cache_mla_ckv[:, l], ctx_kr))
                yd = _diff(zd, diff_lambda[l], sn_l, batch, seq, lam_init, rope=rope_dif,
                           ctx=(cache_diff_k[:, l].reshape(bs, past, D_D), cache_diff_v[:, l].reshape(bs, past, D_D)))
                m0 = jnp.broadcast_to(state_mlstm_m[:, l].reshape(bs, N_UNITS, 1), (bs, N_UNITS, LANES))
                (yc,) = _mlstm(zc, zg, gb_l, hn_l, batch, seq,
                               state=(_pack_state(state_mlstm_C[:, l], state_mlstm_n[:, l]), m0))
            else:
                ya, ckv = _mla(za, qn_l, kvn_l, wuq_l, wukv_l, batch, seq)
                yd = _diff(zd, diff_lambda[l], sn_l, batch, seq, lam_init)
                yc, st_new, m_new = _mlstm(zc, zg, gb_l, hn_l, batch, seq)
                c_new, n_new = _unpack_state(st_new)
                ents.append((
                    ckv.reshape(bp, lp, KV_LORA),
                    za[:, Q_LORA + KV_LORA + 64:Q_LORA + KV_LORA + 64 + ROPE_A].reshape(bp, lp, ROPE_A),
                    zd[:, D_D:2 * D_D].reshape(bp, lp, H_D, 2, DH_D),
                    zd[:, 2 * D_D:].reshape(bp, lp, H_D, 2 * DH_D),
                    c_new, n_new,
                    m_new[:, :, 0].reshape(bp, 2, H_C)))
            yb = _gmlp(zb, vn_l, ws_l, bias_l)
            x = _merge(x, mod_l, norm_g[l], (ya, yb, yc, yd), wm_l, bm_l, wb_l, wo_l, seq)
            x = _ffn(x, mod_l, norm_g[l], w1_l, w2_l, seq)
            if is_sample:
                xs = x
            else:
                xp = x

    stack = lambda j: jnp.stack([e[j] for e in ents], axis=1)
    return (xp.reshape(bp, lp, D_MODEL), xs.reshape(bs, ls, D_MODEL),
            stack(0), stack(1), stack(2), stack(3), stack(4), stack(5), stack(6))
```

```python
import functools
import math

import jax
import jax.numpy as jnp
from jax import lax
from jax.experimental import pallas as pl
from jax.experimental.pallas import tpu as pltpu

F32 = jnp.float32
BF16 = jnp.bfloat16

D_MODEL = 1024
DEPTH = 2
GRID_W = 64
N_BRANCH = 4
MIX_W = 256
H_A, NOPE_A, ROPE_A, VH_A = 4, 64, 32, 64
Q_LORA, KV_LORA = 256, 128
D_B, G_B, CHUNK_B = 256, 4, 128
H_C, DH_C = 4, 64
D_C = H_C * DH_C
H_D, DH_D = 4, 32
D_D = H_D * 2 * DH_D
D_FF = 4 * D_MODEL
IN_A = Q_LORA + KV_LORA + ROPE_A
IN_B = 2 * D_B
IN_C = 4 * D_C + 4 * H_C
IN_D = 3 * D_D
ROPE_BASE = 10000.0
EPS = 1e-6
MLA_SCALE = (NOPE_A + ROPE_A) ** -0.5
DIFF_SCALE = DH_D ** -0.5

LANES = 128
MLSTM_CHUNK = 128
VMEM_LIMIT = 56 * 1024 * 1024
NEG_INF = float("-inf")

ZA_W = 512
ZG_W = 128
Z_OFF_A = 0
Z_OFF_B = Z_OFF_A + ZA_W
Z_OFF_C = Z_OFF_B + IN_B
Z_OFF_G = Z_OFF_C + 4 * D_C
Z_OFF_D = Z_OFF_G + ZG_W
Z_W = Z_OFF_D + IN_D


def _rms(x, g):
    return x * lax.rsqrt(jnp.mean(x * x, axis=-1, keepdims=True) + EPS) * g


def _sigmoid(x):
    return 1.0 / (1.0 + jnp.exp(-x))


def _log_sigmoid(x):
    return jnp.minimum(x, 0.0) - jnp.log1p(jnp.exp(-jnp.abs(x)))


def _dot(a, b):
    return jnp.dot(a, b, preferred_element_type=F32)


def _dot_nt(a, b):
    return lax.dot_general(a, b, (((1,), (1,)), ((), ())), preferred_element_type=F32)


def _lane_group_mask(width, group, index):
    lane = lax.broadcasted_iota(jnp.int32, (1, width), 1)
    return (lane >= index * group) & (lane < (index + 1) * group)


def _mod_kernel(cond_ref, w_ref, b_ref, o_ref):
    c = cond_ref[...]
    s = c * _sigmoid(c)
    o_ref[...] = _dot(s.astype(BF16), w_ref[...].astype(BF16)) + b_ref[...]


def _modulation(cond, w_mod, b_mod):
    rows = cond.shape[0]
    nb = 1024
    return pl.pallas_call(
        _mod_kernel,
        grid=(DEPTH, 6 * D_MODEL // nb),
        in_specs=[
            pl.BlockSpec((rows, D_MODEL), lambda l, j: (0, 0)),
            pl.BlockSpec((None, D_MODEL, nb), lambda l, j: (l, 0, j)),
            pl.BlockSpec((None, 1, nb), lambda l, j: (l, 0, j)),
        ],
        out_specs=pl.BlockSpec((None, rows, nb), lambda l, j: (l, 0, j)),
        out_shape=jax.ShapeDtypeStruct((DEPTH, rows, 6 * D_MODEL), F32),
        name="modulation",
    )(cond, w_mod, b_mod.reshape(DEPTH, 1, 6 * D_MODEL))


TOKEN_TILE = 512


def _layer_block(shape, layer):
    shape = tuple(shape)
    return pl.BlockSpec((None,) + shape, lambda *_: (layer,) + (0,) * len(shape), pipeline_mode=pl.Buffered(1))


def _mod_spec(layer, tm, seq, per_seq):
    def index(i):
        return (layer, 1 + (i * tm) // seq if per_seq else 0, 0, 0)
    return pl.BlockSpec((None, None, 6, D_MODEL), index)


def _in_kernel(x_ref, mod_ref, g_ref, w_ref, za_ref, zb_ref, zc_ref, zg_ref, zd_ref):
    x = x_ref[...]
    h = _rms(x, g_ref[0:1, :]) * (1.0 + mod_ref[1:2, :]) + mod_ref[0:1, :]
    hb = h.astype(BF16)
    za_ref[...] = _dot(hb, w_ref[:, Z_OFF_A:Z_OFF_B])
    zb_ref[...] = _dot(hb, w_ref[:, Z_OFF_B:Z_OFF_C])
    zc_ref[...] = _dot(hb, w_ref[:, Z_OFF_C:Z_OFF_G])
    zg_ref[...] = _dot(hb, w_ref[:, Z_OFF_G:Z_OFF_D])
    zd_ref[...] = _dot(hb, w_ref[:, Z_OFF_D:Z_W])


def _in_proj(x, mod, g, w, layer, seq, per_seq):
    t = x.shape[0]
    tm = TOKEN_TILE
    widths = (ZA_W, IN_B, 4 * D_C, ZG_W, IN_D)
    return pl.pallas_call(
        _in_kernel,
        grid=(t // tm,),
        in_specs=[
            pl.BlockSpec((tm, D_MODEL), lambda i: (i, 0)),
            _mod_spec(layer, tm, seq, per_seq),
            _layer_block((4, D_MODEL), layer),
            _layer_block((D_MODEL, Z_W), layer),
        ],
        out_specs=[pl.BlockSpec((tm, w_), lambda i: (i, 0)) for w_ in widths],
        out_shape=[jax.ShapeDtypeStruct((t, w_), F32) for w_ in widths],
        compiler_params=pltpu.CompilerParams(vmem_limit_bytes=VMEM_LIMIT),
        name="in_proj",
    )(x, mod, g, w)


LOG2E = math.log2(math.e)


def _softmax_pv(q, k, v):
    return _dot(_softmax_unnorm(q, k).astype(BF16), v)


def _softmax_unnorm(q, k):
    s = _dot_nt(q, k)
    return jnp.exp2(s - jnp.max(s, axis=-1, keepdims=True))


def _with_ones(v, lo_mask, keep_lo):
    lane = lax.broadcasted_iota(jnp.int32, (1, LANES), 1)
    if keep_lo:
        return jnp.where(lane == 64, 1.0, jnp.where(lo_mask, v, 0.0))
    return jnp.where(lane == 0, 1.0, jnp.where(lo_mask, 0.0, v))


def _pair_normalise(o_even, o_odd, lo_mask):
    return jnp.where(lo_mask, o_even / o_even[:, 64:65], o_odd / o_odd[:, 0:1])


def _rope128(x, tc, ts1, ts2):
    return x * tc + pltpu.roll(x, LANES - 16, 1) * ts1 + pltpu.roll(x, 16, 1) * ts2


def _mla_kernel(*refs, seq, has_ctx, tq):
    if has_ctx:
        (za_ref, qn_ref, kvn_ref, wuq_ref, wukv_ref, tc_ref, ts1_ref, ts2_ref, cckv_ref, ckr_ref,
         ya_ref, q_s, k_s, v_s, ckv_s, kr_s) = refs
        past = cckv_ref.shape[0]
    else:
        (za_ref, qn_ref, kvn_ref, wuq_ref, wukv_ref, ya_ref, ckv_ref, q_s, k_s, v_s, ckv_s, kr_s) = refs
        past = 0
    za = za_ref[...]
    cq = _rms(za[:, :Q_LORA], qn_ref[...])
    qh = _dot(cq.astype(BF16), wuq_ref[...])
    ckv = _rms(za[:, Q_LORA:Q_LORA + KV_LORA], kvn_ref[...])
    kr = za[:, Q_LORA + KV_LORA:ZA_W]
    if has_ctx:
        tc, ts1, ts2 = tc_ref[...], ts1_ref[...], ts2_ref[...]
        kr = _rope128(kr, tc, ts1, ts2)
        ckv_s[0:past, :] = cckv_ref[...]
        kr_s[0:past, :] = ckr_ref[...]
    else:
        ckv_ref[...] = ckv
    ckv_s[past:past + seq, :] = ckv
    kr_s[past:past + seq, :] = kr
    kv = _dot(ckv_s[...].astype(BF16), wukv_ref[...])
    kr_all = kr_s[...]
    lo = _lane_group_mask(LANES, 64, 0)
    for h in range(H_A):
        qg = qh[:, h * LANES:(h + 1) * LANES]
        if has_ctx:
            qg = _rope128(qg, tc, ts1, ts2)
        q_s[h] = (qg * (MLA_SCALE * LOG2E)).astype(BF16)
        k_s[h] = (kv[:, h * LANES:(h + 1) * LANES] + kr_all).astype(BF16)
        v_s[h] = _with_ones(kv[:, (H_A + h) * LANES:(H_A + h + 1) * LANES], lo, h % 2 == 0).astype(BF16)

    def body(i, carry):
        rows = pl.ds(pl.multiple_of(i * tq, tq), tq)
        for pair in range(H_A // 2):
            o = [_softmax_pv(q_s[h, rows, :], k_s[h], v_s[h]) for h in (2 * pair, 2 * pair + 1)]
            ya_ref[rows, pair * LANES:(pair + 1) * LANES] = _pair_normalise(o[0], o[1], lo)
        return carry

    lax.fori_loop(0, seq // tq, body, 0)


def _ctx_block(past, width, layer):
    return pl.BlockSpec((None, None, past, width), lambda b: (b, layer, 0, 0))


def _mla(za, qn, kvn, wuq, wukv, layer, batch, seq, rope=None, ctx=None):
    has_ctx = ctx is not None
    past = ctx[0].shape[2] if has_ctx else 0
    lk = past + seq
    tq = min(seq, 256)
    full = lambda shape: pl.BlockSpec(shape, lambda b: (0,) * len(shape))
    in_specs = [
        pl.BlockSpec((seq, ZA_W), lambda b: (b, 0)),
        _layer_block((1, Q_LORA), layer), _layer_block((1, KV_LORA), layer),
        _layer_block((Q_LORA, H_A * LANES), layer), _layer_block((KV_LORA, 2 * H_A * LANES), layer),
    ]
    args = [za, qn, kvn, wuq, wukv]
    out_specs = [pl.BlockSpec((seq, MIX_W), lambda b: (b, 0))]
    out_shape = [jax.ShapeDtypeStruct((batch * seq, MIX_W), F32)]
    if has_ctx:
        in_specs += [full((seq, LANES))] * 3
        in_specs += [_ctx_block(past, LANES, layer)] * 2
        args += list(rope) + list(ctx)
    else:
        out_specs.append(pl.BlockSpec((seq, KV_LORA), lambda b: (b, 0)))
        out_shape.append(jax.ShapeDtypeStruct((batch * seq, KV_LORA), F32))
    return pl.pallas_call(
        functools.partial(_mla_kernel, seq=seq, has_ctx=has_ctx, tq=tq),
        grid=(batch,),
        in_specs=in_specs,
        out_specs=out_specs,
        out_shape=out_shape,
        scratch_shapes=[
            pltpu.VMEM((H_A, seq, LANES), BF16),
            pltpu.VMEM((H_A, lk, LANES), BF16),
            pltpu.VMEM((H_A, lk, LANES), BF16),
            pltpu.VMEM((lk, LANES), F32),
            pltpu.VMEM((lk, LANES), F32),
        ],
        compiler_params=pltpu.CompilerParams(vmem_limit_bytes=VMEM_LIMIT),
        name="mla",
    )(*args)


def _diff_kernel(*refs, seq, has_ctx, tq, lam_init):
    if has_ctx:
        (zd_ref, lam_ref, sn_ref, tc_ref, ts1_ref, ts2_ref, ck_ref, cv_ref,
         yd_ref, q_s, k_s, v_s) = refs
        past = ck_ref.shape[0]
    else:
        (zd_ref, lam_ref, sn_ref, yd_ref, q_s, k_s, v_s) = refs
        past = 0
    lo = _lane_group_mask(LANES, 64, 0)
    if has_ctx:
        tc, ts1, ts2 = tc_ref[...], ts1_ref[...], ts2_ref[...]
        k_s[0:past, :] = ck_ref[...].astype(BF16)
        cv = cv_ref[...]
    for half in range(2):
        cols = slice(half * LANES, (half + 1) * LANES)
        q = zd_ref[:, half * LANES:(half + 1) * LANES]
        k = zd_ref[:, D_D + half * LANES:D_D + (half + 1) * LANES]
        v = zd_ref[:, 2 * D_D + half * LANES:2 * D_D + (half + 1) * LANES]
        if has_ctx:
            q = _rope128(q, tc, ts1, ts2)
            k = _rope128(k, tc, ts1, ts2)
            v_s[2 * half, 0:past, :] = jnp.where(lo, cv[:, cols], 0.0).astype(BF16)
            v_s[2 * half + 1, 0:past, :] = jnp.where(lo, 0.0, cv[:, cols]).astype(BF16)
        q_s[:, cols] = (q * (DIFF_SCALE * LOG2E)).astype(BF16)
        k_s[past:past + seq, cols] = k.astype(BF16)
        v_s[2 * half, past:past + seq, :] = jnp.where(lo, v, 0.0).astype(BF16)
        v_s[2 * half + 1, past:past + seq, :] = jnp.where(lo, 0.0, v).astype(BF16)

    lam = lam_ref[...]
    lam_val = (jnp.exp(jnp.sum(lam[0:1] * lam[1:2], axis=-1, keepdims=True))
               - jnp.exp(jnp.sum(lam[2:3] * lam[3:4], axis=-1, keepdims=True)) + lam_init)
    comp_masks = [jnp.where(_lane_group_mask(D_D, DH_D, j), 1.0, 0.0).astype(BF16) for j in range(2 * H_D)]
    sn = sn_ref[...]

    def body(i, carry):
        rows = pl.ds(pl.multiple_of(i * tq, tq), tq)
        qb = q_s[rows, :]
        k_all = k_s[...]
        for pair in range(H_D // 2):
            acc = jnp.zeros((tq, LANES), F32)
            for h in (2 * pair, 2 * pair + 1):
                p1, p2 = [_softmax_unnorm(qb * comp_masks[2 * h + comp], k_all) for comp in range(2)]
                r1 = 1.0 / jnp.sum(p1, axis=-1, keepdims=True)
                r2 = lam_val / jnp.sum(p2, axis=-1, keepdims=True)
                acc = acc + _dot((p1 * r1 - p2 * r2).astype(BF16), v_s[h])
            sq = acc * acc
            ss_lo = jnp.sum(jnp.where(lo, sq, 0.0), axis=-1, keepdims=True)
            ss_hi = jnp.sum(jnp.where(lo, 0.0, sq), axis=-1, keepdims=True)
            r = jnp.where(lo, lax.rsqrt(ss_lo / (2 * DH_D) + EPS), lax.rsqrt(ss_hi / (2 * DH_D) + EPS))
            yd_ref[rows, pair * LANES:(pair + 1) * LANES] = acc * r * sn * (1.0 - lam_init)
        return carry

    lax.fori_loop(0, seq // tq, body, 0)


def _diff(zd, lam, sn, layer, batch, seq, lam_init, rope=None, ctx=None):
    has_ctx = ctx is not None
    past = ctx[0].shape[2] if has_ctx else 0
    lk = past + seq
    tq = min(seq, 256)
    full = lambda shape: pl.BlockSpec(shape, lambda b: (0,) * len(shape))
    in_specs = [pl.BlockSpec((seq, IN_D), lambda b: (b, 0)),
                _layer_block((4, DH_D), layer), _layer_block((1, LANES), layer)]
    args = [zd, lam, sn]
    if has_ctx:
        in_specs += [full((seq, LANES))] * 3
        in_specs += [_ctx_block(past, D_D, layer)] * 2
        args += list(rope) + list(ctx)
    return pl.pallas_call(
        functools.partial(_diff_kernel, seq=seq, has_ctx=has_ctx, tq=tq, lam_init=lam_init),
        grid=(batch,),
        in_specs=in_specs,
        out_specs=pl.BlockSpec((seq, MIX_W), lambda b: (b, 0)),
        out_shape=jax.ShapeDtypeStruct((batch * seq, MIX_W), F32),
        scratch_shapes=[
            pltpu.VMEM((seq, D_D), BF16),
            pltpu.VMEM((lk, D_D), BF16),
            pltpu.VMEM((H_D, lk, LANES), BF16),
        ],
        compiler_params=pltpu.CompilerParams(vmem_limit_bytes=VMEM_LIMIT),
        name="diff_attn",
    )(*args)


def _gmlp_kernel(zb_ref, vn_ref, ws_ref, bias_ref, yb_ref):
    tm = zb_ref.shape[0]
    v = _rms(zb_ref[:, D_B:], vn_ref[...]).astype(BF16)
    bias = bias_ref[...]
    gmasks = [_lane_group_mask(D_B, D_B // G_B, g) for g in range(G_B)]
    for ch in range(tm // CHUNK_B):
        rows = slice(ch * CHUNK_B, (ch + 1) * CHUNK_B)
        vc = v[rows, :]
        mixed = bias
        for g in range(G_B):
            mixed = mixed + jnp.where(gmasks[g], _dot(ws_ref[g], vc), 0.0)
        yb_ref[rows, :] = zb_ref[rows, :D_B] * mixed


def _gmlp(zb, vn, ws, bias, layer):
    t = zb.shape[0]
    tm = TOKEN_TILE
    return pl.pallas_call(
        _gmlp_kernel,
        grid=(t // tm,),
        in_specs=[pl.BlockSpec((tm, IN_B), lambda i: (i, 0)), _layer_block((1, D_B), layer),
                  _layer_block((G_B, CHUNK_B, CHUNK_B), layer), _layer_block((CHUNK_B, D_B), layer)],
        out_specs=pl.BlockSpec((tm, MIX_W), lambda i: (i, 0)),
        out_shape=jax.ShapeDtypeStruct((t, MIX_W), F32),
        name="gmlp",
    )(zb, vn, ws, bias)


def _split3(x):
    hi = x.astype(BF16)
    r1 = x - hi.astype(F32)
    mid = r1.astype(BF16)
    lo = (r1 - mid.astype(F32)).astype(BF16)
    return hi, mid, lo


ST_ROWS = DH_C + 8
N_UNITS = 2 * H_C


def _mlstm_kernel(*refs, seq, has_state):
    ck = MLSTM_CHUNK
    nc = seq // ck
    if has_state:
        (zc_ref, zg_ref, gb_ref, hn_ref, st0_ref, m0_ref,
         yc_ref, k_s, qt_s, kt_s, vt_s, rcc_s, ca_s, wi_s, en_s, ws_s, wo_s, st_s, ht_s) = refs
    else:
        (zc_ref, zg_ref, gb_ref, hn_ref,
         yc_ref, stout_ref, mout_ref, k_s, qt_s, kt_s, vt_s, rcc_s, ca_s, wi_s, en_s, ws_s, wo_s, st_s, ht_s) = refs

    for c in range(nc):
        rows = slice(c * ck, (c + 1) * ck)
        qt_s[c] = zc_ref[rows, 0:D_C].T.astype(BF16)
        kk = zc_ref[rows, D_C:2 * D_C] * (DH_C ** -0.5)
        k_s[rows, :] = kk.astype(BF16)
        kt_s[c] = kk.T.astype(BF16)
        vt_s[c] = zc_ref[rows, 2 * D_C:3 * D_C].T.astype(BF16)

    gates = zg_ref[...] + gb_ref[...]
    lf = _log_sigmoid(gates)
    r_i = lax.broadcasted_iota(jnp.int32, (ck, ck), 0)
    c_i = lax.broadcasted_iota(jnp.int32, (ck, ck), 1)
    tri_fw = jnp.where(c_i <= r_i, 1.0, 0.0).astype(BF16)
    tri_bw = jnp.where(c_i >= r_i, 1.0, 0.0).astype(BF16)
    fw_lane = (lax.broadcasted_iota(jnp.int32, (1, ZG_W), 1) % N_UNITS) < H_C
    i_rows, b_rows = [], []
    for c in range(nc):
        rows = slice(c * ck, (c + 1) * ck)
        p0, p1, p2 = _split3(lf[rows, :])
        b_fw = _dot(tri_fw, p0) + _dot(tri_fw, p1) + _dot(tri_fw, p2)
        b_bw = _dot(tri_bw, p0) + _dot(tri_bw, p1) + _dot(tri_bw, p2)
        b = jnp.where(fw_lane, b_fw, b_bw)
        g_c = gates[rows, :]
        rcc_s[rows, :] = g_c - pltpu.roll(b, ZG_W - N_UNITS, 1)
        i_rows.append(g_c.T[0:N_UNITS, :])
        b_rows.append(b.T[N_UNITS:2 * N_UNITS, :])

    row8 = lax.broadcasted_iota(jnp.int32, (N_UNITS, ck), 0)
    lane8 = lax.broadcasted_iota(jnp.int32, (N_UNITS, ck), 1)
    fw8 = row8 < H_C
    fw81 = fw8[:, 0:1]
    m_prev = m0_ref[:, 0:1] if has_state else jnp.zeros((N_UNITS, 1), F32)
    for i in range(nc):
        i8 = jnp.where(fw8, i_rows[i], i_rows[nc - 1 - i])
        b8 = jnp.where(fw8, b_rows[i], b_rows[nc - 1 - i])
        r8 = i8 - b8
        pm, sm = r8, r8
        sh = 1
        while sh < ck:
            pm = jnp.maximum(pm, jnp.where(lane8 >= sh, pltpu.roll(pm, sh, 1), NEG_INF))
            sm = jnp.maximum(sm, jnp.where(lane8 < ck - sh, pltpu.roll(sm, ck - sh, 1), NEG_INF))
            sh *= 2
        cm = jnp.where(fw8, pm, sm)
        b_t = jnp.where(fw81, b8[:, ck - 1:ck], b8[:, 0:1])
        cm_end = jnp.where(fw81, pm[:, ck - 1:ck], sm[:, 0:1])
        inter = b8 + m_prev
        mt = jnp.maximum(inter, b8 + cm)
        ca_s[i] = b8 - mt
        wi_s[i] = jnp.exp(inter - mt)
        en_s[i] = jnp.exp(-mt)
        m_new = b_t + jnp.maximum(m_prev, cm_end)
        ws_s[i] = jnp.exp(b_t + r8 - m_new)
        wo_s[i] = jnp.broadcast_to(jnp.exp(b_t + m_prev - m_new), (N_UNITS, ck))
        m_prev = m_new

    if has_state:
        st_s[...] = st0_ref[...]
    else:
        st_s[...] = jnp.zeros_like(st_s)

    hmask_bf = [jnp.where(_lane_group_mask(D_C, DH_C, h), 1.0, 0.0).astype(BF16) for h in range(H_C)]
    src_ok = (r_i <= c_i, r_i >= c_i)

    def body(i, carry):
        ca, wi, en, ws, wo = ca_s[i], wi_s[i], en_s[i], ws_s[i], wo_s[i]
        for d in range(2):
            c = i if d == 0 else nc - 1 - i
            rows = pl.ds(pl.multiple_of(c * ck, ck), ck)
            kc = k_s[rows, :]
            q_t, k_t, v_t = qt_s[c], kt_s[c], vt_s[c]
            rc = rcc_s[rows, :]
            for h in range(H_C):
                u = d * H_C + h
                hs = slice(h * DH_C, (h + 1) * DH_C)
                w_i, w_s = wi[u:u + 1, :], ws[u:u + 1, :]
                st = _dot(kc * hmask_bf[h], q_t)
                pt = st * jnp.where(src_ok[d], jnp.exp(rc[:, u:u + 1] + ca[u:u + 1, :]), 0.0)
                state = st_s[u]
                qct = _dot(state.astype(BF16), q_t[hs, :])
                den = w_i * qct[DH_C:DH_C + 1, :] + jnp.sum(pt, axis=0, keepdims=True)
                num = w_i * qct[0:DH_C, :] + _dot(v_t[hs, :], pt.astype(BF16))
                ht_s[d, c, hs, :] = num / jnp.maximum(jnp.abs(den), en[u:u + 1, :])
                vw = jnp.concatenate([v_t[hs, :].astype(F32) * w_s, jnp.broadcast_to(w_s, (8, ck))], axis=0)
                st_s[u] = wo[u:u + 1, 0:DH_C] * state + _dot_nt(vw.astype(BF16), k_t[hs, :])
        return carry

    lax.fori_loop(0, nc, body, 0)

    hn = hn_ref[...]
    for c in range(nc):
        rows = slice(c * ck, (c + 1) * ck)
        h_t = ht_s[0, c] + ht_s[1, c]
        parts = []
        for h in range(H_C):
            x = h_t[h * DH_C:(h + 1) * DH_C, :]
            parts.append(x * lax.rsqrt(jnp.mean(x * x, axis=0, keepdims=True) + EPS))
        h_n = jnp.concatenate(parts, axis=0).T
        yc_ref[rows, :] = _sigmoid(zc_ref[rows, 3 * D_C:4 * D_C]) * (h_n * hn)

    if not has_state:
        stout_ref[...] = st_s[...]
        mout_ref[...] = jnp.broadcast_to(m_prev, (N_UNITS, LANES))


def _mlstm(zc, zg, gb, hn, layer, batch, seq, state=None):
    has_state = state is not None
    nc = seq // MLSTM_CHUNK
    st_spec = pl.BlockSpec((None, N_UNITS, ST_ROWS, DH_C), lambda b: (b, 0, 0, 0))
    m_spec = pl.BlockSpec((None, N_UNITS, LANES), lambda b: (b, 0, 0))
    in_specs = [pl.BlockSpec((seq, 4 * D_C), lambda b: (b, 0)), pl.BlockSpec((seq, ZG_W), lambda b: (b, 0)),
                _layer_block((1, ZG_W), layer), _layer_block((1, D_C), layer)]
    args = [zc, zg, gb, hn]
    out_specs = [pl.BlockSpec((seq, MIX_W), lambda b: (b, 0))]
    out_shape = [jax.ShapeDtypeStruct((batch * seq, MIX_W), F32)]
    if has_state:
        in_specs += [pl.BlockSpec((None, None, N_UNITS, ST_ROWS, DH_C), lambda b: (b, layer, 0, 0, 0)),
                     pl.BlockSpec((None, None, N_UNITS, LANES), lambda b: (b, layer, 0, 0))]
        args += list(state)
    else:
        out_specs += [st_spec, m_spec]
        out_shape += [jax.ShapeDtypeStruct((batch, N_UNITS, ST_ROWS, DH_C), F32),
                      jax.ShapeDtypeStruct((batch, N_UNITS, LANES), F32)]
    step = lambda: pltpu.VMEM((nc, N_UNITS, MLSTM_CHUNK), F32)
    return pl.pallas_call(
        functools.partial(_mlstm_kernel, seq=seq, has_state=has_state),
        grid=(batch,),
        in_specs=in_specs,
        out_specs=out_specs,
        out_shape=out_shape,
        scratch_shapes=[
            pltpu.VMEM((seq, D_C), BF16),
            pltpu.VMEM((nc, D_C, MLSTM_CHUNK), BF16), pltpu.VMEM((nc, D_C, MLSTM_CHUNK), BF16),
            pltpu.VMEM((nc, D_C, MLSTM_CHUNK), BF16),
            pltpu.VMEM((seq, ZG_W), F32),
            step(), step(), step(), step(), step(),
            pltpu.VMEM((N_UNITS, ST_ROWS, DH_C), F32),
            pltpu.VMEM((2, nc, D_C, MLSTM_CHUNK), F32),
        ],
        compiler_params=pltpu.CompilerParams(vmem_limit_bytes=VMEM_LIMIT),
        name="mlstm",
    )(*args)


def _merge_kernel(x_ref, mod_ref, g_ref, ya_ref, yb_ref, yc_ref, yd_ref, wm_ref, bm_ref, wb_ref, wo_ref, o_ref):
    x = x_ref[...]
    h = (_rms(x, g_ref[0:1, :]) * (1.0 + mod_ref[1:2, :]) + mod_ref[0:1, :]).astype(BF16)
    acc = jnp.zeros(x.shape, F32)
    for n, y_ref in enumerate((ya_ref, yb_ref, yc_ref, yd_ref)):
        cols = slice(n * D_MODEL, (n + 1) * D_MODEL)
        gate = _sigmoid(_dot(h, wm_ref[:, cols]) + bm_ref[:, cols])
        acc = acc + gate * _dot(y_ref[...].astype(BF16), wb_ref[n])
    y = _dot(acc.astype(BF16), wo_ref[...])
    o_ref[...] = x + mod_ref[2:3, :] * _rms(y, g_ref[1:2, :])


def _merge(x, mod, g, ys, wm, bm, wb, wo, layer, seq, per_seq):
    t = x.shape[0]
    tm = TOKEN_TILE
    full = lambda shape: _layer_block(shape, layer)
    tok = lambda w_: pl.BlockSpec((tm, w_), lambda i: (i, 0))
    return pl.pallas_call(
        _merge_kernel,
        grid=(t // tm,),
        in_specs=[tok(D_MODEL), _mod_spec(layer, tm, seq, per_seq),
                  full((4, D_MODEL)), tok(MIX_W), tok(MIX_W), tok(MIX_W), tok(MIX_W),
                  full((D_MODEL, N_BRANCH * D_MODEL)), full((1, N_BRANCH * D_MODEL)),
                  full((N_BRANCH, MIX_W, D_MODEL)), full((D_MODEL, D_MODEL))],
        out_specs=tok(D_MODEL),
        out_shape=jax.ShapeDtypeStruct((t, D_MODEL), F32),
        compiler_params=pltpu.CompilerParams(vmem_limit_bytes=VMEM_LIMIT),
        name="merge",
    )(x, mod, g, *ys, wm, bm, wb, wo)


def _ffn_kernel(x_ref, mod_ref, g_ref, w1_ref, w2_ref, o_ref):
    x = x_ref[...]
    h = (_rms(x, g_ref[2:3, :]) * (1.0 + mod_ref[4:5, :]) + mod_ref[3:4, :]).astype(BF16)
    f = jnp.zeros(x.shape, F32)
    for j in range(D_FF // D_MODEL):
        cols = slice(j * D_MODEL, (j + 1) * D_MODEL)
        a = jnp.maximum(_dot(h, w1_ref[:, cols]), 0.0)
        f = f + _dot((a * a).astype(BF16), w2_ref[cols, :])
    o_ref[...] = x + mod_ref[5:6, :] * _rms(f, g_ref[3:4, :])


def _ffn(x, mod, g, w1, w2, layer, seq, per_seq):
    t = x.shape[0]
    tm = TOKEN_TILE
    full = lambda shape: _layer_block(shape, layer)
    tok = lambda w_: pl.BlockSpec((tm, w_), lambda i: (i, 0))
    return pl.pallas_call(
        _ffn_kernel,
        grid=(t // tm,),
        in_specs=[tok(D_MODEL), _mod_spec(layer, tm, seq, per_seq),
                  full((4, D_MODEL)), full((D_MODEL, D_FF)), full((D_FF, D_MODEL))],
        out_specs=tok(D_MODEL),
        out_shape=jax.ShapeDtypeStruct((t, D_MODEL), F32),
        compiler_params=pltpu.CompilerParams(vmem_limit_bytes=VMEM_LIMIT),
        name="ffn",
    )(x, mod, g, w1, w2)


def _arrange_w_in(w):
    z = lambda n: jnp.zeros((D_MODEL, n), w.dtype)
    o_b, o_c = IN_A, IN_A + IN_B
    o_g, o_d = o_c + 4 * D_C, o_c + IN_C
    gates = w[:, o_g:o_d].reshape(D_MODEL, 2, 2, H_C).transpose(0, 2, 1, 3).reshape(D_MODEL, 4 * H_C)
    return jnp.concatenate([
        w[:, :Q_LORA + KV_LORA], z(64), w[:, Q_LORA + KV_LORA:IN_A], z(32),
        w[:, o_b:o_g], gates, z(ZG_W - 4 * H_C), w[:, o_d:]], axis=1).astype(BF16)


def _arrange_w_uq(w):
    w = w.reshape(Q_LORA, H_A, NOPE_A + ROPE_A)
    return jnp.pad(w, ((0, 0), (0, 0), (0, LANES - NOPE_A - ROPE_A))).reshape(Q_LORA, H_A * LANES).astype(BF16)


def _arrange_w_ukv(w):
    w = w.reshape(KV_LORA, H_A, NOPE_A + VH_A)
    wk = jnp.pad(w[:, :, :NOPE_A], ((0, 0), (0, 0), (0, LANES - NOPE_A)))
    v = w[:, :, NOPE_A:]
    zero = jnp.zeros_like(v)
    even = jnp.concatenate([v, zero], axis=-1)
    odd = jnp.concatenate([zero, v], axis=-1)
    wv = jnp.where((jnp.arange(H_A) % 2 == 0)[None, :, None], even, odd)
    return jnp.concatenate([wk.reshape(KV_LORA, -1), wv.reshape(KV_LORA, -1)], axis=1).astype(BF16)


def _rope_tables(rows):
    row = jnp.repeat(jnp.arange(rows), GRID_W).astype(F32)
    col = jnp.tile(jnp.arange(GRID_W), rows).astype(F32)
    nf = ROPE_A // 4
    inv = jnp.exp(-math.log(ROPE_BASE) * jnp.arange(nf, dtype=F32) / nf)
    ang = jnp.concatenate([row[:, None] * inv, col[:, None] * inv], axis=-1)
    cos, sin = jnp.cos(ang), jnp.sin(ang)
    n = cos.shape[0]
    one, zero = jnp.ones((n, 1), F32), jnp.zeros((n, 1), F32)
    rep = lambda a, k: jnp.broadcast_to(a, (n, k))
    mla = (jnp.concatenate([rep(one, 64), cos, cos, rep(one, 32)], axis=1),
           jnp.concatenate([rep(zero, 64), -sin, rep(zero, 48)], axis=1),
           jnp.concatenate([rep(zero, 80), sin, rep(zero, 32)], axis=1))
    z16 = rep(zero, 16)
    dif = (jnp.tile(jnp.concatenate([cos, cos], axis=1), (1, 4)),
           jnp.tile(jnp.concatenate([-sin, z16], axis=1), (1, 4)),
           jnp.tile(jnp.concatenate([z16, sin], axis=1), (1, 4)))
    return mla, dif


def _pack_state(c, n):
    b = c.shape[0]
    ct = jnp.swapaxes(c, -1, -2)
    pad = jnp.zeros((b, 2, H_C, ST_ROWS - DH_C - 1, DH_C), c.dtype)
    return jnp.concatenate([ct, n[:, :, :, None, :], pad], axis=3).reshape(b, N_UNITS, ST_ROWS, DH_C)


def _unpack_state(st):
    b = st.shape[0]
    st = st.reshape(b, 2, H_C, ST_ROWS, DH_C)
    return jnp.swapaxes(st[:, :, :, :DH_C, :], -1, -2), st[:, :, :, DH_C, :]


def kernel(x_prompt, x_sample, cache_mla_ckv, cache_mla_krope, cache_diff_k, cache_diff_v, state_mlstm_C,
           state_mlstm_n, state_mlstm_m, c, c_ctx, w_mod, b_mod, norm_g, w_in, mla_q_norm, w_uq, mla_kv_norm,
           w_ukv, gmlp_v_norm, gmlp_w_s, gmlp_b_s, mlstm_gate_bias, mlstm_head_norm, diff_lambda, diff_sub_norm,
           w_branch, w_merge, b_merge, w_out, w_ff1, w_ff2):
    bp, lp, _ = x_prompt.shape
    bs, ls, _ = x_sample.shape
    past = cache_mla_ckv.shape[2]

    cond = jnp.concatenate([c_ctx[None, :], c, jnp.zeros((16 - 1 - bs, D_MODEL), F32)], axis=0)
    mod = _modulation(cond, w_mod, b_mod).reshape(DEPTH, 16, 6, D_MODEL)
    rope_mla, rope_dif = _rope_tables(ls // GRID_W)

    row = lambda a: a[:, None, :]
    w_in_a = jax.vmap(_arrange_w_in)(w_in)
    wuq_a = jax.vmap(_arrange_w_uq)(w_uq)
    wukv_a = jax.vmap(_arrange_w_ukv)(w_ukv)
    qn, kvn, vn, hn, bm = row(mla_q_norm), row(mla_kv_norm), row(gmlp_v_norm), row(mlstm_head_norm), row(b_merge)
    ws = gmlp_w_s.astype(BF16)
    bias = jnp.repeat(jnp.swapaxes(gmlp_b_s, 1, 2), D_B // G_B, axis=2)
    gb = jnp.pad(mlstm_gate_bias.transpose(0, 2, 1, 3).reshape(DEPTH, 1, 4 * H_C),
                 ((0, 0), (0, 0), (0, ZG_W - 4 * H_C)))
    sn = row(jnp.tile(diff_sub_norm, (1, 2)))
    wm, wb, wo = w_merge.astype(BF16), w_branch.astype(BF16), w_out.astype(BF16)
    w1, w2 = w_ff1.astype(BF16), w_ff2.astype(BF16)

    ctx_kr = jnp.pad(cache_mla_krope, ((0, 0), (0, 0), (0, 0), (64, LANES - 64 - ROPE_A)))
    ctx_dk = cache_diff_k.reshape(bs, DEPTH, past, D_D)
    ctx_dv = cache_diff_v.reshape(bs, DEPTH, past, D_D)
    st0 = jax.vmap(_pack_state, in_axes=1, out_axes=1)(state_mlstm_C, state_mlstm_n)
    m0 = jnp.broadcast_to(state_mlstm_m.reshape(bs, DEPTH, N_UNITS, 1), (bs, DEPTH, N_UNITS, LANES))

    xp = x_prompt.reshape(bp * lp, D_MODEL)
    xs = x_sample.reshape(bs * ls, D_MODEL)
    ents = []
    for l in range(DEPTH):
        lam_init = 0.8 - 0.6 * math.exp(-0.3 * l)
        for is_sample in (False, True):
            x = xs if is_sample else xp
            batch, seq = (bs, ls) if is_sample else (bp, lp)
            za, zb, zc, zg, zd = _in_proj(x, mod, norm_g, w_in_a, l, seq, is_sample)
            if is_sample:
                (ya,) = _mla(za, qn, kvn, wuq_a, wukv_a, l, batch, seq, rope=rope_mla, ctx=(cache_mla_ckv, ctx_kr))
                yd = _diff(zd, diff_lambda, sn, l, batch, seq, lam_init, rope=rope_dif, ctx=(ctx_dk, ctx_dv))
                (yc,) = _mlstm(zc, zg, gb, hn, l, batch, seq, state=(st0, m0))
            else:
                ya, ckv = _mla(za, qn, kvn, wuq_a, wukv_a, l, batch, seq)
                yd = _diff(zd, diff_lambda, sn, l, batch, seq, lam_init)
                yc, st_new, m_new = _mlstm(zc, zg, gb, hn, l, batch, seq)
                c_new, n_new = _unpack_state(st_new)
                ents.append((
                    ckv.reshape(bp, lp, KV_LORA),
                    za[:, Q_LORA + KV_LORA + 64:Q_LORA + KV_LORA + 64 + ROPE_A].reshape(bp, lp, ROPE_A),
                    zd[:, D_D:2 * D_D].reshape(bp, lp, H_D, 2, DH_D),
                    zd[:, 2 * D_D:].reshape(bp, lp, H_D, 2 * DH_D),
                    c_new, n_new,
                    m_new[:, :, 0].reshape(bp, 2, H_C)))
            yb = _gmlp(zb, vn, ws, bias, l)
            x = _merge(x, mod, norm_g, (ya, yb, yc, yd), wm, bm, wb, wo, l, seq, is_sample)
            x = _ffn(x, mod, norm_g, w1, w2, l, seq, is_sample)
            if is_sample:
                xs = x
            else:
                xp = x

    stack = lambda j: jnp.stack([e[j] for e in ents], axis=1)
    return (xp.reshape(bp, lp, D_MODEL), xs.reshape(bs, ls, D_MODEL),
            stack(0), stack(1), stack(2), stack(3), stack(4), stack(5), stack(6))
```

```python
import functools
import math

import jax
import jax.numpy as jnp
from jax import lax
from jax.experimental import pallas as pl
from jax.experimental.pallas import tpu as pltpu

F32 = jnp.float32
BF16 = jnp.bfloat16

D_MODEL = 1024
DEPTH = 2
GRID_W = 64
N_BRANCH = 4
MIX_W = 256
H_A, NOPE_A, ROPE_A, VH_A = 4, 64, 32, 64
Q_LORA, KV_LORA = 256, 128
D_B, G_B, CHUNK_B = 256, 4, 128
H_C, DH_C = 4, 64
D_C = H_C * DH_C
H_D, DH_D = 4, 32
D_D = H_D * 2 * DH_D
D_FF = 4 * D_MODEL
IN_A = Q_LORA + KV_LORA + ROPE_A
IN_B = 2 * D_B
IN_C = 4 * D_C + 4 * H_C
IN_D = 3 * D_D
ROPE_BASE = 10000.0
EPS = 1e-6
MLA_SCALE = (NOPE_A + ROPE_A) ** -0.5
DIFF_SCALE = DH_D ** -0.5

LANES = 128
MLSTM_CHUNK = 128
VMEM_LIMIT = 56 * 1024 * 1024
NEG_INF = float("-inf")

ZA_W = 512
ZG_W = 128
Z_OFF_A = 0
Z_OFF_B = Z_OFF_A + ZA_W
Z_OFF_C = Z_OFF_B + IN_B
Z_OFF_G = Z_OFF_C + 4 * D_C
Z_OFF_D = Z_OFF_G + ZG_W
Z_W = Z_OFF_D + IN_D


def _rms(x, g):
    return x * lax.rsqrt(jnp.mean(x * x, axis=-1, keepdims=True) + EPS) * g


def _sigmoid(x):
    return 1.0 / (1.0 + jnp.exp(-x))


def _log_sigmoid(x):
    return jnp.minimum(x, 0.0) - jnp.log1p(jnp.exp(-jnp.abs(x)))


def _dot(a, b):
    return jnp.dot(a, b, preferred_element_type=F32)


def _dot_nt(a, b):
    return lax.dot_general(a, b, (((1,), (1,)), ((), ())), preferred_element_type=F32)


def _lane_group_mask(width, group, index):
    lane = lax.broadcasted_iota(jnp.int32, (1, width), 1)
    return (lane >= index * group) & (lane < (index + 1) * group)


def _mod_kernel(cond_ref, w_ref, b_ref, o_ref):
    c = cond_ref[...]
    s = c * _sigmoid(c)
    o_ref[...] = _dot(s.astype(BF16), w_ref[...].astype(BF16)) + b_ref[...]


def _modulation(cond, w_mod, b_mod):
    rows = cond.shape[0]
    nb = 1024
    return pl.pallas_call(
        _mod_kernel,
        grid=(DEPTH, 6 * D_MODEL // nb),
        in_specs=[
            pl.BlockSpec((rows, D_MODEL), lambda l, j: (0, 0)),
            pl.BlockSpec((None, D_MODEL, nb), lambda l, j: (l, 0, j)),
            pl.BlockSpec((None, 1, nb), lambda l, j: (l, 0, j)),
        ],
        out_specs=pl.BlockSpec((None, rows, nb), lambda l, j: (l, 0, j)),
        out_shape=jax.ShapeDtypeStruct((DEPTH, rows, 6 * D_MODEL), F32),
        name="modulation",
    )(cond, w_mod, b_mod.reshape(DEPTH, 1, 6 * D_MODEL))


TOKEN_TILE = 512


def _layer_block(shape, layer):
    shape = tuple(shape)
    return pl.BlockSpec((None,) + shape, lambda *_: (layer,) + (0,) * len(shape), pipeline_mode=pl.Buffered(1))


def _mod_spec(layer, tm, seq, per_seq):
    def index(i):
        return (layer, 1 + (i * tm) // seq if per_seq else 0, 0, 0)
    return pl.BlockSpec((None, None, 6, D_MODEL), index)


def _in_kernel(x_ref, mod_ref, g_ref, w_ref, za_ref, zb_ref, zc_ref, zg_ref, zd_ref):
    x = x_ref[...]
    h = _rms(x, g_ref[0:1, :]) * (1.0 + mod_ref[1:2, :]) + mod_ref[0:1, :]
    hb = h.astype(BF16)
    za_ref[...] = _dot(hb, w_ref[:, Z_OFF_A:Z_OFF_B])
    zb_ref[...] = _dot(hb, w_ref[:, Z_OFF_B:Z_OFF_C])
    zc_ref[...] = _dot(hb, w_ref[:, Z_OFF_C:Z_OFF_G])
    zg_ref[...] = _dot(hb, w_ref[:, Z_OFF_G:Z_OFF_D])
    zd_ref[...] = _dot(hb, w_ref[:, Z_OFF_D:Z_W])


def _in_proj(x, mod, g, w, layer, seq, per_seq):
    t = x.shape[0]
    tm = TOKEN_TILE
    widths = (ZA_W, IN_B, 4 * D_C, ZG_W, IN_D)
    return pl.pallas_call(
        _in_kernel,
        grid=(t // tm,),
        in_specs=[
            pl.BlockSpec((tm, D_MODEL), lambda i: (i, 0)),
            _mod_spec(layer, tm, seq, per_seq),
            _layer_block((4, D_MODEL), layer),
            _layer_block((D_MODEL, Z_W), layer),
        ],
        out_specs=[pl.BlockSpec((tm, w_), lambda i: (i, 0)) for w_ in widths],
        out_shape=[jax.ShapeDtypeStruct((t, w_), F32) for w_ in widths],
        compiler_params=pltpu.CompilerParams(vmem_limit_bytes=VMEM_LIMIT),
        name="in_proj",
    )(x, mod, g, w)


LOG2E = math.log2(math.e)


def _with_ones(v, lo_mask, keep_lo):
    lane = lax.broadcasted_iota(jnp.int32, (1, LANES), 1)
    if keep_lo:
        return jnp.where(lane == 64, 1.0, jnp.where(lo_mask, v, 0.0))
    return jnp.where(lane == 0, 1.0, jnp.where(lo_mask, 0.0, v))


def _pair_normalise(o_even, o_odd, lo_mask):
    return jnp.where(lo_mask, o_even / o_even[:, 64:65], o_odd / o_odd[:, 0:1])


def _rope128(x, tc, ts1, ts2):
    return x * tc + pltpu.roll(x, LANES - 16, 1) * ts1 + pltpu.roll(x, 16, 1) * ts2


def _mla_kernel(*refs, seq, has_ctx, tq):
    if has_ctx:
        (za_ref, qn_ref, kvn_ref, wuq_ref, wukv_ref, tc_ref, ts1_ref, ts2_ref, cckv_ref, ckr_ref,
         ya_ref, q_s, k_s, v_s, ckv_s, kr_s) = refs
        past = cckv_ref.shape[0]
    else:
        (za_ref, qn_ref, kvn_ref, wuq_ref, wukv_ref, ya_ref, ckv_ref, q_s, k_s, v_s, ckv_s, kr_s) = refs
        past = 0
    za = za_ref[...]
    cq = _rms(za[:, :Q_LORA], qn_ref[...])
    qh = _dot(cq.astype(BF16), wuq_ref[...])
    ckv = _rms(za[:, Q_LORA:Q_LORA + KV_LORA], kvn_ref[...])
    kr = za[:, Q_LORA + KV_LORA:ZA_W]
    if has_ctx:
        tc, ts1, ts2 = tc_ref[...], ts1_ref[...], ts2_ref[...]
        kr = _rope128(kr, tc, ts1, ts2)
        ckv_s[0:past, :] = cckv_ref[...]
        kr_s[0:past, :] = ckr_ref[...]
    else:
        ckv_ref[...] = ckv
    ckv_s[past:past + seq, :] = ckv
    kr_s[past:past + seq, :] = kr
    kv = _dot(ckv_s[...].astype(BF16), wukv_ref[...])
    kr_all = kr_s[...]
    lo = _lane_group_mask(LANES, 64, 0)
    for h in range(H_A):
        qg = qh[:, h * LANES:(h + 1) * LANES]
        if has_ctx:
            qg = _rope128(qg, tc, ts1, ts2)
        q_s[h] = (qg * (MLA_SCALE * LOG2E)).astype(BF16)
        k_s[h] = (kv[:, h * LANES:(h + 1) * LANES] + kr_all).astype(BF16)
        v_s[h] = _with_ones(kv[:, (H_A + h) * LANES:(H_A + h + 1) * LANES], lo, h % 2 == 0).astype(BF16)

    def body(i, carry):
        rows = pl.ds(pl.multiple_of(i * tq, tq), tq)
        s = [_dot_nt(q_s[h, rows, :], k_s[h]) for h in range(H_A)]
        o = [_dot(jnp.exp2(s[h] - jnp.max(s[h], axis=-1, keepdims=True)).astype(BF16), v_s[h]) for h in range(H_A)]
        for pair in range(H_A // 2):
            ya_ref[rows, pair * LANES:(pair + 1) * LANES] = _pair_normalise(o[2 * pair], o[2 * pair + 1], lo)
        return carry

    lax.fori_loop(0, seq // tq, body, 0)


def _ctx_block(past, width, layer):
    return pl.BlockSpec((None, None, past, width), lambda b: (b, layer, 0, 0))


def _mla(za, qn, kvn, wuq, wukv, layer, batch, seq, rope=None, ctx=None):
    has_ctx = ctx is not None
    past = ctx[0].shape[2] if has_ctx else 0
    lk = past + seq
    tq = min(seq, 256)
    full = lambda shape: pl.BlockSpec(shape, lambda b: (0,) * len(shape))
    in_specs = [
        pl.BlockSpec((seq, ZA_W), lambda b: (b, 0)),
        _layer_block((1, Q_LORA), layer), _layer_block((1, KV_LORA), layer),
        _layer_block((Q_LORA, H_A * LANES), layer), _layer_block((KV_LORA, 2 * H_A * LANES), layer),
    ]
    args = [za, qn, kvn, wuq, wukv]
    out_specs = [pl.BlockSpec((seq, MIX_W), lambda b: (b, 0))]
    out_shape = [jax.ShapeDtypeStruct((batch * seq, MIX_W), F32)]
    if has_ctx:
        in_specs += [full((seq, LANES))] * 3
        in_specs += [_ctx_block(past, LANES, layer)] * 2
        args += list(rope) + list(ctx)
    else:
        out_specs.append(pl.BlockSpec((seq, KV_LORA), lambda b: (b, 0)))
        out_shape.append(jax.ShapeDtypeStruct((batch * seq, KV_LORA), F32))
    return pl.pallas_call(
        functools.partial(_mla_kernel, seq=seq, has_ctx=has_ctx, tq=tq),
        grid=(batch,),
        in_specs=in_specs,
        out_specs=out_specs,
        out_shape=out_shape,
        scratch_shapes=[
            pltpu.VMEM((H_A, seq, LANES), BF16),
            pltpu.VMEM((H_A, lk, LANES), BF16),
            pltpu.VMEM((H_A, lk, LANES), BF16),
            pltpu.VMEM((lk, LANES), F32),
            pltpu.VMEM((lk, LANES), F32),
        ],
        compiler_params=pltpu.CompilerParams(vmem_limit_bytes=VMEM_LIMIT),
        name="mla",
    )(*args)


def _diff_kernel(*refs, seq, has_ctx, tq, lam_init):
    if has_ctx:
        (zd_ref, lam_ref, sn_ref, tc_ref, ts1_ref, ts2_ref, ck_ref, cv_ref,
         yd_ref, q_s, k_s, v_s) = refs
        past = ck_ref.shape[0]
    else:
        (zd_ref, lam_ref, sn_ref, yd_ref, q_s, k_s, v_s) = refs
        past = 0
    lo = _lane_group_mask(LANES, 64, 0)
    if has_ctx:
        tc, ts1, ts2 = tc_ref[...], ts1_ref[...], ts2_ref[...]
        k_s[0:past, :] = ck_ref[...].astype(BF16)
        cv = cv_ref[...]
    for half in range(2):
        cols = slice(half * LANES, (half + 1) * LANES)
        q = zd_ref[:, half * LANES:(half + 1) * LANES]
        k = zd_ref[:, D_D + half * LANES:D_D + (half + 1) * LANES]
        v = zd_ref[:, 2 * D_D + half * LANES:2 * D_D + (half + 1) * LANES]
        if has_ctx:
            q = _rope128(q, tc, ts1, ts2)
            k = _rope128(k, tc, ts1, ts2)
            v_s[2 * half, 0:past, :] = jnp.where(lo, cv[:, cols], 0.0).astype(BF16)
            v_s[2 * half + 1, 0:past, :] = jnp.where(lo, 0.0, cv[:, cols]).astype(BF16)
        q_s[:, cols] = (q * (DIFF_SCALE * LOG2E)).astype(BF16)
        k_s[past:past + seq, cols] = k.astype(BF16)
        v_s[2 * half, past:past + seq, :] = jnp.where(lo, v, 0.0).astype(BF16)
        v_s[2 * half + 1, past:past + seq, :] = jnp.where(lo, 0.0, v).astype(BF16)

    lam = lam_ref[...]
    lam_val = (jnp.exp(jnp.sum(lam[0:1] * lam[1:2], axis=-1, keepdims=True))
               - jnp.exp(jnp.sum(lam[2:3] * lam[3:4], axis=-1, keepdims=True)) + lam_init)
    comp_masks = [jnp.where(_lane_group_mask(D_D, DH_D, j), 1.0, 0.0).astype(BF16) for j in range(2 * H_D)]
    sn = sn_ref[...]

    def body(i, carry):
        rows = pl.ds(pl.multiple_of(i * tq, tq), tq)
        qb = q_s[rows, :]
        k_all = k_s[...]
        for pair in range(H_D // 2):
            heads = (2 * pair, 2 * pair + 1)
            s = [[_dot_nt(qb * comp_masks[2 * h + comp], k_all) for comp in range(2)] for h in heads]
            acc = jnp.zeros((tq, LANES), F32)
            for j, h in enumerate(heads):
                p1, p2 = [jnp.exp2(sc - jnp.max(sc, axis=-1, keepdims=True)) for sc in s[j]]
                r1 = 1.0 / jnp.sum(p1, axis=-1, keepdims=True)
                r2 = lam_val / jnp.sum(p2, axis=-1, keepdims=True)
                acc = acc + _dot((p1 * r1 - p2 * r2).astype(BF16), v_s[h])
            sq = acc * acc
            ss_lo = jnp.sum(jnp.where(lo, sq, 0.0), axis=-1, keepdims=True)
            ss_hi = jnp.sum(jnp.where(lo, 0.0, sq), axis=-1, keepdims=True)
            r = jnp.where(lo, lax.rsqrt(ss_lo / (2 * DH_D) + EPS), lax.rsqrt(ss_hi / (2 * DH_D) + EPS))
            yd_ref[rows, pair * LANES:(pair + 1) * LANES] = acc * r * sn * (1.0 - lam_init)
        return carry

    lax.fori_loop(0, seq // tq, body, 0)


def _diff(zd, lam, sn, layer, batch, seq, lam_init, rope=None, ctx=None):
    has_ctx = ctx is not None
    past = ctx[0].shape[2] if has_ctx else 0
    lk = past + seq
    tq = min(seq, 256)
    full = lambda shape: pl.BlockSpec(shape, lambda b: (0,) * len(shape))
    in_specs = [pl.BlockSpec((seq, IN_D), lambda b: (b, 0)),
                _layer_block((4, DH_D), layer), _layer_block((1, LANES), layer)]
    args = [zd, lam, sn]
    if has_ctx:
        in_specs += [full((seq, LANES))] * 3
        in_specs += [_ctx_block(past, D_D, layer)] * 2
        args += list(rope) + list(ctx)
    return pl.pallas_call(
        functools.partial(_diff_kernel, seq=seq, has_ctx=has_ctx, tq=tq, lam_init=lam_init),
        grid=(batch,),
        in_specs=in_specs,
        out_specs=pl.BlockSpec((seq, MIX_W), lambda b: (b, 0)),
        out_shape=jax.ShapeDtypeStruct((batch * seq, MIX_W), F32),
        scratch_shapes=[
            pltpu.VMEM((seq, D_D), BF16),
            pltpu.VMEM((lk, D_D), BF16),
            pltpu.VMEM((H_D, lk, LANES), BF16),
        ],
        compiler_params=pltpu.CompilerParams(vmem_limit_bytes=VMEM_LIMIT),
        name="diff_attn",
    )(*args)


def _gmlp_kernel(zb_ref, vn_ref, ws_ref, bias_ref, yb_ref):
    tm = zb_ref.shape[0]
    v = _rms(zb_ref[:, D_B:], vn_ref[...]).astype(BF16)
    bias = bias_ref[...]
    gmasks = [_lane_group_mask(D_B, D_B // G_B, g) for g in range(G_B)]
    for ch in range(tm // CHUNK_B):
        rows = slice(ch * CHUNK_B, (ch + 1) * CHUNK_B)
        vc = v[rows, :]
        mixed = bias
        for g in range(G_B):
            mixed = mixed + jnp.where(gmasks[g], _dot(ws_ref[g], vc), 0.0)
        yb_ref[rows, :] = zb_ref[rows, :D_B] * mixed


def _gmlp(zb, vn, ws, bias, layer):
    t = zb.shape[0]
    tm = TOKEN_TILE
    return pl.pallas_call(
        _gmlp_kernel,
        grid=(t // tm,),
        in_specs=[pl.BlockSpec((tm, IN_B), lambda i: (i, 0)), _layer_block((1, D_B), layer),
                  _layer_block((G_B, CHUNK_B, CHUNK_B), layer), _layer_block((CHUNK_B, D_B), layer)],
        out_specs=pl.BlockSpec((tm, MIX_W), lambda i: (i, 0)),
        out_shape=jax.ShapeDtypeStruct((t, MIX_W), F32),
        name="gmlp",
    )(zb, vn, ws, bias)


def _split3(x):
    hi = x.astype(BF16)
    r1 = x - hi.astype(F32)
    mid = r1.astype(BF16)
    lo = (r1 - mid.astype(F32)).astype(BF16)
    return hi, mid, lo


ST_ROWS = DH_C + 8
N_UNITS = 2 * H_C


def _mlstm_kernel(*refs, seq, has_state):
    ck = MLSTM_CHUNK
    nc = seq // ck
    if has_state:
        (zc_ref, zg_ref, gb_ref, hn_ref, st0_ref, m0_ref,
         yc_ref, k_s, qt_s, kt_s, vt_s, rcc_s, ca_s, wi_s, en_s, ws_s, wo_s, st_s, ht_s) = refs
    else:
        (zc_ref, zg_ref, gb_ref, hn_ref,
         yc_ref, stout_ref, mout_ref, k_s, qt_s, kt_s, vt_s, rcc_s, ca_s, wi_s, en_s, ws_s, wo_s, st_s, ht_s) = refs

    for c in range(nc):
        rows = slice(c * ck, (c + 1) * ck)
        qt_s[c] = zc_ref[rows, 0:D_C].T.astype(BF16)
        kk = zc_ref[rows, D_C:2 * D_C] * (DH_C ** -0.5)
        k_s[rows, :] = kk.astype(BF16)
        kt_s[c] = kk.T.astype(BF16)
        vt_s[c] = zc_ref[rows, 2 * D_C:3 * D_C].T.astype(BF16)

    gates = zg_ref[...] + gb_ref[...]
    lf = _log_sigmoid(gates)
    r_i = lax.broadcasted_iota(jnp.int32, (ck, ck), 0)
    c_i = lax.broadcasted_iota(jnp.int32, (ck, ck), 1)
    tri_fw = jnp.where(c_i <= r_i, 1.0, 0.0).astype(BF16)
    tri_bw = jnp.where(c_i >= r_i, 1.0, 0.0).astype(BF16)
    fw_lane = (lax.broadcasted_iota(jnp.int32, (1, ZG_W), 1) % N_UNITS) < H_C
    i_rows, b_rows = [], []
    for c in range(nc):
        rows = slice(c * ck, (c + 1) * ck)
        p0, p1, p2 = _split3(lf[rows, :])
        b_fw = _dot(tri_fw, p0) + _dot(tri_fw, p1) + _dot(tri_fw, p2)
        b_bw = _dot(tri_bw, p0) + _dot(tri_bw, p1) + _dot(tri_bw, p2)
        b = jnp.where(fw_lane, b_fw, b_bw)
        g_c = gates[rows, :]
        rcc_s[rows, :] = g_c - pltpu.roll(b, ZG_W - N_UNITS, 1)
        i_rows.append(g_c.T[0:N_UNITS, :])
        b_rows.append(b.T[N_UNITS:2 * N_UNITS, :])

    row8 = lax.broadcasted_iota(jnp.int32, (N_UNITS, ck), 0)
    lane8 = lax.broadcasted_iota(jnp.int32, (N_UNITS, ck), 1)
    fw8 = row8 < H_C
    fw81 = fw8[:, 0:1]
    m_prev = m0_ref[:, 0:1] if has_state else jnp.zeros((N_UNITS, 1), F32)
    for i in range(nc):
        i8 = jnp.where(fw8, i_rows[i], i_rows[nc - 1 - i])
        b8 = jnp.where(fw8, b_rows[i], b_rows[nc - 1 - i])
        r8 = i8 - b8
        pm, sm = r8, r8
        sh = 1
        while sh < ck:
            pm = jnp.maximum(pm, jnp.where(lane8 >= sh, pltpu.roll(pm, sh, 1), NEG_INF))
            sm = jnp.maximum(sm, jnp.where(lane8 < ck - sh, pltpu.roll(sm, ck - sh, 1), NEG_INF))
            sh *= 2
        cm = jnp.where(fw8, pm, sm)
        b_t = jnp.where(fw81, b8[:, ck - 1:ck], b8[:, 0:1])
        cm_end = jnp.where(fw81, pm[:, ck - 1:ck], sm[:, 0:1])
        inter = b8 + m_prev
        mt = jnp.maximum(inter, b8 + cm)
        ca_s[i] = b8 - mt
        wi_s[i] = jnp.exp(inter - mt)
        en_s[i] = jnp.exp(-mt)
        m_new = b_t + jnp.maximum(m_prev, cm_end)
        ws_s[i] = jnp.exp(b_t + r8 - m_new)
        wo_s[i] = jnp.broadcast_to(jnp.exp(b_t + m_prev - m_new), (N_UNITS, ck))
        m_prev = m_new

    if has_state:
        st_s[...] = st0_ref[...]
    else:
        st_s[...] = jnp.zeros_like(st_s)

    hmask_bf = [jnp.where(_lane_group_mask(D_C, DH_C, h), 1.0, 0.0).astype(BF16) for h in range(H_C)]
    src_ok = (r_i <= c_i, r_i >= c_i)

    def body(i, carry):
        ca, wi, en, ws, wo = ca_s[i], wi_s[i], en_s[i], ws_s[i], wo_s[i]
        chunks = (i, nc - 1 - i)
        units = [(d, h) for d in range(2) for h in range(H_C)]
        hs = [slice(h * DH_C, (h + 1) * DH_C) for h in range(H_C)]
        q_t = [qt_s[c] for c in chunks]
        k_t = [kt_s[c] for c in chunks]
        v_t = [vt_s[c] for c in chunks]
        sts, qcts, states, upds = [], [], [], []
        for d, h in units:
            rows = pl.ds(pl.multiple_of(chunks[d] * ck, ck), ck)
            sts.append(_dot(k_s[rows, :] * hmask_bf[h], q_t[d]))
            states.append(st_s[d * H_C + h])
            qcts.append(_dot(states[-1].astype(BF16), q_t[d][hs[h], :]))
        for u, (d, h) in enumerate(units):
            w_s = ws[u:u + 1, :]
            vw = jnp.concatenate([v_t[d][hs[h], :].astype(F32) * w_s, jnp.broadcast_to(w_s, (8, ck))], axis=0)
            upds.append(_dot_nt(vw.astype(BF16), k_t[d][hs[h], :]))
        pts = []
        for u, (d, h) in enumerate(units):
            rows = pl.ds(pl.multiple_of(chunks[d] * ck, ck), ck)
            decay = jnp.exp(rcc_s[rows, u:u + 1] + ca[u:u + 1, :])
            pts.append(sts[u] * jnp.where(src_ok[d], decay, 0.0))
        nums = [_dot(v_t[d][hs[h], :], pts[u].astype(BF16)) for u, (d, h) in enumerate(units)]
        for u, (d, h) in enumerate(units):
            w_i = wi[u:u + 1, :]
            den = w_i * qcts[u][DH_C:DH_C + 1, :] + jnp.sum(pts[u], axis=0, keepdims=True)
            num = w_i * qcts[u][0:DH_C, :] + nums[u]
            ht_s[d, chunks[d], hs[h], :] = num / jnp.maximum(jnp.abs(den), en[u:u + 1, :])
            st_s[u] = wo[u:u + 1, 0:DH_C] * states[u] + upds[u]
        return carry

    lax.fori_loop(0, nc, body, 0)

    hn = hn_ref[...]
    for c in range(nc):
        rows = slice(c * ck, (c + 1) * ck)
        h_t = ht_s[0, c] + ht_s[1, c]
        parts = []
        for h in range(H_C):
            x = h_t[h * DH_C:(h + 1) * DH_C, :]
            parts.append(x * lax.rsqrt(jnp.mean(x * x, axis=0, keepdims=True) + EPS))
        h_n = jnp.concatenate(parts, axis=0).T
        yc_ref[rows, :] = _sigmoid(zc_ref[rows, 3 * D_C:4 * D_C]) * (h_n * hn)

    if not has_state:
        stout_ref[...] = st_s[...]
        mout_ref[...] = jnp.broadcast_to(m_prev, (N_UNITS, LANES))


def _mlstm(zc, zg, gb, hn, layer, batch, seq, state=None):
    has_state = state is not None
    nc = seq // MLSTM_CHUNK
    st_spec = pl.BlockSpec((None, N_UNITS, ST_ROWS, DH_C), lambda b: (b, 0, 0, 0))
    m_spec = pl.BlockSpec((None, N_UNITS, LANES), lambda b: (b, 0, 0))
    in_specs = [pl.BlockSpec((seq, 4 * D_C), lambda b: (b, 0)), pl.BlockSpec((seq, ZG_W), lambda b: (b, 0)),
                _layer_block((1, ZG_W), layer), _layer_block((1, D_C), layer)]
    args = [zc, zg, gb, hn]
    out_specs = [pl.BlockSpec((seq, MIX_W), lambda b: (b, 0))]
    out_shape = [jax.ShapeDtypeStruct((batch * seq, MIX_W), F32)]
    if has_state:
        in_specs += [pl.BlockSpec((None, None, N_UNITS, ST_ROWS, DH_C), lambda b: (b, layer, 0, 0, 0)),
                     pl.BlockSpec((None, None, N_UNITS, LANES), lambda b: (b, layer, 0, 0))]
        args += list(state)
    else:
        out_specs += [st_spec, m_spec]
        out_shape += [jax.ShapeDtypeStruct((batch, N_UNITS, ST_ROWS, DH_C), F32),
                      jax.ShapeDtypeStruct((batch, N_UNITS, LANES), F32)]
    step = lambda: pltpu.VMEM((nc, N_UNITS, MLSTM_CHUNK), F32)
    return pl.pallas_call(
        functools.partial(_mlstm_kernel, seq=seq, has_state=has_state),
        grid=(batch,),
        in_specs=in_specs,
        out_specs=out_specs,
        out_shape=out_shape,
        scratch_shapes=[
            pltpu.VMEM((seq, D_C), BF16),
            pltpu.VMEM((nc, D_C, MLSTM_CHUNK), BF16), pltpu.VMEM((nc, D_C, MLSTM_CHUNK), BF16),
            pltpu.VMEM((nc, D_C, MLSTM_CHUNK), BF16),
            pltpu.VMEM((seq, ZG_W), F32),
            step(), step(), step(), step(), step(),
            pltpu.VMEM((N_UNITS, ST_ROWS, DH_C), F32),
            pltpu.VMEM((2, nc, D_C, MLSTM_CHUNK), F32),
        ],
        compiler_params=pltpu.CompilerParams(vmem_limit_bytes=VMEM_LIMIT),
        name="mlstm",
    )(*args)


def _merge_kernel(x_ref, mod_ref, g_ref, ya_ref, yb_ref, yc_ref, yd_ref, wm_ref, bm_ref, wb_ref, wo_ref, o_ref):
    x = x_ref[...]
    h = (_rms(x, g_ref[0:1, :]) * (1.0 + mod_ref[1:2, :]) + mod_ref[0:1, :]).astype(BF16)
    acc = jnp.zeros(x.shape, F32)
    for n, y_ref in enumerate((ya_ref, yb_ref, yc_ref, yd_ref)):
        cols = slice(n * D_MODEL, (n + 1) * D_MODEL)
        gate = _sigmoid(_dot(h, wm_ref[:, cols]) + bm_ref[:, cols])
        acc = acc + gate * _dot(y_ref[...].astype(BF16), wb_ref[n])
    y = _dot(acc.astype(BF16), wo_ref[...])
    o_ref[...] = x + mod_ref[2:3, :] * _rms(y, g_ref[1:2, :])


def _merge(x, mod, g, ys, wm, bm, wb, wo, layer, seq, per_seq):
    t = x.shape[0]
    tm = TOKEN_TILE
    full = lambda shape: _layer_block(shape, layer)
    tok = lambda w_: pl.BlockSpec((tm, w_), lambda i: (i, 0))
    return pl.pallas_call(
        _merge_kernel,
        grid=(t // tm,),
        in_specs=[tok(D_MODEL), _mod_spec(layer, tm, seq, per_seq),
                  full((4, D_MODEL)), tok(MIX_W), tok(MIX_W), tok(MIX_W), tok(MIX_W),
                  full((D_MODEL, N_BRANCH * D_MODEL)), full((1, N_BRANCH * D_MODEL)),
                  full((N_BRANCH, MIX_W, D_MODEL)), full((D_MODEL, D_MODEL))],
        out_specs=tok(D_MODEL),
        out_shape=jax.ShapeDtypeStruct((t, D_MODEL), F32),
        compiler_params=pltpu.CompilerParams(vmem_limit_bytes=VMEM_LIMIT),
        name="merge",
    )(x, mod, g, *ys, wm, bm, wb, wo)


def _ffn_kernel(x_ref, mod_ref, g_ref, w1_ref, w2_ref, o_ref):
    x = x_ref[...]
    h = (_rms(x, g_ref[2:3, :]) * (1.0 + mod_ref[4:5, :]) + mod_ref[3:4, :]).astype(BF16)
    f = jnp.zeros(x.shape, F32)
    for j in range(D_FF // D_MODEL):
        cols = slice(j * D_MODEL, (j + 1) * D_MODEL)
        a = jnp.maximum(_dot(h, w1_ref[:, cols]), 0.0)
        f = f + _dot((a * a).astype(BF16), w2_ref[cols, :])
    o_ref[...] = x + mod_ref[5:6, :] * _rms(f, g_ref[3:4, :])


def _ffn(x, mod, g, w1, w2, layer, seq, per_seq):
    t = x.shape[0]
    tm = TOKEN_TILE
    full = lambda shape: _layer_block(shape, layer)
    tok = lambda w_: pl.BlockSpec((tm, w_), lambda i: (i, 0))
    return pl.pallas_call(
        _ffn_kernel,
        grid=(t // tm,),
        in_specs=[tok(D_MODEL), _mod_spec(layer, tm, seq, per_seq),
                  full((4, D_MODEL)), full((D_MODEL, D_FF)), full((D_FF, D_MODEL))],
        out_specs=tok(D_MODEL),
        out_shape=jax.ShapeDtypeStruct((t, D_MODEL), F32),
        compiler_params=pltpu.CompilerParams(vmem_limit_bytes=VMEM_LIMIT),
        name="ffn",
    )(x, mod, g, w1, w2)


def _arrange_w_in(w):
    z = lambda n: jnp.zeros((D_MODEL, n), w.dtype)
    o_b, o_c = IN_A, IN_A + IN_B
    o_g, o_d = o_c + 4 * D_C, o_c + IN_C
    gates = w[:, o_g:o_d].reshape(D_MODEL, 2, 2, H_C).transpose(0, 2, 1, 3).reshape(D_MODEL, 4 * H_C)
    return jnp.concatenate([
        w[:, :Q_LORA + KV_LORA], z(64), w[:, Q_LORA + KV_LORA:IN_A], z(32),
        w[:, o_b:o_g], gates, z(ZG_W - 4 * H_C), w[:, o_d:]], axis=1).astype(BF16)


def _arrange_w_uq(w):
    w = w.reshape(Q_LORA, H_A, NOPE_A + ROPE_A)
    return jnp.pad(w, ((0, 0), (0, 0), (0, LANES - NOPE_A - ROPE_A))).reshape(Q_LORA, H_A * LANES).astype(BF16)


def _arrange_w_ukv(w):
    w = w.reshape(KV_LORA, H_A, NOPE_A + VH_A)
    wk = jnp.pad(w[:, :, :NOPE_A], ((0, 0), (0, 0), (0, LANES - NOPE_A)))
    v = w[:, :, NOPE_A:]
    zero = jnp.zeros_like(v)
    even = jnp.concatenate([v, zero], axis=-1)
    odd = jnp.concatenate([zero, v], axis=-1)
    wv = jnp.where((jnp.arange(H_A) % 2 == 0)[None, :, None], even, odd)
    return jnp.concatenate([wk.reshape(KV_LORA, -1), wv.reshape(KV_LORA, -1)], axis=1).astype(BF16)


def _rope_tables(rows):
    row = jnp.repeat(jnp.arange(rows), GRID_W).astype(F32)
    col = jnp.tile(jnp.arange(GRID_W), rows).astype(F32)
    nf = ROPE_A // 4
    inv = jnp.exp(-math.log(ROPE_BASE) * jnp.arange(nf, dtype=F32) / nf)
    ang = jnp.concatenate([row[:, None] * inv, col[:, None] * inv], axis=-1)
    cos, sin = jnp.cos(ang), jnp.sin(ang)
    n = cos.shape[0]
    one, zero = jnp.ones((n, 1), F32), jnp.zeros((n, 1), F32)
    rep = lambda a, k: jnp.broadcast_to(a, (n, k))
    mla = (jnp.concatenate([rep(one, 64), cos, cos, rep(one, 32)], axis=1),
           jnp.concatenate([rep(zero, 64), -sin, rep(zero, 48)], axis=1),
           jnp.concatenate([rep(zero, 80), sin, rep(zero, 32)], axis=1))
    z16 = rep(zero, 16)
    dif = (jnp.tile(jnp.concatenate([cos, cos], axis=1), (1, 4)),
           jnp.tile(jnp.concatenate([-sin, z16], axis=1), (1, 4)),
           jnp.tile(jnp.concatenate([z16, sin], axis=1), (1, 4)))
    return mla, dif


def _pack_state(c, n):
    b = c.shape[0]
    ct = jnp.swapaxes(c, -1, -2)
    pad = jnp.zeros((b, 2, H_C, ST_ROWS - DH_C - 1, DH_C), c.dtype)
    return jnp.concatenate([ct, n[:, :, :, None, :], pad], axis=3).reshape(b, N_UNITS, ST_ROWS, DH_C)


def _unpack_state(st):
    b = st.shape[0]
    st = st.reshape(b, 2, H_C, ST_ROWS, DH_C)
    return jnp.swapaxes(st[:, :, :, :DH_C, :], -1, -2), st[:, :, :, DH_C, :]


def kernel(x_prompt, x_sample, cache_mla_ckv, cache_mla_krope, cache_diff_k, cache_diff_v, state_mlstm_C,
           state_mlstm_n, state_mlstm_m, c, c_ctx, w_mod, b_mod, norm_g, w_in, mla_q_norm, w_uq, mla_kv_norm,
           w_ukv, gmlp_v_norm, gmlp_w_s, gmlp_b_s, mlstm_gate_bias, mlstm_head_norm, diff_lambda, diff_sub_norm,
           w_branch, w_merge, b_merge, w_out, w_ff1, w_ff2):
    bp, lp, _ = x_prompt.shape
    bs, ls, _ = x_sample.shape
    past = cache_mla_ckv.shape[2]

    cond = jnp.concatenate([c_ctx[None, :], c, jnp.zeros((16 - 1 - bs, D_MODEL), F32)], axis=0)
    mod = _modulation(cond, w_mod, b_mod).reshape(DEPTH, 16, 6, D_MODEL)
    rope_mla, rope_dif = _rope_tables(ls // GRID_W)

    row = lambda a: a[:, None, :]
    w_in_a = jax.vmap(_arrange_w_in)(w_in)
    wuq_a = jax.vmap(_arrange_w_uq)(w_uq)
    wukv_a = jax.vmap(_arrange_w_ukv)(w_ukv)
    qn, kvn, vn, hn, bm = row(mla_q_norm), row(mla_kv_norm), row(gmlp_v_norm), row(mlstm_head_norm), row(b_merge)
    ws = gmlp_w_s.astype(BF16)
    bias = jnp.repeat(jnp.swapaxes(gmlp_b_s, 1, 2), D_B // G_B, axis=2)
    gb = jnp.pad(mlstm_gate_bias.transpose(0, 2, 1, 3).reshape(DEPTH, 1, 4 * H_C),
                 ((0, 0), (0, 0), (0, ZG_W - 4 * H_C)))
    sn = row(jnp.tile(diff_sub_norm, (1, 2)))
    wm, wb, wo = w_merge.astype(BF16), w_branch.astype(BF16), w_out.astype(BF16)
    w1, w2 = w_ff1.astype(BF16), w_ff2.astype(BF16)

    ctx_kr = jnp.pad(cache_mla_krope, ((0, 0), (0, 0), (0, 0), (64, LANES - 64 - ROPE_A)))
    ctx_dk = cache_diff_k.reshape(bs, DEPTH, past, D_D)
    ctx_dv = cache_diff_v.reshape(bs, DEPTH, past, D_D)
    st0 = jax.vmap(_pack_state, in_axes=1, out_axes=1)(state_mlstm_C, state_mlstm_n)
    m0 = jnp.broadcast_to(state_mlstm_m.reshape(bs, DEPTH, N_UNITS, 1), (bs, DEPTH, N_UNITS, LANES))

    xp = x_prompt.reshape(bp * lp, D_MODEL)
    xs = x_sample.reshape(bs * ls, D_MODEL)
    ents = []
    for l in range(DEPTH):
        lam_init = 0.8 - 0.6 * math.exp(-0.3 * l)
        for is_sample in (False, True):
            x = xs if is_sample else xp
            batch, seq = (bs, ls) if is_sample else (bp, lp)
            za, zb, zc, zg, zd = _in_proj(x, mod, norm_g, w_in_a, l, seq, is_sample)
            if is_sample:
                (ya,) = _mla(za, qn, kvn, wuq_a, wukv_a, l, batch, seq, rope=rope_mla, ctx=(cache_mla_ckv, ctx_kr))
                yd = _diff(zd, diff_lambda, sn, l, batch, seq, lam_init, rope=rope_dif, ctx=(ctx_dk, ctx_dv))
                (yc,) = _mlstm(zc, zg, gb, hn, l, batch, seq, state=(st0, m0))
            else:
                ya, ckv = _mla(za, qn, kvn, wuq_a, wukv_a, l, batch, seq)
                yd = _diff(zd, diff_lambda, sn, l, batch, seq, lam_init)
                yc, st_new, m_new = _mlstm(zc, zg, gb, hn, l, batch, seq)
                c_new, n_new = _unpack_state(st_new)
                ents.append((
                    ckv.reshape(bp, lp, KV_LORA),
                    za[:, Q_LORA + KV_LORA + 64:Q_LORA + KV_LORA + 64 + ROPE_A].reshape(bp, lp, ROPE_A),
                    zd[:, D_D:2 * D_D].reshape(bp, lp, H_D, 2, DH_D),
                    zd[:, 2 * D_D:].reshape(bp, lp, H_D, 2 * DH_D),
                    c_new, n_new,
                    m_new[:, :, 0].reshape(bp, 2, H_C)))
            yb = _gmlp(zb, vn, ws, bias, l)
            x = _merge(x, mod, norm_g, (ya, yb, yc, yd), wm, bm, wb, wo, l, seq, is_sample)
            x = _ffn(x, mod, norm_g, w1, w2, l, seq, is_sample)
            if is_sample:
                xs = x
            else:
                xp = x

    stack = lambda j: jnp.stack([e[j] for e in ents], axis=1)
    return (xp.reshape(bp, lp, D_MODEL), xs.reshape(bs, ls, D_MODEL),
            stack(0), stack(1), stack(2), stack(3), stack(4), stack(5), stack(6))
```

```python
import functools
import math

import jax
import jax.numpy as jnp
import numpy as np
from jax import lax
from jax.experimental import pallas as pl
from jax.experimental.pallas import tpu as pltpu

F32 = jnp.float32
BF16 = jnp.bfloat16

D_MODEL = 1024
DEPTH = 2
GRID_W = 64
N_BRANCH = 4
MIX_W = 256
H_A, NOPE_A, ROPE_A, VH_A = 4, 64, 32, 64
Q_LORA, KV_LORA = 256, 128
D_B, G_B, CHUNK_B = 256, 4, 128
H_C, DH_C = 4, 64
D_C = H_C * DH_C
H_D, DH_D = 4, 32
D_D = H_D * 2 * DH_D
D_FF = 4 * D_MODEL
IN_A = Q_LORA + KV_LORA + ROPE_A
IN_B = 2 * D_B
IN_C = 4 * D_C + 4 * H_C
IN_D = 3 * D_D
ROPE_BASE = 10000.0
EPS = 1e-6
MLA_SCALE = (NOPE_A + ROPE_A) ** -0.5
DIFF_SCALE = DH_D ** -0.5

LANES = 128
MLSTM_CHUNK = 128
VMEM_LIMIT = 56 * 1024 * 1024
NEG_INF = float("-inf")

ZA_W = 512
ZG_W = 128
Z_OFF_A = 0
Z_OFF_B = Z_OFF_A + ZA_W
Z_OFF_C = Z_OFF_B + IN_B
Z_OFF_G = Z_OFF_C + 4 * D_C
Z_OFF_D = Z_OFF_G + ZG_W
Z_W = Z_OFF_D + IN_D


def _rms(x, g):
    return x * lax.rsqrt(jnp.mean(x * x, axis=-1, keepdims=True) + EPS) * g


def _sigmoid(x):
    return 1.0 / (1.0 + jnp.exp(-x))


def _log_sigmoid(x):
    return jnp.minimum(x, 0.0) - jnp.log1p(jnp.exp(-jnp.abs(x)))


def _dot(a, b):
    return jnp.dot(a, b, preferred_element_type=F32)


def _dot_nt(a, b):
    return lax.dot_general(a, b, (((1,), (1,)), ((), ())), preferred_element_type=F32)


def _lane_group_mask(width, group, index):
    lane = lax.broadcasted_iota(jnp.int32, (1, width), 1)
    return (lane >= index * group) & (lane < (index + 1) * group)


def _mod_kernel(cond_ref, w_ref, b_ref, o_ref):
    c = cond_ref[...]
    s = c * _sigmoid(c)
    o_ref[...] = _dot(s.astype(BF16), w_ref[...].astype(BF16)) + b_ref[...]


def _modulation(cond, w_mod, b_mod):
    rows = cond.shape[0]
    nb = 1024
    return pl.pallas_call(
        _mod_kernel,
        grid=(DEPTH, 6 * D_MODEL // nb),
        in_specs=[
            pl.BlockSpec((rows, D_MODEL), lambda l, j: (0, 0)),
            pl.BlockSpec((None, D_MODEL, nb), lambda l, j: (l, 0, j)),
            pl.BlockSpec((None, 1, nb), lambda l, j: (l, 0, j)),
        ],
        out_specs=pl.BlockSpec((None, rows, nb), lambda l, j: (l, 0, j)),
        out_shape=jax.ShapeDtypeStruct((DEPTH, rows, 6 * D_MODEL), F32),
        name="modulation",
    )(cond, w_mod, b_mod.reshape(DEPTH, 1, 6 * D_MODEL))


TOKEN_TILE = 512


def _layer_block(shape, layer):
    shape = tuple(shape)
    return pl.BlockSpec((None,) + shape, lambda *_: (layer,) + (0,) * len(shape), pipeline_mode=pl.Buffered(1))


def _mod_spec(layer, tm, seq, per_seq):
    def index(i):
        return (layer, 1 + (i * tm) // seq if per_seq else 0, 0, 0)
    return pl.BlockSpec((None, None, 6, D_MODEL), index)


def _gmlp_tile(zb, vn, ws_ref, bias):
    tm = zb.shape[0]
    v = _rms(zb[:, D_B:], vn).astype(BF16)
    gmasks = [_lane_group_mask(D_B, D_B // G_B, g) for g in range(G_B)]
    out = []
    for ch in range(tm // CHUNK_B):
        vc = v[ch * CHUNK_B:(ch + 1) * CHUNK_B, :]
        mixed = bias
        for g in range(G_B):
            mixed = mixed + jnp.where(gmasks[g], _dot(ws_ref[g], vc), 0.0)
        out.append(zb[ch * CHUNK_B:(ch + 1) * CHUNK_B, :D_B] * mixed)
    return jnp.concatenate(out, axis=0)


def _in_kernel(x_ref, mod_ref, g_ref, w_ref, vn_ref, ws_ref, bias_ref, za_ref, yb_ref, zc_ref, zg_ref, zd_ref):
    x = x_ref[...]
    h = _rms(x, g_ref[0:1, :]) * (1.0 + mod_ref[1:2, :]) + mod_ref[0:1, :]
    hb = h.astype(BF16)
    za_ref[...] = _dot(hb, w_ref[:, Z_OFF_A:Z_OFF_B])
    zc_ref[...] = _dot(hb, w_ref[:, Z_OFF_C:Z_OFF_G])
    zg_ref[...] = _dot(hb, w_ref[:, Z_OFF_G:Z_OFF_D])
    zd_ref[...] = _dot(hb, w_ref[:, Z_OFF_D:Z_W])
    yb_ref[...] = _gmlp_tile(_dot(hb, w_ref[:, Z_OFF_B:Z_OFF_C]), vn_ref[...], ws_ref, bias_ref[...])


def _in_proj(x, mod, g, w, vn, ws, bias, layer, seq, per_seq):
    t = x.shape[0]
    tm = TOKEN_TILE
    widths = (ZA_W, MIX_W, 4 * D_C, ZG_W, IN_D)
    return pl.pallas_call(
        _in_kernel,
        grid=(t // tm,),
        in_specs=[
            pl.BlockSpec((tm, D_MODEL), lambda i: (i, 0)),
            _mod_spec(layer, tm, seq, per_seq),
            _layer_block((4, D_MODEL), layer),
            _layer_block((D_MODEL, Z_W), layer),
            _layer_block((1, D_B), layer),
            _layer_block((G_B, CHUNK_B, CHUNK_B), layer),
            _layer_block((CHUNK_B, D_B), layer),
        ],
        out_specs=[pl.BlockSpec((tm, w_), lambda i: (i, 0)) for w_ in widths],
        out_shape=[jax.ShapeDtypeStruct((t, w_), F32) for w_ in widths],
        compiler_params=pltpu.CompilerParams(vmem_limit_bytes=VMEM_LIMIT),
        name="in_proj",
    )(x, mod, g, w, vn, ws, bias)


LOG2E = math.log2(math.e)


def _with_ones(v, lo_mask, keep_lo):
    lane = lax.broadcasted_iota(jnp.int32, (1, LANES), 1)
    if keep_lo:
        return jnp.where(lane == 64, 1.0, jnp.where(lo_mask, v, 0.0))
    return jnp.where(lane == 0, 1.0, jnp.where(lo_mask, 0.0, v))


def _pair_normalise(o_even, o_odd, lo_mask):
    return jnp.where(lo_mask, o_even / o_even[:, 64:65], o_odd / o_odd[:, 0:1])


def _rope128(x, tc, ts1, ts2):
    return x * tc + pltpu.roll(x, LANES - 16, 1) * ts1 + pltpu.roll(x, 16, 1) * ts2


def _mla_kernel(*refs, seq, has_ctx, tq):
    if has_ctx:
        (za_ref, qn_ref, kvn_ref, wuq_ref, wukv_ref, tc_ref, ts1_ref, ts2_ref, cckv_ref, ckr_ref,
         ya_ref, q_s, k_s, v_s, ckv_s, kr_s) = refs
        past = cckv_ref.shape[0]
    else:
        (za_ref, qn_ref, kvn_ref, wuq_ref, wukv_ref, ya_ref, ckv_ref, q_s, k_s, v_s, ckv_s, kr_s) = refs
        past = 0
    za = za_ref[...]
    cq = _rms(za[:, :Q_LORA], qn_ref[...])
    qh = _dot(cq.astype(BF16), wuq_ref[...])
    ckv = _rms(za[:, Q_LORA:Q_LORA + KV_LORA], kvn_ref[...])
    kr = za[:, Q_LORA + KV_LORA:ZA_W]
    if has_ctx:
        tc, ts1, ts2 = tc_ref[...], ts1_ref[...], ts2_ref[...]
        kr = _rope128(kr, tc, ts1, ts2)
        ckv_s[0:past, :] = cckv_ref[...]
        kr_s[0:past, :] = ckr_ref[...]
    else:
        ckv_ref[...] = ckv
    ckv_s[past:past + seq, :] = ckv
    kr_s[past:past + seq, :] = kr
    kv = _dot(ckv_s[...].astype(BF16), wukv_ref[...])
    kr_all = kr_s[...]
    lo = _lane_group_mask(LANES, 64, 0)
    for h in range(H_A):
        qg = qh[:, h * LANES:(h + 1) * LANES]
        if has_ctx:
            qg = _rope128(qg, tc, ts1, ts2)
        q_s[h] = (qg * (MLA_SCALE * LOG2E)).astype(BF16)
        k_s[h] = (kv[:, h * LANES:(h + 1) * LANES] + kr_all).astype(BF16)
        v_s[h] = _with_ones(kv[:, (H_A + h) * LANES:(H_A + h + 1) * LANES], lo, h % 2 == 0).astype(BF16)

    def body(i, carry):
        rows = pl.ds(pl.multiple_of(i * tq, tq), tq)
        s = [_dot_nt(q_s[h, rows, :], k_s[h]) for h in range(H_A)]
        o = [_dot(jnp.exp2(s[h] - jnp.max(s[h], axis=-1, keepdims=True)).astype(BF16), v_s[h]) for h in range(H_A)]
        for pair in range(H_A // 2):
            ya_ref[rows, pair * LANES:(pair + 1) * LANES] = _pair_normalise(o[2 * pair], o[2 * pair + 1], lo)
        return carry

    lax.fori_loop(0, seq // tq, body, 0)


def _ctx_block(past, width, layer):
    return pl.BlockSpec((None, None, past, width), lambda b: (b, layer, 0, 0))


def _mla(za, qn, kvn, wuq, wukv, layer, batch, seq, rope=None, ctx=None):
    has_ctx = ctx is not None
    past = ctx[0].shape[2] if has_ctx else 0
    lk = past + seq
    tq = min(seq, 256)
    full = lambda shape: pl.BlockSpec(shape, lambda b: (0,) * len(shape))
    in_specs = [
        pl.BlockSpec((seq, ZA_W), lambda b: (b, 0)),
        _layer_block((1, Q_LORA), layer), _layer_block((1, KV_LORA), layer),
        _layer_block((Q_LORA, H_A * LANES), layer), _layer_block((KV_LORA, 2 * H_A * LANES), layer),
    ]
    args = [za, qn, kvn, wuq, wukv]
    out_specs = [pl.BlockSpec((seq, MIX_W), lambda b: (b, 0))]
    out_shape = [jax.ShapeDtypeStruct((batch * seq, MIX_W), F32)]
    if has_ctx:
        in_specs += [full((seq, LANES))] * 3
        in_specs += [_ctx_block(past, LANES, layer)] * 2
        args += list(rope) + list(ctx)
    else:
        out_specs.append(pl.BlockSpec((seq, KV_LORA), lambda b: (b, 0)))
        out_shape.append(jax.ShapeDtypeStruct((batch * seq, KV_LORA), F32))
    return pl.pallas_call(
        functools.partial(_mla_kernel, seq=seq, has_ctx=has_ctx, tq=tq),
        grid=(batch,),
        in_specs=in_specs,
        out_specs=out_specs,
        out_shape=out_shape,
        scratch_shapes=[
            pltpu.VMEM((H_A, seq, LANES), BF16),
            pltpu.VMEM((H_A, lk, LANES), BF16),
            pltpu.VMEM((H_A, lk, LANES), BF16),
            pltpu.VMEM((lk, LANES), F32),
            pltpu.VMEM((lk, LANES), F32),
        ],
        compiler_params=pltpu.CompilerParams(vmem_limit_bytes=VMEM_LIMIT),
        name="mla",
    )(*args)


def _diff_kernel(*refs, seq, has_ctx, tq, lam_init):
    if has_ctx:
        (zd_ref, lam_ref, sn_ref, tc_ref, ts1_ref, ts2_ref, ck_ref, cv_ref,
         yd_ref, q_s, k_s, v_s) = refs
        past = ck_ref.shape[0]
    else:
        (zd_ref, lam_ref, sn_ref, yd_ref, q_s, k_s, v_s) = refs
        past = 0
    lo = _lane_group_mask(LANES, 64, 0)
    if has_ctx:
        tc, ts1, ts2 = tc_ref[...], ts1_ref[...], ts2_ref[...]
        k_s[0:past, :] = ck_ref[...].astype(BF16)
        cv = cv_ref[...]
    for half in range(2):
        cols = slice(half * LANES, (half + 1) * LANES)
        q = zd_ref[:, half * LANES:(half + 1) * LANES]
        k = zd_ref[:, D_D + half * LANES:D_D + (half + 1) * LANES]
        v = zd_ref[:, 2 * D_D + half * LANES:2 * D_D + (half + 1) * LANES]
        if has_ctx:
            q = _rope128(q, tc, ts1, ts2)
            k = _rope128(k, tc, ts1, ts2)
            v_s[2 * half, 0:past, :] = jnp.where(lo, cv[:, cols], 0.0).astype(BF16)
            v_s[2 * half + 1, 0:past, :] = jnp.where(lo, 0.0, cv[:, cols]).astype(BF16)
        q_s[:, cols] = (q * (DIFF_SCALE * LOG2E)).astype(BF16)
        k_s[past:past + seq, cols] = k.astype(BF16)
        v_s[2 * half, past:past + seq, :] = jnp.where(lo, v, 0.0).astype(BF16)
        v_s[2 * half + 1, past:past + seq, :] = jnp.where(lo, 0.0, v).astype(BF16)

    lam = lam_ref[...]
    lam_val = (jnp.exp(jnp.sum(lam[0:1] * lam[1:2], axis=-1, keepdims=True))
               - jnp.exp(jnp.sum(lam[2:3] * lam[3:4], axis=-1, keepdims=True)) + lam_init)
    comp_masks = [jnp.where(_lane_group_mask(D_D, DH_D, j), 1.0, 0.0).astype(BF16) for j in range(2 * H_D)]
    sn = sn_ref[...]

    def body(i, carry):
        rows = pl.ds(pl.multiple_of(i * tq, tq), tq)
        qb = q_s[rows, :]
        k_all = k_s[...]
        for pair in range(H_D // 2):
            heads = (2 * pair, 2 * pair + 1)
            s = [[_dot_nt(qb * comp_masks[2 * h + comp], k_all) for comp in range(2)] for h in heads]
            acc = jnp.zeros((tq, LANES), F32)
            for j, h in enumerate(heads):
                p1, p2 = [jnp.exp2(sc - jnp.max(sc, axis=-1, keepdims=True)) for sc in s[j]]
                l1 = jnp.sum(p1, axis=-1, keepdims=True)
                c = lam_val * l1 / jnp.sum(p2, axis=-1, keepdims=True)
                acc = acc + _dot((p1 - p2 * c).astype(BF16), v_s[h]) / l1
            sq = acc * acc
            ss_lo = jnp.sum(jnp.where(lo, sq, 0.0), axis=-1, keepdims=True)
            ss_hi = jnp.sum(jnp.where(lo, 0.0, sq), axis=-1, keepdims=True)
            r = jnp.where(lo, lax.rsqrt(ss_lo / (2 * DH_D) + EPS), lax.rsqrt(ss_hi / (2 * DH_D) + EPS))
            yd_ref[rows, pair * LANES:(pair + 1) * LANES] = acc * r * sn * (1.0 - lam_init)
        return carry

    lax.fori_loop(0, seq // tq, body, 0)


def _diff(zd, lam, sn, layer, batch, seq, lam_init, rope=None, ctx=None):
    has_ctx = ctx is not None
    past = ctx[0].shape[2] if has_ctx else 0
    lk = past + seq
    tq = min(seq, 256)
    full = lambda shape: pl.BlockSpec(shape, lambda b: (0,) * len(shape))
    in_specs = [pl.BlockSpec((seq, IN_D), lambda b: (b, 0)),
                _layer_block((4, DH_D), layer), _layer_block((1, LANES), layer)]
    args = [zd, lam, sn]
    if has_ctx:
        in_specs += [full((seq, LANES))] * 3
        in_specs += [_ctx_block(past, D_D, layer)] * 2
        args += list(rope) + list(ctx)
    return pl.pallas_call(
        functools.partial(_diff_kernel, seq=seq, has_ctx=has_ctx, tq=tq, lam_init=lam_init),
        grid=(batch,),
        in_specs=in_specs,
        out_specs=pl.BlockSpec((seq, MIX_W), lambda b: (b, 0)),
        out_shape=jax.ShapeDtypeStruct((batch * seq, MIX_W), F32),
        scratch_shapes=[
            pltpu.VMEM((seq, D_D), BF16),
            pltpu.VMEM((lk, D_D), BF16),
            pltpu.VMEM((H_D, lk, LANES), BF16),
        ],
        compiler_params=pltpu.CompilerParams(vmem_limit_bytes=VMEM_LIMIT),
        name="diff_attn",
    )(*args)


def _split3(x):
    hi = x.astype(BF16)
    r1 = x - hi.astype(F32)
    mid = r1.astype(BF16)
    lo = (r1 - mid.astype(F32)).astype(BF16)
    return hi, mid, lo


ST_ROWS = DH_C + 8
N_UNITS = 2 * H_C


def _mlstm_kernel(*refs, seq, has_state):
    ck = MLSTM_CHUNK
    nc = seq // ck
    if has_state:
        (zc_ref, zg_ref, gb_ref, hn_ref, st0_ref, m0_ref,
         yc_ref, k_s, qt_s, kt_s, vt_s, rcc_s, ca_s, wi_s, en_s, ws_s, wo_s, st_s, ht_s) = refs
    else:
        (zc_ref, zg_ref, gb_ref, hn_ref,
         yc_ref, stout_ref, mout_ref, k_s, qt_s, kt_s, vt_s, rcc_s, ca_s, wi_s, en_s, ws_s, wo_s, st_s, ht_s) = refs

    for c in range(nc):
        rows = slice(c * ck, (c + 1) * ck)
        qt_s[c] = zc_ref[rows, 0:D_C].T.astype(BF16)
        kk = zc_ref[rows, D_C:2 * D_C] * (DH_C ** -0.5)
        k_s[rows, :] = kk.astype(BF16)
        kt_s[c] = kk.T.astype(BF16)
        vt_s[c] = zc_ref[rows, 2 * D_C:3 * D_C].T.astype(BF16)

    gates = zg_ref[...] + gb_ref[...]
    lf = _log_sigmoid(gates)
    r_i = lax.broadcasted_iota(jnp.int32, (ck, ck), 0)
    c_i = lax.broadcasted_iota(jnp.int32, (ck, ck), 1)
    tri_fw = jnp.where(c_i <= r_i, 1.0, 0.0).astype(BF16)
    tri_bw = jnp.where(c_i >= r_i, 1.0, 0.0).astype(BF16)
    fw_lane = (lax.broadcasted_iota(jnp.int32, (1, ZG_W), 1) % N_UNITS) < H_C
    i_rows, b_rows = [], []
    for c in range(nc):
        rows = slice(c * ck, (c + 1) * ck)
        p0, p1, p2 = _split3(lf[rows, :])
        b_fw = _dot(tri_fw, p0) + _dot(tri_fw, p1) + _dot(tri_fw, p2)
        b_bw = _dot(tri_bw, p0) + _dot(tri_bw, p1) + _dot(tri_bw, p2)
        b = jnp.where(fw_lane, b_fw, b_bw)
        g_c = gates[rows, :]
        rcc_s[rows, :] = g_c - pltpu.roll(b, ZG_W - N_UNITS, 1)
        i_rows.append(g_c.T[0:N_UNITS, :])
        b_rows.append(b.T[N_UNITS:2 * N_UNITS, :])

    row8 = lax.broadcasted_iota(jnp.int32, (N_UNITS, ck), 0)
    lane8 = lax.broadcasted_iota(jnp.int32, (N_UNITS, ck), 1)
    fw8 = row8 < H_C
    fw81 = fw8[:, 0:1]
    m_prev = m0_ref[:, 0:1] if has_state else jnp.zeros((N_UNITS, 1), F32)
    for i in range(nc):
        i8 = jnp.where(fw8, i_rows[i], i_rows[nc - 1 - i])
        b8 = jnp.where(fw8, b_rows[i], b_rows[nc - 1 - i])
        r8 = i8 - b8
        pm, sm = r8, r8
        sh = 1
        while sh < ck:
            pm = jnp.maximum(pm, jnp.where(lane8 >= sh, pltpu.roll(pm, sh, 1), NEG_INF))
            sm = jnp.maximum(sm, jnp.where(lane8 < ck - sh, pltpu.roll(sm, ck - sh, 1), NEG_INF))
            sh *= 2
        cm = jnp.where(fw8, pm, sm)
        b_t = jnp.where(fw81, b8[:, ck - 1:ck], b8[:, 0:1])
        cm_end = jnp.where(fw81, pm[:, ck - 1:ck], sm[:, 0:1])
        inter = b8 + m_prev
        mt = jnp.maximum(inter, b8 + cm)
        ca_s[i] = b8 - mt
        wi_s[i] = jnp.exp(inter - mt)
        en_s[i] = jnp.exp(-mt)
        m_new = b_t + jnp.maximum(m_prev, cm_end)
        ws_s[i] = jnp.exp(b_t + r8 - m_new)
        wo_s[i] = jnp.broadcast_to(jnp.exp(b_t + m_prev - m_new), (N_UNITS, ck))
        m_prev = m_new

    if has_state:
        st_s[...] = st0_ref[...]
    else:
        st_s[...] = jnp.zeros_like(st_s)

    hmask_bf = [jnp.where(_lane_group_mask(D_C, DH_C, h), 1.0, 0.0).astype(BF16) for h in range(H_C)]
    src_ok = (r_i <= c_i, r_i >= c_i)

    def body(i, carry):
        ca, wi, en, ws, wo = ca_s[i], wi_s[i], en_s[i], ws_s[i], wo_s[i]
        chunks = (i, nc - 1 - i)
        units = [(d, h) for d in range(2) for h in range(H_C)]
        hs = [slice(h * DH_C, (h + 1) * DH_C) for h in range(H_C)]
        q_t = [qt_s[c] for c in chunks]
        k_t = [kt_s[c] for c in chunks]
        v_t = [vt_s[c] for c in chunks]
        sts, qcts, states, upds = [], [], [], []
        for d, h in units:
            rows = pl.ds(pl.multiple_of(chunks[d] * ck, ck), ck)
            sts.append(_dot(k_s[rows, :] * hmask_bf[h], q_t[d]))
            states.append(st_s[d * H_C + h])
            qcts.append(_dot(states[-1].astype(BF16), q_t[d][hs[h], :]))
        for u, (d, h) in enumerate(units):
            w_s = ws[u:u + 1, :]
            vw = jnp.concatenate([v_t[d][hs[h], :].astype(F32) * w_s, jnp.broadcast_to(w_s, (8, ck))], axis=0)
            upds.append(_dot_nt(vw.astype(BF16), k_t[d][hs[h], :]))
        pts = []
        for u, (d, h) in enumerate(units):
            rows = pl.ds(pl.multiple_of(chunks[d] * ck, ck), ck)
            decay = jnp.exp(rcc_s[rows, u:u + 1] + ca[u:u + 1, :])
            pts.append(sts[u] * jnp.where(src_ok[d], decay, 0.0))
        nums = [_dot(v_t[d][hs[h], :], pts[u].astype(BF16)) for u, (d, h) in enumerate(units)]
        for u, (d, h) in enumerate(units):
            w_i = wi[u:u + 1, :]
            den = w_i * qcts[u][DH_C:DH_C + 1, :] + jnp.sum(pts[u], axis=0, keepdims=True)
            num = w_i * qcts[u][0:DH_C, :] + nums[u]
            ht_s[d, chunks[d], hs[h], :] = num / jnp.maximum(jnp.abs(den), en[u:u + 1, :])
            st_s[u] = wo[u:u + 1, 0:DH_C] * states[u] + upds[u]
        return carry

    lax.fori_loop(0, nc, body, 0)

    hn = hn_ref[...]
    for c in range(nc):
        rows = slice(c * ck, (c + 1) * ck)
        h_t = ht_s[0, c] + ht_s[1, c]
        parts = []
        for h in range(H_C):
            x = h_t[h * DH_C:(h + 1) * DH_C, :]
            parts.append(x * lax.rsqrt(jnp.mean(x * x, axis=0, keepdims=True) + EPS))
        h_n = jnp.concatenate(parts, axis=0).T
        yc_ref[rows, :] = _sigmoid(zc_ref[rows, 3 * D_C:4 * D_C]) * (h_n * hn)

    if not has_state:
        stout_ref[...] = st_s[...]
        mout_ref[...] = jnp.broadcast_to(m_prev, (N_UNITS, LANES))


def _mlstm(zc, zg, gb, hn, layer, batch, seq, state=None):
    has_state = state is not None
    nc = seq // MLSTM_CHUNK
    st_spec = pl.BlockSpec((None, N_UNITS, ST_ROWS, DH_C), lambda b: (b, 0, 0, 0))
    m_spec = pl.BlockSpec((None, N_UNITS, LANES), lambda b: (b, 0, 0))
    in_specs = [pl.BlockSpec((seq, 4 * D_C), lambda b: (b, 0)), pl.BlockSpec((seq, ZG_W), lambda b: (b, 0)),
                _layer_block((1, ZG_W), layer), _layer_block((1, D_C), layer)]
    args = [zc, zg, gb, hn]
    out_specs = [pl.BlockSpec((seq, MIX_W), lambda b: (b, 0))]
    out_shape = [jax.ShapeDtypeStruct((batch * seq, MIX_W), F32)]
    if has_state:
        in_specs += [pl.BlockSpec((None, None, N_UNITS, ST_ROWS, DH_C), lambda b: (b, layer, 0, 0, 0)),
                     pl.BlockSpec((None, None, N_UNITS, LANES), lambda b: (b, layer, 0, 0))]
        args += list(state)
    else:
        out_specs += [st_spec, m_spec]
        out_shape += [jax.ShapeDtypeStruct((batch, N_UNITS, ST_ROWS, DH_C), F32),
                      jax.ShapeDtypeStruct((batch, N_UNITS, LANES), F32)]
    step = lambda: pltpu.VMEM((nc, N_UNITS, MLSTM_CHUNK), F32)
    return pl.pallas_call(
        functools.partial(_mlstm_kernel, seq=seq, has_state=has_state),
        grid=(batch,),
        in_specs=in_specs,
        out_specs=out_specs,
        out_shape=out_shape,
        scratch_shapes=[
            pltpu.VMEM((seq, D_C), BF16),
            pltpu.VMEM((nc, D_C, MLSTM_CHUNK), BF16), pltpu.VMEM((nc, D_C, MLSTM_CHUNK), BF16),
            pltpu.VMEM((nc, D_C, MLSTM_CHUNK), BF16),
            pltpu.VMEM((seq, ZG_W), F32),
            step(), step(), step(), step(), step(),
            pltpu.VMEM((N_UNITS, ST_ROWS, DH_C), F32),
            pltpu.VMEM((2, nc, D_C, MLSTM_CHUNK), F32),
        ],
        compiler_params=pltpu.CompilerParams(vmem_limit_bytes=VMEM_LIMIT),
        name="mlstm",
    )(*args)


def _merge_kernel(x_ref, mod_ref, g_ref, ya_ref, yb_ref, yc_ref, yd_ref, wm_ref, bm_ref, wb_ref, wo_ref,
                  w1_ref, w2_ref, o_ref):
    x = x_ref[...]
    h = (_rms(x, g_ref[0:1, :]) * (1.0 + mod_ref[1:2, :]) + mod_ref[0:1, :]).astype(BF16)
    acc = jnp.zeros(x.shape, F32)
    for n, y_ref in enumerate((ya_ref, yb_ref, yc_ref, yd_ref)):
        cols = slice(n * D_MODEL, (n + 1) * D_MODEL)
        gate = _sigmoid(_dot(h, wm_ref[:, cols]) + bm_ref[:, cols])
        acc = acc + gate * _dot(y_ref[...].astype(BF16), wb_ref[n])
    y = _dot(acc.astype(BF16), wo_ref[...])
    x = x + mod_ref[2:3, :] * _rms(y, g_ref[1:2, :])
    h2 = (_rms(x, g_ref[2:3, :]) * (1.0 + mod_ref[4:5, :]) + mod_ref[3:4, :]).astype(BF16)
    f = jnp.zeros(x.shape, F32)
    for j in range(D_FF // D_MODEL):
        cols = slice(j * D_MODEL, (j + 1) * D_MODEL)
        a = jnp.maximum(_dot(h2, w1_ref[:, cols]), 0.0)
        f = f + _dot((a * a).astype(BF16), w2_ref[cols, :])
    o_ref[...] = x + mod_ref[5:6, :] * _rms(f, g_ref[3:4, :])


def _merge_ffn(x, mod, g, ys, wm, bm, wb, wo, w1, w2, layer, seq, per_seq):
    t = x.shape[0]
    tm = TOKEN_TILE
    full = lambda shape: _layer_block(shape, layer)
    tok = lambda w_: pl.BlockSpec((tm, w_), lambda i: (i, 0))
    return pl.pallas_call(
        _merge_kernel,
        grid=(t // tm,),
        in_specs=[tok(D_MODEL), _mod_spec(layer, tm, seq, per_seq),
                  full((4, D_MODEL)), tok(MIX_W), tok(MIX_W), tok(MIX_W), tok(MIX_W),
                  full((D_MODEL, N_BRANCH * D_MODEL)), full((1, N_BRANCH * D_MODEL)),
                  full((N_BRANCH, MIX_W, D_MODEL)), full((D_MODEL, D_MODEL)),
                  full((D_MODEL, D_FF)), full((D_FF, D_MODEL))],
        out_specs=tok(D_MODEL),
        out_shape=jax.ShapeDtypeStruct((t, D_MODEL), F32),
        compiler_params=pltpu.CompilerParams(vmem_limit_bytes=VMEM_LIMIT),
        name="merge_ffn",
    )(x, mod, g, *ys, wm, bm, wb, wo, w1, w2)


def _arrange_w_in(w):
    z = lambda n: jnp.zeros((D_MODEL, n), w.dtype)
    o_b, o_c = IN_A, IN_A + IN_B
    o_g, o_d = o_c + 4 * D_C, o_c + IN_C
    gates = w[:, o_g:o_d].reshape(D_MODEL, 2, 2, H_C).transpose(0, 2, 1, 3).reshape(D_MODEL, 4 * H_C)
    return jnp.concatenate([
        w[:, :Q_LORA + KV_LORA], z(64), w[:, Q_LORA + KV_LORA:IN_A], z(32),
        w[:, o_b:o_g], gates, z(ZG_W - 4 * H_C), w[:, o_d:]], axis=1).astype(BF16)


def _arrange_w_uq(w):
    w = w.reshape(Q_LORA, H_A, NOPE_A + ROPE_A)
    return jnp.pad(w, ((0, 0), (0, 0), (0, LANES - NOPE_A - ROPE_A))).reshape(Q_LORA, H_A * LANES).astype(BF16)


def _arrange_w_ukv(w):
    w = w.reshape(KV_LORA, H_A, NOPE_A + VH_A)
    wk = jnp.pad(w[:, :, :NOPE_A], ((0, 0), (0, 0), (0, LANES - NOPE_A)))
    v = w[:, :, NOPE_A:]
    zero = jnp.zeros_like(v)
    even = jnp.concatenate([v, zero], axis=-1)
    odd = jnp.concatenate([zero, v], axis=-1)
    wv = jnp.where((jnp.arange(H_A) % 2 == 0)[None, :, None], even, odd)
    return jnp.concatenate([wk.reshape(KV_LORA, -1), wv.reshape(KV_LORA, -1)], axis=1).astype(BF16)


def _rope_tables(rows):
    row = np.repeat(np.arange(rows), GRID_W).astype(np.float64)
    col = np.tile(np.arange(GRID_W), rows).astype(np.float64)
    nf = ROPE_A // 4
    inv = np.exp(-math.log(ROPE_BASE) * np.arange(nf, dtype=np.float64) / nf)
    ang = np.concatenate([row[:, None] * inv, col[:, None] * inv], axis=-1)
    cos, sin = np.cos(ang), np.sin(ang)
    n = cos.shape[0]
    ones, zeros = (lambda k: np.ones((n, k))), (lambda k: np.zeros((n, k)))
    mla = (np.concatenate([ones(64), cos, cos, ones(32)], axis=1),
           np.concatenate([zeros(64), -sin, zeros(48)], axis=1),
           np.concatenate([zeros(80), sin, zeros(32)], axis=1))
    dif = (np.tile(np.concatenate([cos, cos], axis=1), (1, 4)),
           np.tile(np.concatenate([-sin, zeros(16)], axis=1), (1, 4)),
           np.tile(np.concatenate([zeros(16), sin], axis=1), (1, 4)))
    as_f32 = lambda ts: tuple(jnp.asarray(t.astype(np.float32)) for t in ts)
    return as_f32(mla), as_f32(dif)


def _pack_state(c, n):
    b = c.shape[0]
    ct = jnp.swapaxes(c, -1, -2)
    pad = jnp.zeros((b, 2, H_C, ST_ROWS - DH_C - 1, DH_C), c.dtype)
    return jnp.concatenate([ct, n[:, :, :, None, :], pad], axis=3).reshape(b, N_UNITS, ST_ROWS, DH_C)


def _unpack_state(st):
    b = st.shape[0]
    st = st.reshape(b, 2, H_C, ST_ROWS, DH_C)
    return jnp.swapaxes(st[:, :, :, :DH_C, :], -1, -2), st[:, :, :, DH_C, :]


def kernel(x_prompt, x_sample, cache_mla_ckv, cache_mla_krope, cache_diff_k, cache_diff_v, state_mlstm_C,
           state_mlstm_n, state_mlstm_m, c, c_ctx, w_mod, b_mod, norm_g, w_in, mla_q_norm, w_uq, mla_kv_norm,
           w_ukv, gmlp_v_norm, gmlp_w_s, gmlp_b_s, mlstm_gate_bias, mlstm_head_norm, diff_lambda, diff_sub_norm,
           w_branch, w_merge, b_merge, w_out, w_ff1, w_ff2):
    bp, lp, _ = x_prompt.shape
    bs, ls, _ = x_sample.shape
    past = cache_mla_ckv.shape[2]

    cond = jnp.concatenate([c_ctx[None, :], c, jnp.zeros((16 - 1 - bs, D_MODEL), F32)], axis=0)
    mod = _modulation(cond, w_mod, b_mod).reshape(DEPTH, 16, 6, D_MODEL)
    rope_mla, rope_dif = _rope_tables(ls // GRID_W)

    row = lambda a: a[:, None, :]
    w_in_a = jax.vmap(_arrange_w_in)(w_in)
    wuq_a = jax.vmap(_arrange_w_uq)(w_uq)
    wukv_a = jax.vmap(_arrange_w_ukv)(w_ukv)
    qn, kvn, vn, hn, bm = row(mla_q_norm), row(mla_kv_norm), row(gmlp_v_norm), row(mlstm_head_norm), row(b_merge)
    ws = gmlp_w_s.astype(BF16)
    bias = jnp.repeat(jnp.swapaxes(gmlp_b_s, 1, 2), D_B // G_B, axis=2)
    gb = jnp.pad(mlstm_gate_bias.transpose(0, 2, 1, 3).reshape(DEPTH, 1, 4 * H_C),
                 ((0, 0), (0, 0), (0, ZG_W - 4 * H_C)))
    sn = row(jnp.tile(diff_sub_norm, (1, 2)))
    wm, wb, wo = w_merge.astype(BF16), w_branch.astype(BF16), w_out.astype(BF16)
    w1, w2 = w_ff1.astype(BF16), w_ff2.astype(BF16)

    ctx_kr = jnp.pad(cache_mla_krope, ((0, 0), (0, 0), (0, 0), (64, LANES - 64 - ROPE_A)))
    ctx_dk = cache_diff_k.reshape(bs, DEPTH, past, D_D)
    ctx_dv = cache_diff_v.reshape(bs, DEPTH, past, D_D)
    st0 = jax.vmap(_pack_state, in_axes=1, out_axes=1)(state_mlstm_C, state_mlstm_n)
    m0 = jnp.broadcast_to(state_mlstm_m.reshape(bs, DEPTH, N_UNITS, 1), (bs, DEPTH, N_UNITS, LANES))

    xp = x_prompt.reshape(bp * lp, D_MODEL)
    xs = x_sample.reshape(bs * ls, D_MODEL)
    ents = []
    for l in range(DEPTH):
        lam_init = 0.8 - 0.6 * math.exp(-0.3 * l)
        for is_sample in (False, True):
            x = xs if is_sample else xp
            batch, seq = (bs, ls) if is_sample else (bp, lp)
            za, yb, zc, zg, zd = _in_proj(x, mod, norm_g, w_in_a, vn, ws, bias, l, seq, is_sample)
            if is_sample:
                (ya,) = _mla(za, qn, kvn, wuq_a, wukv_a, l, batch, seq, rope=rope_mla, ctx=(cache_mla_ckv, ctx_kr))
                yd = _diff(zd, diff_lambda, sn, l, batch, seq, lam_init, rope=rope_dif, ctx=(ctx_dk, ctx_dv))
                (yc,) = _mlstm(zc, zg, gb, hn, l, batch, seq, state=(st0, m0))
            else:
                ya, ckv = _mla(za, qn, kvn, wuq_a, wukv_a, l, batch, seq)
                yd = _diff(zd, diff_lambda, sn, l, batch, seq, lam_init)
                yc, st_new, m_new = _mlstm(zc, zg, gb, hn, l, batch, seq)
                c_new, n_new = _unpack_state(st_new)
                ents.append((
                    ckv.reshape(bp, lp, KV_LORA),
                    za[:, Q_LORA + KV_LORA + 64:Q_LORA + KV_LORA + 64 + ROPE_A].reshape(bp, lp, ROPE_A),
                    zd[:, D_D:2 * D_D].reshape(bp, lp, H_D, 2, DH_D),
                    zd[:, 2 * D_D:].reshape(bp, lp, H_D, 2 * DH_D),
                    c_new, n_new,
                    m_new[:, :, 0].reshape(bp, 2, H_C)))
            x = _merge_ffn(x, mod, norm_g, (ya, yb, yc, yd), wm, bm, wb, wo, w1, w2, l, seq, is_sample)
            if is_sample:
                xs = x
            else:
                xp = x

    stack = lambda j: jnp.stack([e[j] for e in ents], axis=1)
    return (xp.reshape(bp, lp, D_MODEL), xs.reshape(bs, ls, D_MODEL),
            stack(0), stack(1), stack(2), stack(3), stack(4), stack(5), stack(6))
```

```python
import functools
import math

import jax
import jax.numpy as jnp
import numpy as np
from jax import lax
from jax.experimental import pallas as pl
from jax.experimental.pallas import tpu as pltpu

F32 = jnp.float32
BF16 = jnp.bfloat16

D_MODEL = 1024
DEPTH = 2
GRID_W = 64
N_BRANCH = 4
MIX_W = 256
H_A, NOPE_A, ROPE_A, VH_A = 4, 64, 32, 64
Q_LORA, KV_LORA = 256, 128
D_B, G_B, CHUNK_B = 256, 4, 128
H_C, DH_C = 4, 64
D_C = H_C * DH_C
H_D, DH_D = 4, 32
D_D = H_D * 2 * DH_D
D_FF = 4 * D_MODEL
IN_A = Q_LORA + KV_LORA + ROPE_A
IN_B = 2 * D_B
IN_C = 4 * D_C + 4 * H_C
IN_D = 3 * D_D
ROPE_BASE = 10000.0
EPS = 1e-6
MLA_SCALE = (NOPE_A + ROPE_A) ** -0.5
DIFF_SCALE = DH_D ** -0.5

LANES = 128
MLSTM_CHUNK = 128
VMEM_LIMIT = 56 * 1024 * 1024
NEG_INF = float("-inf")

ZA_W = 512
ZG_W = 128
Z_OFF_A = 0
Z_OFF_B = Z_OFF_A + ZA_W
Z_OFF_C = Z_OFF_B + IN_B
Z_OFF_G = Z_OFF_C + 4 * D_C
Z_OFF_D = Z_OFF_G + ZG_W
Z_W = Z_OFF_D + IN_D


def _rms(x, g):
    return x * lax.rsqrt(jnp.mean(x * x, axis=-1, keepdims=True) + EPS) * g


def _sigmoid(x):
    return 1.0 / (1.0 + jnp.exp(-x))


def _log_sigmoid(x):
    return jnp.minimum(x, 0.0) - jnp.log1p(jnp.exp(-jnp.abs(x)))


def _dot(a, b):
    return jnp.dot(a, b, preferred_element_type=F32)


def _dot_nt(a, b):
    return lax.dot_general(a, b, (((1,), (1,)), ((), ())), preferred_element_type=F32)


def _lane_group_mask(width, group, index):
    lane = lax.broadcasted_iota(jnp.int32, (1, width), 1)
    return (lane >= index * group) & (lane < (index + 1) * group)


def _mod_kernel(cond_ref, w_ref, b_ref, o_ref):
    c = cond_ref[...]
    s = c * _sigmoid(c)
    o_ref[...] = _dot(s.astype(BF16), w_ref[...].astype(BF16)) + b_ref[...]


def _modulation(cond, w_mod, b_mod):
    rows = cond.shape[0]
    nb = 1024
    return pl.pallas_call(
        _mod_kernel,
        grid=(DEPTH, 6 * D_MODEL // nb),
        in_specs=[
            pl.BlockSpec((rows, D_MODEL), lambda l, j: (0, 0)),
            pl.BlockSpec((None, D_MODEL, nb), lambda l, j: (l, 0, j)),
            pl.BlockSpec((None, 1, nb), lambda l, j: (l, 0, j)),
        ],
        out_specs=pl.BlockSpec((None, rows, nb), lambda l, j: (l, 0, j)),
        out_shape=jax.ShapeDtypeStruct((DEPTH, rows, 6 * D_MODEL), F32),
        name="modulation",
    )(cond, w_mod, b_mod.reshape(DEPTH, 1, 6 * D_MODEL))


TOKEN_TILE = 512


def _layer_block(shape, layer):
    shape = tuple(shape)
    return pl.BlockSpec((None,) + shape, lambda *_: (layer,) + (0,) * len(shape), pipeline_mode=pl.Buffered(1))


def _mod_spec(layer, tm, seq, per_seq):
    def index(i):
        return (layer, 1 + (i * tm) // seq if per_seq else 0, 0, 0)
    return pl.BlockSpec((None, None, 6, D_MODEL), index)


def _gmlp_tile(zb, vn, ws_ref, bias):
    tm = zb.shape[0]
    v = _rms(zb[:, D_B:], vn).astype(BF16)
    gmasks = [_lane_group_mask(D_B, D_B // G_B, g) for g in range(G_B)]
    out = []
    for ch in range(tm // CHUNK_B):
        vc = v[ch * CHUNK_B:(ch + 1) * CHUNK_B, :]
        mixed = bias
        for g in range(G_B):
            mixed = mixed + jnp.where(gmasks[g], _dot(ws_ref[g], vc), 0.0)
        out.append(zb[ch * CHUNK_B:(ch + 1) * CHUNK_B, :D_B] * mixed)
    return jnp.concatenate(out, axis=0)


def _in_kernel(x_ref, mod_ref, g_ref, w_ref, vn_ref, ws_ref, bias_ref, za_ref, yb_ref, zc_ref, zg_ref, zd_ref):
    x = x_ref[...]
    h = _rms(x, g_ref[0:1, :]) * (1.0 + mod_ref[1:2, :]) + mod_ref[0:1, :]
    hb = h.astype(BF16)
    za_ref[...] = _dot(hb, w_ref[:, Z_OFF_A:Z_OFF_B])
    zc_ref[...] = _dot(hb, w_ref[:, Z_OFF_C:Z_OFF_G])
    zg_ref[...] = _dot(hb, w_ref[:, Z_OFF_G:Z_OFF_D])
    zd_ref[...] = _dot(hb, w_ref[:, Z_OFF_D:Z_W])
    yb_ref[...] = _gmlp_tile(_dot(hb, w_ref[:, Z_OFF_B:Z_OFF_C]), vn_ref[...], ws_ref, bias_ref[...])


def _in_proj(x, mod, g, w, vn, ws, bias, layer, seq, per_seq):
    t = x.shape[0]
    tm = TOKEN_TILE
    widths = (ZA_W, MIX_W, 4 * D_C, ZG_W, IN_D)
    return pl.pallas_call(
        _in_kernel,
        grid=(t // tm,),
        in_specs=[
            pl.BlockSpec((tm, D_MODEL), lambda i: (i, 0)),
            _mod_spec(layer, tm, seq, per_seq),
            _layer_block((4, D_MODEL), layer),
            _layer_block((D_MODEL, Z_W), layer),
            _layer_block((1, D_B), layer),
            _layer_block((G_B, CHUNK_B, CHUNK_B), layer),
            _layer_block((CHUNK_B, D_B), layer),
        ],
        out_specs=[pl.BlockSpec((tm, w_), lambda i: (i, 0)) for w_ in widths],
        out_shape=[jax.ShapeDtypeStruct((t, w_), F32) for w_ in widths],
        compiler_params=pltpu.CompilerParams(vmem_limit_bytes=VMEM_LIMIT),
        name="in_proj",
    )(x, mod, g, w, vn, ws, bias)


LOG2E = math.log2(math.e)


def _with_ones(v, lo_mask, keep_lo):
    lane = lax.broadcasted_iota(jnp.int32, (1, LANES), 1)
    if keep_lo:
        return jnp.where(lane == 64, 1.0, jnp.where(lo_mask, v, 0.0))
    return jnp.where(lane == 0, 1.0, jnp.where(lo_mask, 0.0, v))


def _pair_normalise(o_even, o_odd, lo_mask):
    return jnp.where(lo_mask, o_even / o_even[:, 64:65], o_odd / o_odd[:, 0:1])


def _rope128(x, tc, ts1, ts2):
    return x * tc + pltpu.roll(x, LANES - 16, 1) * ts1 + pltpu.roll(x, 16, 1) * ts2


def _mla_kernel(*refs, seq, has_ctx, tq):
    if has_ctx:
        (za_ref, qn_ref, kvn_ref, wuq_ref, wukv_ref, tc_ref, ts1_ref, ts2_ref, cckv_ref, ckr_ref,
         ya_ref, q_s, k_s, v_s, ckv_s, kr_s) = refs
        past = cckv_ref.shape[0]
    else:
        (za_ref, qn_ref, kvn_ref, wuq_ref, wukv_ref, ya_ref, ckv_ref, q_s, k_s, v_s, ckv_s, kr_s) = refs
        past = 0
    za = za_ref[...]
    cq = _rms(za[:, :Q_LORA], qn_ref[...])
    qh = _dot(cq.astype(BF16), wuq_ref[:, 0:(2 if has_ctx else 1) * H_A * LANES])
    ckv = _rms(za[:, Q_LORA:Q_LORA + KV_LORA], kvn_ref[...])
    kr = za[:, Q_LORA + KV_LORA:ZA_W]
    if has_ctx:
        tc, ts1, ts2 = tc_ref[...], ts1_ref[...], ts2_ref[...]
        kr = _rope128(kr, tc, ts1, ts2)
        ckv_s[0:past, :] = cckv_ref[...]
        kr_s[0:past, :] = ckr_ref[...]
    else:
        ckv_ref[...] = ckv
    ckv_s[past:past + seq, :] = ckv
    kr_s[past:past + seq, :] = kr
    kv = _dot(ckv_s[...].astype(BF16), wukv_ref[...])
    kr_all = kr_s[...]
    lo = _lane_group_mask(LANES, 64, 0)
    for h in range(H_A):
        qg = qh[:, h * LANES:(h + 1) * LANES]
        if has_ctx:
            qg = qg * tc + qh[:, (H_A + h) * LANES:(H_A + h + 1) * LANES] * (ts1 + ts2)
        q_s[h] = (qg * (MLA_SCALE * LOG2E)).astype(BF16)
        k_s[h] = (kv[:, h * LANES:(h + 1) * LANES] + kr_all).astype(BF16)
        v_s[h] = _with_ones(kv[:, (H_A + h) * LANES:(H_A + h + 1) * LANES], lo, h % 2 == 0).astype(BF16)

    def body(i, carry):
        rows = pl.ds(pl.multiple_of(i * tq, tq), tq)
        s = [_dot_nt(q_s[h, rows, :], k_s[h]) for h in range(H_A)]
        o = [_dot(jnp.exp2(s[h] - jnp.max(s[h], axis=-1, keepdims=True)).astype(BF16), v_s[h]) for h in range(H_A)]
        for pair in range(H_A // 2):
            ya_ref[rows, pair * LANES:(pair + 1) * LANES] = _pair_normalise(o[2 * pair], o[2 * pair + 1], lo)
        return carry

    lax.fori_loop(0, seq // tq, body, 0)


def _ctx_block(past, width, layer):
    return pl.BlockSpec((None, None, past, width), lambda b: (b, layer, 0, 0))


def _mla(za, qn, kvn, wuq, wukv, layer, batch, seq, rope=None, ctx=None):
    has_ctx = ctx is not None
    past = ctx[0].shape[2] if has_ctx else 0
    lk = past + seq
    tq = min(seq, 256)
    full = lambda shape: pl.BlockSpec(shape, lambda b: (0,) * len(shape))
    in_specs = [
        pl.BlockSpec((seq, ZA_W), lambda b: (b, 0)),
        _layer_block((1, Q_LORA), layer), _layer_block((1, KV_LORA), layer),
        _layer_block((Q_LORA, 2 * H_A * LANES), layer), _layer_block((KV_LORA, 2 * H_A * LANES), layer),
    ]
    args = [za, qn, kvn, wuq, wukv]
    out_specs = [pl.BlockSpec((seq, MIX_W), lambda b: (b, 0))]
    out_shape = [jax.ShapeDtypeStruct((batch * seq, MIX_W), F32)]
    if has_ctx:
        in_specs += [full((seq, LANES))] * 3
        in_specs += [_ctx_block(past, LANES, layer)] * 2
        args += list(rope) + list(ctx)
    else:
        out_specs.append(pl.BlockSpec((seq, KV_LORA), lambda b: (b, 0)))
        out_shape.append(jax.ShapeDtypeStruct((batch * seq, KV_LORA), F32))
    return pl.pallas_call(
        functools.partial(_mla_kernel, seq=seq, has_ctx=has_ctx, tq=tq),
        grid=(batch,),
        in_specs=in_specs,
        out_specs=out_specs,
        out_shape=out_shape,
        scratch_shapes=[
            pltpu.VMEM((H_A, seq, LANES), BF16),
            pltpu.VMEM((H_A, lk, LANES), BF16),
            pltpu.VMEM((H_A, lk, LANES), BF16),
            pltpu.VMEM((lk, LANES), F32),
            pltpu.VMEM((lk, LANES), F32),
        ],
        compiler_params=pltpu.CompilerParams(vmem_limit_bytes=VMEM_LIMIT),
        name="mla",
    )(*args)


def _diff_kernel(*refs, seq, has_ctx, tq, lam_init):
    if has_ctx:
        (zd_ref, lam_ref, sn_ref, tc_ref, ts1_ref, ts2_ref, ck_ref, cv_ref,
         yd_ref, q_s, k_s, v_s) = refs
        past = ck_ref.shape[0]
    else:
        (zd_ref, lam_ref, sn_ref, yd_ref, q_s, k_s, v_s) = refs
        past = 0
    lo = _lane_group_mask(LANES, 64, 0)
    if has_ctx:
        tc, ts1, ts2 = tc_ref[...], ts1_ref[...], ts2_ref[...]
        k_s[0:past, :] = ck_ref[...].astype(BF16)
        cv = cv_ref[...]
    for half in range(2):
        cols = slice(half * LANES, (half + 1) * LANES)
        q = zd_ref[:, half * LANES:(half + 1) * LANES]
        k = zd_ref[:, D_D + half * LANES:D_D + (half + 1) * LANES]
        v = zd_ref[:, 2 * D_D + half * LANES:2 * D_D + (half + 1) * LANES]
        if has_ctx:
            q = _rope128(q, tc, ts1, ts2)
            k = _rope128(k, tc, ts1, ts2)
            v_s[2 * half, 0:past, :] = _with_ones(cv[:, cols], lo, True).astype(BF16)
            v_s[2 * half + 1, 0:past, :] = _with_ones(cv[:, cols], lo, False).astype(BF16)
        q_s[:, cols] = (q * (DIFF_SCALE * LOG2E)).astype(BF16)
        k_s[past:past + seq, cols] = k.astype(BF16)
        v_s[2 * half, past:past + seq, :] = _with_ones(v, lo, True).astype(BF16)
        v_s[2 * half + 1, past:past + seq, :] = _with_ones(v, lo, False).astype(BF16)

    lam = lam_ref[...]
    lam_val = (jnp.exp(jnp.sum(lam[0:1] * lam[1:2], axis=-1, keepdims=True))
               - jnp.exp(jnp.sum(lam[2:3] * lam[3:4], axis=-1, keepdims=True)) + lam_init)
    comp_masks = [jnp.where(_lane_group_mask(D_D, DH_D, j), 1.0, 0.0).astype(BF16) for j in range(2 * H_D)]
    sn = sn_ref[...]

    def body(i, carry):
        rows = pl.ds(pl.multiple_of(i * tq, tq), tq)
        qb = q_s[rows, :]
        k_all = k_s[...]
        score = lambda h: [_dot_nt(qb * comp_masks[2 * h + comp], k_all) for comp in range(2)]
        s_next = score(0)
        o = []
        for h in range(H_D):
            s = s_next
            if h + 1 < H_D:
                s_next = score(h + 1)
            o1, o2 = [_dot(jnp.exp2(sc - jnp.max(sc, axis=-1, keepdims=True)).astype(BF16), v_s[h]) for sc in s]
            ll = 64 if h % 2 == 0 else 0
            o.append(o1 / o1[:, ll:ll + 1] - lam_val * (o2 / o2[:, ll:ll + 1]))
        for pair in range(H_D // 2):
            acc = jnp.where(lo, o[2 * pair], o[2 * pair + 1])
            sq = acc * acc
            ss_lo = jnp.sum(jnp.where(lo, sq, 0.0), axis=-1, keepdims=True)
            ss_hi = jnp.sum(jnp.where(lo, 0.0, sq), axis=-1, keepdims=True)
            r = jnp.where(lo, lax.rsqrt(ss_lo / (2 * DH_D) + EPS), lax.rsqrt(ss_hi / (2 * DH_D) + EPS))
            yd_ref[rows, pair * LANES:(pair + 1) * LANES] = acc * r * sn * (1.0 - lam_init)
        return carry

    lax.fori_loop(0, seq // tq, body, 0)


def _diff(zd, lam, sn, layer, batch, seq, lam_init, rope=None, ctx=None):
    has_ctx = ctx is not None
    past = ctx[0].shape[2] if has_ctx else 0
    lk = past + seq
    tq = min(seq, 256)
    full = lambda shape: pl.BlockSpec(shape, lambda b: (0,) * len(shape))
    in_specs = [pl.BlockSpec((seq, IN_D), lambda b: (b, 0)),
                _layer_block((4, DH_D), layer), _layer_block((1, LANES), layer)]
    args = [zd, lam, sn]
    if has_ctx:
        in_specs += [full((seq, LANES))] * 3
        in_specs += [_ctx_block(past, D_D, layer)] * 2
        args += list(rope) + list(ctx)
    return pl.pallas_call(
        functools.partial(_diff_kernel, seq=seq, has_ctx=has_ctx, tq=tq, lam_init=lam_init),
        grid=(batch,),
        in_specs=in_specs,
        out_specs=pl.BlockSpec((seq, MIX_W), lambda b: (b, 0)),
        out_shape=jax.ShapeDtypeStruct((batch * seq, MIX_W), F32),
        scratch_shapes=[
            pltpu.VMEM((seq, D_D), BF16),
            pltpu.VMEM((lk, D_D), BF16),
            pltpu.VMEM((H_D, lk, LANES), BF16),
        ],
        compiler_params=pltpu.CompilerParams(vmem_limit_bytes=VMEM_LIMIT),
        name="diff_attn",
    )(*args)


def _split3(x):
    hi = x.astype(BF16)
    r1 = x - hi.astype(F32)
    mid = r1.astype(BF16)
    lo = (r1 - mid.astype(F32)).astype(BF16)
    return hi, mid, lo


ST_ROWS = DH_C + 8
N_UNITS = 2 * H_C


def _mlstm_kernel(*refs, seq, has_state):
    ck = MLSTM_CHUNK
    nc = seq // ck
    if has_state:
        (zc_ref, zg_ref, gb_ref, hn_ref, st0_ref, m0_ref,
         yc_ref, k_s, qt_s, kt_s, vt_s, rcc_s, ca_s, wi_s, en_s, ws_s, wo_s, st_s, ht_s) = refs
    else:
        (zc_ref, zg_ref, gb_ref, hn_ref,
         yc_ref, stout_ref, mout_ref, k_s, qt_s, kt_s, vt_s, rcc_s, ca_s, wi_s, en_s, ws_s, wo_s, st_s, ht_s) = refs

    for c in range(nc):
        rows = slice(c * ck, (c + 1) * ck)
        qt_s[c] = zc_ref[rows, 0:D_C].T.astype(BF16)
        kk = zc_ref[rows, D_C:2 * D_C] * (DH_C ** -0.5)
        k_s[rows, :] = kk.astype(BF16)
        kt_s[c] = kk.T.astype(BF16)
        vt_s[c] = zc_ref[rows, 2 * D_C:3 * D_C].T.astype(BF16)

    gates = zg_ref[...] + gb_ref[...]
    r_i = lax.broadcasted_iota(jnp.int32, (ck, ck), 0)
    c_i = lax.broadcasted_iota(jnp.int32, (ck, ck), 1)
    sum_fw = jnp.where(r_i <= c_i, 1.0, 0.0).astype(BF16)
    sum_bw = jnp.where(r_i >= c_i, 1.0, 0.0).astype(BF16)
    row8 = lax.broadcasted_iota(jnp.int32, (N_UNITS, ck), 0)
    fw8 = row8 < H_C
    fw81 = fw8[:, 0:1]
    i_rows, b_rows = [], []
    for c in range(nc):
        rows = slice(c * ck, (c + 1) * ck)
        g_t = gates[rows, :].T
        i8 = g_t[0:N_UNITS, :]
        p0, p1, p2 = _split3(_log_sigmoid(g_t[N_UNITS:2 * N_UNITS, :]))
        b_fw = _dot(p0, sum_fw) + _dot(p1, sum_fw) + _dot(p2, sum_fw)
        b_bw = _dot(p0, sum_bw) + _dot(p1, sum_bw) + _dot(p2, sum_bw)
        b8 = jnp.where(fw8, b_fw, b_bw)
        rcc_s[rows, :] = jnp.concatenate([i8 - b8, jnp.zeros((ck - N_UNITS, ck), F32)], axis=0).T
        i_rows.append(i8)
        b_rows.append(b8)

    b_steps = [jnp.where(fw8, b_rows[i], b_rows[nc - 1 - i]) for i in range(nc)]
    r_steps = [jnp.where(fw8, i_rows[i], i_rows[nc - 1 - i]) - b_steps[i] for i in range(nc)]
    pm = sm = jnp.concatenate(r_steps, axis=0)
    lane_all = lax.broadcasted_iota(jnp.int32, pm.shape, 1)
    sh = 1
    while sh < ck:
        pm = jnp.maximum(pm, jnp.where(lane_all >= sh, pltpu.roll(pm, sh, 1), NEG_INF))
        sm = jnp.maximum(sm, jnp.where(lane_all < ck - sh, pltpu.roll(sm, ck - sh, 1), NEG_INF))
        sh *= 2
    m_prev = m0_ref[:, 0:1] if has_state else jnp.zeros((N_UNITS, 1), F32)
    for i in range(nc):
        b8, r8 = b_steps[i], r_steps[i]
        pm8, sm8 = pm[i * N_UNITS:(i + 1) * N_UNITS, :], sm[i * N_UNITS:(i + 1) * N_UNITS, :]
        cm = jnp.where(fw8, pm8, sm8)
        b_t = jnp.where(fw81, b8[:, ck - 1:ck], b8[:, 0:1])
        cm_end = jnp.where(fw81, pm8[:, ck - 1:ck], sm8[:, 0:1])
        inter = b8 + m_prev
        mt = jnp.maximum(inter, b8 + cm)
        ca_s[i] = b8 - mt
        wi_s[i] = jnp.exp(inter - mt)
        en_s[i] = jnp.exp(-mt)
        m_new = b_t + jnp.maximum(m_prev, cm_end)
        ws_s[i] = jnp.exp(b_t + r8 - m_new)
        wo_s[i] = jnp.broadcast_to(jnp.exp(b_t + m_prev - m_new), (N_UNITS, ck))
        m_prev = m_new

    if has_state:
        st_s[...] = st0_ref[...]
    else:
        st_s[...] = jnp.zeros_like(st_s)

    hmask_bf = [jnp.where(_lane_group_mask(D_C, DH_C, h), 1.0, 0.0).astype(BF16) for h in range(H_C)]
    src_ok = (r_i <= c_i, r_i >= c_i)

    def body(i, carry):
        ca, wi, en, ws, wo = ca_s[i], wi_s[i], en_s[i], ws_s[i], wo_s[i]
        chunks = (i, nc - 1 - i)
        units = [(d, h) for d in range(2) for h in range(H_C)]
        hs = [slice(h * DH_C, (h + 1) * DH_C) for h in range(H_C)]
        q_t = [qt_s[c] for c in chunks]
        k_t = [kt_s[c] for c in chunks]
        v_t = [vt_s[c] for c in chunks]
        sts, qcts, states, upds = [], [], [], []
        for d, h in units:
            rows = pl.ds(pl.multiple_of(chunks[d] * ck, ck), ck)
            sts.append(_dot(k_s[rows, :] * hmask_bf[h], q_t[d]))
            states.append(st_s[d * H_C + h])
            qcts.append(_dot(states[-1].astype(BF16), q_t[d][hs[h], :]))
        for u, (d, h) in enumerate(units):
            w_s = ws[u:u + 1, :]
            vw = jnp.concatenate([v_t[d][hs[h], :].astype(F32) * w_s, jnp.broadcast_to(w_s, (8, ck))], axis=0)
            upds.append(_dot_nt(vw.astype(BF16), k_t[d][hs[h], :]))
        pts = []
        for u, (d, h) in enumerate(units):
            rows = pl.ds(pl.multiple_of(chunks[d] * ck, ck), ck)
            decay = jnp.exp(rcc_s[rows, u:u + 1] + ca[u:u + 1, :])
            pts.append(sts[u] * jnp.where(src_ok[d], decay, 0.0))
        nums = [_dot(v_t[d][hs[h], :], pts[u].astype(BF16)) for u, (d, h) in enumerate(units)]
        for u, (d, h) in enumerate(units):
            w_i = wi[u:u + 1, :]
            den = w_i * qcts[u][DH_C:DH_C + 1, :] + jnp.sum(pts[u], axis=0, keepdims=True)
            num = w_i * qcts[u][0:DH_C, :] + nums[u]
            ht_s[d, chunks[d], hs[h], :] = num / jnp.maximum(jnp.abs(den), en[u:u + 1, :])
            st_s[u] = wo[u:u + 1, 0:DH_C] * states[u] + upds[u]
        return carry

    lax.fori_loop(0, nc, body, 0)

    hn = hn_ref[...]
    for c in range(nc):
        rows = slice(c * ck, (c + 1) * ck)
        h_t = ht_s[0, c] + ht_s[1, c]
        parts = []
        for h in range(H_C):
            x = h_t[h * DH_C:(h + 1) * DH_C, :]
            parts.append(x * lax.rsqrt(jnp.mean(x * x, axis=0, keepdims=True) + EPS))
        h_n = jnp.concatenate(parts, axis=0).T
        yc_ref[rows, :] = _sigmoid(zc_ref[rows, 3 * D_C:4 * D_C]) * (h_n * hn)

    if not has_state:
        stout_ref[...] = st_s[...]
        mout_ref[...] = jnp.broadcast_to(m_prev, (N_UNITS, LANES))


def _mlstm(zc, zg, gb, hn, layer, batch, seq, state=None):
    has_state = state is not None
    nc = seq // MLSTM_CHUNK
    st_spec = pl.BlockSpec((None, N_UNITS, ST_ROWS, DH_C), lambda b: (b, 0, 0, 0))
    m_spec = pl.BlockSpec((None, N_UNITS, LANES), lambda b: (b, 0, 0))
    in_specs = [pl.BlockSpec((seq, 4 * D_C), lambda b: (b, 0)), pl.BlockSpec((seq, ZG_W), lambda b: (b, 0)),
                _layer_block((1, ZG_W), layer), _layer_block((1, D_C), layer)]
    args = [zc, zg, gb, hn]
    out_specs = [pl.BlockSpec((seq, MIX_W), lambda b: (b, 0))]
    out_shape = [jax.ShapeDtypeStruct((batch * seq, MIX_W), F32)]
    if has_state:
        in_specs += [pl.BlockSpec((None, None, N_UNITS, ST_ROWS, DH_C), lambda b: (b, layer, 0, 0, 0)),
                     pl.BlockSpec((None, None, N_UNITS, LANES), lambda b: (b, layer, 0, 0))]
        args += list(state)
    else:
        out_specs += [st_spec, m_spec]
        out_shape += [jax.ShapeDtypeStruct((batch, N_UNITS, ST_ROWS, DH_C), F32),
                      jax.ShapeDtypeStruct((batch, N_UNITS, LANES), F32)]
    step = lambda: pltpu.VMEM((nc, N_UNITS, MLSTM_CHUNK), F32)
    return pl.pallas_call(
        functools.partial(_mlstm_kernel, seq=seq, has_state=has_state),
        grid=(batch,),
        in_specs=in_specs,
        out_specs=out_specs,
        out_shape=out_shape,
        scratch_shapes=[
            pltpu.VMEM((seq, D_C), BF16),
            pltpu.VMEM((nc, D_C, MLSTM_CHUNK), BF16), pltpu.VMEM((nc, D_C, MLSTM_CHUNK), BF16),
            pltpu.VMEM((nc, D_C, MLSTM_CHUNK), BF16),
            pltpu.VMEM((seq, ZG_W), F32),
            step(), step(), step(), step(), step(),
            pltpu.VMEM((N_UNITS, ST_ROWS, DH_C), F32),
            pltpu.VMEM((2, nc, D_C, MLSTM_CHUNK), F32),
        ],
        compiler_params=pltpu.CompilerParams(vmem_limit_bytes=VMEM_LIMIT),
        name="mlstm",
    )(*args)


def _merge_kernel(x_ref, mod_ref, g_ref, ya_ref, yb_ref, yc_ref, yd_ref, wm_ref, bm_ref, wb_ref, wo_ref,
                  w1_ref, w2_ref, o_ref):
    x = x_ref[...]
    h = (_rms(x, g_ref[0:1, :]) * (1.0 + mod_ref[1:2, :]) + mod_ref[0:1, :]).astype(BF16)
    acc = jnp.zeros(x.shape, F32)
    for n, y_ref in enumerate((ya_ref, yb_ref, yc_ref, yd_ref)):
        cols = slice(n * D_MODEL, (n + 1) * D_MODEL)
        gate = _sigmoid(_dot(h, wm_ref[:, cols]) + bm_ref[:, cols])
        acc = acc + gate * _dot(y_ref[...].astype(BF16), wb_ref[n])
    y = _dot(acc.astype(BF16), wo_ref[...])
    x = x + mod_ref[2:3, :] * _rms(y, g_ref[1:2, :])
    h2 = (_rms(x, g_ref[2:3, :]) * (1.0 + mod_ref[4:5, :]) + mod_ref[3:4, :]).astype(BF16)
    f = jnp.zeros(x.shape, F32)
    for j in range(D_FF // D_MODEL):
        cols = slice(j * D_MODEL, (j + 1) * D_MODEL)
        a = jnp.maximum(_dot(h2, w1_ref[:, cols]), 0.0)
        f = f + _dot((a * a).astype(BF16), w2_ref[cols, :])
    o_ref[...] = x + mod_ref[5:6, :] * _rms(f, g_ref[3:4, :])


def _merge_ffn(x, mod, g, ys, wm, bm, wb, wo, w1, w2, layer, seq, per_seq):
    t = x.shape[0]
    tm = TOKEN_TILE
    full = lambda shape: _layer_block(shape, layer)
    tok = lambda w_: pl.BlockSpec((tm, w_), lambda i: (i, 0))
    return pl.pallas_call(
        _merge_kernel,
        grid=(t // tm,),
        in_specs=[tok(D_MODEL), _mod_spec(layer, tm, seq, per_seq),
                  full((4, D_MODEL)), tok(MIX_W), tok(MIX_W), tok(MIX_W), tok(MIX_W),
                  full((D_MODEL, N_BRANCH * D_MODEL)), full((1, N_BRANCH * D_MODEL)),
                  full((N_BRANCH, MIX_W, D_MODEL)), full((D_MODEL, D_MODEL)),
                  full((D_MODEL, D_FF)), full((D_FF, D_MODEL))],
        out_specs=tok(D_MODEL),
        out_shape=jax.ShapeDtypeStruct((t, D_MODEL), F32),
        compiler_params=pltpu.CompilerParams(vmem_limit_bytes=VMEM_LIMIT),
        name="merge_ffn",
    )(x, mod, g, *ys, wm, bm, wb, wo, w1, w2)


def _arrange_w_in(w):
    z = lambda n: jnp.zeros((D_MODEL, n), w.dtype)
    o_b, o_c = IN_A, IN_A + IN_B
    o_g, o_d = o_c + 4 * D_C, o_c + IN_C
    gates = w[:, o_g:o_d].reshape(D_MODEL, 2, 2, H_C).transpose(0, 2, 1, 3).reshape(D_MODEL, 4 * H_C)
    return jnp.concatenate([
        w[:, :Q_LORA + KV_LORA], z(64), w[:, Q_LORA + KV_LORA:IN_A], z(32),
        w[:, o_b:o_g], gates, z(ZG_W - 4 * H_C), w[:, o_d:]], axis=1).astype(BF16)


def _arrange_w_uq(w):
    w = w.reshape(Q_LORA, H_A, NOPE_A + ROPE_A)
    half = ROPE_A // 2
    swapped = jnp.concatenate([jnp.zeros_like(w[:, :, :NOPE_A]), w[:, :, NOPE_A + half:], w[:, :, NOPE_A:NOPE_A + half]],
                              axis=-1)
    pad = lambda a: jnp.pad(a, ((0, 0), (0, 0), (0, LANES - NOPE_A - ROPE_A))).reshape(Q_LORA, H_A * LANES)
    return jnp.concatenate([pad(w), pad(swapped)], axis=1).astype(BF16)


def _arrange_w_ukv(w):
    w = w.reshape(KV_LORA, H_A, NOPE_A + VH_A)
    wk = jnp.pad(w[:, :, :NOPE_A], ((0, 0), (0, 0), (0, LANES - NOPE_A)))
    v = w[:, :, NOPE_A:]
    zero = jnp.zeros_like(v)
    even = jnp.concatenate([v, zero], axis=-1)
    odd = jnp.concatenate([zero, v], axis=-1)
    wv = jnp.where((jnp.arange(H_A) % 2 == 0)[None, :, None], even, odd)
    return jnp.concatenate([wk.reshape(KV_LORA, -1), wv.reshape(KV_LORA, -1)], axis=1).astype(BF16)


def _rope_tables(rows):
    row = np.repeat(np.arange(rows), GRID_W).astype(np.float64)
    col = np.tile(np.arange(GRID_W), rows).astype(np.float64)
    nf = ROPE_A // 4
    inv = np.exp(-math.log(ROPE_BASE) * np.arange(nf, dtype=np.float64) / nf)
    ang = np.concatenate([row[:, None] * inv, col[:, None] * inv], axis=-1)
    cos, sin = np.cos(ang), np.sin(ang)
    n = cos.shape[0]
    ones, zeros = (lambda k: np.ones((n, k))), (lambda k: np.zeros((n, k)))
    mla = (np.concatenate([ones(64), cos, cos, ones(32)], axis=1),
           np.concatenate([zeros(64), -sin, zeros(48)], axis=1),
           np.concatenate([zeros(80), sin, zeros(32)], axis=1))
    dif = (np.tile(np.concatenate([cos, cos], axis=1), (1, 4)),
           np.tile(np.concatenate([-sin, zeros(16)], axis=1), (1, 4)),
           np.tile(np.concatenate([zeros(16), sin], axis=1), (1, 4)))
    as_f32 = lambda ts: tuple(jnp.asarray(t.astype(np.float32)) for t in ts)
    return as_f32(mla), as_f32(dif)


def _pack_state(c, n):
    b = c.shape[0]
    ct = jnp.swapaxes(c, -1, -2)
    pad = jnp.zeros((b, 2, H_C, ST_ROWS - DH_C - 1, DH_C), c.dtype)
    return jnp.concatenate([ct, n[:, :, :, None, :], pad], axis=3).reshape(b, N_UNITS, ST_ROWS, DH_C)


def _unpack_state(st):
    b = st.shape[0]
    st = st.reshape(b, 2, H_C, ST_ROWS, DH_C)
    return jnp.swapaxes(st[:, :, :, :DH_C, :], -1, -2), st[:, :, :, DH_C, :]


def kernel(x_prompt, x_sample, cache_mla_ckv, cache_mla_krope, cache_diff_k, cache_diff_v, state_mlstm_C,
           state_mlstm_n, state_mlstm_m, c, c_ctx, w_mod, b_mod, norm_g, w_in, mla_q_norm, w_uq, mla_kv_norm,
           w_ukv, gmlp_v_norm, gmlp_w_s, gmlp_b_s, mlstm_gate_bias, mlstm_head_norm, diff_lambda, diff_sub_norm,
           w_branch, w_merge, b_merge, w_out, w_ff1, w_ff2):
    bp, lp, _ = x_prompt.shape
    bs, ls, _ = x_sample.shape
    past = cache_mla_ckv.shape[2]

    cond = jnp.concatenate([c_ctx[None, :], c, jnp.zeros((16 - 1 - bs, D_MODEL), F32)], axis=0)
    mod = _modulation(cond, w_mod, b_mod).reshape(DEPTH, 16, 6, D_MODEL)
    rope_mla, rope_dif = _rope_tables(ls // GRID_W)

    row = lambda a: a[:, None, :]
    w_in_a = jax.vmap(_arrange_w_in)(w_in)
    wuq_a = jax.vmap(_arrange_w_uq)(w_uq)
    wukv_a = jax.vmap(_arrange_w_ukv)(w_ukv)
    qn, kvn, vn, hn, bm = row(mla_q_norm), row(mla_kv_norm), row(gmlp_v_norm), row(mlstm_head_norm), row(b_merge)
    ws = gmlp_w_s.astype(BF16)
    bias = jnp.repeat(jnp.swapaxes(gmlp_b_s, 1, 2), D_B // G_B, axis=2)
    gb = jnp.pad(mlstm_gate_bias.transpose(0, 2, 1, 3).reshape(DEPTH, 1, 4 * H_C),
                 ((0, 0), (0, 0), (0, ZG_W - 4 * H_C)))
    sn = row(jnp.tile(diff_sub_norm, (1, 2)))
    wm, wb, wo = w_merge.astype(BF16), w_branch.astype(BF16), w_out.astype(BF16)
    w1, w2 = w_ff1.astype(BF16), w_ff2.astype(BF16)

    ctx_kr = jnp.pad(cache_mla_krope, ((0, 0), (0, 0), (0, 0), (64, LANES - 64 - ROPE_A)))
    ctx_dk = cache_diff_k.reshape(bs, DEPTH, past, D_D)
    ctx_dv = cache_diff_v.reshape(bs, DEPTH, past, D_D)
    st0 = jax.vmap(_pack_state, in_axes=1, out_axes=1)(state_mlstm_C, state_mlstm_n)
    m0 = jnp.broadcast_to(state_mlstm_m.reshape(bs, DEPTH, N_UNITS, 1), (bs, DEPTH, N_UNITS, LANES))

    xp = x_prompt.reshape(bp * lp, D_MODEL)
    xs = x_sample.reshape(bs * ls, D_MODEL)
    ents = []
    for l in range(DEPTH):
        lam_init = 0.8 - 0.6 * math.exp(-0.3 * l)
        for is_sample in (False, True):
            x = xs if is_sample else xp
            batch, seq = (bs, ls) if is_sample else (bp, lp)
            za, yb, zc, zg, zd = _in_proj(x, mod, norm_g, w_in_a, vn, ws, bias, l, seq, is_sample)
            if is_sample:
                (ya,) = _mla(za, qn, kvn, wuq_a, wukv_a, l, batch, seq, rope=rope_mla, ctx=(cache_mla_ckv, ctx_kr))
                yd = _diff(zd, diff_lambda, sn, l, batch, seq, lam_init, rope=rope_dif, ctx=(ctx_dk, ctx_dv))
                (yc,) = _mlstm(zc, zg, gb, hn, l, batch, seq, state=(st0, m0))
            else:
                ya, ckv = _mla(za, qn, kvn, wuq_a, wukv_a, l, batch, seq)
                yd = _diff(zd, diff_lambda, sn, l, batch, seq, lam_init)
                yc, st_new, m_new = _mlstm(zc, zg, gb, hn, l, batch, seq)
                c_new, n_new = _unpack_state(st_new)
                ents.append((
                    ckv.reshape(bp, lp, KV_LORA),
                    za[:, Q_LORA + KV_LORA + 64:Q_LORA + KV_LORA + 64 + ROPE_A].reshape(bp, lp, ROPE_A),
                    zd[:, D_D:2 * D_D].reshape(bp, lp, H_D, 2, DH_D),
                    zd[:, 2 * D_D:].reshape(bp, lp, H_D, 2 * DH_D),
                    c_new, n_new,
                    m_new[:, :, 0].reshape(bp, 2, H_C)))
            x = _merge_ffn(x, mod, norm_g, (ya, yb, yc, yd), wm, bm, wb, wo, w1, w2, l, seq, is_sample)
            if is_sample:
                xs = x
            else:
                xp = x

    stack = lambda j: jnp.stack([e[j] for e in ents], axis=1)
    return (xp.reshape(bp, lp, D_MODEL), xs.reshape(bs, ls, D_MODEL),
            stack(0), stack(1), stack(2), stack(3), stack(4), stack(5), stack(6))
```

```python
import functools
import math

import jax
import jax.numpy as jnp
import numpy as np
from jax import lax
from jax.experimental import pallas as pl
from jax.experimental.pallas import tpu as pltpu

F32 = jnp.float32
BF16 = jnp.bfloat16

D_MODEL = 1024
DEPTH = 2
GRID_W = 64
N_BRANCH = 4
MIX_W = 256
H_A, NOPE_A, ROPE_A, VH_A = 4, 64, 32, 64
Q_LORA, KV_LORA = 256, 128
D_B, G_B, CHUNK_B = 256, 4, 128
H_C, DH_C = 4, 64
D_C = H_C * DH_C
H_D, DH_D = 4, 32
D_D = H_D * 2 * DH_D
D_FF = 4 * D_MODEL
IN_A = Q_LORA + KV_LORA + ROPE_A
IN_B = 2 * D_B
IN_C = 4 * D_C + 4 * H_C
IN_D = 3 * D_D
ROPE_BASE = 10000.0
EPS = 1e-6
MLA_SCALE = (NOPE_A + ROPE_A) ** -0.5
DIFF_SCALE = DH_D ** -0.5

LANES = 128
MLSTM_CHUNK = 128
VMEM_LIMIT = 56 * 1024 * 1024
NEG_INF = float("-inf")

ZA_W = 512
ZG_W = 128
Z_OFF_A = 0
Z_OFF_B = Z_OFF_A + ZA_W
Z_OFF_C = Z_OFF_B + IN_B
Z_OFF_G = Z_OFF_C + 4 * D_C
Z_OFF_D = Z_OFF_G + ZG_W
Z_W = Z_OFF_D + IN_D


def _rms(x, g):
    return x * lax.rsqrt(jnp.mean(x * x, axis=-1, keepdims=True) + EPS) * g


def _sigmoid(x):
    return 1.0 / (1.0 + jnp.exp(-x))


def _log_sigmoid(x):
    return jnp.minimum(x, 0.0) - jnp.log1p(jnp.exp(-jnp.abs(x)))


def _dot(a, b):
    return jnp.dot(a, b, preferred_element_type=F32)


def _dot_nt(a, b):
    return lax.dot_general(a, b, (((1,), (1,)), ((), ())), preferred_element_type=F32)


def _lane_group_mask(width, group, index):
    lane = lax.broadcasted_iota(jnp.int32, (1, width), 1)
    return (lane >= index * group) & (lane < (index + 1) * group)


def _mod_kernel(cond_ref, w_ref, b_ref, o_ref):
    c = cond_ref[...]
    s = c * _sigmoid(c)
    o_ref[...] = _dot(s.astype(BF16), w_ref[...].astype(BF16)) + b_ref[...]


def _modulation(cond, w_mod, b_mod):
    rows = cond.shape[0]
    nb = 1024
    return pl.pallas_call(
        _mod_kernel,
        grid=(DEPTH, 6 * D_MODEL // nb),
        in_specs=[
            pl.BlockSpec((rows, D_MODEL), lambda l, j: (0, 0)),
            pl.BlockSpec((None, D_MODEL, nb), lambda l, j: (l, 0, j)),
            pl.BlockSpec((None, 1, nb), lambda l, j: (l, 0, j)),
        ],
        out_specs=pl.BlockSpec((None, rows, nb), lambda l, j: (l, 0, j)),
        out_shape=jax.ShapeDtypeStruct((DEPTH, rows, 6 * D_MODEL), F32),
        name="modulation",
    )(cond, w_mod, b_mod.reshape(DEPTH, 1, 6 * D_MODEL))


TOKEN_TILE = 512


def _layer_block(shape, layer):
    shape = tuple(shape)
    return pl.BlockSpec((None,) + shape, lambda *_: (layer,) + (0,) * len(shape), pipeline_mode=pl.Buffered(1))


def _mod_spec(layer, tm, seq, per_seq):
    def index(i):
        return (layer, 1 + (i * tm) // seq if per_seq else 0, 0, 0)
    return pl.BlockSpec((None, None, 6, D_MODEL), index)


def _gmlp_tile(zb, vn, ws_ref, bias):
    tm = zb.shape[0]
    v = _rms(zb[:, D_B:], vn).astype(BF16)
    gmasks = [_lane_group_mask(D_B, D_B // G_B, g) for g in range(G_B)]
    out = []
    for ch in range(tm // CHUNK_B):
        vc = v[ch * CHUNK_B:(ch + 1) * CHUNK_B, :]
        mixed = bias
        for g in range(G_B):
            mixed = mixed + jnp.where(gmasks[g], _dot(ws_ref[g], vc), 0.0)
        out.append(zb[ch * CHUNK_B:(ch + 1) * CHUNK_B, :D_B] * mixed)
    return jnp.concatenate(out, axis=0)


KR_LANE = Q_LORA + KV_LORA + 64


def _in_kernel(x_ref, mod_ref, g_ref, w_ref, vn_ref, ws_ref, bias_ref,
               za_ref, yb_ref, zc_ref, zg_ref, dq_ref, dk_ref, dv_ref, kr_ref):
    x = x_ref[...]
    h = _rms(x, g_ref[0:1, :]) * (1.0 + mod_ref[1:2, :]) + mod_ref[0:1, :]
    hb = h.astype(BF16)
    za = _dot(hb, w_ref[:, Z_OFF_A:Z_OFF_B])
    za_ref[...] = za
    kr_ref[...] = za[:, KR_LANE:KR_LANE + ROPE_A]
    zc_ref[...] = _dot(hb, w_ref[:, Z_OFF_C:Z_OFF_G])
    zg_ref[...] = _dot(hb, w_ref[:, Z_OFF_G:Z_OFF_D])
    for j, d_ref in enumerate((dq_ref, dk_ref, dv_ref)):
        d_ref[...] = _dot(hb, w_ref[:, Z_OFF_D + j * D_D:Z_OFF_D + (j + 1) * D_D])
    yb_ref[...] = _gmlp_tile(_dot(hb, w_ref[:, Z_OFF_B:Z_OFF_C]), vn_ref[...], ws_ref, bias_ref[...])


def _in_proj(x, mod, g, w, vn, ws, bias, layer, seq, per_seq):
    t = x.shape[0]
    tm = TOKEN_TILE
    widths = (ZA_W, MIX_W, 4 * D_C, ZG_W, D_D, D_D, D_D, ROPE_A)
    return pl.pallas_call(
        _in_kernel,
        grid=(t // tm,),
        in_specs=[
            pl.BlockSpec((tm, D_MODEL), lambda i: (i, 0)),
            _mod_spec(layer, tm, seq, per_seq),
            _layer_block((4, D_MODEL), layer),
            _layer_block((D_MODEL, Z_W), layer),
            _layer_block((1, D_B), layer),
            _layer_block((G_B, CHUNK_B, CHUNK_B), layer),
            _layer_block((CHUNK_B, D_B), layer),
        ],
        out_specs=[pl.BlockSpec((tm, w_), lambda i: (i, 0)) for w_ in widths],
        out_shape=[jax.ShapeDtypeStruct((t, w_), F32) for w_ in widths],
        compiler_params=pltpu.CompilerParams(vmem_limit_bytes=VMEM_LIMIT),
        name="in_proj",
    )(x, mod, g, w, vn, ws, bias)


LOG2E = math.log2(math.e)


def _with_ones(v, lo_mask, keep_lo):
    lane = lax.broadcasted_iota(jnp.int32, (1, LANES), 1)
    if keep_lo:
        return jnp.where(lane == 64, 1.0, jnp.where(lo_mask, v, 0.0))
    return jnp.where(lane == 0, 1.0, jnp.where(lo_mask, 0.0, v))


def _pair_normalise(o_even, o_odd, lo_mask):
    return jnp.where(lo_mask, o_even / o_even[:, 64:65], o_odd / o_odd[:, 0:1])


def _rope128(x, tc, ts1, ts2):
    return x * tc + pltpu.roll(x, LANES - 16, 1) * ts1 + pltpu.roll(x, 16, 1) * ts2


def _mla_kernel(*refs, seq, has_ctx, tq):
    if has_ctx:
        (za_ref, qn_ref, kvn_ref, wuq_ref, wukv_ref, tc_ref, ts1_ref, ts2_ref, cckv_ref, ckr_ref,
         ya_ref, q_s, k_s, v_s, ckv_s, kr_s) = refs
        past = cckv_ref.shape[0]
    else:
        (za_ref, qn_ref, kvn_ref, wuq_ref, wukv_ref, ya_ref, ckv_ref, q_s, k_s, v_s, ckv_s, kr_s) = refs
        past = 0
    za = za_ref[...]
    cq = _rms(za[:, :Q_LORA], qn_ref[...])
    qh = _dot(cq.astype(BF16), wuq_ref[:, 0:(2 if has_ctx else 1) * H_A * LANES])
    ckv = _rms(za[:, Q_LORA:Q_LORA + KV_LORA], kvn_ref[...])
    kr = za[:, Q_LORA + KV_LORA:ZA_W]
    if has_ctx:
        tc, ts1, ts2 = tc_ref[...], ts1_ref[...], ts2_ref[...]
        kr = _rope128(kr, tc, ts1, ts2)
        ckv_s[0:past, :] = cckv_ref[...]
        kr_s[0:past, :] = ckr_ref[...]
    else:
        ckv_ref[...] = ckv
    ckv_s[past:past + seq, :] = ckv
    kr_s[past:past + seq, :] = kr
    kv = _dot(ckv_s[...].astype(BF16), wukv_ref[...])
    kr_all = kr_s[...]
    lo = _lane_group_mask(LANES, 64, 0)
    for h in range(H_A):
        qg = qh[:, h * LANES:(h + 1) * LANES]
        if has_ctx:
            qg = qg * tc + qh[:, (H_A + h) * LANES:(H_A + h + 1) * LANES] * (ts1 + ts2)
        q_s[h] = (qg * (MLA_SCALE * LOG2E)).astype(BF16)
        k_s[h] = (kv[:, h * LANES:(h + 1) * LANES] + kr_all).astype(BF16)
        v_s[h] = _with_ones(kv[:, (H_A + h) * LANES:(H_A + h + 1) * LANES], lo, h % 2 == 0).astype(BF16)

    def body(i, carry):
        rows = pl.ds(pl.multiple_of(i * tq, tq), tq)
        s = [_dot_nt(q_s[h, rows, :], k_s[h]) for h in range(H_A)]
        o = [_dot(jnp.exp2(s[h] - jnp.max(s[h], axis=-1, keepdims=True)).astype(BF16), v_s[h]) for h in range(H_A)]
        for pair in range(H_A // 2):
            ya_ref[rows, pair * LANES:(pair + 1) * LANES] = _pair_normalise(o[2 * pair], o[2 * pair + 1], lo)
        return carry

    lax.fori_loop(0, seq // tq, body, 0)


def _ctx_block(past, width, layer):
    return pl.BlockSpec((None, None, past, width), lambda b: (b, layer, 0, 0))


def _mla(za, qn, kvn, wuq, wukv, layer, batch, seq, rope=None, ctx=None):
    has_ctx = ctx is not None
    past = ctx[0].shape[2] if has_ctx else 0
    lk = past + seq
    tq = min(seq, 256)
    full = lambda shape: pl.BlockSpec(shape, lambda b: (0,) * len(shape))
    in_specs = [
        pl.BlockSpec((seq, ZA_W), lambda b: (b, 0)),
        _layer_block((1, Q_LORA), layer), _layer_block((1, KV_LORA), layer),
        _layer_block((Q_LORA, 2 * H_A * LANES), layer), _layer_block((KV_LORA, 2 * H_A * LANES), layer),
    ]
    args = [za, qn, kvn, wuq, wukv]
    out_specs = [pl.BlockSpec((seq, MIX_W), lambda b: (b, 0))]
    out_shape = [jax.ShapeDtypeStruct((batch * seq, MIX_W), F32)]
    if has_ctx:
        in_specs += [full((seq, LANES))] * 3
        in_specs += [_ctx_block(past, LANES, layer)] * 2
        args += list(rope) + list(ctx)
    else:
        out_specs.append(pl.BlockSpec((seq, KV_LORA), lambda b: (b, 0)))
        out_shape.append(jax.ShapeDtypeStruct((batch * seq, KV_LORA), F32))
    return pl.pallas_call(
        functools.partial(_mla_kernel, seq=seq, has_ctx=has_ctx, tq=tq),
        grid=(batch,),
        in_specs=in_specs,
        out_specs=out_specs,
        out_shape=out_shape,
        scratch_shapes=[
            pltpu.VMEM((H_A, seq, LANES), BF16),
            pltpu.VMEM((H_A, lk, LANES), BF16),
            pltpu.VMEM((H_A, lk, LANES), BF16),
            pltpu.VMEM((lk, LANES), F32),
            pltpu.VMEM((lk, LANES), F32),
        ],
        compiler_params=pltpu.CompilerParams(vmem_limit_bytes=VMEM_LIMIT),
        name="mla",
    )(*args)


def _diff_kernel(*refs, seq, has_ctx, tq, lam_init):
    if has_ctx:
        (dq_ref, dk_ref, dv_ref, lam_ref, sn_ref, tc_ref, ts1_ref, ts2_ref, ck_ref, cv_ref,
         yd_ref, q_s, k_s, v_s) = refs
        past = ck_ref.shape[0]
    else:
        (dq_ref, dk_ref, dv_ref, lam_ref, sn_ref, yd_ref, q_s, k_s, v_s) = refs
        past = 0
    lo = _lane_group_mask(LANES, 64, 0)
    if has_ctx:
        tc, ts1, ts2 = tc_ref[...], ts1_ref[...], ts2_ref[...]
        k_s[0:past, :] = ck_ref[...].astype(BF16)
        cv = cv_ref[...]
    for half in range(2):
        cols = slice(half * LANES, (half + 1) * LANES)
        q, k, v = dq_ref[:, cols], dk_ref[:, cols], dv_ref[:, cols]
        if has_ctx:
            q = _rope128(q, tc, ts1, ts2)
            k = _rope128(k, tc, ts1, ts2)
            v_s[2 * half, 0:past, :] = _with_ones(cv[:, cols], lo, True).astype(BF16)
            v_s[2 * half + 1, 0:past, :] = _with_ones(cv[:, cols], lo, False).astype(BF16)
        q_s[:, cols] = (q * (DIFF_SCALE * LOG2E)).astype(BF16)
        k_s[past:past + seq, cols] = k.astype(BF16)
        v_s[2 * half, past:past + seq, :] = _with_ones(v, lo, True).astype(BF16)
        v_s[2 * half + 1, past:past + seq, :] = _with_ones(v, lo, False).astype(BF16)

    lam = lam_ref[...]
    lam_val = (jnp.exp(jnp.sum(lam[0:1] * lam[1:2], axis=-1, keepdims=True))
               - jnp.exp(jnp.sum(lam[2:3] * lam[3:4], axis=-1, keepdims=True)) + lam_init)
    comp_masks = [jnp.where(_lane_group_mask(D_D, DH_D, j), 1.0, 0.0).astype(BF16) for j in range(2 * H_D)]
    sn = sn_ref[...]

    def body(i, carry):
        rows = pl.ds(pl.multiple_of(i * tq, tq), tq)
        qb = q_s[rows, :]
        k_all = k_s[...]
        score = lambda h: [_dot_nt(qb * comp_masks[2 * h + comp], k_all) for comp in range(2)]
        s_next = score(0)
        o = []
        for h in range(H_D):
            s = s_next
            if h + 1 < H_D:
                s_next = score(h + 1)
            o1, o2 = [_dot(jnp.exp2(sc - jnp.max(sc, axis=-1, keepdims=True)).astype(BF16), v_s[h]) for sc in s]
            ll = 64 if h % 2 == 0 else 0
            o.append(o1 / o1[:, ll:ll + 1] - lam_val * (o2 / o2[:, ll:ll + 1]))
        for pair in range(H_D // 2):
            acc = jnp.where(lo, o[2 * pair], o[2 * pair + 1])
            sq = acc * acc
            ss_lo = jnp.sum(jnp.where(lo, sq, 0.0), axis=-1, keepdims=True)
            ss_hi = jnp.sum(jnp.where(lo, 0.0, sq), axis=-1, keepdims=True)
            r = jnp.where(lo, lax.rsqrt(ss_lo / (2 * DH_D) + EPS), lax.rsqrt(ss_hi / (2 * DH_D) + EPS))
            yd_ref[rows, pair * LANES:(pair + 1) * LANES] = acc * r * sn * (1.0 - lam_init)
        return carry

    lax.fori_loop(0, seq // tq, body, 0)


def _diff(dq, dk, dv, lam, sn, layer, batch, seq, lam_init, rope=None, ctx=None):
    has_ctx = ctx is not None
    past = ctx[0].shape[2] if has_ctx else 0
    lk = past + seq
    tq = min(seq, 256)
    full = lambda shape: pl.BlockSpec(shape, lambda b: (0,) * len(shape))
    in_specs = [pl.BlockSpec((seq, D_D), lambda b: (b, 0))] * 3
    in_specs += [_layer_block((4, DH_D), layer), _layer_block((1, LANES), layer)]
    args = [dq, dk, dv, lam, sn]
    if has_ctx:
        in_specs += [full((seq, LANES))] * 3
        in_specs += [_ctx_block(past, D_D, layer)] * 2
        args += list(rope) + list(ctx)
    return pl.pallas_call(
        functools.partial(_diff_kernel, seq=seq, has_ctx=has_ctx, tq=tq, lam_init=lam_init),
        grid=(batch,),
        in_specs=in_specs,
        out_specs=pl.BlockSpec((seq, MIX_W), lambda b: (b, 0)),
        out_shape=jax.ShapeDtypeStruct((batch * seq, MIX_W), F32),
        scratch_shapes=[
            pltpu.VMEM((seq, D_D), BF16),
            pltpu.VMEM((lk, D_D), BF16),
            pltpu.VMEM((H_D, lk, LANES), BF16),
        ],
        compiler_params=pltpu.CompilerParams(vmem_limit_bytes=VMEM_LIMIT),
        name="diff_attn",
    )(*args)


def _split3(x):
    hi = x.astype(BF16)
    r1 = x - hi.astype(F32)
    mid = r1.astype(BF16)
    lo = (r1 - mid.astype(F32)).astype(BF16)
    return hi, mid, lo


ST_ROWS = DH_C + 8
N_UNITS = 2 * H_C


def _mlstm_kernel(*refs, seq, has_state):
    ck = MLSTM_CHUNK
    nc = seq // ck
    if has_state:
        (zc_ref, zg_ref, gb_ref, hn_ref, c0_ref, n0_ref, m0_ref,
         yc_ref, k_s, qt_s, kt_s, vt_s, rcc_s, ca_s, wi_s, en_s, ws_s, wo_s, st_s, ht_s) = refs
    else:
        (zc_ref, zg_ref, gb_ref, hn_ref,
         yc_ref, cout_ref, nout_ref, mout_ref,
         k_s, qt_s, kt_s, vt_s, rcc_s, ca_s, wi_s, en_s, ws_s, wo_s, st_s, ht_s) = refs
    eye = jnp.where(lax.broadcasted_iota(jnp.int32, (DH_C, DH_C), 0) == lax.broadcasted_iota(jnp.int32, (DH_C, DH_C), 1),
                    1.0, 0.0).astype(BF16)

    def transpose_exact(a):
        return sum(_dot_nt(eye, part) for part in _split3(a))

    for c in range(nc):
        rows = slice(c * ck, (c + 1) * ck)
        qt_s[c] = zc_ref[rows, 0:D_C].T.astype(BF16)
        kk = zc_ref[rows, D_C:2 * D_C] * (DH_C ** -0.5)
        k_s[rows, :] = kk.astype(BF16)
        kt_s[c] = kk.T.astype(BF16)
        vt_s[c] = zc_ref[rows, 2 * D_C:3 * D_C].T.astype(BF16)

    gates = zg_ref[...] + gb_ref[...]
    r_i = lax.broadcasted_iota(jnp.int32, (ck, ck), 0)
    c_i = lax.broadcasted_iota(jnp.int32, (ck, ck), 1)
    sum_fw = jnp.where(r_i <= c_i, 1.0, 0.0).astype(BF16)
    sum_bw = jnp.where(r_i >= c_i, 1.0, 0.0).astype(BF16)
    row8 = lax.broadcasted_iota(jnp.int32, (N_UNITS, ck), 0)
    fw8 = row8 < H_C
    fw81 = fw8[:, 0:1]
    i_rows, b_rows = [], []
    for c in range(nc):
        rows = slice(c * ck, (c + 1) * ck)
        g_t = gates[rows, :].T
        i8 = g_t[0:N_UNITS, :]
        p0, p1, p2 = _split3(_log_sigmoid(g_t[N_UNITS:2 * N_UNITS, :]))
        b_fw = _dot(p0, sum_fw) + _dot(p1, sum_fw) + _dot(p2, sum_fw)
        b_bw = _dot(p0, sum_bw) + _dot(p1, sum_bw) + _dot(p2, sum_bw)
        b8 = jnp.where(fw8, b_fw, b_bw)
        rcc_s[rows, :] = jnp.concatenate([i8 - b8, jnp.zeros((ck - N_UNITS, ck), F32)], axis=0).T
        i_rows.append(i8)
        b_rows.append(b8)

    b_steps = [jnp.where(fw8, b_rows[i], b_rows[nc - 1 - i]) for i in range(nc)]
    r_steps = [jnp.where(fw8, i_rows[i], i_rows[nc - 1 - i]) - b_steps[i] for i in range(nc)]
    pm = sm = jnp.concatenate(r_steps, axis=0)
    lane_all = lax.broadcasted_iota(jnp.int32, pm.shape, 1)
    sh = 1
    while sh < ck:
        pm = jnp.maximum(pm, jnp.where(lane_all >= sh, pltpu.roll(pm, sh, 1), NEG_INF))
        sm = jnp.maximum(sm, jnp.where(lane_all < ck - sh, pltpu.roll(sm, ck - sh, 1), NEG_INF))
        sh *= 2
    m_prev = m0_ref[:, 0:1] if has_state else jnp.zeros((N_UNITS, 1), F32)
    for i in range(nc):
        b8, r8 = b_steps[i], r_steps[i]
        pm8, sm8 = pm[i * N_UNITS:(i + 1) * N_UNITS, :], sm[i * N_UNITS:(i + 1) * N_UNITS, :]
        cm = jnp.where(fw8, pm8, sm8)
        b_t = jnp.where(fw81, b8[:, ck - 1:ck], b8[:, 0:1])
        cm_end = jnp.where(fw81, pm8[:, ck - 1:ck], sm8[:, 0:1])
        inter = b8 + m_prev
        mt = jnp.maximum(inter, b8 + cm)
        ca_s[i] = b8 - mt
        wi_s[i] = jnp.exp(inter - mt)
        en_s[i] = jnp.exp(-mt)
        m_new = b_t + jnp.maximum(m_prev, cm_end)
        ws_s[i] = jnp.exp(b_t + r8 - m_new)
        wo_s[i] = jnp.broadcast_to(jnp.exp(b_t + m_prev - m_new), (N_UNITS, ck))
        m_prev = m_new

    if has_state:
        for u in range(N_UNITS):
            st_s[u, 0:DH_C, :] = transpose_exact(c0_ref[u])
            st_s[u, DH_C:ST_ROWS, :] = jnp.broadcast_to(n0_ref[u:u + 1, :], (ST_ROWS - DH_C, DH_C))
    else:
        st_s[...] = jnp.zeros_like(st_s)

    hmask_bf = [jnp.where(_lane_group_mask(D_C, DH_C, h), 1.0, 0.0).astype(BF16) for h in range(H_C)]
    src_ok = (r_i <= c_i, r_i >= c_i)

    def body(i, carry):
        ca, wi, en, ws, wo = ca_s[i], wi_s[i], en_s[i], ws_s[i], wo_s[i]
        chunks = (i, nc - 1 - i)
        units = [(d, h) for d in range(2) for h in range(H_C)]
        hs = [slice(h * DH_C, (h + 1) * DH_C) for h in range(H_C)]
        q_t = [qt_s[c] for c in chunks]
        k_t = [kt_s[c] for c in chunks]
        v_t = [vt_s[c] for c in chunks]
        sts, qcts, states, upds = [], [], [], []
        for d, h in units:
            rows = pl.ds(pl.multiple_of(chunks[d] * ck, ck), ck)
            sts.append(_dot(k_s[rows, :] * hmask_bf[h], q_t[d]))
            states.append(st_s[d * H_C + h])
            qcts.append(_dot(states[-1].astype(BF16), q_t[d][hs[h], :]))
        for u, (d, h) in enumerate(units):
            w_s = ws[u:u + 1, :]
            vw = jnp.concatenate([v_t[d][hs[h], :].astype(F32) * w_s, jnp.broadcast_to(w_s, (8, ck))], axis=0)
            upds.append(_dot_nt(vw.astype(BF16), k_t[d][hs[h], :]))
        pts = []
        for u, (d, h) in enumerate(units):
            rows = pl.ds(pl.multiple_of(chunks[d] * ck, ck), ck)
            decay = jnp.exp(rcc_s[rows, u:u + 1] + ca[u:u + 1, :])
            pts.append(sts[u] * jnp.where(src_ok[d], decay, 0.0))
        nums = [_dot(v_t[d][hs[h], :], pts[u].astype(BF16)) for u, (d, h) in enumerate(units)]
        for u, (d, h) in enumerate(units):
            w_i = wi[u:u + 1, :]
            den = w_i * qcts[u][DH_C:DH_C + 1, :] + jnp.sum(pts[u], axis=0, keepdims=True)
            num = w_i * qcts[u][0:DH_C, :] + nums[u]
            ht_s[d, chunks[d], hs[h], :] = num / jnp.maximum(jnp.abs(den), en[u:u + 1, :])
            st_s[u] = wo[u:u + 1, 0:DH_C] * states[u] + upds[u]
        return carry

    lax.fori_loop(0, nc, body, 0)

    hn = hn_ref[...]
    for c in range(nc):
        rows = slice(c * ck, (c + 1) * ck)
        h_t = ht_s[0, c] + ht_s[1, c]
        parts = []
        for h in range(H_C):
            x = h_t[h * DH_C:(h + 1) * DH_C, :]
            parts.append(x * lax.rsqrt(jnp.mean(x * x, axis=0, keepdims=True) + EPS))
        h_n = jnp.concatenate(parts, axis=0).T
        yc_ref[rows, :] = _sigmoid(zc_ref[rows, 3 * D_C:4 * D_C]) * (h_n * hn)

    if not has_state:
        for u in range(N_UNITS):
            cout_ref[u] = transpose_exact(st_s[u, 0:DH_C, :])
            nout_ref[u:u + 1, :] = st_s[u, DH_C:DH_C + 1, :]
        mout_ref[...] = jnp.broadcast_to(m_prev, (N_UNITS, LANES))


def _mlstm(zc, zg, gb, hn, layer, batch, seq, state=None):
    has_state = state is not None
    nc = seq // MLSTM_CHUNK
    in_specs = [pl.BlockSpec((seq, 4 * D_C), lambda b: (b, 0)), pl.BlockSpec((seq, ZG_W), lambda b: (b, 0)),
                _layer_block((1, ZG_W), layer), _layer_block((1, D_C), layer)]
    args = [zc, zg, gb, hn]
    out_specs = [pl.BlockSpec((seq, MIX_W), lambda b: (b, 0))]
    out_shape = [jax.ShapeDtypeStruct((batch * seq, MIX_W), F32)]
    if has_state:
        in_specs += [pl.BlockSpec((None, None, N_UNITS, DH_C, DH_C), lambda b: (b, layer, 0, 0, 0)),
                     pl.BlockSpec((None, None, N_UNITS, DH_C), lambda b: (b, layer, 0, 0)),
                     pl.BlockSpec((None, None, N_UNITS, LANES), lambda b: (b, layer, 0, 0))]
        args += list(state)
    else:
        out_specs += [pl.BlockSpec((None, N_UNITS, DH_C, DH_C), lambda b: (b, 0, 0, 0)),
                      pl.BlockSpec((None, N_UNITS, DH_C), lambda b: (b, 0, 0)),
                      pl.BlockSpec((None, N_UNITS, LANES), lambda b: (b, 0, 0))]
        out_shape += [jax.ShapeDtypeStruct((batch, N_UNITS, DH_C, DH_C), F32),
                      jax.ShapeDtypeStruct((batch, N_UNITS, DH_C), F32),
                      jax.ShapeDtypeStruct((batch, N_UNITS, LANES), F32)]
    step = lambda: pltpu.VMEM((nc, N_UNITS, MLSTM_CHUNK), F32)
    return pl.pallas_call(
        functools.partial(_mlstm_kernel, seq=seq, has_state=has_state),
        grid=(batch,),
        in_specs=in_specs,
        out_specs=out_specs,
        out_shape=out_shape,
        scratch_shapes=[
            pltpu.VMEM((seq, D_C), BF16),
            pltpu.VMEM((nc, D_C, MLSTM_CHUNK), BF16), pltpu.VMEM((nc, D_C, MLSTM_CHUNK), BF16),
            pltpu.VMEM((nc, D_C, MLSTM_CHUNK), BF16),
            pltpu.VMEM((seq, ZG_W), F32),
            step(), step(), step(), step(), step(),
            pltpu.VMEM((N_UNITS, ST_ROWS, DH_C), F32),
            pltpu.VMEM((2, nc, D_C, MLSTM_CHUNK), F32),
        ],
        compiler_params=pltpu.CompilerParams(vmem_limit_bytes=VMEM_LIMIT),
        name="mlstm",
    )(*args)


def _merge_kernel(x_ref, mod_ref, g_ref, ya_ref, yb_ref, yc_ref, yd_ref, wm_ref, bm_ref, wb_ref, wo_ref,
                  w1_ref, w2_ref, o_ref):
    x = x_ref[...]
    h = (_rms(x, g_ref[0:1, :]) * (1.0 + mod_ref[1:2, :]) + mod_ref[0:1, :]).astype(BF16)
    acc = jnp.zeros(x.shape, F32)
    for n, y_ref in enumerate((ya_ref, yb_ref, yc_ref, yd_ref)):
        cols = slice(n * D_MODEL, (n + 1) * D_MODEL)
        gate = _sigmoid(_dot(h, wm_ref[:, cols]) + bm_ref[:, cols])
        acc = acc + gate * _dot(y_ref[...].astype(BF16), wb_ref[n])
    y = _dot(acc.astype(BF16), wo_ref[...])
    x = x + mod_ref[2:3, :] * _rms(y, g_ref[1:2, :])
    h2 = (_rms(x, g_ref[2:3, :]) * (1.0 + mod_ref[4:5, :]) + mod_ref[3:4, :]).astype(BF16)
    f = jnp.zeros(x.shape, F32)
    for j in range(D_FF // D_MODEL):
        cols = slice(j * D_MODEL, (j + 1) * D_MODEL)
        a = jnp.maximum(_dot(h2, w1_ref[:, cols]), 0.0)
        f = f + _dot((a * a).astype(BF16), w2_ref[cols, :])
    o_ref[...] = x + mod_ref[5:6, :] * _rms(f, g_ref[3:4, :])


def _merge_ffn(x, mod, g, ys, wm, bm, wb, wo, w1, w2, layer, seq, per_seq):
    t = x.shape[0]
    tm = TOKEN_TILE
    full = lambda shape: _layer_block(shape, layer)
    tok = lambda w_: pl.BlockSpec((tm, w_), lambda i: (i, 0))
    return pl.pallas_call(
        _merge_kernel,
        grid=(t // tm,),
        in_specs=[tok(D_MODEL), _mod_spec(layer, tm, seq, per_seq),
                  full((4, D_MODEL)), tok(MIX_W), tok(MIX_W), tok(MIX_W), tok(MIX_W),
                  full((D_MODEL, N_BRANCH * D_MODEL)), full((1, N_BRANCH * D_MODEL)),
                  full((N_BRANCH, MIX_W, D_MODEL)), full((D_MODEL, D_MODEL)),
                  full((D_MODEL, D_FF)), full((D_FF, D_MODEL))],
        out_specs=tok(D_MODEL),
        out_shape=jax.ShapeDtypeStruct((t, D_MODEL), F32),
        compiler_params=pltpu.CompilerParams(vmem_limit_bytes=VMEM_LIMIT),
        name="merge_ffn",
    )(x, mod, g, *ys, wm, bm, wb, wo, w1, w2)


def _arrange_w_in(w):
    z = lambda n: jnp.zeros((D_MODEL, n), w.dtype)
    o_b, o_c = IN_A, IN_A + IN_B
    o_g, o_d = o_c + 4 * D_C, o_c + IN_C
    gates = w[:, o_g:o_d].reshape(D_MODEL, 2, 2, H_C).transpose(0, 2, 1, 3).reshape(D_MODEL, 4 * H_C)
    return jnp.concatenate([
        w[:, :Q_LORA + KV_LORA], z(64), w[:, Q_LORA + KV_LORA:IN_A], z(32),
        w[:, o_b:o_g], gates, z(ZG_W - 4 * H_C), w[:, o_d:]], axis=1).astype(BF16)


def _arrange_w_uq(w):
    w = w.reshape(Q_LORA, H_A, NOPE_A + ROPE_A)
    half = ROPE_A // 2
    swapped = jnp.concatenate([jnp.zeros_like(w[:, :, :NOPE_A]), w[:, :, NOPE_A + half:], w[:, :, NOPE_A:NOPE_A + half]],
                              axis=-1)
    pad = lambda a: jnp.pad(a, ((0, 0), (0, 0), (0, LANES - NOPE_A - ROPE_A))).reshape(Q_LORA, H_A * LANES)
    return jnp.concatenate([pad(w), pad(swapped)], axis=1).astype(BF16)


def _arrange_w_ukv(w):
    w = w.reshape(KV_LORA, H_A, NOPE_A + VH_A)
    wk = jnp.pad(w[:, :, :NOPE_A], ((0, 0), (0, 0), (0, LANES - NOPE_A)))
    v = w[:, :, NOPE_A:]
    zero = jnp.zeros_like(v)
    even = jnp.concatenate([v, zero], axis=-1)
    odd = jnp.concatenate([zero, v], axis=-1)
    wv = jnp.where((jnp.arange(H_A) % 2 == 0)[None, :, None], even, odd)
    return jnp.concatenate([wk.reshape(KV_LORA, -1), wv.reshape(KV_LORA, -1)], axis=1).astype(BF16)


def _rope_tables(rows):
    row = np.repeat(np.arange(rows), GRID_W).astype(np.float64)
    col = np.tile(np.arange(GRID_W), rows).astype(np.float64)
    nf = ROPE_A // 4
    inv = np.exp(-math.log(ROPE_BASE) * np.arange(nf, dtype=np.float64) / nf)
    ang = np.concatenate([row[:, None] * inv, col[:, None] * inv], axis=-1)
    cos, sin = np.cos(ang), np.sin(ang)
    n = cos.shape[0]
    ones, zeros = (lambda k: np.ones((n, k))), (lambda k: np.zeros((n, k)))
    mla = (np.concatenate([ones(64), cos, cos, ones(32)], axis=1),
           np.concatenate([zeros(64), -sin, zeros(48)], axis=1),
           np.concatenate([zeros(80), sin, zeros(32)], axis=1))
    dif = (np.tile(np.concatenate([cos, cos], axis=1), (1, 4)),
           np.tile(np.concatenate([-sin, zeros(16)], axis=1), (1, 4)),
           np.tile(np.concatenate([zeros(16), sin], axis=1), (1, 4)))
    as_f32 = lambda ts: tuple(jnp.asarray(t.astype(np.float32)) for t in ts)
    return as_f32(mla), as_f32(dif)


def kernel(x_prompt, x_sample, cache_mla_ckv, cache_mla_krope, cache_diff_k, cache_diff_v, state_mlstm_C,
           state_mlstm_n, state_mlstm_m, c, c_ctx, w_mod, b_mod, norm_g, w_in, mla_q_norm, w_uq, mla_kv_norm,
           w_ukv, gmlp_v_norm, gmlp_w_s, gmlp_b_s, mlstm_gate_bias, mlstm_head_norm, diff_lambda, diff_sub_norm,
           w_branch, w_merge, b_merge, w_out, w_ff1, w_ff2):
    bp, lp, _ = x_prompt.shape
    bs, ls, _ = x_sample.shape
    past = cache_mla_ckv.shape[2]

    cond = jnp.concatenate([c_ctx[None, :], c, jnp.zeros((16 - 1 - bs, D_MODEL), F32)], axis=0)
    mod = _modulation(cond, w_mod, b_mod).reshape(DEPTH, 16, 6, D_MODEL)
    rope_mla, rope_dif = _rope_tables(ls // GRID_W)

    row = lambda a: a[:, None, :]
    w_in_a = jax.vmap(_arrange_w_in)(w_in)
    wuq_a = jax.vmap(_arrange_w_uq)(w_uq)
    wukv_a = jax.vmap(_arrange_w_ukv)(w_ukv)
    qn, kvn, vn, hn, bm = row(mla_q_norm), row(mla_kv_norm), row(gmlp_v_norm), row(mlstm_head_norm), row(b_merge)
    ws = gmlp_w_s.astype(BF16)
    bias = jnp.repeat(jnp.swapaxes(gmlp_b_s, 1, 2), D_B // G_B, axis=2)
    gb = jnp.pad(mlstm_gate_bias.transpose(0, 2, 1, 3).reshape(DEPTH, 1, 4 * H_C),
                 ((0, 0), (0, 0), (0, ZG_W - 4 * H_C)))
    sn = row(jnp.tile(diff_sub_norm, (1, 2)))
    wm, wb, wo = w_merge.astype(BF16), w_branch.astype(BF16), w_out.astype(BF16)
    w1, w2 = w_ff1.astype(BF16), w_ff2.astype(BF16)

    ctx_kr = jnp.pad(cache_mla_krope, ((0, 0), (0, 0), (0, 0), (64, LANES - 64 - ROPE_A)))
    ctx_dk = cache_diff_k.reshape(bs, DEPTH, past, D_D)
    ctx_dv = cache_diff_v.reshape(bs, DEPTH, past, D_D)
    c0 = state_mlstm_C.reshape(bs, DEPTH, N_UNITS, DH_C, DH_C)
    n0 = state_mlstm_n.reshape(bs, DEPTH, N_UNITS, DH_C)
    m0 = jnp.broadcast_to(state_mlstm_m.reshape(bs, DEPTH, N_UNITS, 1), (bs, DEPTH, N_UNITS, LANES))

    xp = x_prompt.reshape(bp * lp, D_MODEL)
    xs = x_sample.reshape(bs * ls, D_MODEL)
    ents = []
    for l in range(DEPTH):
        lam_init = 0.8 - 0.6 * math.exp(-0.3 * l)
        for is_sample in (False, True):
            x = xs if is_sample else xp
            batch, seq = (bs, ls) if is_sample else (bp, lp)
            za, yb, zc, zg, dq, dk, dv, kr = _in_proj(x, mod, norm_g, w_in_a, vn, ws, bias, l, seq, is_sample)
            if is_sample:
                (ya,) = _mla(za, qn, kvn, wuq_a, wukv_a, l, batch, seq, rope=rope_mla, ctx=(cache_mla_ckv, ctx_kr))
                yd = _diff(dq, dk, dv, diff_lambda, sn, l, batch, seq, lam_init, rope=rope_dif, ctx=(ctx_dk, ctx_dv))
                (yc,) = _mlstm(zc, zg, gb, hn, l, batch, seq, state=(c0, n0, m0))
            else:
                ya, ckv = _mla(za, qn, kvn, wuq_a, wukv_a, l, batch, seq)
                yd = _diff(dq, dk, dv, diff_lambda, sn, l, batch, seq, lam_init)
                yc, c_new, n_new, m_new = _mlstm(zc, zg, gb, hn, l, batch, seq)
                ents.append((
                    ckv.reshape(bp, lp, KV_LORA),
                    kr.reshape(bp, lp, ROPE_A),
                    dk.reshape(bp, lp, H_D, 2, DH_D),
                    dv.reshape(bp, lp, H_D, 2 * DH_D),
                    c_new.reshape(bp, 2, H_C, DH_C, DH_C),
                    n_new.reshape(bp, 2, H_C, DH_C),
                    m_new[:, :, 0].reshape(bp, 2, H_C)))
            x = _merge_ffn(x, mod, norm_g, (ya, yb, yc, yd), wm, bm, wb, wo, w1, w2, l, seq, is_sample)
            if is_sample:
                xs = x
            else:
                xp = x

    stack = lambda j: jnp.stack([e[j] for e in ents], axis=1)
    return (xp.reshape(bp, lp, D_MODEL), xs.reshape(bs, ls, D_MODEL),
            stack(0), stack(1), stack(2), stack(3), stack(4), stack(5), stack(6))
```

```python
import functools
import math

import jax
import jax.numpy as jnp
import numpy as np
from jax import lax
from jax.experimental import pallas as pl
from jax.experimental.pallas import tpu as pltpu

F32 = jnp.float32
BF16 = jnp.bfloat16

D_MODEL = 1024
DEPTH = 2
GRID_W = 64
N_BRANCH = 4
MIX_W = 256
H_A, NOPE_A, ROPE_A, VH_A = 4, 64, 32, 64
Q_LORA, KV_LORA = 256, 128
D_B, G_B, CHUNK_B = 256, 4, 128
H_C, DH_C = 4, 64
D_C = H_C * DH_C
H_D, DH_D = 4, 32
D_D = H_D * 2 * DH_D
D_FF = 4 * D_MODEL
IN_A = Q_LORA + KV_LORA + ROPE_A
IN_B = 2 * D_B
IN_C = 4 * D_C + 4 * H_C
IN_D = 3 * D_D
ROPE_BASE = 10000.0
EPS = 1e-6
MLA_SCALE = (NOPE_A + ROPE_A) ** -0.5
DIFF_SCALE = DH_D ** -0.5

LANES = 128
MLSTM_CHUNK = 128
VMEM_LIMIT = 56 * 1024 * 1024
NEG_INF = float("-inf")

ZA_W = 512
ZG_W = 128
Z_OFF_A = 0
Z_OFF_B = Z_OFF_A + ZA_W
Z_OFF_C = Z_OFF_B + IN_B
Z_OFF_G = Z_OFF_C + 4 * D_C
Z_OFF_D = Z_OFF_G + ZG_W
Z_W = Z_OFF_D + IN_D


def _rms(x, g):
    return x * lax.rsqrt(jnp.mean(x * x, axis=-1, keepdims=True) + EPS) * g


def _sigmoid(x):
    return 1.0 / (1.0 + jnp.exp(-x))


def _log_sigmoid(x):
    return jnp.minimum(x, 0.0) - jnp.log1p(jnp.exp(-jnp.abs(x)))


def _dot(a, b):
    return jnp.dot(a, b, preferred_element_type=F32)


def _dot_nt(a, b):
    return lax.dot_general(a, b, (((1,), (1,)), ((), ())), preferred_element_type=F32)


def _lane_group_mask(width, group, index):
    lane = lax.broadcasted_iota(jnp.int32, (1, width), 1)
    return (lane >= index * group) & (lane < (index + 1) * group)


def _mod_kernel(cond_ref, w_ref, b_ref, o_ref):
    c = cond_ref[...]
    s = c * _sigmoid(c)
    o_ref[...] = _dot(s.astype(BF16), w_ref[...].astype(BF16)) + b_ref[...]


def _modulation(cond, w_mod, b_mod):
    rows = cond.shape[0]
    nb = 1024
    return pl.pallas_call(
        _mod_kernel,
        grid=(DEPTH, 6 * D_MODEL // nb),
        in_specs=[
            pl.BlockSpec((rows, D_MODEL), lambda l, j: (0, 0)),
            pl.BlockSpec((None, D_MODEL, nb), lambda l, j: (l, 0, j)),
            pl.BlockSpec((None, 1, nb), lambda l, j: (l, 0, j)),
        ],
        out_specs=pl.BlockSpec((None, rows, nb), lambda l, j: (l, 0, j)),
        out_shape=jax.ShapeDtypeStruct((DEPTH, rows, 6 * D_MODEL), F32),
        name="modulation",
    )(cond, w_mod, b_mod.reshape(DEPTH, 1, 6 * D_MODEL))


TOKEN_TILE = 512


def _layer_block(shape, layer):
    shape = tuple(shape)
    return pl.BlockSpec((None,) + shape, lambda *_: (layer,) + (0,) * len(shape), pipeline_mode=pl.Buffered(1))


def _mod_spec(layer, tm, seq, per_seq):
    def index(i):
        return (layer, 1 + (i * tm) // seq if per_seq else 0, 0, 0)
    return pl.BlockSpec((None, None, 6, D_MODEL), index)


def _gmlp_tile(zb, vn, ws_ref, bias):
    tm = zb.shape[0]
    v = _rms(zb[:, D_B:], vn).astype(BF16)
    gmasks = [_lane_group_mask(D_B, D_B // G_B, g) for g in range(G_B)]
    out = []
    for ch in range(tm // CHUNK_B):
        vc = v[ch * CHUNK_B:(ch + 1) * CHUNK_B, :]
        mixed = bias
        for g in range(G_B):
            mixed = mixed + jnp.where(gmasks[g], _dot(ws_ref[g], vc), 0.0)
        out.append(zb[ch * CHUNK_B:(ch + 1) * CHUNK_B, :D_B] * mixed)
    return jnp.concatenate(out, axis=0)


def _in_kernel(x_ref, mod_ref, g_ref, w_ref, vn_ref, ws_ref, bias_ref,
               za_ref, yb_ref, zc_ref, zg_ref, dq_ref, dk_ref, dv_ref):
    x = x_ref[...]
    h = _rms(x, g_ref[0:1, :]) * (1.0 + mod_ref[1:2, :]) + mod_ref[0:1, :]
    hb = h.astype(BF16)
    za_ref[...] = _dot(hb, w_ref[:, Z_OFF_A:Z_OFF_B])
    zc_ref[...] = _dot(hb, w_ref[:, Z_OFF_C:Z_OFF_G])
    zg_ref[...] = _dot(hb, w_ref[:, Z_OFF_G:Z_OFF_D])
    for j, d_ref in enumerate((dq_ref, dk_ref, dv_ref)):
        d_ref[...] = _dot(hb, w_ref[:, Z_OFF_D + j * D_D:Z_OFF_D + (j + 1) * D_D])
    yb_ref[...] = _gmlp_tile(_dot(hb, w_ref[:, Z_OFF_B:Z_OFF_C]), vn_ref[...], ws_ref, bias_ref[...])


def _in_proj(x, mod, g, w, vn, ws, bias, layer, seq, per_seq):
    t = x.shape[0]
    tm = TOKEN_TILE
    widths = (ZA_W, MIX_W, 4 * D_C, ZG_W, D_D, D_D, D_D)
    return pl.pallas_call(
        _in_kernel,
        grid=(t // tm,),
        in_specs=[
            pl.BlockSpec((tm, D_MODEL), lambda i: (i, 0)),
            _mod_spec(layer, tm, seq, per_seq),
            _layer_block((4, D_MODEL), layer),
            _layer_block((D_MODEL, Z_W), layer),
            _layer_block((1, D_B), layer),
            _layer_block((G_B, CHUNK_B, CHUNK_B), layer),
            _layer_block((CHUNK_B, D_B), layer),
        ],
        out_specs=[pl.BlockSpec((tm, w_), lambda i: (i, 0)) for w_ in widths],
        out_shape=[jax.ShapeDtypeStruct((t, w_), F32) for w_ in widths],
        compiler_params=pltpu.CompilerParams(vmem_limit_bytes=VMEM_LIMIT),
        name="in_proj",
    )(x, mod, g, w, vn, ws, bias)


LOG2E = math.log2(math.e)


def _with_ones(v, lo_mask, keep_lo):
    lane = lax.broadcasted_iota(jnp.int32, (1, LANES), 1)
    if keep_lo:
        return jnp.where(lane == 64, 1.0, jnp.where(lo_mask, v, 0.0))
    return jnp.where(lane == 0, 1.0, jnp.where(lo_mask, 0.0, v))


def _pair_normalise(o_even, o_odd, lo_mask):
    return jnp.where(lo_mask, o_even / o_even[:, 64:65], o_odd / o_odd[:, 0:1])


def _rope128(x, tc, ts1, ts2):
    return x * tc + pltpu.roll(x, LANES - 16, 1) * ts1 + pltpu.roll(x, 16, 1) * ts2


def _mla_kernel(*refs, seq, has_ctx, tq):
    if has_ctx:
        (za_ref, qn_ref, kvn_ref, wuq_ref, wukv_ref, tc_ref, ts1_ref, ts2_ref, cckv_ref, ckr_ref,
         ya_ref, q_s, k_s, v_s, ckv_s, kr_s) = refs
        past = cckv_ref.shape[0]
    else:
        (za_ref, qn_ref, kvn_ref, wuq_ref, wukv_ref, ya_ref, ckv_ref, krt_ref, q_s, k_s, v_s, ckv_s, kr_s) = refs
        past = 0
    za = za_ref[...]
    cq = _rms(za[:, :Q_LORA], qn_ref[...])
    qh = _dot(cq.astype(BF16), wuq_ref[:, 0:(2 if has_ctx else 1) * H_A * LANES])
    ckv = _rms(za[:, Q_LORA:Q_LORA + KV_LORA], kvn_ref[...])
    kr = za[:, Q_LORA + KV_LORA:ZA_W]
    if has_ctx:
        tc, ts1, ts2 = tc_ref[...], ts1_ref[...], ts2_ref[...]
        kr = _rope128(kr, tc, ts1, ts2)
        ckv_s[0:past, :] = cckv_ref[...]
        kr_s[0:past, :] = ckr_ref[...]
    else:
        ckv_ref[...] = ckv
        krt_ref[...] = kr.T[64:64 + ROPE_A, :]
    ckv_s[past:past + seq, :] = ckv
    kr_s[past:past + seq, :] = kr
    kv = _dot(ckv_s[...].astype(BF16), wukv_ref[...])
    kr_all = kr_s[...]
    lo = _lane_group_mask(LANES, 64, 0)
    for h in range(H_A):
        qg = qh[:, h * LANES:(h + 1) * LANES]
        if has_ctx:
            qg = qg * tc + qh[:, (H_A + h) * LANES:(H_A + h + 1) * LANES] * (ts1 + ts2)
        q_s[h] = (qg * (MLA_SCALE * LOG2E)).astype(BF16)
        k_s[h] = (kv[:, h * LANES:(h + 1) * LANES] + kr_all).astype(BF16)
        v_s[h] = _with_ones(kv[:, (H_A + h) * LANES:(H_A + h + 1) * LANES], lo, h % 2 == 0).astype(BF16)

    def body(i, carry):
        rows = pl.ds(pl.multiple_of(i * tq, tq), tq)
        s = [_dot_nt(q_s[h, rows, :], k_s[h]) for h in range(H_A)]
        o = [_dot(jnp.exp2(s[h] - jnp.max(s[h], axis=-1, keepdims=True)).astype(BF16), v_s[h]) for h in range(H_A)]
        for pair in range(H_A // 2):
            ya_ref[rows, pair * LANES:(pair + 1) * LANES] = _pair_normalise(o[2 * pair], o[2 * pair + 1], lo)
        return carry

    lax.fori_loop(0, seq // tq, body, 0)


def _ctx_block(past, width, layer):
    return pl.BlockSpec((None, None, past, width), lambda b: (b, layer, 0, 0))


def _mla(za, qn, kvn, wuq, wukv, layer, batch, seq, rope=None, ctx=None):
    has_ctx = ctx is not None
    past = ctx[0].shape[2] if has_ctx else 0
    lk = past + seq
    tq = min(seq, 256)
    full = lambda shape: pl.BlockSpec(shape, lambda b: (0,) * len(shape))
    in_specs = [
        pl.BlockSpec((seq, ZA_W), lambda b: (b, 0)),
        _layer_block((1, Q_LORA), layer), _layer_block((1, KV_LORA), layer),
        _layer_block((Q_LORA, 2 * H_A * LANES), layer), _layer_block((KV_LORA, 2 * H_A * LANES), layer),
    ]
    args = [za, qn, kvn, wuq, wukv]
    out_specs = [pl.BlockSpec((seq, MIX_W), lambda b: (b, 0))]
    out_shape = [jax.ShapeDtypeStruct((batch * seq, MIX_W), F32)]
    if has_ctx:
        in_specs += [full((seq, LANES))] * 3
        in_specs += [_ctx_block(past, LANES, layer)] * 2
        args += list(rope) + list(ctx)
    else:
        out_specs += [pl.BlockSpec((seq, KV_LORA), lambda b: (b, 0)),
                      pl.BlockSpec((None, ROPE_A, seq), lambda b: (b, 0, 0))]
        out_shape += [jax.ShapeDtypeStruct((batch * seq, KV_LORA), F32),
                      jax.ShapeDtypeStruct((batch, ROPE_A, seq), F32)]
    return pl.pallas_call(
        functools.partial(_mla_kernel, seq=seq, has_ctx=has_ctx, tq=tq),
        grid=(batch,),
        in_specs=in_specs,
        out_specs=out_specs,
        out_shape=out_shape,
        scratch_shapes=[
            pltpu.VMEM((H_A, seq, LANES), BF16),
            pltpu.VMEM((H_A, lk, LANES), BF16),
            pltpu.VMEM((H_A, lk, LANES), BF16),
            pltpu.VMEM((lk, LANES), F32),
            pltpu.VMEM((lk, LANES), F32),
        ],
        compiler_params=pltpu.CompilerParams(vmem_limit_bytes=VMEM_LIMIT),
        name="mla",
    )(*args)


def _diff_kernel(*refs, seq, has_ctx, tq, lam_init):
    if has_ctx:
        (dq_ref, dk_ref, dv_ref, lam_ref, sn_ref, tc_ref, ts1_ref, ts2_ref, ck_ref, cv_ref,
         yd_ref, q_s, k_s, v_s) = refs
        past = ck_ref.shape[0]
    else:
        (dq_ref, dk_ref, dv_ref, lam_ref, sn_ref, yd_ref, dkt_ref, dvt_ref, q_s, k_s, v_s) = refs
        past = 0
        dkt_ref[...] = dk_ref[...].T
        dvt_ref[...] = dv_ref[...].T
    lo = _lane_group_mask(LANES, 64, 0)
    if has_ctx:
        tc, ts1, ts2 = tc_ref[...], ts1_ref[...], ts2_ref[...]
        k_s[0:past, :] = ck_ref[...].astype(BF16)
        cv = cv_ref[...]
    for half in range(2):
        cols = slice(half * LANES, (half + 1) * LANES)
        q, k, v = dq_ref[:, cols], dk_ref[:, cols], dv_ref[:, cols]
        if has_ctx:
            q = _rope128(q, tc, ts1, ts2)
            k = _rope128(k, tc, ts1, ts2)
            v_s[2 * half, 0:past, :] = _with_ones(cv[:, cols], lo, True).astype(BF16)
            v_s[2 * half + 1, 0:past, :] = _with_ones(cv[:, cols], lo, False).astype(BF16)
        q_s[:, cols] = (q * (DIFF_SCALE * LOG2E)).astype(BF16)
        k_s[past:past + seq, cols] = k.astype(BF16)
        v_s[2 * half, past:past + seq, :] = _with_ones(v, lo, True).astype(BF16)
        v_s[2 * half + 1, past:past + seq, :] = _with_ones(v, lo, False).astype(BF16)

    lam = lam_ref[...]
    lam_val = (jnp.exp(jnp.sum(lam[0:1] * lam[1:2], axis=-1, keepdims=True))
               - jnp.exp(jnp.sum(lam[2:3] * lam[3:4], axis=-1, keepdims=True)) + lam_init)
    comp_masks = [jnp.where(_lane_group_mask(D_D, DH_D, j), 1.0, 0.0).astype(BF16) for j in range(2 * H_D)]
    sn = sn_ref[...]

    def body(i, carry):
        rows = pl.ds(pl.multiple_of(i * tq, tq), tq)
        qb = q_s[rows, :]
        k_all = k_s[...]
        score = lambda h: [_dot_nt(qb * comp_masks[2 * h + comp], k_all) for comp in range(2)]
        s_next = score(0)
        o = []
        for h in range(H_D):
            s = s_next
            if h + 1 < H_D:
                s_next = score(h + 1)
            o1, o2 = [_dot(jnp.exp2(sc - jnp.max(sc, axis=-1, keepdims=True)).astype(BF16), v_s[h]) for sc in s]
            ll = 64 if h % 2 == 0 else 0
            o.append(o1 / o1[:, ll:ll + 1] - lam_val * (o2 / o2[:, ll:ll + 1]))
        for pair in range(H_D // 2):
            acc = jnp.where(lo, o[2 * pair], o[2 * pair + 1])
            sq = acc * acc
            ss_lo = jnp.sum(jnp.where(lo, sq, 0.0), axis=-1, keepdims=True)
            ss_hi = jnp.sum(jnp.where(lo, 0.0, sq), axis=-1, keepdims=True)
            r = jnp.where(lo, lax.rsqrt(ss_lo / (2 * DH_D) + EPS), lax.rsqrt(ss_hi / (2 * DH_D) + EPS))
            yd_ref[rows, pair * LANES:(pair + 1) * LANES] = acc * r * sn * (1.0 - lam_init)
        return carry

    lax.fori_loop(0, seq // tq, body, 0)


def _diff(dq, dk, dv, lam, sn, layer, batch, seq, lam_init, rope=None, ctx=None):
    has_ctx = ctx is not None
    past = ctx[0].shape[2] if has_ctx else 0
    lk = past + seq
    tq = min(seq, 256)
    full = lambda shape: pl.BlockSpec(shape, lambda b: (0,) * len(shape))
    in_specs = [pl.BlockSpec((seq, D_D), lambda b: (b, 0))] * 3
    in_specs += [_layer_block((4, DH_D), layer), _layer_block((1, LANES), layer)]
    args = [dq, dk, dv, lam, sn]
    if has_ctx:
        in_specs += [full((seq, LANES))] * 3
        in_specs += [_ctx_block(past, D_D, layer)] * 2
        args += list(rope) + list(ctx)
    out_specs = [pl.BlockSpec((seq, MIX_W), lambda b: (b, 0))]
    out_shape = [jax.ShapeDtypeStruct((batch * seq, MIX_W), F32)]
    if not has_ctx:
        out_specs += [pl.BlockSpec((None, D_D, seq), lambda b: (b, 0, 0))] * 2
        out_shape += [jax.ShapeDtypeStruct((batch, D_D, seq), F32)] * 2
    return pl.pallas_call(
        functools.partial(_diff_kernel, seq=seq, has_ctx=has_ctx, tq=tq, lam_init=lam_init),
        grid=(batch,),
        in_specs=in_specs,
        out_specs=out_specs,
        out_shape=out_shape,
        scratch_shapes=[
            pltpu.VMEM((seq, D_D), BF16),
            pltpu.VMEM((lk, D_D), BF16),
            pltpu.VMEM((H_D, lk, LANES), BF16),
        ],
        compiler_params=pltpu.CompilerParams(vmem_limit_bytes=VMEM_LIMIT),
        name="diff_attn",
    )(*args)


def _split3(x):
    hi = x.astype(BF16)
    r1 = x - hi.astype(F32)
    mid = r1.astype(BF16)
    lo = (r1 - mid.astype(F32)).astype(BF16)
    return hi, mid, lo


ST_ROWS = DH_C + 8
N_UNITS = 2 * H_C


def _mlstm_kernel(*refs, seq, has_state):
    ck = MLSTM_CHUNK
    nc = seq // ck
    if has_state:
        (zc_ref, zg_ref, gb_ref, hn_ref, c0_ref, n0_ref, m0_ref,
         yc_ref, k_s, qt_s, kt_s, vt_s, rcc_s, ca_s, wi_s, en_s, ws_s, wo_s, st_s, ht_s) = refs
    else:
        (zc_ref, zg_ref, gb_ref, hn_ref,
         yc_ref, cout_ref, nout_ref, mout_ref,
         k_s, qt_s, kt_s, vt_s, rcc_s, ca_s, wi_s, en_s, ws_s, wo_s, st_s, ht_s) = refs
    eye = jnp.where(lax.broadcasted_iota(jnp.int32, (DH_C, DH_C), 0) == lax.broadcasted_iota(jnp.int32, (DH_C, DH_C), 1),
                    1.0, 0.0).astype(BF16)

    def transpose_exact(a):
        return sum(_dot_nt(eye, part) for part in _split3(a))

    for c in range(nc):
        rows = slice(c * ck, (c + 1) * ck)
        qt_s[c] = zc_ref[rows, 0:D_C].T.astype(BF16)
        kk = zc_ref[rows, D_C:2 * D_C] * (DH_C ** -0.5)
        k_s[rows, :] = kk.astype(BF16)
        kt_s[c] = kk.T.astype(BF16)
        vt_s[c] = zc_ref[rows, 2 * D_C:3 * D_C].T.astype(BF16)

    gates = zg_ref[...] + gb_ref[...]
    r_i = lax.broadcasted_iota(jnp.int32, (ck, ck), 0)
    c_i = lax.broadcasted_iota(jnp.int32, (ck, ck), 1)
    sum_fw = jnp.where(r_i <= c_i, 1.0, 0.0).astype(BF16)
    sum_bw = jnp.where(r_i >= c_i, 1.0, 0.0).astype(BF16)
    row8 = lax.broadcasted_iota(jnp.int32, (N_UNITS, ck), 0)
    fw8 = row8 < H_C
    fw81 = fw8[:, 0:1]
    i_rows, b_rows = [], []
    for c in range(nc):
        rows = slice(c * ck, (c + 1) * ck)
        g_t = gates[rows, :].T
        i8 = g_t[0:N_UNITS, :]
        p0, p1, p2 = _split3(_log_sigmoid(g_t[N_UNITS:2 * N_UNITS, :]))
        b_fw = _dot(p0, sum_fw) + _dot(p1, sum_fw) + _dot(p2, sum_fw)
        b_bw = _dot(p0, sum_bw) + _dot(p1, sum_bw) + _dot(p2, sum_bw)
        b8 = jnp.where(fw8, b_fw, b_bw)
        rcc_s[rows, :] = jnp.concatenate([i8 - b8, jnp.zeros((ck - N_UNITS, ck), F32)], axis=0).T
        i_rows.append(i8)
        b_rows.append(b8)

    b_steps = [jnp.where(fw8, b_rows[i], b_rows[nc - 1 - i]) for i in range(nc)]
    r_steps = [jnp.where(fw8, i_rows[i], i_rows[nc - 1 - i]) - b_steps[i] for i in range(nc)]
    pm = sm = jnp.concatenate(r_steps, axis=0)
    lane_all = lax.broadcasted_iota(jnp.int32, pm.shape, 1)
    sh = 1
    while sh < ck:
        pm = jnp.maximum(pm, jnp.where(lane_all >= sh, pltpu.roll(pm, sh, 1), NEG_INF))
        sm = jnp.maximum(sm, jnp.where(lane_all < ck - sh, pltpu.roll(sm, ck - sh, 1), NEG_INF))
        sh *= 2
    m_prev = m0_ref[:, 0:1] if has_state else jnp.zeros((N_UNITS, 1), F32)
    for i in range(nc):
        b8, r8 = b_steps[i], r_steps[i]
        pm8, sm8 = pm[i * N_UNITS:(i + 1) * N_UNITS, :], sm[i * N_UNITS:(i + 1) * N_UNITS, :]
        cm = jnp.where(fw8, pm8, sm8)
        b_t = jnp.where(fw81, b8[:, ck - 1:ck], b8[:, 0:1])
        cm_end = jnp.where(fw81, pm8[:, ck - 1:ck], sm8[:, 0:1])
        inter = b8 + m_prev
        mt = jnp.maximum(inter, b8 + cm)
        ca_s[i] = b8 - mt
        wi_s[i] = jnp.exp(inter - mt)
        en_s[i] = jnp.exp(-mt)
        m_new = b_t + jnp.maximum(m_prev, cm_end)
        ws_s[i] = jnp.exp(b_t + r8 - m_new)
        wo_s[i] = jnp.broadcast_to(jnp.exp(b_t + m_prev - m_new), (N_UNITS, ck))
        m_prev = m_new

    if has_state:
        for u in range(N_UNITS):
            st_s[u, 0:DH_C, :] = transpose_exact(c0_ref[u])
            st_s[u, DH_C:ST_ROWS, :] = jnp.broadcast_to(n0_ref[u:u + 1, :], (ST_ROWS - DH_C, DH_C))
    else:
        st_s[...] = jnp.zeros_like(st_s)

    hmask_bf = [jnp.where(_lane_group_mask(D_C, DH_C, h), 1.0, 0.0).astype(BF16) for h in range(H_C)]
    src_ok = (r_i <= c_i, r_i >= c_i)

    def body(i, carry):
        ca, wi, en, ws, wo = ca_s[i], wi_s[i], en_s[i], ws_s[i], wo_s[i]
        chunks = (i, nc - 1 - i)
        units = [(d, h) for d in range(2) for h in range(H_C)]
        hs = [slice(h * DH_C, (h + 1) * DH_C) for h in range(H_C)]
        q_t = [qt_s[c] for c in chunks]
        k_t = [kt_s[c] for c in chunks]
        v_t = [vt_s[c] for c in chunks]
        sts, qcts, states, upds = [], [], [], []
        for d, h in units:
            rows = pl.ds(pl.multiple_of(chunks[d] * ck, ck), ck)
            sts.append(_dot(k_s[rows, :] * hmask_bf[h], q_t[d]))
            states.append(st_s[d * H_C + h])
            qcts.append(_dot(states[-1].astype(BF16), q_t[d][hs[h], :]))
        for u, (d, h) in enumerate(units):
            w_s = ws[u:u + 1, :]
            vw = jnp.concatenate([v_t[d][hs[h], :].astype(F32) * w_s, jnp.broadcast_to(w_s, (8, ck))], axis=0)
            upds.append(_dot_nt(vw.astype(BF16), k_t[d][hs[h], :]))
        pts = []
        for u, (d, h) in enumerate(units):
            rows = pl.ds(pl.multiple_of(chunks[d] * ck, ck), ck)
            decay = jnp.exp(rcc_s[rows, u:u + 1] + ca[u:u + 1, :])
            pts.append(sts[u] * jnp.where(src_ok[d], decay, 0.0))
        nums = [_dot(v_t[d][hs[h], :], pts[u].astype(BF16)) for u, (d, h) in enumerate(units)]
        for u, (d, h) in enumerate(units):
            w_i = wi[u:u + 1, :]
            den = w_i * qcts[u][DH_C:DH_C + 1, :] + jnp.sum(pts[u], axis=0, keepdims=True)
            num = w_i * qcts[u][0:DH_C, :] + nums[u]
            ht_s[d, chunks[d], hs[h], :] = num / jnp.maximum(jnp.abs(den), en[u:u + 1, :])
            st_s[u] = wo[u:u + 1, 0:DH_C] * states[u] + upds[u]
        return carry

    lax.fori_loop(0, nc, body, 0)

    hn = hn_ref[...]
    for c in range(nc):
        rows = slice(c * ck, (c + 1) * ck)
        h_t = ht_s[0, c] + ht_s[1, c]
        parts = []
        for h in range(H_C):
            x = h_t[h * DH_C:(h + 1) * DH_C, :]
            parts.append(x * lax.rsqrt(jnp.mean(x * x, axis=0, keepdims=True) + EPS))
        h_n = jnp.concatenate(parts, axis=0).T
        yc_ref[rows, :] = _sigmoid(zc_ref[rows, 3 * D_C:4 * D_C]) * (h_n * hn)

    if not has_state:
        for u in range(N_UNITS):
            cout_ref[u] = transpose_exact(st_s[u, 0:DH_C, :])
            nout_ref[u:u + 1, :] = st_s[u, DH_C:DH_C + 1, :]
        mout_ref[...] = jnp.broadcast_to(m_prev, (N_UNITS, LANES))


def _mlstm(zc, zg, gb, hn, layer, batch, seq, state=None):
    has_state = state is not None
    nc = seq // MLSTM_CHUNK
    in_specs = [pl.BlockSpec((seq, 4 * D_C), lambda b: (b, 0)), pl.BlockSpec((seq, ZG_W), lambda b: (b, 0)),
                _layer_block((1, ZG_W), layer), _layer_block((1, D_C), layer)]
    args = [zc, zg, gb, hn]
    out_specs = [pl.BlockSpec((seq, MIX_W), lambda b: (b, 0))]
    out_shape = [jax.ShapeDtypeStruct((batch * seq, MIX_W), F32)]
    if has_state:
        in_specs += [pl.BlockSpec((None, None, N_UNITS, DH_C, DH_C), lambda b: (b, layer, 0, 0, 0)),
                     pl.BlockSpec((None, None, N_UNITS, DH_C), lambda b: (b, layer, 0, 0)),
                     pl.BlockSpec((None, None, N_UNITS, LANES), lambda b: (b, layer, 0, 0))]
        args += list(state)
    else:
        out_specs += [pl.BlockSpec((None, N_UNITS, DH_C, DH_C), lambda b: (b, 0, 0, 0)),
                      pl.BlockSpec((None, N_UNITS, DH_C), lambda b: (b, 0, 0)),
                      pl.BlockSpec((None, N_UNITS, LANES), lambda b: (b, 0, 0))]
        out_shape += [jax.ShapeDtypeStruct((batch, N_UNITS, DH_C, DH_C), F32),
                      jax.ShapeDtypeStruct((batch, N_UNITS, DH_C), F32),
                      jax.ShapeDtypeStruct((batch, N_UNITS, LANES), F32)]
    step = lambda: pltpu.VMEM((nc, N_UNITS, MLSTM_CHUNK), F32)
    return pl.pallas_call(
        functools.partial(_mlstm_kernel, seq=seq, has_state=has_state),
        grid=(batch,),
        in_specs=in_specs,
        out_specs=out_specs,
        out_shape=out_shape,
        scratch_shapes=[
            pltpu.VMEM((seq, D_C), BF16),
            pltpu.VMEM((nc, D_C, MLSTM_CHUNK), BF16), pltpu.VMEM((nc, D_C, MLSTM_CHUNK), BF16),
            pltpu.VMEM((nc, D_C, MLSTM_CHUNK), BF16),
            pltpu.VMEM((seq, ZG_W), F32),
            step(), step(), step(), step(), step(),
            pltpu.VMEM((N_UNITS, ST_ROWS, DH_C), F32),
            pltpu.VMEM((2, nc, D_C, MLSTM_CHUNK), F32),
        ],
        compiler_params=pltpu.CompilerParams(vmem_limit_bytes=VMEM_LIMIT),
        name="mlstm",
    )(*args)


def _merge_kernel(x_ref, mod_ref, g_ref, ya_ref, yb_ref, yc_ref, yd_ref, wm_ref, bm_ref, wb_ref, wo_ref,
                  w1_ref, w2_ref, o_ref):
    x = x_ref[...]
    h = (_rms(x, g_ref[0:1, :]) * (1.0 + mod_ref[1:2, :]) + mod_ref[0:1, :]).astype(BF16)
    acc = jnp.zeros(x.shape, F32)
    for n, y_ref in enumerate((ya_ref, yb_ref, yc_ref, yd_ref)):
        cols = slice(n * D_MODEL, (n + 1) * D_MODEL)
        gate = _sigmoid(_dot(h, wm_ref[:, cols]) + bm_ref[:, cols])
        acc = acc + gate * _dot(y_ref[...].astype(BF16), wb_ref[n])
    y = _dot(acc.astype(BF16), wo_ref[...])
    x = x + mod_ref[2:3, :] * _rms(y, g_ref[1:2, :])
    h2 = (_rms(x, g_ref[2:3, :]) * (1.0 + mod_ref[4:5, :]) + mod_ref[3:4, :]).astype(BF16)
    f = jnp.zeros(x.shape, F32)
    for j in range(D_FF // D_MODEL):
        cols = slice(j * D_MODEL, (j + 1) * D_MODEL)
        a = jnp.maximum(_dot(h2, w1_ref[:, cols]), 0.0)
        f = f + _dot((a * a).astype(BF16), w2_ref[cols, :])
    o_ref[...] = x + mod_ref[5:6, :] * _rms(f, g_ref[3:4, :])


def _merge_ffn(x, mod, g, ys, wm, bm, wb, wo, w1, w2, layer, seq, per_seq):
    t = x.shape[0]
    tm = TOKEN_TILE
    full = lambda shape: _layer_block(shape, layer)
    tok = lambda w_: pl.BlockSpec((tm, w_), lambda i: (i, 0))
    return pl.pallas_call(
        _merge_kernel,
        grid=(t // tm,),
        in_specs=[tok(D_MODEL), _mod_spec(layer, tm, seq, per_seq),
                  full((4, D_MODEL)), tok(MIX_W), tok(MIX_W), tok(MIX_W), tok(MIX_W),
                  full((D_MODEL, N_BRANCH * D_MODEL)), full((1, N_BRANCH * D_MODEL)),
                  full((N_BRANCH, MIX_W, D_MODEL)), full((D_MODEL, D_MODEL)),
                  full((D_MODEL, D_FF)), full((D_FF, D_MODEL))],
        out_specs=tok(D_MODEL),
        out_shape=jax.ShapeDtypeStruct((t, D_MODEL), F32),
        compiler_params=pltpu.CompilerParams(vmem_limit_bytes=VMEM_LIMIT),
        name="merge_ffn",
    )(x, mod, g, *ys, wm, bm, wb, wo, w1, w2)


def _arrange_w_in(w):
    z = lambda n: jnp.zeros((D_MODEL, n), w.dtype)
    o_b, o_c = IN_A, IN_A + IN_B
    o_g, o_d = o_c + 4 * D_C, o_c + IN_C
    gates = w[:, o_g:o_d].reshape(D_MODEL, 2, 2, H_C).transpose(0, 2, 1, 3).reshape(D_MODEL, 4 * H_C)
    return jnp.concatenate([
        w[:, :Q_LORA + KV_LORA], z(64), w[:, Q_LORA + KV_LORA:IN_A], z(32),
        w[:, o_b:o_g], gates, z(ZG_W - 4 * H_C), w[:, o_d:]], axis=1).astype(BF16)


def _arrange_w_uq(w):
    w = w.reshape(Q_LORA, H_A, NOPE_A + ROPE_A)
    half = ROPE_A // 2
    swapped = jnp.concatenate([jnp.zeros_like(w[:, :, :NOPE_A]), w[:, :, NOPE_A + half:], w[:, :, NOPE_A:NOPE_A + half]],
                              axis=-1)
    pad = lambda a: jnp.pad(a, ((0, 0), (0, 0), (0, LANES - NOPE_A - ROPE_A))).reshape(Q_LORA, H_A * LANES)
    return jnp.concatenate([pad(w), pad(swapped)], axis=1).astype(BF16)


def _arrange_w_ukv(w):
    w = w.reshape(KV_LORA, H_A, NOPE_A + VH_A)
    wk = jnp.pad(w[:, :, :NOPE_A], ((0, 0), (0, 0), (0, LANES - NOPE_A)))
    v = w[:, :, NOPE_A:]
    zero = jnp.zeros_like(v)
    even = jnp.concatenate([v, zero], axis=-1)
    odd = jnp.concatenate([zero, v], axis=-1)
    wv = jnp.where((jnp.arange(H_A) % 2 == 0)[None, :, None], even, odd)
    return jnp.concatenate([wk.reshape(KV_LORA, -1), wv.reshape(KV_LORA, -1)], axis=1).astype(BF16)


def _rope_tables(rows):
    row = np.repeat(np.arange(rows), GRID_W).astype(np.float64)
    col = np.tile(np.arange(GRID_W), rows).astype(np.float64)
    nf = ROPE_A // 4
    inv = np.exp(-math.log(ROPE_BASE) * np.arange(nf, dtype=np.float64) / nf)
    ang = np.concatenate([row[:, None] * inv, col[:, None] * inv], axis=-1)
    cos, sin = np.cos(ang), np.sin(ang)
    n = cos.shape[0]
    ones, zeros = (lambda k: np.ones((n, k))), (lambda k: np.zeros((n, k)))
    mla = (np.concatenate([ones(64), cos, cos, ones(32)], axis=1),
           np.concatenate([zeros(64), -sin, zeros(48)], axis=1),
           np.concatenate([zeros(80), sin, zeros(32)], axis=1))
    dif = (np.tile(np.concatenate([cos, cos], axis=1), (1, 4)),
           np.tile(np.concatenate([-sin, zeros(16)], axis=1), (1, 4)),
           np.tile(np.concatenate([zeros(16), sin], axis=1), (1, 4)))
    as_f32 = lambda ts: tuple(jnp.asarray(t.astype(np.float32)) for t in ts)
    return as_f32(mla), as_f32(dif)


def kernel(x_prompt, x_sample, cache_mla_ckv, cache_mla_krope, cache_diff_k, cache_diff_v, state_mlstm_C,
           state_mlstm_n, state_mlstm_m, c, c_ctx, w_mod, b_mod, norm_g, w_in, mla_q_norm, w_uq, mla_kv_norm,
           w_ukv, gmlp_v_norm, gmlp_w_s, gmlp_b_s, mlstm_gate_bias, mlstm_head_norm, diff_lambda, diff_sub_norm,
           w_branch, w_merge, b_merge, w_out, w_ff1, w_ff2):
    bp, lp, _ = x_prompt.shape
    bs, ls, _ = x_sample.shape
    past = cache_mla_ckv.shape[2]

    cond = jnp.concatenate([c_ctx[None, :], c, jnp.zeros((16 - 1 - bs, D_MODEL), F32)], axis=0)
    mod = _modulation(cond, w_mod, b_mod).reshape(DEPTH, 16, 6, D_MODEL)
    rope_mla, rope_dif = _rope_tables(ls // GRID_W)

    row = lambda a: a[:, None, :]
    w_in_a = jax.vmap(_arrange_w_in)(w_in)
    wuq_a = jax.vmap(_arrange_w_uq)(w_uq)
    wukv_a = jax.vmap(_arrange_w_ukv)(w_ukv)
    qn, kvn, vn, hn, bm = row(mla_q_norm), row(mla_kv_norm), row(gmlp_v_norm), row(mlstm_head_norm), row(b_merge)
    ws = gmlp_w_s.astype(BF16)
    bias = jnp.repeat(jnp.swapaxes(gmlp_b_s, 1, 2), D_B // G_B, axis=2)
    gb = jnp.pad(mlstm_gate_bias.transpose(0, 2, 1, 3).reshape(DEPTH, 1, 4 * H_C),
                 ((0, 0), (0, 0), (0, ZG_W - 4 * H_C)))
    sn = row(jnp.tile(diff_sub_norm, (1, 2)))
    wm, wb, wo = w_merge.astype(BF16), w_branch.astype(BF16), w_out.astype(BF16)
    w1, w2 = w_ff1.astype(BF16), w_ff2.astype(BF16)

    ctx_kr = jnp.pad(cache_mla_krope, ((0, 0), (0, 0), (0, 0), (64, LANES - 64 - ROPE_A)))
    ctx_dk = cache_diff_k.reshape(bs, DEPTH, past, D_D)
    ctx_dv = cache_diff_v.reshape(bs, DEPTH, past, D_D)
    c0 = state_mlstm_C.reshape(bs, DEPTH, N_UNITS, DH_C, DH_C)
    n0 = state_mlstm_n.reshape(bs, DEPTH, N_UNITS, DH_C)
    m0 = jnp.broadcast_to(state_mlstm_m.reshape(bs, DEPTH, N_UNITS, 1), (bs, DEPTH, N_UNITS, LANES))

    xp = x_prompt.reshape(bp * lp, D_MODEL)
    xs = x_sample.reshape(bs * ls, D_MODEL)
    ents = []
    for l in range(DEPTH):
        lam_init = 0.8 - 0.6 * math.exp(-0.3 * l)
        for is_sample in (False, True):
            x = xs if is_sample else xp
            batch, seq = (bs, ls) if is_sample else (bp, lp)
            za, yb, zc, zg, dq, dk, dv = _in_proj(x, mod, norm_g, w_in_a, vn, ws, bias, l, seq, is_sample)
            if is_sample:
                (ya,) = _mla(za, qn, kvn, wuq_a, wukv_a, l, batch, seq, rope=rope_mla, ctx=(cache_mla_ckv, ctx_kr))
                (yd,) = _diff(dq, dk, dv, diff_lambda, sn, l, batch, seq, lam_init, rope=rope_dif, ctx=(ctx_dk, ctx_dv))
                (yc,) = _mlstm(zc, zg, gb, hn, l, batch, seq, state=(c0, n0, m0))
            else:
                ya, ckv, krt = _mla(za, qn, kvn, wuq_a, wukv_a, l, batch, seq)
                yd, dkt, dvt = _diff(dq, dk, dv, diff_lambda, sn, l, batch, seq, lam_init)
                yc, c_new, n_new, m_new = _mlstm(zc, zg, gb, hn, l, batch, seq)
                ents.append((
                    ckv.reshape(bp, lp, KV_LORA),
                    krt.transpose(0, 2, 1),
                    dkt.reshape(bp, H_D, 2, DH_D, lp).transpose(0, 4, 1, 2, 3),
                    dvt.reshape(bp, H_D, 2 * DH_D, lp).transpose(0, 3, 1, 2),
                    c_new.reshape(bp, 2, H_C, DH_C, DH_C),
                    n_new.reshape(bp, 2, H_C, DH_C),
                    m_new[:, :, 0].reshape(bp, 2, H_C)))
            x = _merge_ffn(x, mod, norm_g, (ya, yb, yc, yd), wm, bm, wb, wo, w1, w2, l, seq, is_sample)
            if is_sample:
                xs = x
            else:
                xp = x

    stack = lambda j: jnp.stack([e[j] for e in ents], axis=1)
    return (xp.reshape(bp, lp, D_MODEL), xs.reshape(bs, ls, D_MODEL),
            stack(0), stack(1), stack(2), stack(3), stack(4), stack(5), stack(6))
```

```python
import functools
import math

import jax
import jax.numpy as jnp
import numpy as np
from jax import lax
from jax.experimental import pallas as pl
from jax.experimental.pallas import tpu as pltpu

F32 = jnp.float32
BF16 = jnp.bfloat16

D_MODEL = 1024
DEPTH = 2
GRID_W = 64
N_BRANCH = 4
MIX_W = 256
H_A, NOPE_A, ROPE_A, VH_A = 4, 64, 32, 64
Q_LORA, KV_LORA = 256, 128
D_B, G_B, CHUNK_B = 256, 4, 128
H_C, DH_C = 4, 64
D_C = H_C * DH_C
H_D, DH_D = 4, 32
D_D = H_D * 2 * DH_D
D_FF = 4 * D_MODEL
IN_A = Q_LORA + KV_LORA + ROPE_A
IN_B = 2 * D_B
IN_C = 4 * D_C + 4 * H_C
IN_D = 3 * D_D
ROPE_BASE = 10000.0
EPS = 1e-6
MLA_SCALE = (NOPE_A + ROPE_A) ** -0.5
DIFF_SCALE = DH_D ** -0.5

LANES = 128
MLSTM_CHUNK = 128
VMEM_LIMIT = 56 * 1024 * 1024
NEG_INF = float("-inf")

ZA_W = 512
ZG_W = 128
Z_OFF_A = 0
Z_OFF_B = Z_OFF_A + ZA_W
Z_OFF_C = Z_OFF_B + IN_B
Z_OFF_G = Z_OFF_C + 4 * D_C
Z_OFF_D = Z_OFF_G + ZG_W
Z_W = Z_OFF_D + IN_D


def _rms(x, g):
    return x * lax.rsqrt(jnp.mean(x * x, axis=-1, keepdims=True) + EPS) * g


def _sigmoid(x):
    return 1.0 / (1.0 + jnp.exp(-x))


def _log_sigmoid(x):
    return jnp.minimum(x, 0.0) - jnp.log1p(jnp.exp(-jnp.abs(x)))


def _dot(a, b):
    return jnp.dot(a, b, preferred_element_type=F32)


def _dot_nt(a, b):
    return lax.dot_general(a, b, (((1,), (1,)), ((), ())), preferred_element_type=F32)


def _lane_group_mask(width, group, index):
    lane = lax.broadcasted_iota(jnp.int32, (1, width), 1)
    return (lane >= index * group) & (lane < (index + 1) * group)


def _mod_kernel(cond_ref, w_ref, b_ref, o_ref):
    c = cond_ref[...]
    s = c * _sigmoid(c)
    o_ref[...] = _dot(s.astype(BF16), w_ref[...].astype(BF16)) + b_ref[...]


def _modulation(cond, w_mod, b_mod):
    rows = cond.shape[0]
    nb = 1024
    return pl.pallas_call(
        _mod_kernel,
        grid=(DEPTH, 6 * D_MODEL // nb),
        in_specs=[
            pl.BlockSpec((rows, D_MODEL), lambda l, j: (0, 0)),
            pl.BlockSpec((None, D_MODEL, nb), lambda l, j: (l, 0, j)),
            pl.BlockSpec((None, 1, nb), lambda l, j: (l, 0, j)),
        ],
        out_specs=pl.BlockSpec((None, rows, nb), lambda l, j: (l, 0, j)),
        out_shape=jax.ShapeDtypeStruct((DEPTH, rows, 6 * D_MODEL), F32),
        name="modulation",
    )(cond, w_mod, b_mod.reshape(DEPTH, 1, 6 * D_MODEL))


TOKEN_TILE = 512


def _layer_block(shape, layer):
    shape = tuple(shape)
    return pl.BlockSpec((None,) + shape, lambda *_: (layer,) + (0,) * len(shape), pipeline_mode=pl.Buffered(1))


def _mod_spec(layer, tm, seq, per_seq):
    def index(i):
        return (layer, 1 + (i * tm) // seq if per_seq else 0, 0, 0)
    return pl.BlockSpec((None, None, 6, D_MODEL), index)


def _gmlp_tile(zb, vn, ws_ref, bias):
    tm = zb.shape[0]
    v = _rms(zb[:, D_B:], vn).astype(BF16)
    gmasks = [_lane_group_mask(D_B, D_B // G_B, g) for g in range(G_B)]
    out = []
    for ch in range(tm // CHUNK_B):
        vc = v[ch * CHUNK_B:(ch + 1) * CHUNK_B, :]
        mixed = bias
        for g in range(G_B):
            mixed = mixed + jnp.where(gmasks[g], _dot(ws_ref[g], vc), 0.0)
        out.append(zb[ch * CHUNK_B:(ch + 1) * CHUNK_B, :D_B] * mixed)
    return jnp.concatenate(out, axis=0)


def _in_kernel(x_ref, mod_ref, g_ref, w_ref, vn_ref, ws_ref, bias_ref,
               za_ref, yb_ref, zc_ref, zg_ref, dq_ref, dk_ref, dv_ref):
    x = x_ref[...]
    h = _rms(x, g_ref[0:1, :]) * (1.0 + mod_ref[1:2, :]) + mod_ref[0:1, :]
    hb = h.astype(BF16)
    za_ref[...] = _dot(hb, w_ref[:, Z_OFF_A:Z_OFF_B])
    zc_ref[...] = _dot(hb, w_ref[:, Z_OFF_C:Z_OFF_G])
    zg_ref[...] = _dot(hb, w_ref[:, Z_OFF_G:Z_OFF_D])
    for j, d_ref in enumerate((dq_ref, dk_ref, dv_ref)):
        d_ref[...] = _dot(hb, w_ref[:, Z_OFF_D + j * D_D:Z_OFF_D + (j + 1) * D_D])
    yb_ref[...] = _gmlp_tile(_dot(hb, w_ref[:, Z_OFF_B:Z_OFF_C]), vn_ref[...], ws_ref, bias_ref[...])


def _in_proj(x, mod, g, w, vn, ws, bias, layer, seq, per_seq):
    t = x.shape[0]
    tm = TOKEN_TILE
    widths = (ZA_W, MIX_W, 4 * D_C, ZG_W, D_D, D_D, D_D)
    return pl.pallas_call(
        _in_kernel,
        grid=(t // tm,),
        in_specs=[
            pl.BlockSpec((tm, D_MODEL), lambda i: (i, 0)),
            _mod_spec(layer, tm, seq, per_seq),
            _layer_block((4, D_MODEL), layer),
            _layer_block((D_MODEL, Z_W), layer),
            _layer_block((1, D_B), layer),
            _layer_block((G_B, CHUNK_B, CHUNK_B), layer),
            _layer_block((CHUNK_B, D_B), layer),
        ],
        out_specs=[pl.BlockSpec((tm, w_), lambda i: (i, 0)) for w_ in widths],
        out_shape=[jax.ShapeDtypeStruct((t, w_), F32) for w_ in widths],
        compiler_params=pltpu.CompilerParams(vmem_limit_bytes=VMEM_LIMIT),
        name="in_proj",
    )(x, mod, g, w, vn, ws, bias)


LOG2E = math.log2(math.e)


def _with_ones(v, lo_mask, keep_lo):
    lane = lax.broadcasted_iota(jnp.int32, (1, LANES), 1)
    if keep_lo:
        return jnp.where(lane == 64, 1.0, jnp.where(lo_mask, v, 0.0))
    return jnp.where(lane == 0, 1.0, jnp.where(lo_mask, 0.0, v))


def _pair_normalise(o_even, o_odd, lo_mask):
    return jnp.where(lo_mask, o_even / o_even[:, 64:65], o_odd / o_odd[:, 0:1])


def _rope128(x, tc, ts1, ts2):
    return x * tc + pltpu.roll(x, LANES - 16, 1) * ts1 + pltpu.roll(x, 16, 1) * ts2


def _mla_kernel(*refs, seq, has_ctx, tq):
    if has_ctx:
        (za_ref, qn_ref, kvn_ref, wuq_ref, wukv_ref, tc_ref, ts1_ref, ts2_ref, cckv_ref, ckr_ref,
         ya_ref, q_s, k_s, v_s, ckv_s, kr_s) = refs
        past = cckv_ref.shape[0]
    else:
        (za_ref, qn_ref, kvn_ref, wuq_ref, wukv_ref, ya_ref, ckv_ref, krt_ref, q_s, k_s, v_s, ckv_s, kr_s) = refs
        past = 0
    za = za_ref[...]
    cq = _rms(za[:, :Q_LORA], qn_ref[...])
    qh = _dot(cq.astype(BF16), wuq_ref[:, 0:(2 if has_ctx else 1) * H_A * LANES])
    ckv = _rms(za[:, Q_LORA:Q_LORA + KV_LORA], kvn_ref[...])
    kr = za[:, Q_LORA + KV_LORA:ZA_W]
    if has_ctx:
        tc, ts1, ts2 = tc_ref[...], ts1_ref[...], ts2_ref[...]
        kr = _rope128(kr, tc, ts1, ts2)
        ckv_s[0:past, :] = cckv_ref[...]
        krt = jnp.concatenate([jnp.zeros((64, past), F32), ckr_ref[...], jnp.zeros((LANES - 64 - ROPE_A, past), F32)],
                              axis=0)
        kr_s[0:past, :] = krt.T
    else:
        ckv_ref[...] = ckv
        krt_ref[...] = kr.T[64:64 + ROPE_A, :]
    ckv_s[past:past + seq, :] = ckv
    kr_s[past:past + seq, :] = kr
    kv = _dot(ckv_s[...].astype(BF16), wukv_ref[...])
    kr_all = kr_s[...]
    lo = _lane_group_mask(LANES, 64, 0)
    for h in range(H_A):
        qg = qh[:, h * LANES:(h + 1) * LANES]
        if has_ctx:
            qg = qg * tc + qh[:, (H_A + h) * LANES:(H_A + h + 1) * LANES] * (ts1 + ts2)
        q_s[h] = (qg * (MLA_SCALE * LOG2E)).astype(BF16)
        k_s[h] = (kv[:, h * LANES:(h + 1) * LANES] + kr_all).astype(BF16)
        v_s[h] = _with_ones(kv[:, (H_A + h) * LANES:(H_A + h + 1) * LANES], lo, h % 2 == 0).astype(BF16)

    def body(i, carry):
        rows = pl.ds(pl.multiple_of(i * tq, tq), tq)
        s = [_dot_nt(q_s[h, rows, :], k_s[h]) for h in range(H_A)]
        o = [_dot(jnp.exp2(s[h] - jnp.max(s[h], axis=-1, keepdims=True)).astype(BF16), v_s[h]) for h in range(H_A)]
        for pair in range(H_A // 2):
            ya_ref[rows, pair * LANES:(pair + 1) * LANES] = _pair_normalise(o[2 * pair], o[2 * pair + 1], lo)
        return carry

    lax.fori_loop(0, seq // tq, body, 0)


def _ctx_block(rows, cols, layer):
    return pl.BlockSpec((None, None, rows, cols), lambda b: (b, layer, 0, 0))


def _mla(za, qn, kvn, wuq, wukv, layer, batch, seq, rope=None, ctx=None):
    has_ctx = ctx is not None
    past = ctx[0].shape[2] if has_ctx else 0
    lk = past + seq
    tq = min(seq, 256)
    full = lambda shape: pl.BlockSpec(shape, lambda b: (0,) * len(shape))
    in_specs = [
        pl.BlockSpec((seq, ZA_W), lambda b: (b, 0)),
        _layer_block((1, Q_LORA), layer), _layer_block((1, KV_LORA), layer),
        _layer_block((Q_LORA, 2 * H_A * LANES), layer), _layer_block((KV_LORA, 2 * H_A * LANES), layer),
    ]
    args = [za, qn, kvn, wuq, wukv]
    out_specs = [pl.BlockSpec((seq, MIX_W), lambda b: (b, 0))]
    out_shape = [jax.ShapeDtypeStruct((batch * seq, MIX_W), F32)]
    if has_ctx:
        in_specs += [full((seq, LANES))] * 3
        in_specs += [_ctx_block(past, KV_LORA, layer), _ctx_block(ROPE_A, past, layer)]
        args += list(rope) + list(ctx)
    else:
        out_specs += [pl.BlockSpec((seq, KV_LORA), lambda b: (b, 0)),
                      pl.BlockSpec((None, ROPE_A, seq), lambda b: (b, 0, 0))]
        out_shape += [jax.ShapeDtypeStruct((batch * seq, KV_LORA), F32),
                      jax.ShapeDtypeStruct((batch, ROPE_A, seq), F32)]
    return pl.pallas_call(
        functools.partial(_mla_kernel, seq=seq, has_ctx=has_ctx, tq=tq),
        grid=(batch,),
        in_specs=in_specs,
        out_specs=out_specs,
        out_shape=out_shape,
        scratch_shapes=[
            pltpu.VMEM((H_A, seq, LANES), BF16),
            pltpu.VMEM((H_A, lk, LANES), BF16),
            pltpu.VMEM((H_A, lk, LANES), BF16),
            pltpu.VMEM((lk, LANES), F32),
            pltpu.VMEM((lk, LANES), F32),
        ],
        compiler_params=pltpu.CompilerParams(vmem_limit_bytes=VMEM_LIMIT),
        name="mla",
    )(*args)


def _diff_kernel(*refs, seq, has_ctx, tq, lam_init):
    if has_ctx:
        (dq_ref, dk_ref, dv_ref, lam_ref, sn_ref, tc_ref, ts1_ref, ts2_ref, ck_ref, cv_ref,
         yd_ref, q_s, k_s, v_s) = refs
        past = ck_ref.shape[1]
    else:
        (dq_ref, dk_ref, dv_ref, lam_ref, sn_ref, yd_ref, dkt_ref, dvt_ref, q_s, k_s, v_s) = refs
        past = 0
        dkt_ref[...] = dk_ref[...].T
        dvt_ref[...] = dv_ref[...].T
    lo = _lane_group_mask(LANES, 64, 0)
    if has_ctx:
        tc, ts1, ts2 = tc_ref[...], ts1_ref[...], ts2_ref[...]
        k_s[0:past, :] = ck_ref[...].T.astype(BF16)
        cv = cv_ref[...].T
    for half in range(2):
        cols = slice(half * LANES, (half + 1) * LANES)
        q, k, v = dq_ref[:, cols], dk_ref[:, cols], dv_ref[:, cols]
        if has_ctx:
            q = _rope128(q, tc, ts1, ts2)
            k = _rope128(k, tc, ts1, ts2)
            v_s[2 * half, 0:past, :] = _with_ones(cv[:, cols], lo, True).astype(BF16)
            v_s[2 * half + 1, 0:past, :] = _with_ones(cv[:, cols], lo, False).astype(BF16)
        q_s[:, cols] = (q * (DIFF_SCALE * LOG2E)).astype(BF16)
        k_s[past:past + seq, cols] = k.astype(BF16)
        v_s[2 * half, past:past + seq, :] = _with_ones(v, lo, True).astype(BF16)
        v_s[2 * half + 1, past:past + seq, :] = _with_ones(v, lo, False).astype(BF16)

    lam = lam_ref[...]
    lam_val = (jnp.exp(jnp.sum(lam[0:1] * lam[1:2], axis=-1, keepdims=True))
               - jnp.exp(jnp.sum(lam[2:3] * lam[3:4], axis=-1, keepdims=True)) + lam_init)
    comp_masks = [jnp.where(_lane_group_mask(D_D, DH_D, j), 1.0, 0.0).astype(BF16) for j in range(2 * H_D)]
    sn = sn_ref[...]

    def body(i, carry):
        rows = pl.ds(pl.multiple_of(i * tq, tq), tq)
        qb = q_s[rows, :]
        k_all = k_s[...]
        score = lambda h: [_dot_nt(qb * comp_masks[2 * h + comp], k_all) for comp in range(2)]
        s_next = score(0)
        o = []
        for h in range(H_D):
            s = s_next
            if h + 1 < H_D:
                s_next = score(h + 1)
            o1, o2 = [_dot(jnp.exp2(sc - jnp.max(sc, axis=-1, keepdims=True)).astype(BF16), v_s[h]) for sc in s]
            ll = 64 if h % 2 == 0 else 0
            o.append(o1 / o1[:, ll:ll + 1] - lam_val * (o2 / o2[:, ll:ll + 1]))
        for pair in range(H_D // 2):
            acc = jnp.where(lo, o[2 * pair], o[2 * pair + 1])
            sq = acc * acc
            ss_lo = jnp.sum(jnp.where(lo, sq, 0.0), axis=-1, keepdims=True)
            ss_hi = jnp.sum(jnp.where(lo, 0.0, sq), axis=-1, keepdims=True)
            r = jnp.where(lo, lax.rsqrt(ss_lo / (2 * DH_D) + EPS), lax.rsqrt(ss_hi / (2 * DH_D) + EPS))
            yd_ref[rows, pair * LANES:(pair + 1) * LANES] = acc * r * sn * (1.0 - lam_init)
        return carry

    lax.fori_loop(0, seq // tq, body, 0)


def _diff(dq, dk, dv, lam, sn, layer, batch, seq, lam_init, rope=None, ctx=None):
    has_ctx = ctx is not None
    past = ctx[0].shape[3] if has_ctx else 0
    lk = past + seq
    tq = min(seq, 256)
    full = lambda shape: pl.BlockSpec(shape, lambda b: (0,) * len(shape))
    in_specs = [pl.BlockSpec((seq, D_D), lambda b: (b, 0))] * 3
    in_specs += [_layer_block((4, DH_D), layer), _layer_block((1, LANES), layer)]
    args = [dq, dk, dv, lam, sn]
    if has_ctx:
        in_specs += [full((seq, LANES))] * 3
        in_specs += [_ctx_block(D_D, past, layer)] * 2
        args += list(rope) + list(ctx)
    out_specs = [pl.BlockSpec((seq, MIX_W), lambda b: (b, 0))]
    out_shape = [jax.ShapeDtypeStruct((batch * seq, MIX_W), F32)]
    if not has_ctx:
        out_specs += [pl.BlockSpec((None, D_D, seq), lambda b: (b, 0, 0))] * 2
        out_shape += [jax.ShapeDtypeStruct((batch, D_D, seq), F32)] * 2
    return pl.pallas_call(
        functools.partial(_diff_kernel, seq=seq, has_ctx=has_ctx, tq=tq, lam_init=lam_init),
        grid=(batch,),
        in_specs=in_specs,
        out_specs=out_specs,
        out_shape=out_shape,
        scratch_shapes=[
            pltpu.VMEM((seq, D_D), BF16),
            pltpu.VMEM((lk, D_D), BF16),
            pltpu.VMEM((H_D, lk, LANES), BF16),
        ],
        compiler_params=pltpu.CompilerParams(vmem_limit_bytes=VMEM_LIMIT),
        name="diff_attn",
    )(*args)


def _split3(x):
    hi = x.astype(BF16)
    r1 = x - hi.astype(F32)
    mid = r1.astype(BF16)
    lo = (r1 - mid.astype(F32)).astype(BF16)
    return hi, mid, lo


ST_ROWS = DH_C + 8
N_UNITS = 2 * H_C


def _mlstm_kernel(*refs, seq, has_state):
    ck = MLSTM_CHUNK
    nc = seq // ck
    if has_state:
        (zc_ref, zg_ref, gb_ref, hn_ref, c0_ref, n0_ref, m0_ref,
         yc_ref, k_s, qt_s, kt_s, vt_s, rcc_s, ca_s, wi_s, en_s, ws_s, wo_s, st_s, ht_s) = refs
    else:
        (zc_ref, zg_ref, gb_ref, hn_ref,
         yc_ref, cout_ref, nout_ref, mout_ref,
         k_s, qt_s, kt_s, vt_s, rcc_s, ca_s, wi_s, en_s, ws_s, wo_s, st_s, ht_s) = refs
    eye = jnp.where(lax.broadcasted_iota(jnp.int32, (DH_C, DH_C), 0) == lax.broadcasted_iota(jnp.int32, (DH_C, DH_C), 1),
                    1.0, 0.0).astype(BF16)

    def transpose_exact(a):
        return sum(_dot_nt(eye, part) for part in _split3(a))

    for c in range(nc):
        rows = slice(c * ck, (c + 1) * ck)
        qt_s[c] = zc_ref[rows, 0:D_C].T.astype(BF16)
        kk = zc_ref[rows, D_C:2 * D_C] * (DH_C ** -0.5)
        k_s[rows, :] = kk.astype(BF16)
        kt_s[c] = kk.T.astype(BF16)
        vt_s[c] = zc_ref[rows, 2 * D_C:3 * D_C].T.astype(BF16)

    gates = zg_ref[...] + gb_ref[...]
    r_i = lax.broadcasted_iota(jnp.int32, (ck, ck), 0)
    c_i = lax.broadcasted_iota(jnp.int32, (ck, ck), 1)
    sum_fw = jnp.where(r_i <= c_i, 1.0, 0.0).astype(BF16)
    sum_bw = jnp.where(r_i >= c_i, 1.0, 0.0).astype(BF16)
    row8 = lax.broadcasted_iota(jnp.int32, (N_UNITS, ck), 0)
    fw8 = row8 < H_C
    fw81 = fw8[:, 0:1]
    i_rows, b_rows = [], []
    for c in range(nc):
        rows = slice(c * ck, (c + 1) * ck)
        g_t = gates[rows, :].T
        i8 = g_t[0:N_UNITS, :]
        p0, p1, p2 = _split3(_log_sigmoid(g_t[N_UNITS:2 * N_UNITS, :]))
        b_fw = _dot(p0, sum_fw) + _dot(p1, sum_fw) + _dot(p2, sum_fw)
        b_bw = _dot(p0, sum_bw) + _dot(p1, sum_bw) + _dot(p2, sum_bw)
        b8 = jnp.where(fw8, b_fw, b_bw)
        rcc_s[rows, :] = jnp.concatenate([i8 - b8, jnp.zeros((ck - N_UNITS, ck), F32)], axis=0).T
        i_rows.append(i8)
        b_rows.append(b8)

    b_steps = [jnp.where(fw8, b_rows[i], b_rows[nc - 1 - i]) for i in range(nc)]
    r_steps = [jnp.where(fw8, i_rows[i], i_rows[nc - 1 - i]) - b_steps[i] for i in range(nc)]
    pm = sm = jnp.concatenate(r_steps, axis=0)
    lane_all = lax.broadcasted_iota(jnp.int32, pm.shape, 1)
    sh = 1
    while sh < ck:
        pm = jnp.maximum(pm, jnp.where(lane_all >= sh, pltpu.roll(pm, sh, 1), NEG_INF))
        sm = jnp.maximum(sm, jnp.where(lane_all < ck - sh, pltpu.roll(sm, ck - sh, 1), NEG_INF))
        sh *= 2
    m_prev = m0_ref[:, 0:1] if has_state else jnp.zeros((N_UNITS, 1), F32)
    for i in range(nc):
        b8, r8 = b_steps[i], r_steps[i]
        pm8, sm8 = pm[i * N_UNITS:(i + 1) * N_UNITS, :], sm[i * N_UNITS:(i + 1) * N_UNITS, :]
        cm = jnp.where(fw8, pm8, sm8)
        b_t = jnp.where(fw81, b8[:, ck - 1:ck], b8[:, 0:1])
        cm_end = jnp.where(fw81, pm8[:, ck - 1:ck], sm8[:, 0:1])
        inter = b8 + m_prev
        mt = jnp.maximum(inter, b8 + cm)
        ca_s[i] = b8 - mt
        wi_s[i] = jnp.exp(inter - mt)
        en_s[i] = jnp.exp(-mt)
        m_new = b_t + jnp.maximum(m_prev, cm_end)
        ws_s[i] = jnp.exp(b_t + r8 - m_new)
        wo_s[i] = jnp.broadcast_to(jnp.exp(b_t + m_prev - m_new), (N_UNITS, ck))
        m_prev = m_new

    if has_state:
        for u in range(N_UNITS):
            st_s[u, 0:DH_C, :] = transpose_exact(c0_ref[u])
            st_s[u, DH_C:ST_ROWS, :] = jnp.broadcast_to(n0_ref[u:u + 1, :], (ST_ROWS - DH_C, DH_C))
    else:
        st_s[...] = jnp.zeros_like(st_s)

    hmask_bf = [jnp.where(_lane_group_mask(D_C, DH_C, h), 1.0, 0.0).astype(BF16) for h in range(H_C)]
    src_ok = (r_i <= c_i, r_i >= c_i)

    def body(i, carry):
        ca, wi, en, ws, wo = ca_s[i], wi_s[i], en_s[i], ws_s[i], wo_s[i]
        chunks = (i, nc - 1 - i)
        units = [(d, h) for d in range(2) for h in range(H_C)]
        hs = [slice(h * DH_C, (h + 1) * DH_C) for h in range(H_C)]
        q_t = [qt_s[c] for c in chunks]
        k_t = [kt_s[c] for c in chunks]
        v_t = [vt_s[c] for c in chunks]
        sts, qcts, states, upds = [], [], [], []
        for d, h in units:
            rows = pl.ds(pl.multiple_of(chunks[d] * ck, ck), ck)
            sts.append(_dot(k_s[rows, :] * hmask_bf[h], q_t[d]))
            states.append(st_s[d * H_C + h])
            qcts.append(_dot(states[-1].astype(BF16), q_t[d][hs[h], :]))
        for u, (d, h) in enumerate(units):
            w_s = ws[u:u + 1, :]
            vw = jnp.concatenate([v_t[d][hs[h], :].astype(F32) * w_s, jnp.broadcast_to(w_s, (8, ck))], axis=0)
            upds.append(_dot_nt(vw.astype(BF16), k_t[d][hs[h], :]))
        pts = []
        for u, (d, h) in enumerate(units):
            rows = pl.ds(pl.multiple_of(chunks[d] * ck, ck), ck)
            decay = jnp.exp(rcc_s[rows, u:u + 1] + ca[u:u + 1, :])
            pts.append(sts[u] * jnp.where(src_ok[d], decay, 0.0))
        nums = [_dot(v_t[d][hs[h], :], pts[u].astype(BF16)) for u, (d, h) in enumerate(units)]
        for u, (d, h) in enumerate(units):
            w_i = wi[u:u + 1, :]
            den = w_i * qcts[u][DH_C:DH_C + 1, :] + jnp.sum(pts[u], axis=0, keepdims=True)
            num = w_i * qcts[u][0:DH_C, :] + nums[u]
            ht_s[d, chunks[d], hs[h], :] = num / jnp.maximum(jnp.abs(den), en[u:u + 1, :])
            st_s[u] = wo[u:u + 1, 0:DH_C] * states[u] + upds[u]
        return carry

    lax.fori_loop(0, nc, body, 0)

    hn = hn_ref[...]
    for c in range(nc):
        rows = slice(c * ck, (c + 1) * ck)
        h_t = ht_s[0, c] + ht_s[1, c]
        parts = []
        for h in range(H_C):
            x = h_t[h * DH_C:(h + 1) * DH_C, :]
            parts.append(x * lax.rsqrt(jnp.mean(x * x, axis=0, keepdims=True) + EPS))
        h_n = jnp.concatenate(parts, axis=0).T
        yc_ref[rows, :] = _sigmoid(zc_ref[rows, 3 * D_C:4 * D_C]) * (h_n * hn)

    if not has_state:
        for u in range(N_UNITS):
            cout_ref[u] = transpose_exact(st_s[u, 0:DH_C, :])
            nout_ref[u:u + 1, :] = st_s[u, DH_C:DH_C + 1, :]
        mout_ref[...] = jnp.broadcast_to(m_prev, (N_UNITS, LANES))


def _mlstm(zc, zg, gb, hn, layer, batch, seq, state=None):
    has_state = state is not None
    nc = seq // MLSTM_CHUNK
    in_specs = [pl.BlockSpec((seq, 4 * D_C), lambda b: (b, 0)), pl.BlockSpec((seq, ZG_W), lambda b: (b, 0)),
                _layer_block((1, ZG_W), layer), _layer_block((1, D_C), layer)]
    args = [zc, zg, gb, hn]
    out_specs = [pl.BlockSpec((seq, MIX_W), lambda b: (b, 0))]
    out_shape = [jax.ShapeDtypeStruct((batch * seq, MIX_W), F32)]
    if has_state:
        in_specs += [pl.BlockSpec((None, None, N_UNITS, DH_C, DH_C), lambda b: (b, layer, 0, 0, 0)),
                     pl.BlockSpec((None, None, N_UNITS, DH_C), lambda b: (b, layer, 0, 0)),
                     pl.BlockSpec((None, None, N_UNITS, LANES), lambda b: (b, layer, 0, 0))]
        args += list(state)
    else:
        out_specs += [pl.BlockSpec((None, N_UNITS, DH_C, DH_C), lambda b: (b, 0, 0, 0)),
                      pl.BlockSpec((None, N_UNITS, DH_C), lambda b: (b, 0, 0)),
                      pl.BlockSpec((None, N_UNITS, LANES), lambda b: (b, 0, 0))]
        out_shape += [jax.ShapeDtypeStruct((batch, N_UNITS, DH_C, DH_C), F32),
                      jax.ShapeDtypeStruct((batch, N_UNITS, DH_C), F32),
                      jax.ShapeDtypeStruct((batch, N_UNITS, LANES), F32)]
    step = lambda: pltpu.VMEM((nc, N_UNITS, MLSTM_CHUNK), F32)
    return pl.pallas_call(
        functools.partial(_mlstm_kernel, seq=seq, has_state=has_state),
        grid=(batch,),
        in_specs=in_specs,
        out_specs=out_specs,
        out_shape=out_shape,
        scratch_shapes=[
            pltpu.VMEM((seq, D_C), BF16),
            pltpu.VMEM((nc, D_C, MLSTM_CHUNK), BF16), pltpu.VMEM((nc, D_C, MLSTM_CHUNK), BF16),
            pltpu.VMEM((nc, D_C, MLSTM_CHUNK), BF16),
            pltpu.VMEM((seq, ZG_W), F32),
            step(), step(), step(), step(), step(),
            pltpu.VMEM((N_UNITS, ST_ROWS, DH_C), F32),
            pltpu.VMEM((2, nc, D_C, MLSTM_CHUNK), F32),
        ],
        compiler_params=pltpu.CompilerParams(vmem_limit_bytes=VMEM_LIMIT),
        name="mlstm",
    )(*args)


def _merge_kernel(x_ref, mod_ref, g_ref, ya_ref, yb_ref, yc_ref, yd_ref, wm_ref, bm_ref, wb_ref, wo_ref,
                  w1_ref, w2_ref, o_ref):
    x = x_ref[...]
    h = (_rms(x, g_ref[0:1, :]) * (1.0 + mod_ref[1:2, :]) + mod_ref[0:1, :]).astype(BF16)
    acc = jnp.zeros(x.shape, F32)
    for n, y_ref in enumerate((ya_ref, yb_ref, yc_ref, yd_ref)):
        cols = slice(n * D_MODEL, (n + 1) * D_MODEL)
        gate = _sigmoid(_dot(h, wm_ref[:, cols]) + bm_ref[:, cols])
        acc = acc + gate * _dot(y_ref[...].astype(BF16), wb_ref[n])
    y = _dot(acc.astype(BF16), wo_ref[...])
    x = x + mod_ref[2:3, :] * _rms(y, g_ref[1:2, :])
    h2 = (_rms(x, g_ref[2:3, :]) * (1.0 + mod_ref[4:5, :]) + mod_ref[3:4, :]).astype(BF16)
    f = jnp.zeros(x.shape, F32)
    for j in range(D_FF // D_MODEL):
        cols = slice(j * D_MODEL, (j + 1) * D_MODEL)
        a = jnp.maximum(_dot(h2, w1_ref[:, cols]), 0.0)
        f = f + _dot((a * a).astype(BF16), w2_ref[cols, :])
    o_ref[...] = x + mod_ref[5:6, :] * _rms(f, g_ref[3:4, :])


def _merge_ffn(x, mod, g, ys, wm, bm, wb, wo, w1, w2, layer, seq, per_seq):
    t = x.shape[0]
    tm = TOKEN_TILE
    full = lambda shape: _layer_block(shape, layer)
    tok = lambda w_: pl.BlockSpec((tm, w_), lambda i: (i, 0))
    return pl.pallas_call(
        _merge_kernel,
        grid=(t // tm,),
        in_specs=[tok(D_MODEL), _mod_spec(layer, tm, seq, per_seq),
                  full((4, D_MODEL)), tok(MIX_W), tok(MIX_W), tok(MIX_W), tok(MIX_W),
                  full((D_MODEL, N_BRANCH * D_MODEL)), full((1, N_BRANCH * D_MODEL)),
                  full((N_BRANCH, MIX_W, D_MODEL)), full((D_MODEL, D_MODEL)),
                  full((D_MODEL, D_FF)), full((D_FF, D_MODEL))],
        out_specs=tok(D_MODEL),
        out_shape=jax.ShapeDtypeStruct((t, D_MODEL), F32),
        compiler_params=pltpu.CompilerParams(vmem_limit_bytes=VMEM_LIMIT),
        name="merge_ffn",
    )(x, mod, g, *ys, wm, bm, wb, wo, w1, w2)


def _arrange_w_in(w):
    wt = w.T.astype(BF16)
    z = lambda n: jnp.zeros((n, D_MODEL), BF16)
    o_b, o_c = IN_A, IN_A + IN_B
    o_g, o_d = o_c + 4 * D_C, o_c + IN_C
    gates = wt[o_g:o_d].reshape(2, 2, H_C, D_MODEL).transpose(1, 0, 2, 3).reshape(4 * H_C, D_MODEL)
    return jnp.concatenate([
        wt[:Q_LORA + KV_LORA], z(64), wt[Q_LORA + KV_LORA:IN_A], z(32),
        wt[o_b:o_g], gates, z(ZG_W - 4 * H_C), wt[o_d:]], axis=0).T


def _arrange_w_uq(w):
    w = w.reshape(Q_LORA, H_A, NOPE_A + ROPE_A)
    half = ROPE_A // 2
    swapped = jnp.concatenate([jnp.zeros_like(w[:, :, :NOPE_A]), w[:, :, NOPE_A + half:], w[:, :, NOPE_A:NOPE_A + half]],
                              axis=-1)
    pad = lambda a: jnp.pad(a, ((0, 0), (0, 0), (0, LANES - NOPE_A - ROPE_A))).reshape(Q_LORA, H_A * LANES)
    return jnp.concatenate([pad(w), pad(swapped)], axis=1).astype(BF16)


def _arrange_w_ukv(w):
    w = w.reshape(KV_LORA, H_A, NOPE_A + VH_A)
    wk = jnp.pad(w[:, :, :NOPE_A], ((0, 0), (0, 0), (0, LANES - NOPE_A)))
    v = w[:, :, NOPE_A:]
    zero = jnp.zeros_like(v)
    even = jnp.concatenate([v, zero], axis=-1)
    odd = jnp.concatenate([zero, v], axis=-1)
    wv = jnp.where((jnp.arange(H_A) % 2 == 0)[None, :, None], even, odd)
    return jnp.concatenate([wk.reshape(KV_LORA, -1), wv.reshape(KV_LORA, -1)], axis=1).astype(BF16)


def _rope_tables(rows):
    row = np.repeat(np.arange(rows), GRID_W).astype(np.float64)
    col = np.tile(np.arange(GRID_W), rows).astype(np.float64)
    nf = ROPE_A // 4
    inv = np.exp(-math.log(ROPE_BASE) * np.arange(nf, dtype=np.float64) / nf)
    ang = np.concatenate([row[:, None] * inv, col[:, None] * inv], axis=-1)
    cos, sin = np.cos(ang), np.sin(ang)
    n = cos.shape[0]
    ones, zeros = (lambda k: np.ones((n, k))), (lambda k: np.zeros((n, k)))
    mla = (np.concatenate([ones(64), cos, cos, ones(32)], axis=1),
           np.concatenate([zeros(64), -sin, zeros(48)], axis=1),
           np.concatenate([zeros(80), sin, zeros(32)], axis=1))
    dif = (np.tile(np.concatenate([cos, cos], axis=1), (1, 4)),
           np.tile(np.concatenate([-sin, zeros(16)], axis=1), (1, 4)),
           np.tile(np.concatenate([zeros(16), sin], axis=1), (1, 4)))
    as_f32 = lambda ts: tuple(jnp.asarray(t.astype(np.float32)) for t in ts)
    return as_f32(mla), as_f32(dif)


def kernel(x_prompt, x_sample, cache_mla_ckv, cache_mla_krope, cache_diff_k, cache_diff_v, state_mlstm_C,
           state_mlstm_n, state_mlstm_m, c, c_ctx, w_mod, b_mod, norm_g, w_in, mla_q_norm, w_uq, mla_kv_norm,
           w_ukv, gmlp_v_norm, gmlp_w_s, gmlp_b_s, mlstm_gate_bias, mlstm_head_norm, diff_lambda, diff_sub_norm,
           w_branch, w_merge, b_merge, w_out, w_ff1, w_ff2):
    bp, lp, _ = x_prompt.shape
    bs, ls, _ = x_sample.shape
    past = cache_mla_ckv.shape[2]

    cond = jnp.concatenate([c_ctx[None, :], c, jnp.zeros((16 - 1 - bs, D_MODEL), F32)], axis=0)
    mod = _modulation(cond, w_mod, b_mod).reshape(DEPTH, 16, 6, D_MODEL)
    rope_mla, rope_dif = _rope_tables(ls // GRID_W)

    row = lambda a: a[:, None, :]
    w_in_a = jax.vmap(_arrange_w_in)(w_in)
    wuq_a = jax.vmap(_arrange_w_uq)(w_uq)
    wukv_a = jax.vmap(_arrange_w_ukv)(w_ukv)
    qn, kvn, vn, hn, bm = row(mla_q_norm), row(mla_kv_norm), row(gmlp_v_norm), row(mlstm_head_norm), row(b_merge)
    ws = gmlp_w_s.astype(BF16)
    bias = jnp.repeat(jnp.swapaxes(gmlp_b_s, 1, 2), D_B // G_B, axis=2)
    gb = jnp.pad(mlstm_gate_bias.transpose(0, 2, 1, 3).reshape(DEPTH, 1, 4 * H_C),
                 ((0, 0), (0, 0), (0, ZG_W - 4 * H_C)))
    sn = row(jnp.tile(diff_sub_norm, (1, 2)))
    wm, wb, wo = w_merge.astype(BF16), w_branch.astype(BF16), w_out.astype(BF16)
    w1, w2 = w_ff1.astype(BF16), w_ff2.astype(BF16)

    ctx_kr = cache_mla_krope.transpose(0, 1, 3, 2)
    ctx_dk = cache_diff_k.transpose(0, 1, 3, 4, 5, 2).reshape(bs, DEPTH, D_D, past)
    ctx_dv = cache_diff_v.transpose(0, 1, 3, 4, 2).reshape(bs, DEPTH, D_D, past)
    c0 = state_mlstm_C.reshape(bs, DEPTH, N_UNITS, DH_C, DH_C)
    n0 = state_mlstm_n.reshape(bs, DEPTH, N_UNITS, DH_C)
    m0 = jnp.broadcast_to(state_mlstm_m.reshape(bs, DEPTH, N_UNITS, 1), (bs, DEPTH, N_UNITS, LANES))

    xp = x_prompt.reshape(bp * lp, D_MODEL)
    xs = x_sample.reshape(bs * ls, D_MODEL)
    ents = []
    for l in range(DEPTH):
        lam_init = 0.8 - 0.6 * math.exp(-0.3 * l)
        for is_sample in (False, True):
            x = xs if is_sample else xp
            batch, seq = (bs, ls) if is_sample else (bp, lp)
            za, yb, zc, zg, dq, dk, dv = _in_proj(x, mod, norm_g, w_in_a, vn, ws, bias, l, seq, is_sample)
            if is_sample:
                (ya,) = _mla(za, qn, kvn, wuq_a, wukv_a, l, batch, seq, rope=rope_mla, ctx=(cache_mla_ckv, ctx_kr))
                (yd,) = _diff(dq, dk, dv, diff_lambda, sn, l, batch, seq, lam_init, rope=rope_dif, ctx=(ctx_dk, ctx_dv))
                (yc,) = _mlstm(zc, zg, gb, hn, l, batch, seq, state=(c0, n0, m0))
            else:
                ya, ckv, krt = _mla(za, qn, kvn, wuq_a, wukv_a, l, batch, seq)
                yd, dkt, dvt = _diff(dq, dk, dv, diff_lambda, sn, l, batch, seq, lam_init)
                yc, c_new, n_new, m_new = _mlstm(zc, zg, gb, hn, l, batch, seq)
                ents.append((
                    ckv.reshape(bp, lp, KV_LORA),
                    krt.transpose(0, 2, 1),
                    dkt.reshape(bp, H_D, 2, DH_D, lp).transpose(0, 4, 1, 2, 3),
                    dvt.reshape(bp, H_D, 2 * DH_D, lp).transpose(0, 3, 1, 2),
                    c_new.reshape(bp, 2, H_C, DH_C, DH_C),
                    n_new.reshape(bp, 2, H_C, DH_C),
                    m_new[:, :, 0].reshape(bp, 2, H_C)))
            x = _merge_ffn(x, mod, norm_g, (ya, yb, yc, yd), wm, bm, wb, wo, w1, w2, l, seq, is_sample)
            if is_sample:
                xs = x
            else:
                xp = x

    stack = lambda j: jnp.stack([e[j] for e in ents], axis=1)
    return (xp.reshape(bp, lp, D_MODEL), xs.reshape(bs, ls, D_MODEL),
            stack(0), stack(1), stack(2), stack(3), stack(4), stack(5), stack(6))
```

```python
import functools
import math

import jax
import jax.numpy as jnp
import numpy as np
from jax import lax
from jax.experimental import pallas as pl
from jax.experimental.pallas import tpu as pltpu

F32 = jnp.float32
BF16 = jnp.bfloat16

D_MODEL = 1024
DEPTH = 2
GRID_W = 64
N_BRANCH = 4
MIX_W = 256
H_A, NOPE_A, ROPE_A, VH_A = 4, 64, 32, 64
Q_LORA, KV_LORA = 256, 128
D_B, G_B, CHUNK_B = 256, 4, 128
H_C, DH_C = 4, 64
D_C = H_C * DH_C
H_D, DH_D = 4, 32
D_D = H_D * 2 * DH_D
D_FF = 4 * D_MODEL
IN_A = Q_LORA + KV_LORA + ROPE_A
IN_B = 2 * D_B
IN_C = 4 * D_C + 4 * H_C
IN_D = 3 * D_D
ROPE_BASE = 10000.0
EPS = 1e-6
MLA_SCALE = (NOPE_A + ROPE_A) ** -0.5
DIFF_SCALE = DH_D ** -0.5

LANES = 128
MLSTM_CHUNK = 128
VMEM_LIMIT = 56 * 1024 * 1024
NEG_INF = float("-inf")

ZA_W = 512
ZG_W = 128
Z_OFF_A = 0
Z_OFF_B = Z_OFF_A + ZA_W
Z_OFF_C = Z_OFF_B + IN_B
Z_OFF_G = Z_OFF_C + 4 * D_C
Z_OFF_D = Z_OFF_G + ZG_W
Z_W = Z_OFF_D + IN_D


def _rms(x, g):
    return x * lax.rsqrt(jnp.mean(x * x, axis=-1, keepdims=True) + EPS) * g


def _sigmoid(x):
    return 1.0 / (1.0 + jnp.exp(-x))


def _log_sigmoid(x):
    return jnp.minimum(x, 0.0) - jnp.log1p(jnp.exp(-jnp.abs(x)))


def _dot(a, b):
    return jnp.dot(a, b, preferred_element_type=F32)


def _dot_nt(a, b):
    return lax.dot_general(a, b, (((1,), (1,)), ((), ())), preferred_element_type=F32)


def _lane_group_mask(width, group, index):
    lane = lax.broadcasted_iota(jnp.int32, (1, width), 1)
    return (lane >= index * group) & (lane < (index + 1) * group)


def _mod_kernel(cond_ref, w_ref, b_ref, o_ref):
    c = cond_ref[...]
    s = c * _sigmoid(c)
    o_ref[...] = _dot(s.astype(BF16), w_ref[...].astype(BF16)) + b_ref[...]


def _modulation(cond, w_mod, b_mod):
    rows = cond.shape[0]
    nb = 1024
    return pl.pallas_call(
        _mod_kernel,
        grid=(DEPTH, 6 * D_MODEL // nb),
        in_specs=[
            pl.BlockSpec((rows, D_MODEL), lambda l, j: (0, 0)),
            pl.BlockSpec((None, D_MODEL, nb), lambda l, j: (l, 0, j)),
            pl.BlockSpec((None, 1, nb), lambda l, j: (l, 0, j)),
        ],
        out_specs=pl.BlockSpec((None, rows, nb), lambda l, j: (l, 0, j)),
        out_shape=jax.ShapeDtypeStruct((DEPTH, rows, 6 * D_MODEL), F32),
        name="modulation",
    )(cond, w_mod, b_mod.reshape(DEPTH, 1, 6 * D_MODEL))


TOKEN_TILE = 512


def _layer_block(shape, layer):
    shape = tuple(shape)
    return pl.BlockSpec((None,) + shape, lambda *_: (layer,) + (0,) * len(shape), pipeline_mode=pl.Buffered(1))


def _mod_spec(layer, tm, seq, per_seq):
    def index(i):
        return (layer, 1 + (i * tm) // seq if per_seq else 0, 0, 0)
    return pl.BlockSpec((None, None, 6, D_MODEL), index)


def _gmlp_tile(zb, vn, ws_ref, bias):
    tm = zb.shape[0]
    v = _rms(zb[:, D_B:], vn).astype(BF16)
    gmasks = [_lane_group_mask(D_B, D_B // G_B, g) for g in range(G_B)]
    out = []
    for ch in range(tm // CHUNK_B):
        vc = v[ch * CHUNK_B:(ch + 1) * CHUNK_B, :]
        mixed = bias
        for g in range(G_B):
            mixed = mixed + jnp.where(gmasks[g], _dot(ws_ref[g], vc), 0.0)
        out.append(zb[ch * CHUNK_B:(ch + 1) * CHUNK_B, :D_B] * mixed)
    return jnp.concatenate(out, axis=0)


def _in_kernel(x_ref, mod_ref, g_ref, w_ref, vn_ref, ws_ref, bias_ref,
               za_ref, yb_ref, zc_ref, zg_ref, dq_ref, dk_ref, dv_ref):
    x = x_ref[...]
    h = _rms(x, g_ref[0:1, :]) * (1.0 + mod_ref[1:2, :]) + mod_ref[0:1, :]
    hb = h.astype(BF16)
    za_ref[...] = _dot(hb, w_ref[:, Z_OFF_A:Z_OFF_B])
    zc_ref[...] = _dot(hb, w_ref[:, Z_OFF_C:Z_OFF_G])
    zg_ref[...] = _dot(hb, w_ref[:, Z_OFF_G:Z_OFF_D])
    for j, d_ref in enumerate((dq_ref, dk_ref, dv_ref)):
        d_ref[...] = _dot(hb, w_ref[:, Z_OFF_D + j * D_D:Z_OFF_D + (j + 1) * D_D])
    yb_ref[...] = _gmlp_tile(_dot(hb, w_ref[:, Z_OFF_B:Z_OFF_C]), vn_ref[...], ws_ref, bias_ref[...])


def _in_proj(x, mod, g, w, vn, ws, bias, layer, seq, per_seq):
    t = x.shape[0]
    tm = TOKEN_TILE
    widths = (ZA_W, MIX_W, 4 * D_C, ZG_W, D_D, D_D, D_D)
    return pl.pallas_call(
        _in_kernel,
        grid=(t // tm,),
        in_specs=[
            pl.BlockSpec((tm, D_MODEL), lambda i: (i, 0)),
            _mod_spec(layer, tm, seq, per_seq),
            _layer_block((4, D_MODEL), layer),
            _layer_block((D_MODEL, Z_W), layer),
            _layer_block((1, D_B), layer),
            _layer_block((G_B, CHUNK_B, CHUNK_B), layer),
            _layer_block((CHUNK_B, D_B), layer),
        ],
        out_specs=[pl.BlockSpec((tm, w_), lambda i: (i, 0)) for w_ in widths],
        out_shape=[jax.ShapeDtypeStruct((t, w_), F32) for w_ in widths],
        compiler_params=pltpu.CompilerParams(vmem_limit_bytes=VMEM_LIMIT),
        name="in_proj",
    )(x, mod, g, w, vn, ws, bias)


LOG2E = math.log2(math.e)


def _with_ones(v, lo_mask, keep_lo):
    lane = lax.broadcasted_iota(jnp.int32, (1, LANES), 1)
    if keep_lo:
        return jnp.where(lane == 64, 1.0, jnp.where(lo_mask, v, 0.0))
    return jnp.where(lane == 0, 1.0, jnp.where(lo_mask, 0.0, v))


def _pair_normalise(o_even, o_odd, lo_mask):
    return jnp.where(lo_mask, o_even / o_even[:, 64:65], o_odd / o_odd[:, 0:1])


def _rope128(x, tc, ts1, ts2):
    return x * tc + pltpu.roll(x, LANES - 16, 1) * ts1 + pltpu.roll(x, 16, 1) * ts2


def _mla_kernel(*refs, seq, has_ctx, tq):
    if has_ctx:
        (za_ref, qn_ref, kvn_ref, wuq_ref, wukv_ref, tc_ref, ts1_ref, ts2_ref, cckv_ref, ckr_ref,
         ya_ref, q_s, k_s, v_s, ckv_s, kr_s) = refs
        past = cckv_ref.shape[0]
    else:
        (za_ref, qn_ref, kvn_ref, wuq_ref, wukv_ref, ya_ref, ckv_ref, krt_ref, q_s, k_s, v_s, ckv_s, kr_s) = refs
        past = 0
    za = za_ref[...]
    cq = _rms(za[:, :Q_LORA], qn_ref[...])
    qh = _dot(cq.astype(BF16), wuq_ref[:, 0:(2 if has_ctx else 1) * H_A * LANES])
    ckv = _rms(za[:, Q_LORA:Q_LORA + KV_LORA], kvn_ref[...])
    kr = za[:, Q_LORA + KV_LORA:ZA_W]
    if has_ctx:
        tc, ts1, ts2 = tc_ref[...], ts1_ref[...], ts2_ref[...]
        kr = _rope128(kr, tc, ts1, ts2)
        ckv_s[0:past, :] = cckv_ref[...]
        krt = jnp.concatenate([jnp.zeros((64, past), F32), ckr_ref[...], jnp.zeros((LANES - 64 - ROPE_A, past), F32)],
                              axis=0)
        kr_s[0:past, :] = krt.T
    else:
        ckv_ref[...] = ckv
        krt_ref[...] = kr.T[64:64 + ROPE_A, :]
    ckv_s[past:past + seq, :] = ckv
    kr_s[past:past + seq, :] = kr
    kv = _dot(ckv_s[...].astype(BF16), wukv_ref[...])
    kr_all = kr_s[...]
    lo = _lane_group_mask(LANES, 64, 0)
    for h in range(H_A):
        qg = qh[:, h * LANES:(h + 1) * LANES]
        if has_ctx:
            qg = qg * tc + qh[:, (H_A + h) * LANES:(H_A + h + 1) * LANES] * (ts1 + ts2)
        q_s[h] = (qg * (MLA_SCALE * LOG2E)).astype(BF16)
        k_s[h] = (kv[:, h * LANES:(h + 1) * LANES] + kr_all).astype(BF16)
        v_s[h] = _with_ones(kv[:, (H_A + h) * LANES:(H_A + h + 1) * LANES], lo, h % 2 == 0).astype(BF16)

    def body(i, carry):
        rows = pl.ds(pl.multiple_of(i * tq, tq), tq)
        s = [_dot_nt(q_s[h, rows, :], k_s[h]) for h in range(H_A)]
        o = [_dot(jnp.exp2(s[h] - jnp.max(s[h], axis=-1, keepdims=True)).astype(BF16), v_s[h]) for h in range(H_A)]
        for pair in range(H_A // 2):
            ya_ref[rows, pair * LANES:(pair + 1) * LANES] = _pair_normalise(o[2 * pair], o[2 * pair + 1], lo)
        return carry

    lax.fori_loop(0, seq // tq, body, 0)


def _ctx_block(rows, cols, layer):
    return pl.BlockSpec((None, None, rows, cols), lambda b: (b, layer, 0, 0))


def _mla(za, qn, kvn, wuq, wukv, layer, batch, seq, rope=None, ctx=None):
    has_ctx = ctx is not None
    past = ctx[0].shape[2] if has_ctx else 0
    lk = past + seq
    tq = min(seq, 256)
    full = lambda shape: pl.BlockSpec(shape, lambda b: (0,) * len(shape))
    in_specs = [
        pl.BlockSpec((seq, ZA_W), lambda b: (b, 0)),
        _layer_block((1, Q_LORA), layer), _layer_block((1, KV_LORA), layer),
        _layer_block((Q_LORA, 2 * H_A * LANES), layer), _layer_block((KV_LORA, 2 * H_A * LANES), layer),
    ]
    args = [za, qn, kvn, wuq, wukv]
    out_specs = [pl.BlockSpec((seq, MIX_W), lambda b: (b, 0))]
    out_shape = [jax.ShapeDtypeStruct((batch * seq, MIX_W), F32)]
    if has_ctx:
        in_specs += [full((seq, LANES))] * 3
        in_specs += [_ctx_block(past, KV_LORA, layer), _ctx_block(ROPE_A, past, layer)]
        args += list(rope) + list(ctx)
    else:
        out_specs += [pl.BlockSpec((seq, KV_LORA), lambda b: (b, 0)),
                      pl.BlockSpec((None, ROPE_A, seq), lambda b: (b, 0, 0))]
        out_shape += [jax.ShapeDtypeStruct((batch * seq, KV_LORA), F32),
                      jax.ShapeDtypeStruct((batch, ROPE_A, seq), F32)]
    return pl.pallas_call(
        functools.partial(_mla_kernel, seq=seq, has_ctx=has_ctx, tq=tq),
        grid=(batch,),
        in_specs=in_specs,
        out_specs=out_specs,
        out_shape=out_shape,
        scratch_shapes=[
            pltpu.VMEM((H_A, seq, LANES), BF16),
            pltpu.VMEM((H_A, lk, LANES), BF16),
            pltpu.VMEM((H_A, lk, LANES), BF16),
            pltpu.VMEM((lk, LANES), F32),
            pltpu.VMEM((lk, LANES), F32),
        ],
        compiler_params=pltpu.CompilerParams(vmem_limit_bytes=VMEM_LIMIT),
        name="mla",
    )(*args)


def _diff_kernel(*refs, seq, has_ctx, tq, lam_init):
    if has_ctx:
        (dq_ref, dk_ref, dv_ref, lam_ref, sn_ref, tc_ref, ts1_ref, ts2_ref, ck_ref, cv_ref,
         yd_ref, q_s, k_s, v_s) = refs
        past = ck_ref.shape[1]
    else:
        (dq_ref, dk_ref, dv_ref, lam_ref, sn_ref, yd_ref, dkt_ref, dvt_ref, q_s, k_s, v_s) = refs
        past = 0
        dkt_ref[...] = dk_ref[...].T
        dvt_ref[...] = dv_ref[...].T
    lo = _lane_group_mask(LANES, 64, 0)
    if has_ctx:
        tc, ts1, ts2 = tc_ref[...], ts1_ref[...], ts2_ref[...]
        k_s[0:past, :] = ck_ref[...].T.astype(BF16)
        cv = cv_ref[...].T
    for half in range(2):
        cols = slice(half * LANES, (half + 1) * LANES)
        q, k, v = dq_ref[:, cols], dk_ref[:, cols], dv_ref[:, cols]
        if has_ctx:
            q = _rope128(q, tc, ts1, ts2)
            k = _rope128(k, tc, ts1, ts2)
            v_s[2 * half, 0:past, :] = _with_ones(cv[:, cols], lo, True).astype(BF16)
            v_s[2 * half + 1, 0:past, :] = _with_ones(cv[:, cols], lo, False).astype(BF16)
        q_s[:, cols] = (q * (DIFF_SCALE * LOG2E)).astype(BF16)
        k_s[past:past + seq, cols] = k.astype(BF16)
        v_s[2 * half, past:past + seq, :] = _with_ones(v, lo, True).astype(BF16)
        v_s[2 * half + 1, past:past + seq, :] = _with_ones(v, lo, False).astype(BF16)

    lam = lam_ref[...]
    lam_val = (jnp.exp(jnp.sum(lam[0:1] * lam[1:2], axis=-1, keepdims=True))
               - jnp.exp(jnp.sum(lam[2:3] * lam[3:4], axis=-1, keepdims=True)) + lam_init)
    comp_masks = [jnp.where(_lane_group_mask(D_D, DH_D, j), 1.0, 0.0).astype(BF16) for j in range(2 * H_D)]
    sn = sn_ref[...]

    def body(i, carry):
        rows = pl.ds(pl.multiple_of(i * tq, tq), tq)
        qb = q_s[rows, :]
        k_all = k_s[...]
        score = lambda h: [_dot_nt(qb * comp_masks[2 * h + comp], k_all) for comp in range(2)]
        s_next = score(0)
        o = []
        for h in range(H_D):
            s = s_next
            if h + 1 < H_D:
                s_next = score(h + 1)
            o1, o2 = [_dot(jnp.exp2(sc - jnp.max(sc, axis=-1, keepdims=True)).astype(BF16), v_s[h]) for sc in s]
            ll = 64 if h % 2 == 0 else 0
            o.append(o1 / o1[:, ll:ll + 1] - lam_val * (o2 / o2[:, ll:ll + 1]))
        for pair in range(H_D // 2):
            acc = jnp.where(lo, o[2 * pair], o[2 * pair + 1])
            sq = acc * acc
            ss_lo = jnp.sum(jnp.where(lo, sq, 0.0), axis=-1, keepdims=True)
            ss_hi = jnp.sum(jnp.where(lo, 0.0, sq), axis=-1, keepdims=True)
            r = jnp.where(lo, lax.rsqrt(ss_lo / (2 * DH_D) + EPS), lax.rsqrt(ss_hi / (2 * DH_D) + EPS))
            yd_ref[rows, pair * LANES:(pair + 1) * LANES] = acc * r * sn * (1.0 - lam_init)
        return carry

    lax.fori_loop(0, seq // tq, body, 0)


def _diff(dq, dk, dv, lam, sn, layer, batch, seq, lam_init, rope=None, ctx=None):
    has_ctx = ctx is not None
    past = ctx[0].shape[3] if has_ctx else 0
    lk = past + seq
    tq = min(seq, 256)
    full = lambda shape: pl.BlockSpec(shape, lambda b: (0,) * len(shape))
    in_specs = [pl.BlockSpec((seq, D_D), lambda b: (b, 0))] * 3
    in_specs += [_layer_block((4, DH_D), layer), _layer_block((1, LANES), layer)]
    args = [dq, dk, dv, lam, sn]
    if has_ctx:
        in_specs += [full((seq, LANES))] * 3
        in_specs += [_ctx_block(D_D, past, layer)] * 2
        args += list(rope) + list(ctx)
    out_specs = [pl.BlockSpec((seq, MIX_W), lambda b: (b, 0))]
    out_shape = [jax.ShapeDtypeStruct((batch * seq, MIX_W), F32)]
    if not has_ctx:
        out_specs += [pl.BlockSpec((None, D_D, seq), lambda b: (b, 0, 0))] * 2
        out_shape += [jax.ShapeDtypeStruct((batch, D_D, seq), F32)] * 2
    return pl.pallas_call(
        functools.partial(_diff_kernel, seq=seq, has_ctx=has_ctx, tq=tq, lam_init=lam_init),
        grid=(batch,),
        in_specs=in_specs,
        out_specs=out_specs,
        out_shape=out_shape,
        scratch_shapes=[
            pltpu.VMEM((seq, D_D), BF16),
            pltpu.VMEM((lk, D_D), BF16),
            pltpu.VMEM((H_D, lk, LANES), BF16),
        ],
        compiler_params=pltpu.CompilerParams(vmem_limit_bytes=VMEM_LIMIT),
        name="diff_attn",
    )(*args)


def _split3(x):
    hi = x.astype(BF16)
    r1 = x - hi.astype(F32)
    mid = r1.astype(BF16)
    lo = (r1 - mid.astype(F32)).astype(BF16)
    return hi, mid, lo


ST_ROWS = DH_C + 8
N_UNITS = 2 * H_C
MLSTM_GROUP_TOKENS = 1024


def _mlstm_kernel(*refs, seq, ng, has_state):
    ck = MLSTM_CHUNK
    nc = seq // ck
    if has_state:
        (zc_ref, zg_ref, gb_ref, hn_ref, c0_ref, n0_ref, m0_ref,
         yc_ref, k_s, qt_s, kt_s, vt_s, rcc_s, ca_s, wi_s, en_s, ws_s, wo_s, st_s, ht_s) = refs
    else:
        (zc_ref, zg_ref, gb_ref, hn_ref,
         yc_ref, cout_ref, nout_ref, mout_ref,
         k_s, qt_s, kt_s, vt_s, rcc_s, ca_s, wi_s, en_s, ws_s, wo_s, st_s, ht_s) = refs
    eye = jnp.where(lax.broadcasted_iota(jnp.int32, (DH_C, DH_C), 0) == lax.broadcasted_iota(jnp.int32, (DH_C, DH_C), 1),
                    1.0, 0.0).astype(BF16)

    def transpose_exact(a):
        return sum(_dot_nt(eye, part) for part in _split3(a))

    def chunk_rows(g, c):
        return slice(g * seq + c * ck, g * seq + (c + 1) * ck)

    for g in range(ng):
        for c in range(nc):
            rows = chunk_rows(g, c)
            qt_s[g, c] = zc_ref[rows, 0:D_C].T.astype(BF16)
            kk = zc_ref[rows, D_C:2 * D_C] * (DH_C ** -0.5)
            k_s[rows, :] = kk.astype(BF16)
            kt_s[g, c] = kk.T.astype(BF16)
            vt_s[g, c] = zc_ref[rows, 2 * D_C:3 * D_C].T.astype(BF16)

    gates = zg_ref[...] + gb_ref[...]
    r_i = lax.broadcasted_iota(jnp.int32, (ck, ck), 0)
    c_i = lax.broadcasted_iota(jnp.int32, (ck, ck), 1)
    sum_fw = jnp.where(r_i <= c_i, 1.0, 0.0).astype(BF16)
    sum_bw = jnp.where(r_i >= c_i, 1.0, 0.0).astype(BF16)
    row8 = lax.broadcasted_iota(jnp.int32, (N_UNITS, ck), 0)
    fw8 = row8 < H_C
    fw81 = fw8[:, 0:1]
    b_steps, r_steps = [], []
    for g in range(ng):
        i_rows, b_rows = [], []
        for c in range(nc):
            rows = chunk_rows(g, c)
            g_t = gates[rows, :].T
            i8 = g_t[0:N_UNITS, :]
            p0, p1, p2 = _split3(_log_sigmoid(g_t[N_UNITS:2 * N_UNITS, :]))
            b_fw = _dot(p0, sum_fw) + _dot(p1, sum_fw) + _dot(p2, sum_fw)
            b_bw = _dot(p0, sum_bw) + _dot(p1, sum_bw) + _dot(p2, sum_bw)
            b8 = jnp.where(fw8, b_fw, b_bw)
            rcc_s[rows, :] = jnp.concatenate([i8 - b8, jnp.zeros((ck - N_UNITS, ck), F32)], axis=0).T
            i_rows.append(i8)
            b_rows.append(b8)
        b_steps.append([jnp.where(fw8, b_rows[i], b_rows[nc - 1 - i]) for i in range(nc)])
        r_steps.append([jnp.where(fw8, i_rows[i], i_rows[nc - 1 - i]) - b_steps[g][i] for i in range(nc)])

    pm = sm = jnp.concatenate([r for g in range(ng) for r in r_steps[g]], axis=0)
    lane_all = lax.broadcasted_iota(jnp.int32, pm.shape, 1)
    sh = 1
    while sh < ck:
        pm = jnp.maximum(pm, jnp.where(lane_all >= sh, pltpu.roll(pm, sh, 1), NEG_INF))
        sm = jnp.maximum(sm, jnp.where(lane_all < ck - sh, pltpu.roll(sm, ck - sh, 1), NEG_INF))
        sh *= 2
    m_last = []
    for g in range(ng):
        m_prev = m0_ref[g, :, 0:1] if has_state else jnp.zeros((N_UNITS, 1), F32)
        for i in range(nc):
            b8, r8 = b_steps[g][i], r_steps[g][i]
            at = (g * nc + i) * N_UNITS
            pm8, sm8 = pm[at:at + N_UNITS, :], sm[at:at + N_UNITS, :]
            cm = jnp.where(fw8, pm8, sm8)
            b_t = jnp.where(fw81, b8[:, ck - 1:ck], b8[:, 0:1])
            cm_end = jnp.where(fw81, pm8[:, ck - 1:ck], sm8[:, 0:1])
            inter = b8 + m_prev
            mt = jnp.maximum(inter, b8 + cm)
            ca_s[g, i] = b8 - mt
            wi_s[g, i] = jnp.exp(inter - mt)
            en_s[g, i] = jnp.exp(-mt)
            m_new = b_t + jnp.maximum(m_prev, cm_end)
            ws_s[g, i] = jnp.exp(b_t + r8 - m_new)
            wo_s[g, i] = jnp.broadcast_to(jnp.exp(b_t + m_prev - m_new), (N_UNITS, ck))
            m_prev = m_new
        m_last.append(m_prev)

    if has_state:
        for g in range(ng):
            for u in range(N_UNITS):
                st_s[g * N_UNITS + u, 0:DH_C, :] = transpose_exact(c0_ref[g, u])
                st_s[g * N_UNITS + u, DH_C:ST_ROWS, :] = jnp.broadcast_to(n0_ref[g, u:u + 1, :],
                                                                         (ST_ROWS - DH_C, DH_C))
    else:
        st_s[...] = jnp.zeros_like(st_s)

    hmask_bf = [jnp.where(_lane_group_mask(D_C, DH_C, h), 1.0, 0.0).astype(BF16) for h in range(H_C)]
    src_ok = (r_i <= c_i, r_i >= c_i)
    hs = [slice(h * DH_C, (h + 1) * DH_C) for h in range(H_C)]
    units = [(g, d, h) for g in range(ng) for d in range(2) for h in range(H_C)]

    def body(i, carry):
        chunks = (i, nc - 1 - i)
        ca, wi, en, ws, wo = [[ref[g, i] for g in range(ng)] for ref in (ca_s, wi_s, en_s, ws_s, wo_s)]
        q_t = [[qt_s[g, c] for c in chunks] for g in range(ng)]
        k_t = [[kt_s[g, c] for c in chunks] for g in range(ng)]
        v_t = [[vt_s[g, c] for c in chunks] for g in range(ng)]
        rows = [[pl.ds(pl.multiple_of(g * seq + c * ck, ck), ck) for c in chunks] for g in range(ng)]
        sts, qcts, states, upds, pts = [], [], [], [], []
        for g, d, h in units:
            sts.append(_dot(k_s[rows[g][d], :] * hmask_bf[h], q_t[g][d]))
            states.append(st_s[g * N_UNITS + d * H_C + h])
            qcts.append(_dot(states[-1].astype(BF16), q_t[g][d][hs[h], :]))
        for g, d, h in units:
            u = d * H_C + h
            w_s = ws[g][u:u + 1, :]
            vw = jnp.concatenate([v_t[g][d][hs[h], :].astype(F32) * w_s, jnp.broadcast_to(w_s, (8, ck))], axis=0)
            upds.append(_dot_nt(vw.astype(BF16), k_t[g][d][hs[h], :]))
        for j, (g, d, h) in enumerate(units):
            u = d * H_C + h
            decay = jnp.exp(rcc_s[rows[g][d], u:u + 1] + ca[g][u:u + 1, :])
            pts.append(sts[j] * jnp.where(src_ok[d], decay, 0.0))
        nums = [_dot(v_t[g][d][hs[h], :], pts[j].astype(BF16)) for j, (g, d, h) in enumerate(units)]
        for j, (g, d, h) in enumerate(units):
            u = d * H_C + h
            w_i = wi[g][u:u + 1, :]
            den = w_i * qcts[j][DH_C:DH_C + 1, :] + jnp.sum(pts[j], axis=0, keepdims=True)
            num = w_i * qcts[j][0:DH_C, :] + nums[j]
            ht_s[g, d, chunks[d], hs[h], :] = num / jnp.maximum(jnp.abs(den), en[g][u:u + 1, :])
            st_s[g * N_UNITS + u] = wo[g][u:u + 1, 0:DH_C] * states[j] + upds[j]
        return carry

    lax.fori_loop(0, nc, body, 0)

    hn = hn_ref[...]
    for g in range(ng):
        for c in range(nc):
            rows = chunk_rows(g, c)
            h_t = ht_s[g, 0, c] + ht_s[g, 1, c]
            parts = []
            for h in range(H_C):
                x = h_t[hs[h], :]
                parts.append(x * lax.rsqrt(jnp.mean(x * x, axis=0, keepdims=True) + EPS))
            h_n = jnp.concatenate(parts, axis=0).T
            yc_ref[rows, :] = _sigmoid(zc_ref[rows, 3 * D_C:4 * D_C]) * (h_n * hn)

    if not has_state:
        for g in range(ng):
            for u in range(N_UNITS):
                cout_ref[g, u] = transpose_exact(st_s[g * N_UNITS + u, 0:DH_C, :])
                nout_ref[g, u:u + 1, :] = st_s[g * N_UNITS + u, DH_C:DH_C + 1, :]
            mout_ref[g] = jnp.broadcast_to(m_last[g], (N_UNITS, LANES))


def _mlstm(zc, zg, gb, hn, layer, batch, seq, state=None):
    has_state = state is not None
    nc = seq // MLSTM_CHUNK
    ng = max(2, MLSTM_GROUP_TOKENS // seq)
    in_specs = [pl.BlockSpec((ng * seq, 4 * D_C), lambda b: (b, 0)), pl.BlockSpec((ng * seq, ZG_W), lambda b: (b, 0)),
                _layer_block((1, ZG_W), layer), _layer_block((1, D_C), layer)]
    args = [zc, zg, gb, hn]
    out_specs = [pl.BlockSpec((ng * seq, MIX_W), lambda b: (b, 0))]
    out_shape = [jax.ShapeDtypeStruct((batch * seq, MIX_W), F32)]
    if has_state:
        in_specs += [pl.BlockSpec((ng, None, N_UNITS, DH_C, DH_C), lambda b: (b, layer, 0, 0, 0)),
                     pl.BlockSpec((ng, None, N_UNITS, DH_C), lambda b: (b, layer, 0, 0)),
                     pl.BlockSpec((ng, None, N_UNITS, LANES), lambda b: (b, layer, 0, 0))]
        args += list(state)
    else:
        out_specs += [pl.BlockSpec((ng, N_UNITS, DH_C, DH_C), lambda b: (b, 0, 0, 0)),
                      pl.BlockSpec((ng, N_UNITS, DH_C), lambda b: (b, 0, 0)),
                      pl.BlockSpec((ng, N_UNITS, LANES), lambda b: (b, 0, 0))]
        out_shape += [jax.ShapeDtypeStruct((batch, N_UNITS, DH_C, DH_C), F32),
                      jax.ShapeDtypeStruct((batch, N_UNITS, DH_C), F32),
                      jax.ShapeDtypeStruct((batch, N_UNITS, LANES), F32)]
    step = lambda: pltpu.VMEM((ng, nc, N_UNITS, MLSTM_CHUNK), F32)
    transposed = lambda: pltpu.VMEM((ng, nc, D_C, MLSTM_CHUNK), BF16)
    return pl.pallas_call(
        functools.partial(_mlstm_kernel, seq=seq, ng=ng, has_state=has_state),
        grid=(batch // ng,),
        in_specs=in_specs,
        out_specs=out_specs,
        out_shape=out_shape,
        scratch_shapes=[
            pltpu.VMEM((ng * seq, D_C), BF16),
            transposed(), transposed(), transposed(),
            pltpu.VMEM((ng * seq, ZG_W), F32),
            step(), step(), step(), step(), step(),
            pltpu.VMEM((ng * N_UNITS, ST_ROWS, DH_C), F32),
            pltpu.VMEM((ng, 2, nc, D_C, MLSTM_CHUNK), F32),
        ],
        compiler_params=pltpu.CompilerParams(vmem_limit_bytes=VMEM_LIMIT),
        name="mlstm",
    )(*args)


def _merge_kernel(x_ref, mod_ref, g_ref, ya_ref, yb_ref, yc_ref, yd_ref, wm_ref, bm_ref, wb_ref, wo_ref,
                  w1_ref, w2_ref, o_ref):
    x = x_ref[...]
    h = (_rms(x, g_ref[0:1, :]) * (1.0 + mod_ref[1:2, :]) + mod_ref[0:1, :]).astype(BF16)
    acc = jnp.zeros(x.shape, F32)
    for n, y_ref in enumerate((ya_ref, yb_ref, yc_ref, yd_ref)):
        cols = slice(n * D_MODEL, (n + 1) * D_MODEL)
        gate = _sigmoid(_dot(h, wm_ref[:, cols]) + bm_ref[:, cols])
        acc = acc + gate * _dot(y_ref[...].astype(BF16), wb_ref[n])
    y = _dot(acc.astype(BF16), wo_ref[...])
    x = x + mod_ref[2:3, :] * _rms(y, g_ref[1:2, :])
    h2 = (_rms(x, g_ref[2:3, :]) * (1.0 + mod_ref[4:5, :]) + mod_ref[3:4, :]).astype(BF16)
    f = jnp.zeros(x.shape, F32)
    for j in range(D_FF // D_MODEL):
        cols = slice(j * D_MODEL, (j + 1) * D_MODEL)
        a = jnp.maximum(_dot(h2, w1_ref[:, cols]), 0.0)
        f = f + _dot((a * a).astype(BF16), w2_ref[cols, :])
    o_ref[...] = x + mod_ref[5:6, :] * _rms(f, g_ref[3:4, :])


def _merge_ffn(x, mod, g, ys, wm, bm, wb, wo, w1, w2, layer, seq, per_seq):
    t = x.shape[0]
    tm = TOKEN_TILE
    full = lambda shape: _layer_block(shape, layer)
    tok = lambda w_: pl.BlockSpec((tm, w_), lambda i: (i, 0))
    return pl.pallas_call(
        _merge_kernel,
        grid=(t // tm,),
        in_specs=[tok(D_MODEL), _mod_spec(layer, tm, seq, per_seq),
                  full((4, D_MODEL)), tok(MIX_W), tok(MIX_W), tok(MIX_W), tok(MIX_W),
                  full((D_MODEL, N_BRANCH * D_MODEL)), full((1, N_BRANCH * D_MODEL)),
                  full((N_BRANCH, MIX_W, D_MODEL)), full((D_MODEL, D_MODEL)),
                  full((D_MODEL, D_FF)), full((D_FF, D_MODEL))],
        out_specs=tok(D_MODEL),
        out_shape=jax.ShapeDtypeStruct((t, D_MODEL), F32),
        compiler_params=pltpu.CompilerParams(vmem_limit_bytes=VMEM_LIMIT),
        name="merge_ffn",
    )(x, mod, g, *ys, wm, bm, wb, wo, w1, w2)


def _arrange_w_in(w):
    wt = w.T.astype(BF16)
    z = lambda n: jnp.zeros((n, D_MODEL), BF16)
    o_b, o_c = IN_A, IN_A + IN_B
    o_g, o_d = o_c + 4 * D_C, o_c + IN_C
    gates = wt[o_g:o_d].reshape(2, 2, H_C, D_MODEL).transpose(1, 0, 2, 3).reshape(4 * H_C, D_MODEL)
    return jnp.concatenate([
        wt[:Q_LORA + KV_LORA], z(64), wt[Q_LORA + KV_LORA:IN_A], z(32),
        wt[o_b:o_g], gates, z(ZG_W - 4 * H_C), wt[o_d:]], axis=0).T


def _arrange_w_uq(w):
    w = w.reshape(Q_LORA, H_A, NOPE_A + ROPE_A)
    half = ROPE_A // 2
    swapped = jnp.concatenate([jnp.zeros_like(w[:, :, :NOPE_A]), w[:, :, NOPE_A + half:], w[:, :, NOPE_A:NOPE_A + half]],
                              axis=-1)
    pad = lambda a: jnp.pad(a, ((0, 0), (0, 0), (0, LANES - NOPE_A - ROPE_A))).reshape(Q_LORA, H_A * LANES)
    return jnp.concatenate([pad(w), pad(swapped)], axis=1).astype(BF16)


def _arrange_w_ukv(w):
    w = w.reshape(KV_LORA, H_A, NOPE_A + VH_A)
    wk = jnp.pad(w[:, :, :NOPE_A], ((0, 0), (0, 0), (0, LANES - NOPE_A)))
    v = w[:, :, NOPE_A:]
    zero = jnp.zeros_like(v)
    even = jnp.concatenate([v, zero], axis=-1)
    odd = jnp.concatenate([zero, v], axis=-1)
    wv = jnp.where((jnp.arange(H_A) % 2 == 0)[None, :, None], even, odd)
    return jnp.concatenate([wk.reshape(KV_LORA, -1), wv.reshape(KV_LORA, -1)], axis=1).astype(BF16)


def _rope_tables(rows):
    row = np.repeat(np.arange(rows), GRID_W).astype(np.float64)
    col = np.tile(np.arange(GRID_W), rows).astype(np.float64)
    nf = ROPE_A // 4
    inv = np.exp(-math.log(ROPE_BASE) * np.arange(nf, dtype=np.float64) / nf)
    ang = np.concatenate([row[:, None] * inv, col[:, None] * inv], axis=-1)
    cos, sin = np.cos(ang), np.sin(ang)
    n = cos.shape[0]
    ones, zeros = (lambda k: np.ones((n, k))), (lambda k: np.zeros((n, k)))
    mla = (np.concatenate([ones(64), cos, cos, ones(32)], axis=1),
           np.concatenate([zeros(64), -sin, zeros(48)], axis=1),
           np.concatenate([zeros(80), sin, zeros(32)], axis=1))
    dif = (np.tile(np.concatenate([cos, cos], axis=1), (1, 4)),
           np.tile(np.concatenate([-sin, zeros(16)], axis=1), (1, 4)),
           np.tile(np.concatenate([zeros(16), sin], axis=1), (1, 4)))
    as_f32 = lambda ts: tuple(jnp.asarray(t.astype(np.float32)) for t in ts)
    return as_f32(mla), as_f32(dif)


def kernel(x_prompt, x_sample, cache_mla_ckv, cache_mla_krope, cache_diff_k, cache_diff_v, state_mlstm_C,
           state_mlstm_n, state_mlstm_m, c, c_ctx, w_mod, b_mod, norm_g, w_in, mla_q_norm, w_uq, mla_kv_norm,
           w_ukv, gmlp_v_norm, gmlp_w_s, gmlp_b_s, mlstm_gate_bias, mlstm_head_norm, diff_lambda, diff_sub_norm,
           w_branch, w_merge, b_merge, w_out, w_ff1, w_ff2):
    bp, lp, _ = x_prompt.shape
    bs, ls, _ = x_sample.shape
    past = cache_mla_ckv.shape[2]

    cond = jnp.concatenate([c_ctx[None, :], c, jnp.zeros((16 - 1 - bs, D_MODEL), F32)], axis=0)
    mod = _modulation(cond, w_mod, b_mod).reshape(DEPTH, 16, 6, D_MODEL)
    rope_mla, rope_dif = _rope_tables(ls // GRID_W)

    row = lambda a: a[:, None, :]
    w_in_a = jax.vmap(_arrange_w_in)(w_in)
    wuq_a = jax.vmap(_arrange_w_uq)(w_uq)
    wukv_a = jax.vmap(_arrange_w_ukv)(w_ukv)
    qn, kvn, vn, hn, bm = row(mla_q_norm), row(mla_kv_norm), row(gmlp_v_norm), row(mlstm_head_norm), row(b_merge)
    ws = gmlp_w_s.astype(BF16)
    bias = jnp.repeat(jnp.swapaxes(gmlp_b_s, 1, 2), D_B // G_B, axis=2)
    gb = jnp.pad(mlstm_gate_bias.transpose(0, 2, 1, 3).reshape(DEPTH, 1, 4 * H_C),
                 ((0, 0), (0, 0), (0, ZG_W - 4 * H_C)))
    sn = row(jnp.tile(diff_sub_norm, (1, 2)))
    wm, wb, wo = w_merge.astype(BF16), w_branch.astype(BF16), w_out.astype(BF16)
    w1, w2 = w_ff1.astype(BF16), w_ff2.astype(BF16)

    ctx_kr = cache_mla_krope.transpose(0, 1, 3, 2)
    ctx_dk = cache_diff_k.transpose(0, 1, 3, 4, 5, 2).reshape(bs, DEPTH, D_D, past)
    ctx_dv = cache_diff_v.transpose(0, 1, 3, 4, 2).reshape(bs, DEPTH, D_D, past)
    c0 = state_mlstm_C.reshape(bs, DEPTH, N_UNITS, DH_C, DH_C)
    n0 = state_mlstm_n.reshape(bs, DEPTH, N_UNITS, DH_C)
    m0 = jnp.broadcast_to(state_mlstm_m.reshape(bs, DEPTH, N_UNITS, 1), (bs, DEPTH, N_UNITS, LANES))

    xp = x_prompt.reshape(bp * lp, D_MODEL)
    xs = x_sample.reshape(bs * ls, D_MODEL)
    ents = []
    for l in range(DEPTH):
        lam_init = 0.8 - 0.6 * math.exp(-0.3 * l)
        for is_sample in (False, True):
            x = xs if is_sample else xp
            batch, seq = (bs, ls) if is_sample else (bp, lp)
            za, yb, zc, zg, dq, dk, dv = _in_proj(x, mod, norm_g, w_in_a, vn, ws, bias, l, seq, is_sample)
            if is_sample:
                (ya,) = _mla(za, qn, kvn, wuq_a, wukv_a, l, batch, seq, rope=rope_mla, ctx=(cache_mla_ckv, ctx_kr))
                (yd,) = _diff(dq, dk, dv, diff_lambda, sn, l, batch, seq, lam_init, rope=rope_dif, ctx=(ctx_dk, ctx_dv))
                (yc,) = _mlstm(zc, zg, gb, hn, l, batch, seq, state=(c0, n0, m0))
            else:
                ya, ckv, krt = _mla(za, qn, kvn, wuq_a, wukv_a, l, batch, seq)
                yd, dkt, dvt = _diff(dq, dk, dv, diff_lambda, sn, l, batch, seq, lam_init)
                yc, c_new, n_new, m_new = _mlstm(zc, zg, gb, hn, l, batch, seq)
                ents.append((
                    ckv.reshape(bp, lp, KV_LORA),
                    krt.transpose(0, 2, 1),
                    dkt.reshape(bp, H_D, 2, DH_D, lp).transpose(0, 4, 1, 2, 3),
                    dvt.reshape(bp, H_D, 2 * DH_D, lp).transpose(0, 3, 1, 2),
                    c_new.reshape(bp, 2, H_C, DH_C, DH_C),
                    n_new.reshape(bp, 2, H_C, DH_C),
                    m_new[:, :, 0].reshape(bp, 2, H_C)))
            x = _merge_ffn(x, mod, norm_g, (ya, yb, yc, yd), wm, bm, wb, wo, w1, w2, l, seq, is_sample)
            if is_sample:
                xs = x
            else:
                xp = x

    stack = lambda j: jnp.stack([e[j] for e in ents], axis=1)
    return (xp.reshape(bp, lp, D_MODEL), xs.reshape(bs, ls, D_MODEL),
            stack(0), stack(1), stack(2), stack(3), stack(4), stack(5), stack(6))
```

```python
import functools
import math

import jax
import jax.numpy as jnp
import numpy as np
from jax import lax
from jax.experimental import pallas as pl
from jax.experimental.pallas import tpu as pltpu

F32 = jnp.float32
BF16 = jnp.bfloat16

D_MODEL = 1024
DEPTH = 2
GRID_W = 64
N_BRANCH = 4
MIX_W = 256
H_A, NOPE_A, ROPE_A, VH_A = 4, 64, 32, 64
Q_LORA, KV_LORA = 256, 128
D_B, G_B, CHUNK_B = 256, 4, 128
H_C, DH_C = 4, 64
D_C = H_C * DH_C
H_D, DH_D = 4, 32
D_D = H_D * 2 * DH_D
D_FF = 4 * D_MODEL
IN_A = Q_LORA + KV_LORA + ROPE_A
IN_B = 2 * D_B
IN_C = 4 * D_C + 4 * H_C
IN_D = 3 * D_D
ROPE_BASE = 10000.0
EPS = 1e-6
MLA_SCALE = (NOPE_A + ROPE_A) ** -0.5
DIFF_SCALE = DH_D ** -0.5

LANES = 128
MLSTM_CHUNK = 128
VMEM_LIMIT = 56 * 1024 * 1024
NEG_INF = float("-inf")

ZA_W = 512
ZG_W = 128
Z_OFF_A = 0
Z_OFF_B = Z_OFF_A + ZA_W
Z_OFF_C = Z_OFF_B + IN_B
Z_OFF_G = Z_OFF_C + 4 * D_C
Z_OFF_D = Z_OFF_G + ZG_W
Z_W = Z_OFF_D + IN_D


def _rms(x, g):
    return x * lax.rsqrt(jnp.mean(x * x, axis=-1, keepdims=True) + EPS) * g


def _sigmoid(x):
    return 1.0 / (1.0 + jnp.exp(-x))


def _log_sigmoid(x):
    return jnp.minimum(x, 0.0) - jnp.log1p(jnp.exp(-jnp.abs(x)))


def _dot(a, b):
    return jnp.dot(a, b, preferred_element_type=F32)


def _dot_nt(a, b):
    return lax.dot_general(a, b, (((1,), (1,)), ((), ())), preferred_element_type=F32)


def _lane_group_mask(width, group, index):
    lane = lax.broadcasted_iota(jnp.int32, (1, width), 1)
    return (lane >= index * group) & (lane < (index + 1) * group)


def _mod_kernel(cond_ref, w_ref, b_ref, o_ref):
    c = cond_ref[...]
    s = c * _sigmoid(c)
    o_ref[...] = _dot(s.astype(BF16), w_ref[...].astype(BF16)) + b_ref[...]


def _modulation(cond, w_mod, b_mod):
    rows = cond.shape[0]
    nb = 1024
    return pl.pallas_call(
        _mod_kernel,
        grid=(DEPTH, 6 * D_MODEL // nb),
        in_specs=[
            pl.BlockSpec((rows, D_MODEL), lambda l, j: (0, 0)),
            pl.BlockSpec((None, D_MODEL, nb), lambda l, j: (l, 0, j)),
            pl.BlockSpec((None, 1, nb), lambda l, j: (l, 0, j)),
        ],
        out_specs=pl.BlockSpec((None, rows, nb), lambda l, j: (l, 0, j)),
        out_shape=jax.ShapeDtypeStruct((DEPTH, rows, 6 * D_MODEL), F32),
        name="modulation",
    )(cond, w_mod, b_mod.reshape(DEPTH, 1, 6 * D_MODEL))


TOKEN_TILE = 512


def _layer_block(shape, layer):
    shape = tuple(shape)
    return pl.BlockSpec((None,) + shape, lambda *_: (layer,) + (0,) * len(shape), pipeline_mode=pl.Buffered(1))


def _mod_spec(layer, tm, seq, per_seq):
    def index(i):
        return (layer, 1 + (i * tm) // seq if per_seq else 0, 0, 0)
    return pl.BlockSpec((None, None, 6, D_MODEL), index)


def _gmlp_tile(zb, vn, ws_ref, bias):
    tm = zb.shape[0]
    v = _rms(zb[:, D_B:], vn).astype(BF16)
    gmasks = [_lane_group_mask(D_B, D_B // G_B, g) for g in range(G_B)]
    out = []
    for ch in range(tm // CHUNK_B):
        vc = v[ch * CHUNK_B:(ch + 1) * CHUNK_B, :]
        mixed = bias
        for g in range(G_B):
            mixed = mixed + jnp.where(gmasks[g], _dot(ws_ref[g], vc), 0.0)
        out.append(zb[ch * CHUNK_B:(ch + 1) * CHUNK_B, :D_B] * mixed)
    return jnp.concatenate(out, axis=0)


def _in_kernel(x_ref, mod_ref, g_ref, w_ref, vn_ref, ws_ref, bias_ref,
               za_ref, yb_ref, zc_ref, zg_ref, dq_ref, dk_ref, dv_ref):
    x = x_ref[...]
    h = _rms(x, g_ref[0:1, :]) * (1.0 + mod_ref[1:2, :]) + mod_ref[0:1, :]
    hb = h.astype(BF16)
    za_ref[...] = _dot(hb, w_ref[:, Z_OFF_A:Z_OFF_B])
    zc_ref[...] = _dot(hb, w_ref[:, Z_OFF_C:Z_OFF_G])
    zg_ref[...] = _dot(hb, w_ref[:, Z_OFF_G:Z_OFF_D])
    for j, d_ref in enumerate((dq_ref, dk_ref, dv_ref)):
        d_ref[...] = _dot(hb, w_ref[:, Z_OFF_D + j * D_D:Z_OFF_D + (j + 1) * D_D])
    yb_ref[...] = _gmlp_tile(_dot(hb, w_ref[:, Z_OFF_B:Z_OFF_C]), vn_ref[...], ws_ref, bias_ref[...])


def _in_proj(x, mod, g, w, vn, ws, bias, layer, seq, per_seq):
    t = x.shape[0]
    tm = TOKEN_TILE
    widths = (ZA_W, MIX_W, 4 * D_C, ZG_W, D_D, D_D, D_D)
    return pl.pallas_call(
        _in_kernel,
        grid=(t // tm,),
        in_specs=[
            pl.BlockSpec((tm, D_MODEL), lambda i: (i, 0)),
            _mod_spec(layer, tm, seq, per_seq),
            _layer_block((4, D_MODEL), layer),
            _layer_block((D_MODEL, Z_W), layer),
            _layer_block((1, D_B), layer),
            _layer_block((G_B, CHUNK_B, CHUNK_B), layer),
            _layer_block((CHUNK_B, D_B), layer),
        ],
        out_specs=[pl.BlockSpec((tm, w_), lambda i: (i, 0)) for w_ in widths],
        out_shape=[jax.ShapeDtypeStruct((t, w_), F32) for w_ in widths],
        compiler_params=pltpu.CompilerParams(vmem_limit_bytes=VMEM_LIMIT),
        name="in_proj",
    )(x, mod, g, w, vn, ws, bias)


LOG2E = math.log2(math.e)
ATTN_GROUP_TOKENS = 1024


def _with_ones(v, lo_mask, keep_lo):
    lane = lax.broadcasted_iota(jnp.int32, (1, LANES), 1)
    if keep_lo:
        return jnp.where(lane == 64, 1.0, jnp.where(lo_mask, v, 0.0))
    return jnp.where(lane == 0, 1.0, jnp.where(lo_mask, 0.0, v))


def _pair_normalise(o_even, o_odd, lo_mask):
    return jnp.where(lo_mask, o_even / o_even[:, 64:65], o_odd / o_odd[:, 0:1])


def _rope128(x, tc, ts1, ts2):
    return x * tc + pltpu.roll(x, LANES - 16, 1) * ts1 + pltpu.roll(x, 16, 1) * ts2


def _mla_kernel(*refs, seq, ng, has_ctx, tq):
    rows_all = ng * seq
    if has_ctx:
        (za_ref, qn_ref, kvn_ref, wuq_ref, wukv_ref, tc_ref, ts1_ref, ts2_ref, cckv_ref, ckr_ref,
         ya_ref, q_s, k_s, v_s, ckv_s, kr_s) = refs
        past = cckv_ref.shape[0]
    else:
        (za_ref, qn_ref, kvn_ref, wuq_ref, wukv_ref, ya_ref, ckv_ref, krt_ref, q_s, k_s, v_s, ckv_s, kr_s) = refs
        past = 0
    lk = past + seq
    za = za_ref[...]
    cq = _rms(za[:, :Q_LORA], qn_ref[...])
    qh = _dot(cq.astype(BF16), wuq_ref[:, 0:(2 if has_ctx else 1) * H_A * LANES])
    ckv = _rms(za[:, Q_LORA:Q_LORA + KV_LORA], kvn_ref[...])
    kr = za[:, Q_LORA + KV_LORA:ZA_W]
    if has_ctx:
        tc, ts1, ts2 = tc_ref[...], ts1_ref[...], ts2_ref[...]
        kr = _rope128(kr, tc, ts1, ts2)
        ckv_s[0:past, :] = cckv_ref[...]
        krt = jnp.concatenate([jnp.zeros((64, past), F32), ckr_ref[...], jnp.zeros((LANES - 64 - ROPE_A, past), F32)],
                              axis=0)
        kr_s[0:past, :] = krt.T
    else:
        ckv_ref[...] = ckv
        for g in range(ng):
            krt_ref[g] = kr[g * seq:(g + 1) * seq, :].T[64:64 + ROPE_A, :]
    ckv_s[past:past + rows_all, :] = ckv
    kr_s[past:past + rows_all, :] = kr
    kv = _dot(ckv_s[...].astype(BF16), wukv_ref[...])
    kr_all = kr_s[...]
    lo = _lane_group_mask(LANES, 64, 0)
    for h in range(H_A):
        qg = qh[:, h * LANES:(h + 1) * LANES]
        if has_ctx:
            qg = qg * tc + qh[:, (H_A + h) * LANES:(H_A + h + 1) * LANES] * (ts1 + ts2)
        q_s[h] = (qg * (MLA_SCALE * LOG2E)).astype(BF16)
        k_s[h] = (kv[:, h * LANES:(h + 1) * LANES] + kr_all).astype(BF16)
        v_s[h] = _with_ones(kv[:, (H_A + h) * LANES:(H_A + h + 1) * LANES], lo, h % 2 == 0).astype(BF16)

    def body(i, carry):
        rows = pl.ds(pl.multiple_of(i * tq, tq), tq)
        keys = pl.ds(pl.multiple_of(((i * tq) // seq) * seq, seq), lk)
        s = [_dot_nt(q_s[h, rows, :], k_s[h, keys, :]) for h in range(H_A)]
        o = [_dot(jnp.exp2(s[h] - jnp.max(s[h], axis=-1, keepdims=True)).astype(BF16), v_s[h, keys, :])
             for h in range(H_A)]
        for pair in range(H_A // 2):
            ya_ref[rows, pair * LANES:(pair + 1) * LANES] = _pair_normalise(o[2 * pair], o[2 * pair + 1], lo)
        return carry

    lax.fori_loop(0, rows_all // tq, body, 0, unroll=not has_ctx)


def _ctx_block(rows, cols, layer):
    return pl.BlockSpec((None, None, rows, cols), lambda b: (b, layer, 0, 0))


def _mla(za, qn, kvn, wuq, wukv, layer, batch, seq, rope=None, ctx=None):
    has_ctx = ctx is not None
    past = ctx[0].shape[2] if has_ctx else 0
    ng = 1 if has_ctx else ATTN_GROUP_TOKENS // seq
    rows = ng * seq
    keys = past + rows
    tq = min(seq, 256)
    full = lambda shape: pl.BlockSpec(shape, lambda b: (0,) * len(shape))
    in_specs = [
        pl.BlockSpec((rows, ZA_W), lambda b: (b, 0)),
        _layer_block((1, Q_LORA), layer), _layer_block((1, KV_LORA), layer),
        _layer_block((Q_LORA, 2 * H_A * LANES), layer), _layer_block((KV_LORA, 2 * H_A * LANES), layer),
    ]
    args = [za, qn, kvn, wuq, wukv]
    out_specs = [pl.BlockSpec((rows, MIX_W), lambda b: (b, 0))]
    out_shape = [jax.ShapeDtypeStruct((batch * seq, MIX_W), F32)]
    if has_ctx:
        in_specs += [full((seq, LANES))] * 3
        in_specs += [_ctx_block(past, KV_LORA, layer), _ctx_block(ROPE_A, past, layer)]
        args += list(rope) + list(ctx)
    else:
        out_specs += [pl.BlockSpec((rows, KV_LORA), lambda b: (b, 0)),
                      pl.BlockSpec((ng, ROPE_A, seq), lambda b: (b, 0, 0))]
        out_shape += [jax.ShapeDtypeStruct((batch * seq, KV_LORA), F32),
                      jax.ShapeDtypeStruct((batch, ROPE_A, seq), F32)]
    return pl.pallas_call(
        functools.partial(_mla_kernel, seq=seq, ng=ng, has_ctx=has_ctx, tq=tq),
        grid=(batch // ng,),
        in_specs=in_specs,
        out_specs=out_specs,
        out_shape=out_shape,
        scratch_shapes=[
            pltpu.VMEM((H_A, rows, LANES), BF16),
            pltpu.VMEM((H_A, keys, LANES), BF16),
            pltpu.VMEM((H_A, keys, LANES), BF16),
            pltpu.VMEM((keys, LANES), F32),
            pltpu.VMEM((keys, LANES), F32),
        ],
        compiler_params=pltpu.CompilerParams(vmem_limit_bytes=VMEM_LIMIT),
        name="mla",
    )(*args)


def _diff_kernel(*refs, seq, ng, has_ctx, tq, lam_init):
    rows_all = ng * seq
    if has_ctx:
        (dq_ref, dk_ref, dv_ref, lam_ref, sn_ref, tc_ref, ts1_ref, ts2_ref, ck_ref, cv_ref,
         yd_ref, q_s, k_s, v_s) = refs
        past = ck_ref.shape[1]
    else:
        (dq_ref, dk_ref, dv_ref, lam_ref, sn_ref, yd_ref, dkt_ref, dvt_ref, q_s, k_s, v_s) = refs
        past = 0
        for g in range(ng):
            dkt_ref[g] = dk_ref[g * seq:(g + 1) * seq, :].T
            dvt_ref[g] = dv_ref[g * seq:(g + 1) * seq, :].T
    lk = past + seq
    lo = _lane_group_mask(LANES, 64, 0)
    if has_ctx:
        tc, ts1, ts2 = tc_ref[...], ts1_ref[...], ts2_ref[...]
        k_s[0:past, :] = ck_ref[...].T.astype(BF16)
        cv = cv_ref[...].T
    for half in range(2):
        cols = slice(half * LANES, (half + 1) * LANES)
        q, k, v = dq_ref[:, cols], dk_ref[:, cols], dv_ref[:, cols]
        if has_ctx:
            q = _rope128(q, tc, ts1, ts2)
            k = _rope128(k, tc, ts1, ts2)
            v_s[2 * half, 0:past, :] = _with_ones(cv[:, cols], lo, True).astype(BF16)
            v_s[2 * half + 1, 0:past, :] = _with_ones(cv[:, cols], lo, False).astype(BF16)
        q_s[:, cols] = (q * (DIFF_SCALE * LOG2E)).astype(BF16)
        k_s[past:past + rows_all, cols] = k.astype(BF16)
        v_s[2 * half, past:past + rows_all, :] = _with_ones(v, lo, True).astype(BF16)
        v_s[2 * half + 1, past:past + rows_all, :] = _with_ones(v, lo, False).astype(BF16)

    lam = lam_ref[...]
    lam_val = (jnp.exp(jnp.sum(lam[0:1] * lam[1:2], axis=-1, keepdims=True))
               - jnp.exp(jnp.sum(lam[2:3] * lam[3:4], axis=-1, keepdims=True)) + lam_init)
    comp_masks = [jnp.where(_lane_group_mask(D_D, DH_D, j), 1.0, 0.0).astype(BF16) for j in range(2 * H_D)]
    sn = sn_ref[...]

    def body(i, carry):
        rows = pl.ds(pl.multiple_of(i * tq, tq), tq)
        qb = q_s[rows, :]
        keys = pl.ds(pl.multiple_of(((i * tq) // seq) * seq, seq), lk)
        k_all = k_s[keys, :]
        score = lambda h: [_dot_nt(qb * comp_masks[2 * h + comp], k_all) for comp in range(2)]
        s_next = score(0)
        o = []
        for h in range(H_D):
            s = s_next
            if h + 1 < H_D:
                s_next = score(h + 1)
            o1, o2 = [_dot(jnp.exp2(sc - jnp.max(sc, axis=-1, keepdims=True)).astype(BF16), v_s[h, keys, :])
                      for sc in s]
            ll = 64 if h % 2 == 0 else 0
            o.append(o1 / o1[:, ll:ll + 1] - lam_val * (o2 / o2[:, ll:ll + 1]))
        for pair in range(H_D // 2):
            acc = jnp.where(lo, o[2 * pair], o[2 * pair + 1])
            sq = acc * acc
            ss_lo = jnp.sum(jnp.where(lo, sq, 0.0), axis=-1, keepdims=True)
            ss_hi = jnp.sum(jnp.where(lo, 0.0, sq), axis=-1, keepdims=True)
            r = jnp.where(lo, lax.rsqrt(ss_lo / (2 * DH_D) + EPS), lax.rsqrt(ss_hi / (2 * DH_D) + EPS))
            yd_ref[rows, pair * LANES:(pair + 1) * LANES] = acc * r * sn * (1.0 - lam_init)
        return carry

    lax.fori_loop(0, rows_all // tq, body, 0, unroll=not has_ctx)


def _diff(dq, dk, dv, lam, sn, layer, batch, seq, lam_init, rope=None, ctx=None):
    has_ctx = ctx is not None
    past = ctx[0].shape[3] if has_ctx else 0
    ng = 1 if has_ctx else ATTN_GROUP_TOKENS // seq
    rows = ng * seq
    keys = past + rows
    tq = min(seq, 256)
    full = lambda shape: pl.BlockSpec(shape, lambda b: (0,) * len(shape))
    in_specs = [pl.BlockSpec((rows, D_D), lambda b: (b, 0))] * 3
    in_specs += [_layer_block((4, DH_D), layer), _layer_block((1, LANES), layer)]
    args = [dq, dk, dv, lam, sn]
    if has_ctx:
        in_specs += [full((seq, LANES))] * 3
        in_specs += [_ctx_block(D_D, past, layer)] * 2
        args += list(rope) + list(ctx)
    out_specs = [pl.BlockSpec((rows, MIX_W), lambda b: (b, 0))]
    out_shape = [jax.ShapeDtypeStruct((batch * seq, MIX_W), F32)]
    if not has_ctx:
        out_specs += [pl.BlockSpec((ng, D_D, seq), lambda b: (b, 0, 0))] * 2
        out_shape += [jax.ShapeDtypeStruct((batch, D_D, seq), F32)] * 2
    return pl.pallas_call(
        functools.partial(_diff_kernel, seq=seq, ng=ng, has_ctx=has_ctx, tq=tq, lam_init=lam_init),
        grid=(batch // ng,),
        in_specs=in_specs,
        out_specs=out_specs,
        out_shape=out_shape,
        scratch_shapes=[
            pltpu.VMEM((rows, D_D), BF16),
            pltpu.VMEM((keys, D_D), BF16),
            pltpu.VMEM((H_D, keys, LANES), BF16),
        ],
        compiler_params=pltpu.CompilerParams(vmem_limit_bytes=VMEM_LIMIT),
        name="diff_attn",
    )(*args)


def _split3(x):
    hi = x.astype(BF16)
    r1 = x - hi.astype(F32)
    mid = r1.astype(BF16)
    lo = (r1 - mid.astype(F32)).astype(BF16)
    return hi, mid, lo


ST_ROWS = DH_C + 8
N_UNITS = 2 * H_C
MLSTM_GROUP_TOKENS = 1024


def _mlstm_kernel(*refs, seq, ng, has_state):
    ck = MLSTM_CHUNK
    nc = seq // ck
    if has_state:
        (zc_ref, zg_ref, gb_ref, hn_ref, c0_ref, n0_ref, m0_ref,
         yc_ref, k_s, qt_s, kt_s, vt_s, rcc_s, ca_s, wi_s, en_s, ws_s, wo_s, st_s, ht_s) = refs
    else:
        (zc_ref, zg_ref, gb_ref, hn_ref,
         yc_ref, cout_ref, nout_ref, mout_ref,
         k_s, qt_s, kt_s, vt_s, rcc_s, ca_s, wi_s, en_s, ws_s, wo_s, st_s, ht_s) = refs
    eye = jnp.where(lax.broadcasted_iota(jnp.int32, (DH_C, DH_C), 0) == lax.broadcasted_iota(jnp.int32, (DH_C, DH_C), 1),
                    1.0, 0.0).astype(BF16)

    def transpose_exact(a):
        return sum(_dot_nt(eye, part) for part in _split3(a))

    def chunk_rows(g, c):
        return slice(g * seq + c * ck, g * seq + (c + 1) * ck)

    for g in range(ng):
        for c in range(nc):
            rows = chunk_rows(g, c)
            qt_s[g, c] = zc_ref[rows, 0:D_C].T.astype(BF16)
            kk = zc_ref[rows, D_C:2 * D_C] * (DH_C ** -0.5)
            k_s[rows, :] = kk.astype(BF16)
            kt_s[g, c] = kk.T.astype(BF16)
            vt_s[g, c] = zc_ref[rows, 2 * D_C:3 * D_C].T.astype(BF16)

    gates = zg_ref[...] + gb_ref[...]
    r_i = lax.broadcasted_iota(jnp.int32, (ck, ck), 0)
    c_i = lax.broadcasted_iota(jnp.int32, (ck, ck), 1)
    sum_fw = jnp.where(r_i <= c_i, 1.0, 0.0).astype(BF16)
    sum_bw = jnp.where(r_i >= c_i, 1.0, 0.0).astype(BF16)
    row8 = lax.broadcasted_iota(jnp.int32, (N_UNITS, ck), 0)
    fw8 = row8 < H_C
    fw81 = fw8[:, 0:1]
    b_steps, r_steps = [], []
    for g in range(ng):
        i_rows, b_rows = [], []
        for c in range(nc):
            rows = chunk_rows(g, c)
            g_t = gates[rows, :].T
            i8 = g_t[0:N_UNITS, :]
            p0, p1, p2 = _split3(_log_sigmoid(g_t[N_UNITS:2 * N_UNITS, :]))
            b_fw = _dot(p0, sum_fw) + _dot(p1, sum_fw) + _dot(p2, sum_fw)
            b_bw = _dot(p0, sum_bw) + _dot(p1, sum_bw) + _dot(p2, sum_bw)
            b8 = jnp.where(fw8, b_fw, b_bw)
            rcc_s[rows, :] = jnp.concatenate([i8 - b8, jnp.zeros((ck - N_UNITS, ck), F32)], axis=0).T
            i_rows.append(i8)
            b_rows.append(b8)
        b_steps.append([jnp.where(fw8, b_rows[i], b_rows[nc - 1 - i]) for i in range(nc)])
        r_steps.append([jnp.where(fw8, i_rows[i], i_rows[nc - 1 - i]) - b_steps[g][i] for i in range(nc)])

    pm = sm = jnp.concatenate([r for g in range(ng) for r in r_steps[g]], axis=0)
    lane_all = lax.broadcasted_iota(jnp.int32, pm.shape, 1)
    sh = 1
    while sh < ck:
        pm = jnp.maximum(pm, jnp.where(lane_all >= sh, pltpu.roll(pm, sh, 1), NEG_INF))
        sm = jnp.maximum(sm, jnp.where(lane_all < ck - sh, pltpu.roll(sm, ck - sh, 1), NEG_INF))
        sh *= 2
    m_last = []
    for g in range(ng):
        m_prev = m0_ref[g, :, 0:1] if has_state else jnp.zeros((N_UNITS, 1), F32)
        for i in range(nc):
            b8, r8 = b_steps[g][i], r_steps[g][i]
            at = (g * nc + i) * N_UNITS
            pm8, sm8 = pm[at:at + N_UNITS, :], sm[at:at + N_UNITS, :]
            cm = jnp.where(fw8, pm8, sm8)
            b_t = jnp.where(fw81, b8[:, ck - 1:ck], b8[:, 0:1])
            cm_end = jnp.where(fw81, pm8[:, ck - 1:ck], sm8[:, 0:1])
            inter = b8 + m_prev
            mt = jnp.maximum(inter, b8 + cm)
            ca_s[g, i] = b8 - mt
            wi_s[g, i] = jnp.exp(inter - mt)
            en_s[g, i] = jnp.exp(-mt)
            m_new = b_t + jnp.maximum(m_prev, cm_end)
            ws_s[g, i] = jnp.exp(b_t + r8 - m_new)
            wo_s[g, i] = jnp.broadcast_to(jnp.exp(b_t + m_prev - m_new), (N_UNITS, ck))
            m_prev = m_new
        m_last.append(m_prev)

    if has_state:
        for g in range(ng):
            for u in range(N_UNITS):
                st_s[g * N_UNITS + u, 0:DH_C, :] = transpose_exact(c0_ref[g, u])
                st_s[g * N_UNITS + u, DH_C:ST_ROWS, :] = jnp.broadcast_to(n0_ref[g, u:u + 1, :],
                                                                         (ST_ROWS - DH_C, DH_C))
    else:
        st_s[...] = jnp.zeros_like(st_s)

    hmask_bf = [jnp.where(_lane_group_mask(D_C, DH_C, h), 1.0, 0.0).astype(BF16) for h in range(H_C)]
    src_ok = (r_i <= c_i, r_i >= c_i)
    hs = [slice(h * DH_C, (h + 1) * DH_C) for h in range(H_C)]
    units = [(g, d, h) for g in range(ng) for d in range(2) for h in range(H_C)]

    def body(i, carry):
        chunks = (i, nc - 1 - i)
        ca, wi, en, ws, wo = [[ref[g, i] for g in range(ng)] for ref in (ca_s, wi_s, en_s, ws_s, wo_s)]
        q_t = [[qt_s[g, c] for c in chunks] for g in range(ng)]
        k_t = [[kt_s[g, c] for c in chunks] for g in range(ng)]
        v_t = [[vt_s[g, c] for c in chunks] for g in range(ng)]
        rows = [[pl.ds(pl.multiple_of(g * seq + c * ck, ck), ck) for c in chunks] for g in range(ng)]
        sts, qcts, states, upds, pts = [], [], [], [], []
        for g, d, h in units:
            sts.append(_dot(k_s[rows[g][d], :] * hmask_bf[h], q_t[g][d]))
            states.append(st_s[g * N_UNITS + d * H_C + h])
            qcts.append(_dot(states[-1].astype(BF16), q_t[g][d][hs[h], :]))
        for g, d, h in units:
            u = d * H_C + h
            w_s = ws[g][u:u + 1, :]
            vw = jnp.concatenate([v_t[g][d][hs[h], :].astype(F32) * w_s, jnp.broadcast_to(w_s, (8, ck))], axis=0)
            upds.append(_dot_nt(vw.astype(BF16), k_t[g][d][hs[h], :]))
        for j, (g, d, h) in enumerate(units):
            u = d * H_C + h
            decay = jnp.exp(rcc_s[rows[g][d], u:u + 1] + ca[g][u:u + 1, :])
            pts.append(sts[j] * jnp.where(src_ok[d], decay, 0.0))
        nums = [_dot(v_t[g][d][hs[h], :], pts[j].astype(BF16)) for j, (g, d, h) in enumerate(units)]
        for j, (g, d, h) in enumerate(units):
            u = d * H_C + h
            w_i = wi[g][u:u + 1, :]
            den = w_i * qcts[j][DH_C:DH_C + 1, :] + jnp.sum(pts[j], axis=0, keepdims=True)
            num = w_i * qcts[j][0:DH_C, :] + nums[j]
            ht_s[g, d, chunks[d], hs[h], :] = num / jnp.maximum(jnp.abs(den), en[g][u:u + 1, :])
            st_s[g * N_UNITS + u] = wo[g][u:u + 1, 0:DH_C] * states[j] + upds[j]
        return carry

    lax.fori_loop(0, nc, body, 0)

    hn = hn_ref[...]
    for g in range(ng):
        for c in range(nc):
            rows = chunk_rows(g, c)
            h_t = ht_s[g, 0, c] + ht_s[g, 1, c]
            parts = []
            for h in range(H_C):
                x = h_t[hs[h], :]
                parts.append(x * lax.rsqrt(jnp.mean(x * x, axis=0, keepdims=True) + EPS))
            h_n = jnp.concatenate(parts, axis=0).T
            yc_ref[rows, :] = _sigmoid(zc_ref[rows, 3 * D_C:4 * D_C]) * (h_n * hn)

    if not has_state:
        for g in range(ng):
            for u in range(N_UNITS):
                cout_ref[g, u] = transpose_exact(st_s[g * N_UNITS + u, 0:DH_C, :])
                nout_ref[g, u:u + 1, :] = st_s[g * N_UNITS + u, DH_C:DH_C + 1, :]
            mout_ref[g] = jnp.broadcast_to(m_last[g], (N_UNITS, LANES))


def _mlstm(zc, zg, gb, hn, layer, batch, seq, state=None):
    has_state = state is not None
    nc = seq // MLSTM_CHUNK
    ng = max(2, MLSTM_GROUP_TOKENS // seq)
    in_specs = [pl.BlockSpec((ng * seq, 4 * D_C), lambda b: (b, 0)), pl.BlockSpec((ng * seq, ZG_W), lambda b: (b, 0)),
                _layer_block((1, ZG_W), layer), _layer_block((1, D_C), layer)]
    args = [zc, zg, gb, hn]
    out_specs = [pl.BlockSpec((ng * seq, MIX_W), lambda b: (b, 0))]
    out_shape = [jax.ShapeDtypeStruct((batch * seq, MIX_W), F32)]
    if has_state:
        in_specs += [pl.BlockSpec((ng, None, N_UNITS, DH_C, DH_C), lambda b: (b, layer, 0, 0, 0)),
                     pl.BlockSpec((ng, None, N_UNITS, DH_C), lambda b: (b, layer, 0, 0)),
                     pl.BlockSpec((ng, None, N_UNITS, LANES), lambda b: (b, layer, 0, 0))]
        args += list(state)
    else:
        out_specs += [pl.BlockSpec((ng, N_UNITS, DH_C, DH_C), lambda b: (b, 0, 0, 0)),
                      pl.BlockSpec((ng, N_UNITS, DH_C), lambda b: (b, 0, 0)),
                      pl.BlockSpec((ng, N_UNITS, LANES), lambda b: (b, 0, 0))]
        out_shape += [jax.ShapeDtypeStruct((batch, N_UNITS, DH_C, DH_C), F32),
                      jax.ShapeDtypeStruct((batch, N_UNITS, DH_C), F32),
                      jax.ShapeDtypeStruct((batch, N_UNITS, LANES), F32)]
    step = lambda: pltpu.VMEM((ng, nc, N_UNITS, MLSTM_CHUNK), F32)
    transposed = lambda: pltpu.VMEM((ng, nc, D_C, MLSTM_CHUNK), BF16)
    return pl.pallas_call(
        functools.partial(_mlstm_kernel, seq=seq, ng=ng, has_state=has_state),
        grid=(batch // ng,),
        in_specs=in_specs,
        out_specs=out_specs,
        out_shape=out_shape,
        scratch_shapes=[
            pltpu.VMEM((ng * seq, D_C), BF16),
            transposed(), transposed(), transposed(),
            pltpu.VMEM((ng * seq, ZG_W), F32),
            step(), step(), step(), step(), step(),
            pltpu.VMEM((ng * N_UNITS, ST_ROWS, DH_C), F32),
            pltpu.VMEM((ng, 2, nc, D_C, MLSTM_CHUNK), F32),
        ],
        compiler_params=pltpu.CompilerParams(vmem_limit_bytes=VMEM_LIMIT),
        name="mlstm",
    )(*args)


def _merge_kernel(x_ref, mod_ref, g_ref, ya_ref, yb_ref, yc_ref, yd_ref, wm_ref, bm_ref, wb_ref, wo_ref,
                  w1_ref, w2_ref, o_ref):
    x = x_ref[...]
    h = (_rms(x, g_ref[0:1, :]) * (1.0 + mod_ref[1:2, :]) + mod_ref[0:1, :]).astype(BF16)
    acc = jnp.zeros(x.shape, F32)
    for n, y_ref in enumerate((ya_ref, yb_ref, yc_ref, yd_ref)):
        cols = slice(n * D_MODEL, (n + 1) * D_MODEL)
        gate = _sigmoid(_dot(h, wm_ref[:, cols]) + bm_ref[:, cols])
        acc = acc + gate * _dot(y_ref[...].astype(BF16), wb_ref[n])
    y = _dot(acc.astype(BF16), wo_ref[...])
    x = x + mod_ref[2:3, :] * _rms(y, g_ref[1:2, :])
    h2 = (_rms(x, g_ref[2:3, :]) * (1.0 + mod_ref[4:5, :]) + mod_ref[3:4, :]).astype(BF16)
    f = jnp.zeros(x.shape, F32)
    for j in range(D_FF // D_MODEL):
        cols = slice(j * D_MODEL, (j + 1) * D_MODEL)
        a = jnp.maximum(_dot(h2, w1_ref[:, cols]), 0.0)
        f = f + _dot((a * a).astype(BF16), w2_ref[cols, :])
    o_ref[...] = x + mod_ref[5:6, :] * _rms(f, g_ref[3:4, :])


def _merge_ffn(x, mod, g, ys, wm, bm, wb, wo, w1, w2, layer, seq, per_seq):
    t = x.shape[0]
    tm = TOKEN_TILE
    full = lambda shape: _layer_block(shape, layer)
    tok = lambda w_: pl.BlockSpec((tm, w_), lambda i: (i, 0))
    return pl.pallas_call(
        _merge_kernel,
        grid=(t // tm,),
        in_specs=[tok(D_MODEL), _mod_spec(layer, tm, seq, per_seq),
                  full((4, D_MODEL)), tok(MIX_W), tok(MIX_W), tok(MIX_W), tok(MIX_W),
                  full((D_MODEL, N_BRANCH * D_MODEL)), full((1, N_BRANCH * D_MODEL)),
                  full((N_BRANCH, MIX_W, D_MODEL)), full((D_MODEL, D_MODEL)),
                  full((D_MODEL, D_FF)), full((D_FF, D_MODEL))],
        out_specs=tok(D_MODEL),
        out_shape=jax.ShapeDtypeStruct((t, D_MODEL), F32),
        compiler_params=pltpu.CompilerParams(vmem_limit_bytes=VMEM_LIMIT),
        name="merge_ffn",
    )(x, mod, g, *ys, wm, bm, wb, wo, w1, w2)


def _arrange_w_in(w):
    wt = w.T.astype(BF16)
    z = lambda n: jnp.zeros((n, D_MODEL), BF16)
    o_b, o_c = IN_A, IN_A + IN_B
    o_g, o_d = o_c + 4 * D_C, o_c + IN_C
    gates = wt[o_g:o_d].reshape(2, 2, H_C, D_MODEL).transpose(1, 0, 2, 3).reshape(4 * H_C, D_MODEL)
    return jnp.concatenate([
        wt[:Q_LORA + KV_LORA], z(64), wt[Q_LORA + KV_LORA:IN_A], z(32),
        wt[o_b:o_g], gates, z(ZG_W - 4 * H_C), wt[o_d:]], axis=0).T


def _arrange_w_uq(w):
    w = w.reshape(Q_LORA, H_A, NOPE_A + ROPE_A)
    half = ROPE_A // 2
    swapped = jnp.concatenate([jnp.zeros_like(w[:, :, :NOPE_A]), w[:, :, NOPE_A + half:], w[:, :, NOPE_A:NOPE_A + half]],
                              axis=-1)
    pad = lambda a: jnp.pad(a, ((0, 0), (0, 0), (0, LANES - NOPE_A - ROPE_A))).reshape(Q_LORA, H_A * LANES)
    return jnp.concatenate([pad(w), pad(swapped)], axis=1).astype(BF16)


def _arrange_w_ukv(w):
    w = w.reshape(KV_LORA, H_A, NOPE_A + VH_A)
    wk = jnp.pad(w[:, :, :NOPE_A], ((0, 0), (0, 0), (0, LANES - NOPE_A)))
    v = w[:, :, NOPE_A:]
    zero = jnp.zeros_like(v)
    even = jnp.concatenate([v, zero], axis=-1)
    odd = jnp.concatenate([zero, v], axis=-1)
    wv = jnp.where((jnp.arange(H_A) % 2 == 0)[None, :, None], even, odd)
    return jnp.concatenate([wk.reshape(KV_LORA, -1), wv.reshape(KV_LORA, -1)], axis=1).astype(BF16)


def _rope_tables(rows):
    row = np.repeat(np.arange(rows), GRID_W).astype(np.float64)
    col = np.tile(np.arange(GRID_W), rows).astype(np.float64)
    nf = ROPE_A // 4
    inv = np.exp(-math.log(ROPE_BASE) * np.arange(nf, dtype=np.float64) / nf)
    ang = np.concatenate([row[:, None] * inv, col[:, None] * inv], axis=-1)
    cos, sin = np.cos(ang), np.sin(ang)
    n = cos.shape[0]
    ones, zeros = (lambda k: np.ones((n, k))), (lambda k: np.zeros((n, k)))
    mla = (np.concatenate([ones(64), cos, cos, ones(32)], axis=1),
           np.concatenate([zeros(64), -sin, zeros(48)], axis=1),
           np.concatenate([zeros(80), sin, zeros(32)], axis=1))
    dif = (np.tile(np.concatenate([cos, cos], axis=1), (1, 4)),
           np.tile(np.concatenate([-sin, zeros(16)], axis=1), (1, 4)),
           np.tile(np.concatenate([zeros(16), sin], axis=1), (1, 4)))
    as_f32 = lambda ts: tuple(jnp.asarray(t.astype(np.float32)) for t in ts)
    return as_f32(mla), as_f32(dif)


def kernel(x_prompt, x_sample, cache_mla_ckv, cache_mla_krope, cache_diff_k, cache_diff_v, state_mlstm_C,
           state_mlstm_n, state_mlstm_m, c, c_ctx, w_mod, b_mod, norm_g, w_in, mla_q_norm, w_uq, mla_kv_norm,
           w_ukv, gmlp_v_norm, gmlp_w_s, gmlp_b_s, mlstm_gate_bias, mlstm_head_norm, diff_lambda, diff_sub_norm,
           w_branch, w_merge, b_merge, w_out, w_ff1, w_ff2):
    bp, lp, _ = x_prompt.shape
    bs, ls, _ = x_sample.shape
    past = cache_mla_ckv.shape[2]

    cond = jnp.concatenate([c_ctx[None, :], c, jnp.zeros((16 - 1 - bs, D_MODEL), F32)], axis=0)
    mod = _modulation(cond, w_mod, b_mod).reshape(DEPTH, 16, 6, D_MODEL)
    rope_mla, rope_dif = _rope_tables(ls // GRID_W)

    row = lambda a: a[:, None, :]
    w_in_a = jax.vmap(_arrange_w_in)(w_in)
    wuq_a = jax.vmap(_arrange_w_uq)(w_uq)
    wukv_a = jax.vmap(_arrange_w_ukv)(w_ukv)
    qn, kvn, vn, hn, bm = row(mla_q_norm), row(mla_kv_norm), row(gmlp_v_norm), row(mlstm_head_norm), row(b_merge)
    ws = gmlp_w_s.astype(BF16)
    bias = jnp.repeat(jnp.swapaxes(gmlp_b_s, 1, 2), D_B // G_B, axis=2)
    gb = jnp.pad(mlstm_gate_bias.transpose(0, 2, 1, 3).reshape(DEPTH, 1, 4 * H_C),
                 ((0, 0), (0, 0), (0, ZG_W - 4 * H_C)))
    sn = row(jnp.tile(diff_sub_norm, (1, 2)))
    wm, wb, wo = w_merge.astype(BF16), w_branch.astype(BF16), w_out.astype(BF16)
    w1, w2 = w_ff1.astype(BF16), w_ff2.astype(BF16)

    ctx_kr = cache_mla_krope.transpose(0, 1, 3, 2)
    ctx_dk = cache_diff_k.transpose(0, 1, 3, 4, 5, 2).reshape(bs, DEPTH, D_D, past)
    ctx_dv = cache_diff_v.transpose(0, 1, 3, 4, 2).reshape(bs, DEPTH, D_D, past)
    c0 = state_mlstm_C.reshape(bs, DEPTH, N_UNITS, DH_C, DH_C)
    n0 = state_mlstm_n.reshape(bs, DEPTH, N_UNITS, DH_C)
    m0 = jnp.broadcast_to(state_mlstm_m.reshape(bs, DEPTH, N_UNITS, 1), (bs, DEPTH, N_UNITS, LANES))

    xp = x_prompt.reshape(bp * lp, D_MODEL)
    xs = x_sample.reshape(bs * ls, D_MODEL)
    ents = []
    for l in range(DEPTH):
        lam_init = 0.8 - 0.6 * math.exp(-0.3 * l)
        for is_sample in (False, True):
            x = xs if is_sample else xp
            batch, seq = (bs, ls) if is_sample else (bp, lp)
            za, yb, zc, zg, dq, dk, dv = _in_proj(x, mod, norm_g, w_in_a, vn, ws, bias, l, seq, is_sample)
            if is_sample:
                (ya,) = _mla(za, qn, kvn, wuq_a, wukv_a, l, batch, seq, rope=rope_mla, ctx=(cache_mla_ckv, ctx_kr))
                (yd,) = _diff(dq, dk, dv, diff_lambda, sn, l, batch, seq, lam_init, rope=rope_dif, ctx=(ctx_dk, ctx_dv))
                (yc,) = _mlstm(zc, zg, gb, hn, l, batch, seq, state=(c0, n0, m0))
            else:
                ya, ckv, krt = _mla(za, qn, kvn, wuq_a, wukv_a, l, batch, seq)
                yd, dkt, dvt = _diff(dq, dk, dv, diff_lambda, sn, l, batch, seq, lam_init)
                yc, c_new, n_new, m_new = _mlstm(zc, zg, gb, hn, l, batch, seq)
                ents.append((
                    ckv.reshape(bp, lp, KV_LORA),
                    krt.transpose(0, 2, 1),
                    dkt.reshape(bp, H_D, 2, DH_D, lp).transpose(0, 4, 1, 2, 3),
                    dvt.reshape(bp, H_D, 2 * DH_D, lp).transpose(0, 3, 1, 2),
                    c_new.reshape(bp, 2, H_C, DH_C, DH_C),
                    n_new.reshape(bp, 2, H_C, DH_C),
                    m_new[:, :, 0].reshape(bp, 2, H_C)))
            x = _merge_ffn(x, mod, norm_g, (ya, yb, yc, yd), wm, bm, wb, wo, w1, w2, l, seq, is_sample)
            if is_sample:
                xs = x
            else:
                xp = x

    stack = lambda j: jnp.stack([e[j] for e in ents], axis=1)
    return (xp.reshape(bp, lp, D_MODEL), xs.reshape(bs, ls, D_MODEL),
            stack(0), stack(1), stack(2), stack(3), stack(4), stack(5), stack(6))
```

```python
import functools
import math

import jax
import jax.numpy as jnp
import numpy as np
from jax import lax
from jax.experimental import pallas as pl
from jax.experimental.pallas import tpu as pltpu

F32 = jnp.float32
BF16 = jnp.bfloat16

D_MODEL = 1024
DEPTH = 2
GRID_W = 64
N_BRANCH = 4
MIX_W = 256
H_A, NOPE_A, ROPE_A, VH_A = 4, 64, 32, 64
Q_LORA, KV_LORA = 256, 128
D_B, G_B, CHUNK_B = 256, 4, 128
H_C, DH_C = 4, 64
D_C = H_C * DH_C
H_D, DH_D = 4, 32
D_D = H_D * 2 * DH_D
D_FF = 4 * D_MODEL
IN_A = Q_LORA + KV_LORA + ROPE_A
IN_B = 2 * D_B
IN_C = 4 * D_C + 4 * H_C
IN_D = 3 * D_D
ROPE_BASE = 10000.0
EPS = 1e-6
MLA_SCALE = (NOPE_A + ROPE_A) ** -0.5
DIFF_SCALE = DH_D ** -0.5

LANES = 128
MLSTM_CHUNK = 128
VMEM_LIMIT = 56 * 1024 * 1024
NEG_INF = float("-inf")

ZA_W = 512
ZG_W = 128
Z_OFF_A = 0
Z_OFF_B = Z_OFF_A + ZA_W
Z_OFF_C = Z_OFF_B + IN_B
Z_OFF_G = Z_OFF_C + 4 * D_C
Z_OFF_D = Z_OFF_G + ZG_W
Z_W = Z_OFF_D + IN_D


def _rms(x, g):
    return x * lax.rsqrt(jnp.mean(x * x, axis=-1, keepdims=True) + EPS) * g


def _sigmoid(x):
    return 1.0 / (1.0 + jnp.exp(-x))


def _log_sigmoid(x):
    return jnp.minimum(x, 0.0) - jnp.log1p(jnp.exp(-jnp.abs(x)))


def _dot(a, b):
    return jnp.dot(a, b, preferred_element_type=F32)


def _dot_nt(a, b):
    return lax.dot_general(a, b, (((1,), (1,)), ((), ())), preferred_element_type=F32)


def _lane_group_mask(width, group, index):
    lane = lax.broadcasted_iota(jnp.int32, (1, width), 1)
    return (lane >= index * group) & (lane < (index + 1) * group)


def _mod_kernel(cond_ref, w_ref, b_ref, o_ref):
    c = cond_ref[...]
    s = c * _sigmoid(c)
    o_ref[...] = _dot(s.astype(BF16), w_ref[...].astype(BF16)) + b_ref[...]


def _modulation(cond, w_mod, b_mod):
    rows = cond.shape[0]
    nb = 2048
    return pl.pallas_call(
        _mod_kernel,
        grid=(DEPTH, 6 * D_MODEL // nb),
        in_specs=[
            pl.BlockSpec((rows, D_MODEL), lambda l, j: (0, 0)),
            pl.BlockSpec((None, D_MODEL, nb), lambda l, j: (l, 0, j)),
            pl.BlockSpec((None, 1, nb), lambda l, j: (l, 0, j)),
        ],
        out_specs=pl.BlockSpec((None, rows, nb), lambda l, j: (l, 0, j)),
        out_shape=jax.ShapeDtypeStruct((DEPTH, rows, 6 * D_MODEL), F32),
        name="modulation",
    )(cond, w_mod, b_mod.reshape(DEPTH, 1, 6 * D_MODEL))


TOKEN_TILE = 512


def _layer_block(shape, layer):
    shape = tuple(shape)
    return pl.BlockSpec((None,) + shape, lambda *_: (layer,) + (0,) * len(shape), pipeline_mode=pl.Buffered(1))


def _mod_spec(layer, tm, seq, per_seq):
    def index(i):
        return (layer, 1 + (i * tm) // seq if per_seq else 0, 0, 0)
    return pl.BlockSpec((None, None, 6, D_MODEL), index)


def _gmlp_tile(zb, vn, ws_ref, bias):
    tm = zb.shape[0]
    v = _rms(zb[:, D_B:], vn).astype(BF16)
    gmasks = [_lane_group_mask(D_B, D_B // G_B, g) for g in range(G_B)]
    out = []
    for ch in range(tm // CHUNK_B):
        vc = v[ch * CHUNK_B:(ch + 1) * CHUNK_B, :]
        mixed = bias
        for g in range(G_B):
            mixed = mixed + jnp.where(gmasks[g], _dot(ws_ref[g], vc), 0.0)
        out.append(zb[ch * CHUNK_B:(ch + 1) * CHUNK_B, :D_B] * mixed)
    return jnp.concatenate(out, axis=0)


def _in_kernel(*refs, rotary):
    if rotary:
        x_ref, mod_ref, g_ref, w_ref, vn_ref, ws_ref, bias_ref, tc_ref, ts1_ref, ts2_ref = refs[:10]
    else:
        x_ref, mod_ref, g_ref, w_ref, vn_ref, ws_ref, bias_ref = refs[:7]
    za_ref, yb_ref, zc_ref, zg_ref, dq_ref, dk_ref, dv_ref = refs[-7:]
    x = x_ref[...]
    h = _rms(x, g_ref[0:1, :]) * (1.0 + mod_ref[1:2, :]) + mod_ref[0:1, :]
    hb = h.astype(BF16)
    za_ref[...] = _dot(hb, w_ref[:, Z_OFF_A:Z_OFF_B])
    zc_ref[...] = _dot(hb, w_ref[:, Z_OFF_C:Z_OFF_G])
    zg_ref[...] = _dot(hb, w_ref[:, Z_OFF_G:Z_OFF_D])
    for j, d_ref in enumerate((dq_ref, dk_ref, dv_ref)):
        d = _dot(hb, w_ref[:, Z_OFF_D + j * D_D:Z_OFF_D + (j + 1) * D_D])
        if rotary and j < 2:
            tc, ts1, ts2 = tc_ref[...], ts1_ref[...], ts2_ref[...]
            d = jnp.concatenate([_rope128(d[:, c * LANES:(c + 1) * LANES], tc, ts1, ts2) for c in range(D_D // LANES)],
                                axis=1)
        d_ref[...] = d
    yb_ref[...] = _gmlp_tile(_dot(hb, w_ref[:, Z_OFF_B:Z_OFF_C]), vn_ref[...], ws_ref, bias_ref[...])


def _in_proj(x, mod, g, w, vn, ws, bias, layer, seq, per_seq, rope=None):
    t = x.shape[0]
    tm = TOKEN_TILE
    widths = (ZA_W, MIX_W, 4 * D_C, ZG_W, D_D, D_D, D_D)
    in_specs = [
        pl.BlockSpec((tm, D_MODEL), lambda i: (i, 0)),
        _mod_spec(layer, tm, seq, per_seq),
        _layer_block((4, D_MODEL), layer),
        _layer_block((D_MODEL, Z_W), layer),
        _layer_block((1, D_B), layer),
        _layer_block((G_B, CHUNK_B, CHUNK_B), layer),
        _layer_block((CHUNK_B, D_B), layer),
    ]
    args = [x, mod, g, w, vn, ws, bias]
    if rope is not None:
        in_specs += [pl.BlockSpec((tm, LANES), lambda i: (i % (seq // tm), 0))] * 3
        args += list(rope)
    return pl.pallas_call(
        functools.partial(_in_kernel, rotary=rope is not None),
        grid=(t // tm,),
        in_specs=in_specs,
        out_specs=[pl.BlockSpec((tm, w_), lambda i: (i, 0)) for w_ in widths],
        out_shape=[jax.ShapeDtypeStruct((t, w_), F32) for w_ in widths],
        compiler_params=pltpu.CompilerParams(vmem_limit_bytes=VMEM_LIMIT),
        name="in_proj",
    )(*args)


LOG2E = math.log2(math.e)
ATTN_GROUP_TOKENS = 1024


def _with_ones(v, lo_mask, keep_lo):
    lane = lax.broadcasted_iota(jnp.int32, (1, LANES), 1)
    if keep_lo:
        return jnp.where(lane == 64, 1.0, jnp.where(lo_mask, v, 0.0))
    return jnp.where(lane == 0, 1.0, jnp.where(lo_mask, 0.0, v))


def _pair_normalise(o_even, o_odd, lo_mask):
    return jnp.where(lo_mask, o_even / o_even[:, 64:65], o_odd / o_odd[:, 0:1])


def _rope128(x, tc, ts1, ts2):
    return x * tc + pltpu.roll(x, LANES - 16, 1) * ts1 + pltpu.roll(x, 16, 1) * ts2


def _mla_kernel(*refs, seq, ng, has_ctx, tq):
    rows_all = ng * seq
    if has_ctx:
        (za_ref, qn_ref, kvn_ref, wuq_ref, wukv_ref, tc_ref, ts1_ref, ts2_ref, cckv_ref, ckr_ref,
         ya_ref, q_s, k_s, v_s, ckv_s, kr_s) = refs
        past = cckv_ref.shape[0]
    else:
        (za_ref, qn_ref, kvn_ref, wuq_ref, wukv_ref, ya_ref, ckv_ref, krt_ref, q_s, k_s, v_s, ckv_s, kr_s) = refs
        past = 0
    lk = past + seq
    za = za_ref[...]
    cq = _rms(za[:, :Q_LORA], qn_ref[...])
    qh = _dot(cq.astype(BF16), wuq_ref[:, 0:(2 if has_ctx else 1) * H_A * LANES])
    ckv = _rms(za[:, Q_LORA:Q_LORA + KV_LORA], kvn_ref[...])
    kr = za[:, Q_LORA + KV_LORA:ZA_W]
    if has_ctx:
        tc, ts1, ts2 = tc_ref[...], ts1_ref[...], ts2_ref[...]
        kr = _rope128(kr, tc, ts1, ts2)
        ckv_s[0:past, :] = cckv_ref[...]
        krt = jnp.concatenate([jnp.zeros((64, past), F32), ckr_ref[...], jnp.zeros((LANES - 64 - ROPE_A, past), F32)],
                              axis=0)
        kr_s[0:past, :] = krt.T
    else:
        ckv_ref[...] = ckv
        for g in range(ng):
            krt_ref[g] = kr[g * seq:(g + 1) * seq, :].T[64:64 + ROPE_A, :]
    ckv_s[past:past + rows_all, :] = ckv
    kr_s[past:past + rows_all, :] = kr
    kv = _dot(ckv_s[...].astype(BF16), wukv_ref[...])
    kr_all = kr_s[...]
    lo = _lane_group_mask(LANES, 64, 0)
    for h in range(H_A):
        qg = qh[:, h * LANES:(h + 1) * LANES]
        if has_ctx:
            qg = qg * tc + qh[:, (H_A + h) * LANES:(H_A + h + 1) * LANES] * (ts1 + ts2)
        q_s[h] = (qg * (MLA_SCALE * LOG2E)).astype(BF16)
        k_s[h] = (kv[:, h * LANES:(h + 1) * LANES] + kr_all).astype(BF16)
        v_s[h] = _with_ones(kv[:, (H_A + h) * LANES:(H_A + h + 1) * LANES], lo, h % 2 == 0).astype(BF16)

    def body(i, carry):
        rows = pl.ds(pl.multiple_of(i * tq, tq), tq)
        keys = pl.ds(pl.multiple_of(((i * tq) // seq) * seq, seq), lk)
        s = [_dot_nt(q_s[h, rows, :], k_s[h, keys, :]) for h in range(H_A)]
        o = [_dot(jnp.exp2(s[h] - jnp.max(s[h], axis=-1, keepdims=True)).astype(BF16), v_s[h, keys, :])
             for h in range(H_A)]
        for pair in range(H_A // 2):
            ya_ref[rows, pair * LANES:(pair + 1) * LANES] = _pair_normalise(o[2 * pair], o[2 * pair + 1], lo)
        return carry

    lax.fori_loop(0, rows_all // tq, body, 0, unroll=not has_ctx)


def _ctx_block(rows, cols, layer):
    return pl.BlockSpec((None, None, rows, cols), lambda b: (b, layer, 0, 0))


def _mla(za, qn, kvn, wuq, wukv, layer, batch, seq, rope=None, ctx=None):
    has_ctx = ctx is not None
    past = ctx[0].shape[2] if has_ctx else 0
    ng = 1 if has_ctx else ATTN_GROUP_TOKENS // seq
    rows = ng * seq
    keys = past + rows
    tq = min(seq, 256)
    full = lambda shape: pl.BlockSpec(shape, lambda b: (0,) * len(shape))
    in_specs = [
        pl.BlockSpec((rows, ZA_W), lambda b: (b, 0)),
        _layer_block((1, Q_LORA), layer), _layer_block((1, KV_LORA), layer),
        _layer_block((Q_LORA, 2 * H_A * LANES), layer), _layer_block((KV_LORA, 2 * H_A * LANES), layer),
    ]
    args = [za, qn, kvn, wuq, wukv]
    out_specs = [pl.BlockSpec((rows, MIX_W), lambda b: (b, 0))]
    out_shape = [jax.ShapeDtypeStruct((batch * seq, MIX_W), F32)]
    if has_ctx:
        in_specs += [full((seq, LANES))] * 3
        in_specs += [_ctx_block(past, KV_LORA, layer), _ctx_block(ROPE_A, past, layer)]
        args += list(rope) + list(ctx)
    else:
        out_specs += [pl.BlockSpec((rows, KV_LORA), lambda b: (b, 0)),
                      pl.BlockSpec((ng, ROPE_A, seq), lambda b: (b, 0, 0))]
        out_shape += [jax.ShapeDtypeStruct((batch * seq, KV_LORA), F32),
                      jax.ShapeDtypeStruct((batch, ROPE_A, seq), F32)]
    return pl.pallas_call(
        functools.partial(_mla_kernel, seq=seq, ng=ng, has_ctx=has_ctx, tq=tq),
        grid=(batch // ng,),
        in_specs=in_specs,
        out_specs=out_specs,
        out_shape=out_shape,
        scratch_shapes=[
            pltpu.VMEM((H_A, rows, LANES), BF16),
            pltpu.VMEM((H_A, keys, LANES), BF16),
            pltpu.VMEM((H_A, keys, LANES), BF16),
            pltpu.VMEM((keys, LANES), F32),
            pltpu.VMEM((keys, LANES), F32),
        ],
        compiler_params=pltpu.CompilerParams(vmem_limit_bytes=VMEM_LIMIT),
        name="mla",
    )(*args)


def _diff_kernel(*refs, seq, ng, has_ctx, n_prev, tq, lam_init):
    rows_all = ng * seq
    if has_ctx:
        (dq_ref, dk_ref, dv_ref, lam_ref, sn_ref, ck_ref, cv_ref, yd_ref, q_s, k_s, v_s) = refs
        past = ck_ref.shape[1]
    else:
        dq_ref, dk_ref, dv_ref, lam_ref, sn_ref = refs[:5]
        yd_ref, dkt_ref, dvt_ref, q_s, k_s, v_s = refs[-6:]
        past = 0
        for g in range(ng):
            if n_prev:
                dkt_ref[g, 0:n_prev] = refs[5][g]
                dvt_ref[g, 0:n_prev] = refs[6][g]
            dkt_ref[g, n_prev] = dk_ref[g * seq:(g + 1) * seq, :].T
            dvt_ref[g, n_prev] = dv_ref[g * seq:(g + 1) * seq, :].T
    lk = past + seq
    lo = _lane_group_mask(LANES, 64, 0)
    if has_ctx:
        k_s[0:past, :] = ck_ref[...].T.astype(BF16)
        cv = cv_ref[...].T
    for half in range(2):
        cols = slice(half * LANES, (half + 1) * LANES)
        q, k, v = dq_ref[:, cols], dk_ref[:, cols], dv_ref[:, cols]
        if has_ctx:
            v_s[2 * half, 0:past, :] = _with_ones(cv[:, cols], lo, True).astype(BF16)
            v_s[2 * half + 1, 0:past, :] = _with_ones(cv[:, cols], lo, False).astype(BF16)
        q_s[:, cols] = (q * (DIFF_SCALE * LOG2E)).astype(BF16)
        k_s[past:past + rows_all, cols] = k.astype(BF16)
        v_s[2 * half, past:past + rows_all, :] = _with_ones(v, lo, True).astype(BF16)
        v_s[2 * half + 1, past:past + rows_all, :] = _with_ones(v, lo, False).astype(BF16)

    lam = lam_ref[...]
    lam_val = (jnp.exp(jnp.sum(lam[0:1] * lam[1:2], axis=-1, keepdims=True))
               - jnp.exp(jnp.sum(lam[2:3] * lam[3:4], axis=-1, keepdims=True)) + lam_init)
    comp_masks = [jnp.where(_lane_group_mask(D_D, DH_D, j), 1.0, 0.0).astype(BF16) for j in range(2 * H_D)]
    sn = sn_ref[...]

    def body(i, carry):
        rows = pl.ds(pl.multiple_of(i * tq, tq), tq)
        qb = q_s[rows, :]
        keys = pl.ds(pl.multiple_of(((i * tq) // seq) * seq, seq), lk)
        k_all = k_s[keys, :]
        score = lambda h: [_dot_nt(qb * comp_masks[2 * h + comp], k_all) for comp in range(2)]
        s_next = score(0)
        o = []
        for h in range(H_D):
            s = s_next
            if h + 1 < H_D:
                s_next = score(h + 1)
            o1, o2 = [_dot(jnp.exp2(sc - jnp.max(sc, axis=-1, keepdims=True)).astype(BF16), v_s[h, keys, :])
                      for sc in s]
            ll = 64 if h % 2 == 0 else 0
            o.append(o1 / o1[:, ll:ll + 1] - lam_val * (o2 / o2[:, ll:ll + 1]))
        for pair in range(H_D // 2):
            acc = jnp.where(lo, o[2 * pair], o[2 * pair + 1])
            sq = acc * acc
            ss_lo = jnp.sum(jnp.where(lo, sq, 0.0), axis=-1, keepdims=True)
            ss_hi = jnp.sum(jnp.where(lo, 0.0, sq), axis=-1, keepdims=True)
            r = jnp.where(lo, lax.rsqrt(ss_lo / (2 * DH_D) + EPS), lax.rsqrt(ss_hi / (2 * DH_D) + EPS))
            yd_ref[rows, pair * LANES:(pair + 1) * LANES] = acc * r * sn * (1.0 - lam_init)
        return carry

    lax.fori_loop(0, rows_all // tq, body, 0, unroll=not has_ctx)


def _diff(dq, dk, dv, lam, sn, layer, batch, seq, lam_init, ctx=None, prev=None):
    has_ctx = ctx is not None
    n_prev = prev[0].shape[1] if prev is not None else 0
    past = ctx[0].shape[3] if has_ctx else 0
    ng = 1
    rows = ng * seq
    keys = past + rows
    tq = min(seq, 256)
    in_specs = [pl.BlockSpec((rows, D_D), lambda b: (b, 0))] * 3
    in_specs += [_layer_block((4, DH_D), layer), _layer_block((1, LANES), layer)]
    args = [dq, dk, dv, lam, sn]
    if has_ctx:
        in_specs += [_ctx_block(D_D, past, layer)] * 2
        args += list(ctx)
    if n_prev:
        in_specs += [pl.BlockSpec((ng, n_prev, D_D, seq), lambda b: (b, 0, 0, 0))] * 2
        args += list(prev)
    out_specs = [pl.BlockSpec((rows, MIX_W), lambda b: (b, 0))]
    out_shape = [jax.ShapeDtypeStruct((batch * seq, MIX_W), F32)]
    if not has_ctx:
        out_specs += [pl.BlockSpec((ng, n_prev + 1, D_D, seq), lambda b: (b, 0, 0, 0))] * 2
        out_shape += [jax.ShapeDtypeStruct((batch, n_prev + 1, D_D, seq), F32)] * 2
    return pl.pallas_call(
        functools.partial(_diff_kernel, seq=seq, ng=ng, has_ctx=has_ctx, n_prev=n_prev, tq=tq, lam_init=lam_init),
        grid=(batch // ng,),
        in_specs=in_specs,
        out_specs=out_specs,
        out_shape=out_shape,
        scratch_shapes=[
            pltpu.VMEM((rows, D_D), BF16),
            pltpu.VMEM((keys, D_D), BF16),
            pltpu.VMEM((H_D, keys, LANES), BF16),
        ],
        compiler_params=pltpu.CompilerParams(vmem_limit_bytes=VMEM_LIMIT),
        name="diff_attn",
    )(*args)


def _split3(x):
    hi = x.astype(BF16)
    r1 = x - hi.astype(F32)
    mid = r1.astype(BF16)
    lo = (r1 - mid.astype(F32)).astype(BF16)
    return hi, mid, lo


ST_ROWS = DH_C + 8
N_UNITS = 2 * H_C
MLSTM_GROUP_TOKENS = 1024


def _mlstm_kernel(*refs, seq, ng, has_state):
    ck = MLSTM_CHUNK
    nc = seq // ck
    if has_state:
        (zc_ref, zg_ref, gb_ref, hn_ref, c0_ref, n0_ref, m0_ref,
         yc_ref, k_s, qt_s, kt_s, vt_s, rcc_s, ca_s, wi_s, en_s, ws_s, wo_s, st_s, ht_s) = refs
    else:
        (zc_ref, zg_ref, gb_ref, hn_ref,
         yc_ref, cout_ref, nout_ref, mout_ref,
         k_s, qt_s, kt_s, vt_s, rcc_s, ca_s, wi_s, en_s, ws_s, wo_s, st_s, ht_s) = refs
    eye = jnp.where(lax.broadcasted_iota(jnp.int32, (DH_C, DH_C), 0) == lax.broadcasted_iota(jnp.int32, (DH_C, DH_C), 1),
                    1.0, 0.0).astype(BF16)

    def transpose_exact(a):
        return sum(_dot_nt(eye, part) for part in _split3(a))

    def chunk_rows(g, c):
        return slice(g * seq + c * ck, g * seq + (c + 1) * ck)

    for g in range(ng):
        for c in range(nc):
            rows = chunk_rows(g, c)
            qt_s[g, c] = zc_ref[rows, 0:D_C].T.astype(BF16)
            kk = zc_ref[rows, D_C:2 * D_C] * (DH_C ** -0.5)
            k_s[rows, :] = kk.astype(BF16)
            kt_s[g, c] = kk.T.astype(BF16)
            vt_s[g, c] = zc_ref[rows, 2 * D_C:3 * D_C].T.astype(BF16)

    gates = zg_ref[...] + gb_ref[...]
    r_i = lax.broadcasted_iota(jnp.int32, (ck, ck), 0)
    c_i = lax.broadcasted_iota(jnp.int32, (ck, ck), 1)
    sum_fw = jnp.where(r_i <= c_i, 1.0, 0.0).astype(BF16)
    sum_bw = jnp.where(r_i >= c_i, 1.0, 0.0).astype(BF16)
    row8 = lax.broadcasted_iota(jnp.int32, (N_UNITS, ck), 0)
    fw8 = row8 < H_C
    fw81 = fw8[:, 0:1]
    b_steps, r_steps = [], []
    for g in range(ng):
        i_rows, b_rows = [], []
        for c in range(nc):
            rows = chunk_rows(g, c)
            g_t = gates[rows, :].T
            i8 = g_t[0:N_UNITS, :]
            p0, p1, p2 = _split3(_log_sigmoid(g_t[N_UNITS:2 * N_UNITS, :]))
            b_fw = _dot(p0, sum_fw) + _dot(p1, sum_fw) + _dot(p2, sum_fw)
            b_bw = _dot(p0, sum_bw) + _dot(p1, sum_bw) + _dot(p2, sum_bw)
            b8 = jnp.where(fw8, b_fw, b_bw)
            rcc_s[rows, :] = jnp.concatenate([i8 - b8, jnp.zeros((ck - N_UNITS, ck), F32)], axis=0).T
            i_rows.append(i8)
            b_rows.append(b8)
        b_steps.append([jnp.where(fw8, b_rows[i], b_rows[nc - 1 - i]) for i in range(nc)])
        r_steps.append([jnp.where(fw8, i_rows[i], i_rows[nc - 1 - i]) - b_steps[g][i] for i in range(nc)])

    pm = sm = jnp.concatenate([r for g in range(ng) for r in r_steps[g]], axis=0)
    lane_all = lax.broadcasted_iota(jnp.int32, pm.shape, 1)
    sh = 1
    while sh < ck:
        pm = jnp.maximum(pm, jnp.where(lane_all >= sh, pltpu.roll(pm, sh, 1), NEG_INF))
        sm = jnp.maximum(sm, jnp.where(lane_all < ck - sh, pltpu.roll(sm, ck - sh, 1), NEG_INF))
        sh *= 2
    m_last = []
    for g in range(ng):
        m_prev = m0_ref[g, :, 0:1] if has_state else jnp.zeros((N_UNITS, 1), F32)
        for i in range(nc):
            b8, r8 = b_steps[g][i], r_steps[g][i]
            at = (g * nc + i) * N_UNITS
            pm8, sm8 = pm[at:at + N_UNITS, :], sm[at:at + N_UNITS, :]
            cm = jnp.where(fw8, pm8, sm8)
            b_t = jnp.where(fw81, b8[:, ck - 1:ck], b8[:, 0:1])
            cm_end = jnp.where(fw81, pm8[:, ck - 1:ck], sm8[:, 0:1])
            inter = b8 + m_prev
            mt = jnp.maximum(inter, b8 + cm)
            ca_s[g, i] = b8 - mt
            wi_s[g, i] = jnp.exp(inter - mt)
            en_s[g, i] = jnp.exp(-mt)
            m_new = b_t + jnp.maximum(m_prev, cm_end)
            ws_s[g, i] = jnp.exp(b_t + r8 - m_new)
            wo_s[g, i] = jnp.broadcast_to(jnp.exp(b_t + m_prev - m_new), (N_UNITS, ck))
            m_prev = m_new
        m_last.append(m_prev)

    if has_state:
        for g in range(ng):
            for u in range(N_UNITS):
                st_s[g * N_UNITS + u, 0:DH_C, :] = transpose_exact(c0_ref[g, u])
                st_s[g * N_UNITS + u, DH_C:ST_ROWS, :] = jnp.broadcast_to(n0_ref[g, u:u + 1, :],
                                                                         (ST_ROWS - DH_C, DH_C))
    else:
        st_s[...] = jnp.zeros_like(st_s)

    hmask_bf = [jnp.where(_lane_group_mask(D_C, DH_C, h), 1.0, 0.0).astype(BF16) for h in range(H_C)]
    src_ok = (r_i <= c_i, r_i >= c_i)
    hs = [slice(h * DH_C, (h + 1) * DH_C) for h in range(H_C)]
    units = [(g, d, h) for g in range(ng) for d in range(2) for h in range(H_C)]

    def body(i, carry):
        chunks = (i, nc - 1 - i)
        ca, wi, en, ws, wo = [[ref[g, i] for g in range(ng)] for ref in (ca_s, wi_s, en_s, ws_s, wo_s)]
        q_t = [[qt_s[g, c] for c in chunks] for g in range(ng)]
        k_t = [[kt_s[g, c] for c in chunks] for g in range(ng)]
        v_t = [[vt_s[g, c] for c in chunks] for g in range(ng)]
        rows = [[pl.ds(pl.multiple_of(g * seq + c * ck, ck), ck) for c in chunks] for g in range(ng)]
        sts, qcts, states, upds, pts = [], [], [], [], []
        for g, d, h in units:
            sts.append(_dot(k_s[rows[g][d], :] * hmask_bf[h], q_t[g][d]))
            states.append(st_s[g * N_UNITS + d * H_C + h])
            qcts.append(_dot(states[-1].astype(BF16), q_t[g][d][hs[h], :]))
        for g, d, h in units:
            u = d * H_C + h
            w_s = ws[g][u:u + 1, :]
            vw = jnp.concatenate([v_t[g][d][hs[h], :].astype(F32) * w_s, jnp.broadcast_to(w_s, (8, ck))], axis=0)
            upds.append(_dot_nt(vw.astype(BF16), k_t[g][d][hs[h], :]))
        for j, (g, d, h) in enumerate(units):
            u = d * H_C + h
            decay = jnp.exp(rcc_s[rows[g][d], u:u + 1] + ca[g][u:u + 1, :])
            pts.append(sts[j] * jnp.where(src_ok[d], decay, 0.0))
        nums = [_dot(v_t[g][d][hs[h], :], pts[j].astype(BF16)) for j, (g, d, h) in enumerate(units)]
        for j, (g, d, h) in enumerate(units):
            u = d * H_C + h
            w_i = wi[g][u:u + 1, :]
            den = w_i * qcts[j][DH_C:DH_C + 1, :] + jnp.sum(pts[j], axis=0, keepdims=True)
            num = w_i * qcts[j][0:DH_C, :] + nums[j]
            ht_s[g, d, chunks[d], hs[h], :] = num / jnp.maximum(jnp.abs(den), en[g][u:u + 1, :])
            st_s[g * N_UNITS + u] = wo[g][u:u + 1, 0:DH_C] * states[j] + upds[j]
        return carry

    lax.fori_loop(0, nc, body, 0)

    hn = hn_ref[...]
    for g in range(ng):
        for c in range(nc):
            rows = chunk_rows(g, c)
            h_t = ht_s[g, 0, c] + ht_s[g, 1, c]
            parts = []
            for h in range(H_C):
                x = h_t[hs[h], :]
                parts.append(x * lax.rsqrt(jnp.mean(x * x, axis=0, keepdims=True) + EPS))
            h_n = jnp.concatenate(parts, axis=0).T
            yc_ref[rows, :] = _sigmoid(zc_ref[rows, 3 * D_C:4 * D_C]) * (h_n * hn)

    if not has_state:
        for g in range(ng):
            for u in range(N_UNITS):
                cout_ref[g, u] = transpose_exact(st_s[g * N_UNITS + u, 0:DH_C, :])
                nout_ref[g, u:u + 1, :] = st_s[g * N_UNITS + u, DH_C:DH_C + 1, :]
            mout_ref[g] = jnp.broadcast_to(m_last[g], (N_UNITS, LANES))


def _mlstm(zc, zg, gb, hn, layer, batch, seq, state=None):
    has_state = state is not None
    nc = seq // MLSTM_CHUNK
    ng = max(2, MLSTM_GROUP_TOKENS // seq)
    in_specs = [pl.BlockSpec((ng * seq, 4 * D_C), lambda b: (b, 0)), pl.BlockSpec((ng * seq, ZG_W), lambda b: (b, 0)),
                _layer_block((1, ZG_W), layer), _layer_block((1, D_C), layer)]
    args = [zc, zg, gb, hn]
    out_specs = [pl.BlockSpec((ng * seq, MIX_W), lambda b: (b, 0))]
    out_shape = [jax.ShapeDtypeStruct((batch * seq, MIX_W), F32)]
    if has_state:
        in_specs += [pl.BlockSpec((ng, None, N_UNITS, DH_C, DH_C), lambda b: (b, layer, 0, 0, 0)),
                     pl.BlockSpec((ng, None, N_UNITS, DH_C), lambda b: (b, layer, 0, 0)),
                     pl.BlockSpec((ng, None, N_UNITS, LANES), lambda b: (b, layer, 0, 0))]
        args += list(state)
    else:
        out_specs += [pl.BlockSpec((ng, N_UNITS, DH_C, DH_C), lambda b: (b, 0, 0, 0)),
                      pl.BlockSpec((ng, N_UNITS, DH_C), lambda b: (b, 0, 0)),
                      pl.BlockSpec((ng, N_UNITS, LANES), lambda b: (b, 0, 0))]
        out_shape += [jax.ShapeDtypeStruct((batch, N_UNITS, DH_C, DH_C), F32),
                      jax.ShapeDtypeStruct((batch, N_UNITS, DH_C), F32),
                      jax.ShapeDtypeStruct((batch, N_UNITS, LANES), F32)]
    step = lambda: pltpu.VMEM((ng, nc, N_UNITS, MLSTM_CHUNK), F32)
    transposed = lambda: pltpu.VMEM((ng, nc, D_C, MLSTM_CHUNK), BF16)
    return pl.pallas_call(
        functools.partial(_mlstm_kernel, seq=seq, ng=ng, has_state=has_state),
        grid=(batch // ng,),
        in_specs=in_specs,
        out_specs=out_specs,
        out_shape=out_shape,
        scratch_shapes=[
            pltpu.VMEM((ng * seq, D_C), BF16),
            transposed(), transposed(), transposed(),
            pltpu.VMEM((ng * seq, ZG_W), F32),
            step(), step(), step(), step(), step(),
            pltpu.VMEM((ng * N_UNITS, ST_ROWS, DH_C), F32),
            pltpu.VMEM((ng, 2, nc, D_C, MLSTM_CHUNK), F32),
        ],
        compiler_params=pltpu.CompilerParams(vmem_limit_bytes=VMEM_LIMIT),
        name="mlstm",
    )(*args)


def _merge_kernel(x_ref, mod_ref, g_ref, ya_ref, yb_ref, yc_ref, yd_ref, wm_ref, bm_ref, wb_ref, wo_ref,
                  w1_ref, w2_ref, o_ref):
    x = x_ref[...]
    h = (_rms(x, g_ref[0:1, :]) * (1.0 + mod_ref[1:2, :]) + mod_ref[0:1, :]).astype(BF16)
    acc = jnp.zeros(x.shape, F32)
    for n, y_ref in enumerate((ya_ref, yb_ref, yc_ref, yd_ref)):
        cols = slice(n * D_MODEL, (n + 1) * D_MODEL)
        gate = _sigmoid(_dot(h, wm_ref[:, cols]) + bm_ref[:, cols])
        acc = acc + gate * _dot(y_ref[...].astype(BF16), wb_ref[n])
    y = _dot(acc.astype(BF16), wo_ref[...])
    x = x + mod_ref[2:3, :] * _rms(y, g_ref[1:2, :])
    h2 = (_rms(x, g_ref[2:3, :]) * (1.0 + mod_ref[4:5, :]) + mod_ref[3:4, :]).astype(BF16)
    f = jnp.zeros(x.shape, F32)
    for j in range(D_FF // D_MODEL):
        cols = slice(j * D_MODEL, (j + 1) * D_MODEL)
        a = jnp.maximum(_dot(h2, w1_ref[:, cols]), 0.0)
        f = f + _dot((a * a).astype(BF16), w2_ref[cols, :])
    o_ref[...] = x + mod_ref[5:6, :] * _rms(f, g_ref[3:4, :])


def _merge_ffn(x, mod, g, ys, wm, bm, wb, wo, w1, w2, layer, seq, per_seq):
    t = x.shape[0]
    tm = TOKEN_TILE
    full = lambda shape: _layer_block(shape, layer)
    tok = lambda w_: pl.BlockSpec((tm, w_), lambda i: (i, 0))
    return pl.pallas_call(
        _merge_kernel,
        grid=(t // tm,),
        in_specs=[tok(D_MODEL), _mod_spec(layer, tm, seq, per_seq),
                  full((4, D_MODEL)), tok(MIX_W), tok(MIX_W), tok(MIX_W), tok(MIX_W),
                  full((D_MODEL, N_BRANCH * D_MODEL)), full((1, N_BRANCH * D_MODEL)),
                  full((N_BRANCH, MIX_W, D_MODEL)), full((D_MODEL, D_MODEL)),
                  full((D_MODEL, D_FF)), full((D_FF, D_MODEL))],
        out_specs=tok(D_MODEL),
        out_shape=jax.ShapeDtypeStruct((t, D_MODEL), F32),
        compiler_params=pltpu.CompilerParams(vmem_limit_bytes=VMEM_LIMIT),
        name="merge_ffn",
    )(x, mod, g, *ys, wm, bm, wb, wo, w1, w2)


def _arrange_w_in(w):
    wt = w.T.astype(BF16)
    z = lambda n: jnp.zeros((n, D_MODEL), BF16)
    o_b, o_c = IN_A, IN_A + IN_B
    o_g, o_d = o_c + 4 * D_C, o_c + IN_C
    gates = wt[o_g:o_d].reshape(2, 2, H_C, D_MODEL).transpose(1, 0, 2, 3).reshape(4 * H_C, D_MODEL)
    return jnp.concatenate([
        wt[:Q_LORA + KV_LORA], z(64), wt[Q_LORA + KV_LORA:IN_A], z(32),
        wt[o_b:o_g], gates, z(ZG_W - 4 * H_C), wt[o_d:]], axis=0).T


def _arrange_w_uq(w):
    w = w.reshape(Q_LORA, H_A, NOPE_A + ROPE_A)
    half = ROPE_A // 2
    swapped = jnp.concatenate([jnp.zeros_like(w[:, :, :NOPE_A]), w[:, :, NOPE_A + half:], w[:, :, NOPE_A:NOPE_A + half]],
                              axis=-1)
    pad = lambda a: jnp.pad(a, ((0, 0), (0, 0), (0, LANES - NOPE_A - ROPE_A))).reshape(Q_LORA, H_A * LANES)
    return jnp.concatenate([pad(w), pad(swapped)], axis=1).astype(BF16)


def _arrange_w_ukv(w):
    w = w.reshape(KV_LORA, H_A, NOPE_A + VH_A)
    wk = jnp.pad(w[:, :, :NOPE_A], ((0, 0), (0, 0), (0, LANES - NOPE_A)))
    v = w[:, :, NOPE_A:]
    zero = jnp.zeros_like(v)
    even = jnp.concatenate([v, zero], axis=-1)
    odd = jnp.concatenate([zero, v], axis=-1)
    wv = jnp.where((jnp.arange(H_A) % 2 == 0)[None, :, None], even, odd)
    return jnp.concatenate([wk.reshape(KV_LORA, -1), wv.reshape(KV_LORA, -1)], axis=1).astype(BF16)


def _rope_tables(rows):
    row = np.repeat(np.arange(rows), GRID_W).astype(np.float64)
    col = np.tile(np.arange(GRID_W), rows).astype(np.float64)
    nf = ROPE_A // 4
    inv = np.exp(-math.log(ROPE_BASE) * np.arange(nf, dtype=np.float64) / nf)
    ang = np.concatenate([row[:, None] * inv, col[:, None] * inv], axis=-1)
    cos, sin = np.cos(ang), np.sin(ang)
    n = cos.shape[0]
    ones, zeros = (lambda k: np.ones((n, k))), (lambda k: np.zeros((n, k)))
    mla = (np.concatenate([ones(64), cos, cos, ones(32)], axis=1),
           np.concatenate([zeros(64), -sin, zeros(48)], axis=1),
           np.concatenate([zeros(80), sin, zeros(32)], axis=1))
    dif = (np.tile(np.concatenate([cos, cos], axis=1), (1, 4)),
           np.tile(np.concatenate([-sin, zeros(16)], axis=1), (1, 4)),
           np.tile(np.concatenate([zeros(16), sin], axis=1), (1, 4)))
    as_f32 = lambda ts: tuple(jnp.asarray(t.astype(np.float32)) for t in ts)
    return as_f32(mla), as_f32(dif)


def kernel(x_prompt, x_sample, cache_mla_ckv, cache_mla_krope, cache_diff_k, cache_diff_v, state_mlstm_C,
           state_mlstm_n, state_mlstm_m, c, c_ctx, w_mod, b_mod, norm_g, w_in, mla_q_norm, w_uq, mla_kv_norm,
           w_ukv, gmlp_v_norm, gmlp_w_s, gmlp_b_s, mlstm_gate_bias, mlstm_head_norm, diff_lambda, diff_sub_norm,
           w_branch, w_merge, b_merge, w_out, w_ff1, w_ff2):
    bp, lp, _ = x_prompt.shape
    bs, ls, _ = x_sample.shape
    past = cache_mla_ckv.shape[2]

    cond = jnp.concatenate([c_ctx[None, :], c, jnp.zeros((16 - 1 - bs, D_MODEL), F32)], axis=0)
    mod = _modulation(cond, w_mod, b_mod).reshape(DEPTH, 16, 6, D_MODEL)
    rope_mla, rope_dif = _rope_tables(ls // GRID_W)

    row = lambda a: a[:, None, :]
    w_in_a = jax.vmap(_arrange_w_in)(w_in)
    wuq_a = jax.vmap(_arrange_w_uq)(w_uq)
    wukv_a = jax.vmap(_arrange_w_ukv)(w_ukv)
    qn, kvn, vn, hn, bm = row(mla_q_norm), row(mla_kv_norm), row(gmlp_v_norm), row(mlstm_head_norm), row(b_merge)
    ws = gmlp_w_s.astype(BF16)
    bias = jnp.repeat(jnp.swapaxes(gmlp_b_s, 1, 2), D_B // G_B, axis=2)
    gb = jnp.pad(mlstm_gate_bias.transpose(0, 2, 1, 3).reshape(DEPTH, 1, 4 * H_C),
                 ((0, 0), (0, 0), (0, ZG_W - 4 * H_C)))
    sn = row(jnp.tile(diff_sub_norm, (1, 2)))
    wm, wb, wo = w_merge.astype(BF16), w_branch.astype(BF16), w_out.astype(BF16)
    w1, w2 = w_ff1.astype(BF16), w_ff2.astype(BF16)

    ctx_kr = cache_mla_krope.transpose(0, 1, 3, 2)
    ctx_dk = cache_diff_k.transpose(0, 1, 3, 4, 5, 2).reshape(bs, DEPTH, D_D, past)
    ctx_dv = cache_diff_v.transpose(0, 1, 3, 4, 2).reshape(bs, DEPTH, D_D, past)
    c0 = state_mlstm_C.reshape(bs, DEPTH, N_UNITS, DH_C, DH_C)
    n0 = state_mlstm_n.reshape(bs, DEPTH, N_UNITS, DH_C)
    m0 = jnp.broadcast_to(state_mlstm_m.reshape(bs, DEPTH, N_UNITS, 1), (bs, DEPTH, N_UNITS, LANES))

    xp = x_prompt.reshape(bp * lp, D_MODEL)
    xs = x_sample.reshape(bs * ls, D_MODEL)
    ents = []
    diff_kv = None
    for l in range(DEPTH):
        lam_init = 0.8 - 0.6 * math.exp(-0.3 * l)
        for is_sample in (False, True):
            x = xs if is_sample else xp
            batch, seq = (bs, ls) if is_sample else (bp, lp)
            za, yb, zc, zg, dq, dk, dv = _in_proj(x, mod, norm_g, w_in_a, vn, ws, bias, l, seq, is_sample,
                                                  rope=rope_dif if is_sample else None)
            if is_sample:
                (ya,) = _mla(za, qn, kvn, wuq_a, wukv_a, l, batch, seq, rope=rope_mla, ctx=(cache_mla_ckv, ctx_kr))
                (yd,) = _diff(dq, dk, dv, diff_lambda, sn, l, batch, seq, lam_init, ctx=(ctx_dk, ctx_dv))
                (yc,) = _mlstm(zc, zg, gb, hn, l, batch, seq, state=(c0, n0, m0))
            else:
                ya, ckv, krt = _mla(za, qn, kvn, wuq_a, wukv_a, l, batch, seq)
                yd, *diff_kv = _diff(dq, dk, dv, diff_lambda, sn, l, batch, seq, lam_init, prev=diff_kv)
                yc, c_new, n_new, m_new = _mlstm(zc, zg, gb, hn, l, batch, seq)
                ents.append((
                    ckv.reshape(bp, lp, KV_LORA),
                    krt.transpose(0, 2, 1),
                    c_new.reshape(bp, 2, H_C, DH_C, DH_C),
                    n_new.reshape(bp, 2, H_C, DH_C),
                    m_new[:, :, 0].reshape(bp, 2, H_C)))
            x = _merge_ffn(x, mod, norm_g, (ya, yb, yc, yd), wm, bm, wb, wo, w1, w2, l, seq, is_sample)
            if is_sample:
                xs = x
            else:
                xp = x

    stack = lambda j: jnp.stack([e[j] for e in ents], axis=1)
    dkt, dvt = diff_kv
    new_diff_k = dkt.reshape(bp, DEPTH, H_D, 2, DH_D, lp).transpose(0, 1, 5, 2, 3, 4)
    new_diff_v = dvt.reshape(bp, DEPTH, H_D, 2 * DH_D, lp).transpose(0, 1, 4, 2, 3)
    return (xp.reshape(bp, lp, D_MODEL), xs.reshape(bs, ls, D_MODEL),
            stack(0), stack(1), new_diff_k, new_diff_v, stack(2), stack(3), stack(4))
```

```python
import functools
import math

import jax
import jax.numpy as jnp
import numpy as np
from jax import lax
from jax.experimental import pallas as pl
from jax.experimental.pallas import tpu as pltpu

F32 = jnp.float32
BF16 = jnp.bfloat16

D_MODEL = 1024
DEPTH = 2
GRID_W = 64
N_BRANCH = 4
MIX_W = 256
H_A, NOPE_A, ROPE_A, VH_A = 4, 64, 32, 64
Q_LORA, KV_LORA = 256, 128
D_B, G_B, CHUNK_B = 256, 4, 128
H_C, DH_C = 4, 64
D_C = H_C * DH_C
H_D, DH_D = 4, 32
D_D = H_D * 2 * DH_D
D_FF = 4 * D_MODEL
IN_A = Q_LORA + KV_LORA + ROPE_A
IN_B = 2 * D_B
IN_C = 4 * D_C + 4 * H_C
IN_D = 3 * D_D
ROPE_BASE = 10000.0
EPS = 1e-6
MLA_SCALE = (NOPE_A + ROPE_A) ** -0.5
DIFF_SCALE = DH_D ** -0.5

LANES = 128
MLSTM_CHUNK = 128
VMEM_LIMIT = 56 * 1024 * 1024
NEG_INF = float("-inf")

ZA_W = 512
ZG_W = 128
Z_OFF_A = 0
Z_OFF_B = Z_OFF_A + ZA_W
Z_OFF_C = Z_OFF_B + IN_B
Z_OFF_G = Z_OFF_C + 4 * D_C
Z_OFF_D = Z_OFF_G + ZG_W
Z_W = Z_OFF_D + IN_D


def _rms(x, g):
    return x * lax.rsqrt(jnp.mean(x * x, axis=-1, keepdims=True) + EPS) * g


def _sigmoid(x):
    return 1.0 / (1.0 + jnp.exp(-x))


def _log_sigmoid(x):
    return jnp.minimum(x, 0.0) - jnp.log1p(jnp.exp(-jnp.abs(x)))


def _dot(a, b):
    return jnp.dot(a, b, preferred_element_type=F32)


def _dot_nt(a, b):
    return lax.dot_general(a, b, (((1,), (1,)), ((), ())), preferred_element_type=F32)


def _lane_group_mask(width, group, index):
    lane = lax.broadcasted_iota(jnp.int32, (1, width), 1)
    return (lane >= index * group) & (lane < (index + 1) * group)


def _mod_kernel(cond_ref, w_ref, b_ref, o_ref):
    c = cond_ref[...]
    s = c * _sigmoid(c)
    o_ref[...] = _dot(s.astype(BF16), w_ref[...].astype(BF16)) + b_ref[...]


def _modulation(cond, w_mod, b_mod):
    rows = cond.shape[0]
    nb = 2048
    return pl.pallas_call(
        _mod_kernel,
        grid=(DEPTH, 6 * D_MODEL // nb),
        in_specs=[
            pl.BlockSpec((rows, D_MODEL), lambda l, j: (0, 0)),
            pl.BlockSpec((None, D_MODEL, nb), lambda l, j: (l, 0, j)),
            pl.BlockSpec((None, 1, nb), lambda l, j: (l, 0, j)),
        ],
        out_specs=pl.BlockSpec((None, rows, nb), lambda l, j: (l, 0, j)),
        out_shape=jax.ShapeDtypeStruct((DEPTH, rows, 6 * D_MODEL), F32),
        name="modulation",
    )(cond, w_mod, b_mod.reshape(DEPTH, 1, 6 * D_MODEL))


TOKEN_TILE = 512
IN_PROJ_TILE = 1024


def _layer_block(shape, layer):
    shape = tuple(shape)
    return pl.BlockSpec((None,) + shape, lambda *_: (layer,) + (0,) * len(shape), pipeline_mode=pl.Buffered(1))


def _mod_spec(layer, tm, seq, per_seq):
    def index(i):
        return (layer, 1 + (i * tm) // seq if per_seq else 0, 0, 0)
    return pl.BlockSpec((None, None, 6, D_MODEL), index)


def _gmlp_tile(zb, vn, ws_ref, bias):
    tm = zb.shape[0]
    v = _rms(zb[:, D_B:], vn).astype(BF16)
    gmasks = [_lane_group_mask(D_B, D_B // G_B, g) for g in range(G_B)]
    out = []
    for ch in range(tm // CHUNK_B):
        vc = v[ch * CHUNK_B:(ch + 1) * CHUNK_B, :]
        mixed = bias
        for g in range(G_B):
            mixed = mixed + jnp.where(gmasks[g], _dot(ws_ref[g], vc), 0.0)
        out.append(zb[ch * CHUNK_B:(ch + 1) * CHUNK_B, :D_B] * mixed)
    return jnp.concatenate(out, axis=0)


def _in_kernel(*refs, rotary):
    if rotary:
        x_ref, mod_ref, g_ref, w_ref, vn_ref, ws_ref, bias_ref, tc_ref, ts1_ref, ts2_ref = refs[:10]
    else:
        x_ref, mod_ref, g_ref, w_ref, vn_ref, ws_ref, bias_ref = refs[:7]
    za_ref, yb_ref, zc_ref, zg_ref, dq_ref, dk_ref, dv_ref = refs[-7:]
    x = x_ref[...]
    h = _rms(x, g_ref[0:1, :]) * (1.0 + mod_ref[1:2, :]) + mod_ref[0:1, :]
    hb = h.astype(BF16)
    za_ref[...] = _dot(hb, w_ref[:, Z_OFF_A:Z_OFF_B])
    zc_ref[...] = _dot(hb, w_ref[:, Z_OFF_C:Z_OFF_G])
    zg_ref[...] = _dot(hb, w_ref[:, Z_OFF_G:Z_OFF_D])
    for j, d_ref in enumerate((dq_ref, dk_ref, dv_ref)):
        d = _dot(hb, w_ref[:, Z_OFF_D + j * D_D:Z_OFF_D + (j + 1) * D_D])
        if rotary and j < 2:
            tc, ts1, ts2 = tc_ref[...], ts1_ref[...], ts2_ref[...]
            d = jnp.concatenate([_rope128(d[:, c * LANES:(c + 1) * LANES], tc, ts1, ts2) for c in range(D_D // LANES)],
                                axis=1)
        d_ref[...] = d
    yb_ref[...] = _gmlp_tile(_dot(hb, w_ref[:, Z_OFF_B:Z_OFF_C]), vn_ref[...], ws_ref, bias_ref[...])


def _in_proj(x, mod, g, w, vn, ws, bias, layer, seq, per_seq, rope=None):
    t = x.shape[0]
    tm = IN_PROJ_TILE
    widths = (ZA_W, MIX_W, 4 * D_C, ZG_W, D_D, D_D, D_D)
    in_specs = [
        pl.BlockSpec((tm, D_MODEL), lambda i: (i, 0)),
        _mod_spec(layer, tm, seq, per_seq),
        _layer_block((4, D_MODEL), layer),
        _layer_block((D_MODEL, Z_W), layer),
        _layer_block((1, D_B), layer),
        _layer_block((G_B, CHUNK_B, CHUNK_B), layer),
        _layer_block((CHUNK_B, D_B), layer),
    ]
    args = [x, mod, g, w, vn, ws, bias]
    if rope is not None:
        in_specs += [pl.BlockSpec((tm, LANES), lambda i: (i % (seq // tm), 0))] * 3
        args += list(rope)
    return pl.pallas_call(
        functools.partial(_in_kernel, rotary=rope is not None),
        grid=(t // tm,),
        in_specs=in_specs,
        out_specs=[pl.BlockSpec((tm, w_), lambda i: (i, 0)) for w_ in widths],
        out_shape=[jax.ShapeDtypeStruct((t, w_), F32) for w_ in widths],
        compiler_params=pltpu.CompilerParams(vmem_limit_bytes=VMEM_LIMIT),
        name="in_proj",
    )(*args)


LOG2E = math.log2(math.e)
ATTN_GROUP_TOKENS = 1024


def _with_ones(v, lo_mask, keep_lo):
    lane = lax.broadcasted_iota(jnp.int32, (1, LANES), 1)
    if keep_lo:
        return jnp.where(lane == 64, 1.0, jnp.where(lo_mask, v, 0.0))
    return jnp.where(lane == 0, 1.0, jnp.where(lo_mask, 0.0, v))


def _pair_normalise(o_even, o_odd, lo_mask):
    return jnp.where(lo_mask, o_even / o_even[:, 64:65], o_odd / o_odd[:, 0:1])


def _rope128(x, tc, ts1, ts2):
    return x * tc + pltpu.roll(x, LANES - 16, 1) * ts1 + pltpu.roll(x, 16, 1) * ts2


def _mla_kernel(*refs, seq, ng, has_ctx, tq):
    rows_all = ng * seq
    if has_ctx:
        (za_ref, qn_ref, kvn_ref, wuq_ref, wukv_ref, tc_ref, ts1_ref, ts2_ref, cckv_ref, ckr_ref,
         ya_ref, q_s, k_s, v_s, ckv_s, kr_s) = refs
        past = cckv_ref.shape[0]
    else:
        (za_ref, qn_ref, kvn_ref, wuq_ref, wukv_ref, ya_ref, ckv_ref, krt_ref, q_s, k_s, v_s, ckv_s, kr_s) = refs
        past = 0
    lk = past + seq
    za = za_ref[...]
    cq = _rms(za[:, :Q_LORA], qn_ref[...])
    qh = _dot(cq.astype(BF16), wuq_ref[:, 0:(2 if has_ctx else 1) * H_A * LANES])
    ckv = _rms(za[:, Q_LORA:Q_LORA + KV_LORA], kvn_ref[...])
    kr = za[:, Q_LORA + KV_LORA:ZA_W]
    if has_ctx:
        tc, ts1, ts2 = tc_ref[...], ts1_ref[...], ts2_ref[...]
        kr = _rope128(kr, tc, ts1, ts2)
        ckv_s[0:past, :] = cckv_ref[...]
        krt = jnp.concatenate([jnp.zeros((64, past), F32), ckr_ref[...], jnp.zeros((LANES - 64 - ROPE_A, past), F32)],
                              axis=0)
        kr_s[0:past, :] = krt.T
    else:
        ckv_ref[...] = ckv
        for g in range(ng):
            krt_ref[g] = kr[g * seq:(g + 1) * seq, :].T[64:64 + ROPE_A, :]
    ckv_s[past:past + rows_all, :] = ckv
    kr_s[past:past + rows_all, :] = kr
    kv = _dot(ckv_s[...].astype(BF16), wukv_ref[...])
    kr_all = kr_s[...]
    lo = _lane_group_mask(LANES, 64, 0)
    for h in range(H_A):
        qg = qh[:, h * LANES:(h + 1) * LANES]
        if has_ctx:
            qg = qg * tc + qh[:, (H_A + h) * LANES:(H_A + h + 1) * LANES] * (ts1 + ts2)
        q_s[h] = (qg * (MLA_SCALE * LOG2E)).astype(BF16)
        k_s[h] = (kv[:, h * LANES:(h + 1) * LANES] + kr_all).astype(BF16)
        v_s[h] = _with_ones(kv[:, (H_A + h) * LANES:(H_A + h + 1) * LANES], lo, h % 2 == 0).astype(BF16)

    def body(i, carry):
        rows = pl.ds(pl.multiple_of(i * tq, tq), tq)
        keys = pl.ds(pl.multiple_of(((i * tq) // seq) * seq, seq), lk)
        s = [_dot_nt(q_s[h, rows, :], k_s[h, keys, :]) for h in range(H_A)]
        o = [_dot(jnp.exp2(s[h] - jnp.max(s[h], axis=-1, keepdims=True)).astype(BF16), v_s[h, keys, :])
             for h in range(H_A)]
        for pair in range(H_A // 2):
            ya_ref[rows, pair * LANES:(pair + 1) * LANES] = _pair_normalise(o[2 * pair], o[2 * pair + 1], lo)
        return carry

    lax.fori_loop(0, rows_all // tq, body, 0, unroll=not has_ctx)


def _ctx_block(rows, cols, layer):
    return pl.BlockSpec((None, None, rows, cols), lambda b: (b, layer, 0, 0))


def _mla(za, qn, kvn, wuq, wukv, layer, batch, seq, rope=None, ctx=None):
    has_ctx = ctx is not None
    past = ctx[0].shape[2] if has_ctx else 0
    ng = 1 if has_ctx else ATTN_GROUP_TOKENS // seq
    rows = ng * seq
    keys = past + rows
    tq = min(seq, 256)
    full = lambda shape: pl.BlockSpec(shape, lambda b: (0,) * len(shape))
    in_specs = [
        pl.BlockSpec((rows, ZA_W), lambda b: (b, 0)),
        _layer_block((1, Q_LORA), layer), _layer_block((1, KV_LORA), layer),
        _layer_block((Q_LORA, 2 * H_A * LANES), layer), _layer_block((KV_LORA, 2 * H_A * LANES), layer),
    ]
    args = [za, qn, kvn, wuq, wukv]
    out_specs = [pl.BlockSpec((rows, MIX_W), lambda b: (b, 0))]
    out_shape = [jax.ShapeDtypeStruct((batch * seq, MIX_W), F32)]
    if has_ctx:
        in_specs += [full((seq, LANES))] * 3
        in_specs += [_ctx_block(past, KV_LORA, layer), _ctx_block(ROPE_A, past, layer)]
        args += list(rope) + list(ctx)
    else:
        out_specs += [pl.BlockSpec((rows, KV_LORA), lambda b: (b, 0)),
                      pl.BlockSpec((ng, ROPE_A, seq), lambda b: (b, 0, 0))]
        out_shape += [jax.ShapeDtypeStruct((batch * seq, KV_LORA), F32),
                      jax.ShapeDtypeStruct((batch, ROPE_A, seq), F32)]
    return pl.pallas_call(
        functools.partial(_mla_kernel, seq=seq, ng=ng, has_ctx=has_ctx, tq=tq),
        grid=(batch // ng,),
        in_specs=in_specs,
        out_specs=out_specs,
        out_shape=out_shape,
        scratch_shapes=[
            pltpu.VMEM((H_A, rows, LANES), BF16),
            pltpu.VMEM((H_A, keys, LANES), BF16),
            pltpu.VMEM((H_A, keys, LANES), BF16),
            pltpu.VMEM((keys, LANES), F32),
            pltpu.VMEM((keys, LANES), F32),
        ],
        compiler_params=pltpu.CompilerParams(vmem_limit_bytes=VMEM_LIMIT),
        name="mla",
    )(*args)


def _diff_kernel(*refs, seq, ng, has_ctx, n_prev, tq, lam_init):
    rows_all = ng * seq
    if has_ctx:
        (dq_ref, dk_ref, dv_ref, lam_ref, sn_ref, ck_ref, cv_ref, yd_ref, q_s, k_s, v_s) = refs
        past = ck_ref.shape[1]
    else:
        dq_ref, dk_ref, dv_ref, lam_ref, sn_ref = refs[:5]
        yd_ref, dkt_ref, dvt_ref, q_s, k_s, v_s = refs[-6:]
        past = 0
        for g in range(ng):
            if n_prev:
                dkt_ref[g, 0:n_prev] = refs[5][g]
                dvt_ref[g, 0:n_prev] = refs[6][g]
            dkt_ref[g, n_prev] = dk_ref[g * seq:(g + 1) * seq, :].T
            dvt_ref[g, n_prev] = dv_ref[g * seq:(g + 1) * seq, :].T
    lk = past + seq
    lo = _lane_group_mask(LANES, 64, 0)
    if has_ctx:
        k_s[0:past, :] = ck_ref[...].T.astype(BF16)
        cv = cv_ref[...].T
    for half in range(2):
        cols = slice(half * LANES, (half + 1) * LANES)
        q, k, v = dq_ref[:, cols], dk_ref[:, cols], dv_ref[:, cols]
        if has_ctx:
            v_s[2 * half, 0:past, :] = _with_ones(cv[:, cols], lo, True).astype(BF16)
            v_s[2 * half + 1, 0:past, :] = _with_ones(cv[:, cols], lo, False).astype(BF16)
        q_s[:, cols] = (q * (DIFF_SCALE * LOG2E)).astype(BF16)
        k_s[past:past + rows_all, cols] = k.astype(BF16)
        v_s[2 * half, past:past + rows_all, :] = _with_ones(v, lo, True).astype(BF16)
        v_s[2 * half + 1, past:past + rows_all, :] = _with_ones(v, lo, False).astype(BF16)

    lam = lam_ref[...]
    lam_val = (jnp.exp(jnp.sum(lam[0:1] * lam[1:2], axis=-1, keepdims=True))
               - jnp.exp(jnp.sum(lam[2:3] * lam[3:4], axis=-1, keepdims=True)) + lam_init)
    comp_masks = [jnp.where(_lane_group_mask(D_D, DH_D, j), 1.0, 0.0).astype(BF16) for j in range(2 * H_D)]
    sn = sn_ref[...]

    def body(i, carry):
        rows = pl.ds(pl.multiple_of(i * tq, tq), tq)
        qb = q_s[rows, :]
        keys = pl.ds(pl.multiple_of(((i * tq) // seq) * seq, seq), lk)
        k_all = k_s[keys, :]
        score = lambda h: [_dot_nt(qb * comp_masks[2 * h + comp], k_all) for comp in range(2)]
        s_next = score(0)
        o = []
        for h in range(H_D):
            s = s_next
            if h + 1 < H_D:
                s_next = score(h + 1)
            o1, o2 = [_dot(jnp.exp2(sc - jnp.max(sc, axis=-1, keepdims=True)).astype(BF16), v_s[h, keys, :])
                      for sc in s]
            ll = 64 if h % 2 == 0 else 0
            o.append(o1 / o1[:, ll:ll + 1] - lam_val * (o2 / o2[:, ll:ll + 1]))
        for pair in range(H_D // 2):
            acc = jnp.where(lo, o[2 * pair], o[2 * pair + 1])
            sq = acc * acc
            ss_lo = jnp.sum(jnp.where(lo, sq, 0.0), axis=-1, keepdims=True)
            ss_hi = jnp.sum(jnp.where(lo, 0.0, sq), axis=-1, keepdims=True)
            r = jnp.where(lo, lax.rsqrt(ss_lo / (2 * DH_D) + EPS), lax.rsqrt(ss_hi / (2 * DH_D) + EPS))
            yd_ref[rows, pair * LANES:(pair + 1) * LANES] = acc * r * sn * (1.0 - lam_init)
        return carry

    lax.fori_loop(0, rows_all // tq, body, 0, unroll=not has_ctx)


def _diff(dq, dk, dv, lam, sn, layer, batch, seq, lam_init, ctx=None, prev=None):
    has_ctx = ctx is not None
    n_prev = prev[0].shape[1] if prev is not None else 0
    past = ctx[0].shape[3] if has_ctx else 0
    ng = 1
    rows = ng * seq
    keys = past + rows
    tq = min(seq, 256)
    in_specs = [pl.BlockSpec((rows, D_D), lambda b: (b, 0))] * 3
    in_specs += [_layer_block((4, DH_D), layer), _layer_block((1, LANES), layer)]
    args = [dq, dk, dv, lam, sn]
    if has_ctx:
        in_specs += [_ctx_block(D_D, past, layer)] * 2
        args += list(ctx)
    if n_prev:
        in_specs += [pl.BlockSpec((ng, n_prev, D_D, seq), lambda b: (b, 0, 0, 0))] * 2
        args += list(prev)
    out_specs = [pl.BlockSpec((rows, MIX_W), lambda b: (b, 0))]
    out_shape = [jax.ShapeDtypeStruct((batch * seq, MIX_W), F32)]
    if not has_ctx:
        out_specs += [pl.BlockSpec((ng, n_prev + 1, D_D, seq), lambda b: (b, 0, 0, 0))] * 2
        out_shape += [jax.ShapeDtypeStruct((batch, n_prev + 1, D_D, seq), F32)] * 2
    return pl.pallas_call(
        functools.partial(_diff_kernel, seq=seq, ng=ng, has_ctx=has_ctx, n_prev=n_prev, tq=tq, lam_init=lam_init),
        grid=(batch // ng,),
        in_specs=in_specs,
        out_specs=out_specs,
        out_shape=out_shape,
        scratch_shapes=[
            pltpu.VMEM((rows, D_D), BF16),
            pltpu.VMEM((keys, D_D), BF16),
            pltpu.VMEM((H_D, keys, LANES), BF16),
        ],
        compiler_params=pltpu.CompilerParams(vmem_limit_bytes=VMEM_LIMIT),
        name="diff_attn",
    )(*args)


def _split3(x):
    hi = x.astype(BF16)
    r1 = x - hi.astype(F32)
    mid = r1.astype(BF16)
    lo = (r1 - mid.astype(F32)).astype(BF16)
    return hi, mid, lo


ST_ROWS = DH_C + 8
N_UNITS = 2 * H_C
MLSTM_GROUP_TOKENS = 1024


def _mlstm_kernel(*refs, seq, ng, has_state):
    ck = MLSTM_CHUNK
    nc = seq // ck
    if has_state:
        (zc_ref, zg_ref, gb_ref, hn_ref, c0_ref, n0_ref, m0_ref,
         yc_ref, k_s, qt_s, kt_s, vt_s, rcc_s, ca_s, wi_s, en_s, ws_s, wo_s, st_s, ht_s) = refs
    else:
        (zc_ref, zg_ref, gb_ref, hn_ref,
         yc_ref, cout_ref, nout_ref, mout_ref,
         k_s, qt_s, kt_s, vt_s, rcc_s, ca_s, wi_s, en_s, ws_s, wo_s, st_s, ht_s) = refs
    eye = jnp.where(lax.broadcasted_iota(jnp.int32, (DH_C, DH_C), 0) == lax.broadcasted_iota(jnp.int32, (DH_C, DH_C), 1),
                    1.0, 0.0).astype(BF16)

    def transpose_exact(a):
        return sum(_dot_nt(eye, part) for part in _split3(a))

    def chunk_rows(g, c):
        return slice(g * seq + c * ck, g * seq + (c + 1) * ck)

    for g in range(ng):
        for c in range(nc):
            rows = chunk_rows(g, c)
            qt_s[g, c] = zc_ref[rows, 0:D_C].T.astype(BF16)
            kk = zc_ref[rows, D_C:2 * D_C] * (DH_C ** -0.5)
            k_s[rows, :] = kk.astype(BF16)
            kt_s[g, c] = kk.T.astype(BF16)
            vt_s[g, c] = zc_ref[rows, 2 * D_C:3 * D_C].T.astype(BF16)

    gates = zg_ref[...] + gb_ref[...]
    r_i = lax.broadcasted_iota(jnp.int32, (ck, ck), 0)
    c_i = lax.broadcasted_iota(jnp.int32, (ck, ck), 1)
    sum_fw = jnp.where(r_i <= c_i, 1.0, 0.0).astype(BF16)
    sum_bw = jnp.where(r_i >= c_i, 1.0, 0.0).astype(BF16)
    row8 = lax.broadcasted_iota(jnp.int32, (N_UNITS, ck), 0)
    fw8 = row8 < H_C
    fw81 = fw8[:, 0:1]
    b_steps, r_steps = [], []
    for g in range(ng):
        i_rows, b_rows = [], []
        for c in range(nc):
            rows = chunk_rows(g, c)
            g_t = gates[rows, :].T
            i8 = g_t[0:N_UNITS, :]
            p0, p1, p2 = _split3(_log_sigmoid(g_t[N_UNITS:2 * N_UNITS, :]))
            b_fw = _dot(p0, sum_fw) + _dot(p1, sum_fw) + _dot(p2, sum_fw)
            b_bw = _dot(p0, sum_bw) + _dot(p1, sum_bw) + _dot(p2, sum_bw)
            b8 = jnp.where(fw8, b_fw, b_bw)
            rcc_s[rows, :] = jnp.concatenate([i8 - b8, jnp.zeros((ck - N_UNITS, ck), F32)], axis=0).T
            i_rows.append(i8)
            b_rows.append(b8)
        b_steps.append([jnp.where(fw8, b_rows[i], b_rows[nc - 1 - i]) for i in range(nc)])
        r_steps.append([jnp.where(fw8, i_rows[i], i_rows[nc - 1 - i]) - b_steps[g][i] for i in range(nc)])

    pm = sm = jnp.concatenate([r for g in range(ng) for r in r_steps[g]], axis=0)
    lane_all = lax.broadcasted_iota(jnp.int32, pm.shape, 1)
    sh = 1
    while sh < ck:
        pm = jnp.maximum(pm, jnp.where(lane_all >= sh, pltpu.roll(pm, sh, 1), NEG_INF))
        sm = jnp.maximum(sm, jnp.where(lane_all < ck - sh, pltpu.roll(sm, ck - sh, 1), NEG_INF))
        sh *= 2
    m_last = []
    for g in range(ng):
        m_prev = m0_ref[g, :, 0:1] if has_state else jnp.zeros((N_UNITS, 1), F32)
        for i in range(nc):
            b8, r8 = b_steps[g][i], r_steps[g][i]
            at = (g * nc + i) * N_UNITS
            pm8, sm8 = pm[at:at + N_UNITS, :], sm[at:at + N_UNITS, :]
            cm = jnp.where(fw8, pm8, sm8)
            b_t = jnp.where(fw81, b8[:, ck - 1:ck], b8[:, 0:1])
            cm_end = jnp.where(fw81, pm8[:, ck - 1:ck], sm8[:, 0:1])
            inter = b8 + m_prev
            mt = jnp.maximum(inter, b8 + cm)
            ca_s[g, i] = b8 - mt
            wi_s[g, i] = jnp.exp(inter - mt)
            en_s[g, i] = jnp.exp(-mt)
            m_new = b_t + jnp.maximum(m_prev, cm_end)
            ws_s[g, i] = jnp.exp(b_t + r8 - m_new)
            wo_s[g, i] = jnp.broadcast_to(jnp.exp(b_t + m_prev - m_new), (N_UNITS, ck))
            m_prev = m_new
        m_last.append(m_prev)

    if has_state:
        for g in range(ng):
            for u in range(N_UNITS):
                st_s[g * N_UNITS + u, 0:DH_C, :] = transpose_exact(c0_ref[g, u])
                st_s[g * N_UNITS + u, DH_C:ST_ROWS, :] = jnp.broadcast_to(n0_ref[g, u:u + 1, :],
                                                                         (ST_ROWS - DH_C, DH_C))
    else:
        st_s[...] = jnp.zeros_like(st_s)

    hmask_bf = [jnp.where(_lane_group_mask(D_C, DH_C, h), 1.0, 0.0).astype(BF16) for h in range(H_C)]
    src_ok = (r_i <= c_i, r_i >= c_i)
    hs = [slice(h * DH_C, (h + 1) * DH_C) for h in range(H_C)]
    units = [(g, d, h) for g in range(ng) for d in range(2) for h in range(H_C)]

    def body(i, carry):
        chunks = (i, nc - 1 - i)
        ca, wi, en, ws, wo = [[ref[g, i] for g in range(ng)] for ref in (ca_s, wi_s, en_s, ws_s, wo_s)]
        q_t = [[qt_s[g, c] for c in chunks] for g in range(ng)]
        k_t = [[kt_s[g, c] for c in chunks] for g in range(ng)]
        v_t = [[vt_s[g, c] for c in chunks] for g in range(ng)]
        rows = [[pl.ds(pl.multiple_of(g * seq + c * ck, ck), ck) for c in chunks] for g in range(ng)]
        sts, qcts, states, upds, pts = [], [], [], [], []
        for g, d, h in units:
            sts.append(_dot(k_s[rows[g][d], :] * hmask_bf[h], q_t[g][d]))
            states.append(st_s[g * N_UNITS + d * H_C + h])
            qcts.append(_dot(states[-1].astype(BF16), q_t[g][d][hs[h], :]))
        for g, d, h in units:
            u = d * H_C + h
            w_s = ws[g][u:u + 1, :]
            vw = jnp.concatenate([v_t[g][d][hs[h], :].astype(F32) * w_s, jnp.broadcast_to(w_s, (8, ck))], axis=0)
            upds.append(_dot_nt(vw.astype(BF16), k_t[g][d][hs[h], :]))
        for j, (g, d, h) in enumerate(units):
            u = d * H_C + h
            decay = jnp.exp(rcc_s[rows[g][d], u:u + 1] + ca[g][u:u + 1, :])
            pts.append(sts[j] * jnp.where(src_ok[d], decay, 0.0))
        nums = [_dot(v_t[g][d][hs[h], :], pts[j].astype(BF16)) for j, (g, d, h) in enumerate(units)]
        for j, (g, d, h) in enumerate(units):
            u = d * H_C + h
            w_i = wi[g][u:u + 1, :]
            den = w_i * qcts[j][DH_C:DH_C + 1, :] + jnp.sum(pts[j], axis=0, keepdims=True)
            num = w_i * qcts[j][0:DH_C, :] + nums[j]
            ht_s[g, d, chunks[d], hs[h], :] = num / jnp.maximum(jnp.abs(den), en[g][u:u + 1, :])
            st_s[g * N_UNITS + u] = wo[g][u:u + 1, 0:DH_C] * states[j] + upds[j]
        return carry

    lax.fori_loop(0, nc, body, 0)

    hn = hn_ref[...]
    for g in range(ng):
        for c in range(nc):
            rows = chunk_rows(g, c)
            h_t = ht_s[g, 0, c] + ht_s[g, 1, c]
            parts = []
            for h in range(H_C):
                x = h_t[hs[h], :]
                parts.append(x * lax.rsqrt(jnp.mean(x * x, axis=0, keepdims=True) + EPS))
            h_n = jnp.concatenate(parts, axis=0).T
            yc_ref[rows, :] = _sigmoid(zc_ref[rows, 3 * D_C:4 * D_C]) * (h_n * hn)

    if not has_state:
        for g in range(ng):
            for u in range(N_UNITS):
                cout_ref[g, u] = transpose_exact(st_s[g * N_UNITS + u, 0:DH_C, :])
                nout_ref[g, u:u + 1, :] = st_s[g * N_UNITS + u, DH_C:DH_C + 1, :]
            mout_ref[g] = jnp.broadcast_to(m_last[g], (N_UNITS, LANES))


def _mlstm(zc, zg, gb, hn, layer, batch, seq, state=None):
    has_state = state is not None
    nc = seq // MLSTM_CHUNK
    ng = max(2, MLSTM_GROUP_TOKENS // seq)
    in_specs = [pl.BlockSpec((ng * seq, 4 * D_C), lambda b: (b, 0)), pl.BlockSpec((ng * seq, ZG_W), lambda b: (b, 0)),
                _layer_block((1, ZG_W), layer), _layer_block((1, D_C), layer)]
    args = [zc, zg, gb, hn]
    out_specs = [pl.BlockSpec((ng * seq, MIX_W), lambda b: (b, 0))]
    out_shape = [jax.ShapeDtypeStruct((batch * seq, MIX_W), F32)]
    if has_state:
        in_specs += [pl.BlockSpec((ng, None, N_UNITS, DH_C, DH_C), lambda b: (b, layer, 0, 0, 0)),
                     pl.BlockSpec((ng, None, N_UNITS, DH_C), lambda b: (b, layer, 0, 0)),
                     pl.BlockSpec((ng, None, N_UNITS, LANES), lambda b: (b, layer, 0, 0))]
        args += list(state)
    else:
        out_specs += [pl.BlockSpec((ng, N_UNITS, DH_C, DH_C), lambda b: (b, 0, 0, 0)),
                      pl.BlockSpec((ng, N_UNITS, DH_C), lambda b: (b, 0, 0)),
                      pl.BlockSpec((ng, N_UNITS, LANES), lambda b: (b, 0, 0))]
        out_shape += [jax.ShapeDtypeStruct((batch, N_UNITS, DH_C, DH_C), F32),
                      jax.ShapeDtypeStruct((batch, N_UNITS, DH_C), F32),
                      jax.ShapeDtypeStruct((batch, N_UNITS, LANES), F32)]
    step = lambda: pltpu.VMEM((ng, nc, N_UNITS, MLSTM_CHUNK), F32)
    transposed = lambda: pltpu.VMEM((ng, nc, D_C, MLSTM_CHUNK), BF16)
    return pl.pallas_call(
        functools.partial(_mlstm_kernel, seq=seq, ng=ng, has_state=has_state),
        grid=(batch // ng,),
        in_specs=in_specs,
        out_specs=out_specs,
        out_shape=out_shape,
        scratch_shapes=[
            pltpu.VMEM((ng * seq, D_C), BF16),
            transposed(), transposed(), transposed(),
            pltpu.VMEM((ng * seq, ZG_W), F32),
            step(), step(), step(), step(), step(),
            pltpu.VMEM((ng * N_UNITS, ST_ROWS, DH_C), F32),
            pltpu.VMEM((ng, 2, nc, D_C, MLSTM_CHUNK), F32),
        ],
        compiler_params=pltpu.CompilerParams(vmem_limit_bytes=VMEM_LIMIT),
        name="mlstm",
    )(*args)


def _merge_kernel(x_ref, mod_ref, g_ref, ya_ref, yb_ref, yc_ref, yd_ref, wm_ref, bm_ref, wb_ref, wo_ref,
                  w1_ref, w2_ref, o_ref):
    x = x_ref[...]
    h = (_rms(x, g_ref[0:1, :]) * (1.0 + mod_ref[1:2, :]) + mod_ref[0:1, :]).astype(BF16)
    acc = jnp.zeros(x.shape, F32)
    for n, y_ref in enumerate((ya_ref, yb_ref, yc_ref, yd_ref)):
        cols = slice(n * D_MODEL, (n + 1) * D_MODEL)
        gate = _sigmoid(_dot(h, wm_ref[:, cols]) + bm_ref[:, cols])
        acc = acc + gate * _dot(y_ref[...].astype(BF16), wb_ref[n])
    y = _dot(acc.astype(BF16), wo_ref[...])
    x = x + mod_ref[2:3, :] * _rms(y, g_ref[1:2, :])
    h2 = (_rms(x, g_ref[2:3, :]) * (1.0 + mod_ref[4:5, :]) + mod_ref[3:4, :]).astype(BF16)
    f = jnp.zeros(x.shape, F32)
    for j in range(D_FF // D_MODEL):
        cols = slice(j * D_MODEL, (j + 1) * D_MODEL)
        a = jnp.maximum(_dot(h2, w1_ref[:, cols]), 0.0)
        f = f + _dot((a * a).astype(BF16), w2_ref[cols, :])
    o_ref[...] = x + mod_ref[5:6, :] * _rms(f, g_ref[3:4, :])


def _merge_ffn(x, mod, g, ys, wm, bm, wb, wo, w1, w2, layer, seq, per_seq):
    t = x.shape[0]
    tm = TOKEN_TILE
    full = lambda shape: _layer_block(shape, layer)
    tok = lambda w_: pl.BlockSpec((tm, w_), lambda i: (i, 0))
    return pl.pallas_call(
        _merge_kernel,
        grid=(t // tm,),
        in_specs=[tok(D_MODEL), _mod_spec(layer, tm, seq, per_seq),
                  full((4, D_MODEL)), tok(MIX_W), tok(MIX_W), tok(MIX_W), tok(MIX_W),
                  full((D_MODEL, N_BRANCH * D_MODEL)), full((1, N_BRANCH * D_MODEL)),
                  full((N_BRANCH, MIX_W, D_MODEL)), full((D_MODEL, D_MODEL)),
                  full((D_MODEL, D_FF)), full((D_FF, D_MODEL))],
        out_specs=tok(D_MODEL),
        out_shape=jax.ShapeDtypeStruct((t, D_MODEL), F32),
        compiler_params=pltpu.CompilerParams(vmem_limit_bytes=VMEM_LIMIT),
        name="merge_ffn",
    )(x, mod, g, *ys, wm, bm, wb, wo, w1, w2)


def _arrange_w_in(w):
    wt = w.T.astype(BF16)
    z = lambda n: jnp.zeros((n, D_MODEL), BF16)
    o_b, o_c = IN_A, IN_A + IN_B
    o_g, o_d = o_c + 4 * D_C, o_c + IN_C
    gates = wt[o_g:o_d].reshape(2, 2, H_C, D_MODEL).transpose(1, 0, 2, 3).reshape(4 * H_C, D_MODEL)
    return jnp.concatenate([
        wt[:Q_LORA + KV_LORA], z(64), wt[Q_LORA + KV_LORA:IN_A], z(32),
        wt[o_b:o_g], gates, z(ZG_W - 4 * H_C), wt[o_d:]], axis=0).T


def _arrange_w_uq(w):
    w = w.reshape(Q_LORA, H_A, NOPE_A + ROPE_A)
    half = ROPE_A // 2
    swapped = jnp.concatenate([jnp.zeros_like(w[:, :, :NOPE_A]), w[:, :, NOPE_A + half:], w[:, :, NOPE_A:NOPE_A + half]],
                              axis=-1)
    pad = lambda a: jnp.pad(a, ((0, 0), (0, 0), (0, LANES - NOPE_A - ROPE_A))).reshape(Q_LORA, H_A * LANES)
    return jnp.concatenate([pad(w), pad(swapped)], axis=1).astype(BF16)


def _arrange_w_ukv(w):
    w = w.reshape(KV_LORA, H_A, NOPE_A + VH_A)
    wk = jnp.pad(w[:, :, :NOPE_A], ((0, 0), (0, 0), (0, LANES - NOPE_A)))
    v = w[:, :, NOPE_A:]
    zero = jnp.zeros_like(v)
    even = jnp.concatenate([v, zero], axis=-1)
    odd = jnp.concatenate([zero, v], axis=-1)
    wv = jnp.where((jnp.arange(H_A) % 2 == 0)[None, :, None], even, odd)
    return jnp.concatenate([wk.reshape(KV_LORA, -1), wv.reshape(KV_LORA, -1)], axis=1).astype(BF16)


def _rope_tables(rows):
    row = np.repeat(np.arange(rows), GRID_W).astype(np.float64)
    col = np.tile(np.arange(GRID_W), rows).astype(np.float64)
    nf = ROPE_A // 4
    inv = np.exp(-math.log(ROPE_BASE) * np.arange(nf, dtype=np.float64) / nf)
    ang = np.concatenate([row[:, None] * inv, col[:, None] * inv], axis=-1)
    cos, sin = np.cos(ang), np.sin(ang)
    n = cos.shape[0]
    ones, zeros = (lambda k: np.ones((n, k))), (lambda k: np.zeros((n, k)))
    mla = (np.concatenate([ones(64), cos, cos, ones(32)], axis=1),
           np.concatenate([zeros(64), -sin, zeros(48)], axis=1),
           np.concatenate([zeros(80), sin, zeros(32)], axis=1))
    dif = (np.tile(np.concatenate([cos, cos], axis=1), (1, 4)),
           np.tile(np.concatenate([-sin, zeros(16)], axis=1), (1, 4)),
           np.tile(np.concatenate([zeros(16), sin], axis=1), (1, 4)))
    as_f32 = lambda ts: tuple(jnp.asarray(t.astype(np.float32)) for t in ts)
    return as_f32(mla), as_f32(dif)


def kernel(x_prompt, x_sample, cache_mla_ckv, cache_mla_krope, cache_diff_k, cache_diff_v, state_mlstm_C,
           state_mlstm_n, state_mlstm_m, c, c_ctx, w_mod, b_mod, norm_g, w_in, mla_q_norm, w_uq, mla_kv_norm,
           w_ukv, gmlp_v_norm, gmlp_w_s, gmlp_b_s, mlstm_gate_bias, mlstm_head_norm, diff_lambda, diff_sub_norm,
           w_branch, w_merge, b_merge, w_out, w_ff1, w_ff2):
    bp, lp, _ = x_prompt.shape
    bs, ls, _ = x_sample.shape
    past = cache_mla_ckv.shape[2]

    cond = jnp.concatenate([c_ctx[None, :], c, jnp.zeros((16 - 1 - bs, D_MODEL), F32)], axis=0)
    mod = _modulation(cond, w_mod, b_mod).reshape(DEPTH, 16, 6, D_MODEL)
    rope_mla, rope_dif = _rope_tables(ls // GRID_W)

    row = lambda a: a[:, None, :]
    w_in_a = jax.vmap(_arrange_w_in)(w_in)
    wuq_a = jax.vmap(_arrange_w_uq)(w_uq)
    wukv_a = jax.vmap(_arrange_w_ukv)(w_ukv)
    qn, kvn, vn, hn, bm = row(mla_q_norm), row(mla_kv_norm), row(gmlp_v_norm), row(mlstm_head_norm), row(b_merge)
    ws = gmlp_w_s.astype(BF16)
    bias = jnp.repeat(jnp.swapaxes(gmlp_b_s, 1, 2), D_B // G_B, axis=2)
    gb = jnp.pad(mlstm_gate_bias.transpose(0, 2, 1, 3).reshape(DEPTH, 1, 4 * H_C),
                 ((0, 0), (0, 0), (0, ZG_W - 4 * H_C)))
    sn = row(jnp.tile(diff_sub_norm, (1, 2)))
    wm, wb, wo = w_merge.astype(BF16), w_branch.astype(BF16), w_out.astype(BF16)
    w1, w2 = w_ff1.astype(BF16), w_ff2.astype(BF16)

    ctx_kr = cache_mla_krope.transpose(0, 1, 3, 2)
    ctx_dk = cache_diff_k.transpose(0, 1, 3, 4, 5, 2).reshape(bs, DEPTH, D_D, past)
    ctx_dv = cache_diff_v.transpose(0, 1, 3, 4, 2).reshape(bs, DEPTH, D_D, past)
    c0 = state_mlstm_C.reshape(bs, DEPTH, N_UNITS, DH_C, DH_C)
    n0 = state_mlstm_n.reshape(bs, DEPTH, N_UNITS, DH_C)
    m0 = jnp.broadcast_to(state_mlstm_m.reshape(bs, DEPTH, N_UNITS, 1), (bs, DEPTH, N_UNITS, LANES))

    xp = x_prompt.reshape(bp * lp, D_MODEL)
    xs = x_sample.reshape(bs * ls, D_MODEL)
    ents = []
    diff_kv = None
    for l in range(DEPTH):
        lam_init = 0.8 - 0.6 * math.exp(-0.3 * l)
        for is_sample in (False, True):
            x = xs if is_sample else xp
            batch, seq = (bs, ls) if is_sample else (bp, lp)
            za, yb, zc, zg, dq, dk, dv = _in_proj(x, mod, norm_g, w_in_a, vn, ws, bias, l, seq, is_sample,
                                                  rope=rope_dif if is_sample else None)
            if is_sample:
                (ya,) = _mla(za, qn, kvn, wuq_a, wukv_a, l, batch, seq, rope=rope_mla, ctx=(cache_mla_ckv, ctx_kr))
                (yd,) = _diff(dq, dk, dv, diff_lambda, sn, l, batch, seq, lam_init, ctx=(ctx_dk, ctx_dv))
                (yc,) = _mlstm(zc, zg, gb, hn, l, batch, seq, state=(c0, n0, m0))
            else:
                ya, ckv, krt = _mla(za, qn, kvn, wuq_a, wukv_a, l, batch, seq)
                yd, *diff_kv = _diff(dq, dk, dv, diff_lambda, sn, l, batch, seq, lam_init, prev=diff_kv)
                yc, c_new, n_new, m_new = _mlstm(zc, zg, gb, hn, l, batch, seq)
                ents.append((
                    ckv.reshape(bp, lp, KV_LORA),
                    krt.transpose(0, 2, 1),
                    c_new.reshape(bp, 2, H_C, DH_C, DH_C),
                    n_new.reshape(bp, 2, H_C, DH_C),
                    m_new[:, :, 0].reshape(bp, 2, H_C)))
            x = _merge_ffn(x, mod, norm_g, (ya, yb, yc, yd), wm, bm, wb, wo, w1, w2, l, seq, is_sample)
            if is_sample:
                xs = x
            else:
                xp = x

    stack = lambda j: jnp.stack([e[j] for e in ents], axis=1)
    dkt, dvt = diff_kv
    new_diff_k = dkt.reshape(bp, DEPTH, H_D, 2, DH_D, lp).transpose(0, 1, 5, 2, 3, 4)
    new_diff_v = dvt.reshape(bp, DEPTH, H_D, 2 * DH_D, lp).transpose(0, 1, 4, 2, 3)
    return (xp.reshape(bp, lp, D_MODEL), xs.reshape(bs, ls, D_MODEL),
            stack(0), stack(1), new_diff_k, new_diff_v, stack(2), stack(3), stack(4))
```

```python
import functools
import math

import jax
import jax.numpy as jnp
import numpy as np
from jax import lax
from jax.experimental import pallas as pl
from jax.experimental.pallas import tpu as pltpu

F32 = jnp.float32
BF16 = jnp.bfloat16

D_MODEL = 1024
DEPTH = 2
GRID_W = 64
N_BRANCH = 4
MIX_W = 256
H_A, NOPE_A, ROPE_A, VH_A = 4, 64, 32, 64
Q_LORA, KV_LORA = 256, 128
D_B, G_B, CHUNK_B = 256, 4, 128
H_C, DH_C = 4, 64
D_C = H_C * DH_C
H_D, DH_D = 4, 32
D_D = H_D * 2 * DH_D
D_FF = 4 * D_MODEL
IN_A = Q_LORA + KV_LORA + ROPE_A
IN_B = 2 * D_B
IN_C = 4 * D_C + 4 * H_C
IN_D = 3 * D_D
ROPE_BASE = 10000.0
EPS = 1e-6
MLA_SCALE = (NOPE_A + ROPE_A) ** -0.5
DIFF_SCALE = DH_D ** -0.5

MIX_DTYPE = BF16
LANES = 128
MLSTM_CHUNK = 128
VMEM_LIMIT = 56 * 1024 * 1024
NEG_INF = float("-inf")

ZA_W = 512
ZG_W = 128
Z_OFF_A = 0
Z_OFF_B = Z_OFF_A + ZA_W
Z_OFF_C = Z_OFF_B + IN_B
Z_OFF_G = Z_OFF_C + 4 * D_C
Z_OFF_D = Z_OFF_G + ZG_W
Z_W = Z_OFF_D + IN_D


def _rms(x, g):
    return x * lax.rsqrt(jnp.mean(x * x, axis=-1, keepdims=True) + EPS) * g


def _sigmoid(x):
    return 1.0 / (1.0 + jnp.exp(-x))


def _log_sigmoid(x):
    return jnp.minimum(x, 0.0) - jnp.log1p(jnp.exp(-jnp.abs(x)))


def _dot(a, b):
    return jnp.dot(a, b, preferred_element_type=F32)


def _dot_nt(a, b):
    return lax.dot_general(a, b, (((1,), (1,)), ((), ())), preferred_element_type=F32)


def _lane_group_mask(width, group, index):
    lane = lax.broadcasted_iota(jnp.int32, (1, width), 1)
    return (lane >= index * group) & (lane < (index + 1) * group)


def _mod_kernel(cond_ref, w_ref, b_ref, o_ref):
    c = cond_ref[...]
    s = c * _sigmoid(c)
    o_ref[...] = _dot(s.astype(BF16), w_ref[...].astype(BF16)) + b_ref[...]


def _modulation(cond, w_mod, b_mod):
    rows = cond.shape[0]
    nb = 2048
    return pl.pallas_call(
        _mod_kernel,
        grid=(DEPTH, 6 * D_MODEL // nb),
        in_specs=[
            pl.BlockSpec((rows, D_MODEL), lambda l, j: (0, 0)),
            pl.BlockSpec((None, D_MODEL, nb), lambda l, j: (l, 0, j)),
            pl.BlockSpec((None, 1, nb), lambda l, j: (l, 0, j)),
        ],
        out_specs=pl.BlockSpec((None, rows, nb), lambda l, j: (l, 0, j)),
        out_shape=jax.ShapeDtypeStruct((DEPTH, rows, 6 * D_MODEL), F32),
        name="modulation",
    )(cond, w_mod, b_mod.reshape(DEPTH, 1, 6 * D_MODEL))


TOKEN_TILE = 512
IN_PROJ_TILE = 1024


def _layer_block(shape, layer):
    shape = tuple(shape)
    return pl.BlockSpec((None,) + shape, lambda *_: (layer,) + (0,) * len(shape), pipeline_mode=pl.Buffered(1))


def _mod_spec(layer, tm, seq, per_seq):
    def index(i):
        return (layer, 1 + (i * tm) // seq if per_seq else 0, 0, 0)
    return pl.BlockSpec((None, None, 6, D_MODEL), index)


def _gmlp_tile(zb, vn, ws_ref, bias):
    tm = zb.shape[0]
    v = _rms(zb[:, D_B:], vn).astype(BF16)
    gmasks = [_lane_group_mask(D_B, D_B // G_B, g) for g in range(G_B)]
    out = []
    for ch in range(tm // CHUNK_B):
        vc = v[ch * CHUNK_B:(ch + 1) * CHUNK_B, :]
        mixed = bias
        for g in range(G_B):
            mixed = mixed + jnp.where(gmasks[g], _dot(ws_ref[g], vc), 0.0)
        out.append(zb[ch * CHUNK_B:(ch + 1) * CHUNK_B, :D_B] * mixed)
    return jnp.concatenate(out, axis=0)


def _in_kernel(*refs, rotary):
    if rotary:
        x_ref, mod_ref, g_ref, w_ref, vn_ref, ws_ref, bias_ref, tc_ref, ts1_ref, ts2_ref = refs[:10]
    else:
        x_ref, mod_ref, g_ref, w_ref, vn_ref, ws_ref, bias_ref = refs[:7]
    za_ref, yb_ref, zc_ref, zg_ref, dq_ref, dk_ref, dv_ref = refs[-7:]
    x = x_ref[...]
    h = _rms(x, g_ref[0:1, :]) * (1.0 + mod_ref[1:2, :]) + mod_ref[0:1, :]
    hb = h.astype(BF16)
    za_ref[...] = _dot(hb, w_ref[:, Z_OFF_A:Z_OFF_B])
    zc_ref[...] = _dot(hb, w_ref[:, Z_OFF_C:Z_OFF_G])
    zg_ref[...] = _dot(hb, w_ref[:, Z_OFF_G:Z_OFF_D])
    for j, d_ref in enumerate((dq_ref, dk_ref, dv_ref)):
        d = _dot(hb, w_ref[:, Z_OFF_D + j * D_D:Z_OFF_D + (j + 1) * D_D])
        if rotary and j < 2:
            tc, ts1, ts2 = tc_ref[...], ts1_ref[...], ts2_ref[...]
            d = jnp.concatenate([_rope128(d[:, c * LANES:(c + 1) * LANES], tc, ts1, ts2) for c in range(D_D // LANES)],
                                axis=1)
        d_ref[...] = d
    yb = _gmlp_tile(_dot(hb, w_ref[:, Z_OFF_B:Z_OFF_C]), vn_ref[...], ws_ref, bias_ref[...])
    yb_ref[...] = yb.astype(yb_ref.dtype)


def _in_proj(x, mod, g, w, vn, ws, bias, layer, seq, per_seq, rope=None):
    t = x.shape[0]
    tm = IN_PROJ_TILE
    widths = (ZA_W, MIX_W, 4 * D_C, ZG_W, D_D, D_D, D_D)
    in_specs = [
        pl.BlockSpec((tm, D_MODEL), lambda i: (i, 0)),
        _mod_spec(layer, tm, seq, per_seq),
        _layer_block((4, D_MODEL), layer),
        _layer_block((D_MODEL, Z_W), layer),
        _layer_block((1, D_B), layer),
        _layer_block((G_B, CHUNK_B, CHUNK_B), layer),
        _layer_block((CHUNK_B, D_B), layer),
    ]
    args = [x, mod, g, w, vn, ws, bias]
    if rope is not None:
        in_specs += [pl.BlockSpec((tm, LANES), lambda i: (i % (seq // tm), 0))] * 3
        args += list(rope)
    return pl.pallas_call(
        functools.partial(_in_kernel, rotary=rope is not None),
        grid=(t // tm,),
        in_specs=in_specs,
        out_specs=[pl.BlockSpec((tm, w_), lambda i: (i, 0)) for w_ in widths],
        out_shape=[jax.ShapeDtypeStruct((t, w_), MIX_DTYPE if j == 1 else F32) for j, w_ in enumerate(widths)],
        compiler_params=pltpu.CompilerParams(vmem_limit_bytes=VMEM_LIMIT),
        name="in_proj",
    )(*args)


LOG2E = math.log2(math.e)
ATTN_GROUP_TOKENS = 1024


def _with_ones(v, lo_mask, keep_lo):
    lane = lax.broadcasted_iota(jnp.int32, (1, LANES), 1)
    if keep_lo:
        return jnp.where(lane == 64, 1.0, jnp.where(lo_mask, v, 0.0))
    return jnp.where(lane == 0, 1.0, jnp.where(lo_mask, 0.0, v))


def _pair_normalise(o_even, o_odd, lo_mask):
    return jnp.where(lo_mask, o_even / o_even[:, 64:65], o_odd / o_odd[:, 0:1])


def _rope128(x, tc, ts1, ts2):
    return x * tc + pltpu.roll(x, LANES - 16, 1) * ts1 + pltpu.roll(x, 16, 1) * ts2


def _mla_kernel(*refs, seq, ng, has_ctx, tq):
    rows_all = ng * seq
    if has_ctx:
        (za_ref, qn_ref, kvn_ref, wuq_ref, wukv_ref, tc_ref, ts1_ref, ts2_ref, cckv_ref, ckr_ref,
         ya_ref, q_s, k_s, v_s, ckv_s, kr_s) = refs
        past = cckv_ref.shape[0]
    else:
        (za_ref, qn_ref, kvn_ref, wuq_ref, wukv_ref, ya_ref, ckv_ref, krt_ref, q_s, k_s, v_s, ckv_s, kr_s) = refs
        past = 0
    lk = past + seq
    za = za_ref[...]
    cq = _rms(za[:, :Q_LORA], qn_ref[...])
    qh = _dot(cq.astype(BF16), wuq_ref[:, 0:(2 if has_ctx else 1) * H_A * LANES])
    ckv = _rms(za[:, Q_LORA:Q_LORA + KV_LORA], kvn_ref[...])
    kr = za[:, Q_LORA + KV_LORA:ZA_W]
    if has_ctx:
        tc, ts1, ts2 = tc_ref[...], ts1_ref[...], ts2_ref[...]
        kr = _rope128(kr, tc, ts1, ts2)
        ckv_s[0:past, :] = cckv_ref[...]
        krt = jnp.concatenate([jnp.zeros((64, past), F32), ckr_ref[...], jnp.zeros((LANES - 64 - ROPE_A, past), F32)],
                              axis=0)
        kr_s[0:past, :] = krt.T
    else:
        ckv_ref[...] = ckv
        for g in range(ng):
            krt_ref[g] = kr[g * seq:(g + 1) * seq, :].T[64:64 + ROPE_A, :]
    ckv_s[past:past + rows_all, :] = ckv
    kr_s[past:past + rows_all, :] = kr
    kv = _dot(ckv_s[...].astype(BF16), wukv_ref[...])
    kr_all = kr_s[...]
    lo = _lane_group_mask(LANES, 64, 0)
    for h in range(H_A):
        qg = qh[:, h * LANES:(h + 1) * LANES]
        if has_ctx:
            qg = qg * tc + qh[:, (H_A + h) * LANES:(H_A + h + 1) * LANES] * (ts1 + ts2)
        q_s[h] = (qg * (MLA_SCALE * LOG2E)).astype(BF16)
        k_s[h] = (kv[:, h * LANES:(h + 1) * LANES] + kr_all).astype(BF16)
        v_s[h] = _with_ones(kv[:, (H_A + h) * LANES:(H_A + h + 1) * LANES], lo, h % 2 == 0).astype(BF16)

    def body(i, carry):
        rows = pl.ds(pl.multiple_of(i * tq, tq), tq)
        keys = pl.ds(pl.multiple_of(((i * tq) // seq) * seq, seq), lk)
        s = [_dot_nt(q_s[h, rows, :], k_s[h, keys, :]) for h in range(H_A)]
        o = [_dot(jnp.exp2(s[h] - jnp.max(s[h], axis=-1, keepdims=True)).astype(BF16), v_s[h, keys, :])
             for h in range(H_A)]
        for pair in range(H_A // 2):
            ya_ref[rows, pair * LANES:(pair + 1) * LANES] = _pair_normalise(o[2 * pair], o[2 * pair + 1],
                                                                            lo).astype(ya_ref.dtype)
        return carry

    lax.fori_loop(0, rows_all // tq, body, 0, unroll=not has_ctx)


def _ctx_block(rows, cols, layer):
    return pl.BlockSpec((None, None, rows, cols), lambda b: (b, layer, 0, 0))


def _mla(za, qn, kvn, wuq, wukv, layer, batch, seq, rope=None, ctx=None):
    has_ctx = ctx is not None
    past = ctx[0].shape[2] if has_ctx else 0
    ng = 1 if has_ctx else ATTN_GROUP_TOKENS // seq
    rows = ng * seq
    keys = past + rows
    tq = min(seq, 256)
    full = lambda shape: pl.BlockSpec(shape, lambda b: (0,) * len(shape))
    in_specs = [
        pl.BlockSpec((rows, ZA_W), lambda b: (b, 0)),
        _layer_block((1, Q_LORA), layer), _layer_block((1, KV_LORA), layer),
        _layer_block((Q_LORA, 2 * H_A * LANES), layer), _layer_block((KV_LORA, 2 * H_A * LANES), layer),
    ]
    args = [za, qn, kvn, wuq, wukv]
    out_specs = [pl.BlockSpec((rows, MIX_W), lambda b: (b, 0))]
    out_shape = [jax.ShapeDtypeStruct((batch * seq, MIX_W), MIX_DTYPE)]
    if has_ctx:
        in_specs += [full((seq, LANES))] * 3
        in_specs += [_ctx_block(past, KV_LORA, layer), _ctx_block(ROPE_A, past, layer)]
        args += list(rope) + list(ctx)
    else:
        out_specs += [pl.BlockSpec((rows, KV_LORA), lambda b: (b, 0)),
                      pl.BlockSpec((ng, ROPE_A, seq), lambda b: (b, 0, 0))]
        out_shape += [jax.ShapeDtypeStruct((batch * seq, KV_LORA), F32),
                      jax.ShapeDtypeStruct((batch, ROPE_A, seq), F32)]
    return pl.pallas_call(
        functools.partial(_mla_kernel, seq=seq, ng=ng, has_ctx=has_ctx, tq=tq),
        grid=(batch // ng,),
        in_specs=in_specs,
        out_specs=out_specs,
        out_shape=out_shape,
        scratch_shapes=[
            pltpu.VMEM((H_A, rows, LANES), BF16),
            pltpu.VMEM((H_A, keys, LANES), BF16),
            pltpu.VMEM((H_A, keys, LANES), BF16),
            pltpu.VMEM((keys, LANES), F32),
            pltpu.VMEM((keys, LANES), F32),
        ],
        compiler_params=pltpu.CompilerParams(vmem_limit_bytes=VMEM_LIMIT),
        name="mla",
    )(*args)


def _diff_kernel(*refs, seq, ng, has_ctx, n_prev, tq, lam_init):
    rows_all = ng * seq
    if has_ctx:
        (dq_ref, dk_ref, dv_ref, lam_ref, sn_ref, ck_ref, cv_ref, yd_ref, q_s, k_s, v_s) = refs
        past = ck_ref.shape[1]
    else:
        dq_ref, dk_ref, dv_ref, lam_ref, sn_ref = refs[:5]
        yd_ref, dkt_ref, dvt_ref, q_s, k_s, v_s = refs[-6:]
        past = 0
        for g in range(ng):
            if n_prev:
                dkt_ref[g, 0:n_prev] = refs[5][g]
                dvt_ref[g, 0:n_prev] = refs[6][g]
            dkt_ref[g, n_prev] = dk_ref[g * seq:(g + 1) * seq, :].T
            dvt_ref[g, n_prev] = dv_ref[g * seq:(g + 1) * seq, :].T
    lk = past + seq
    lo = _lane_group_mask(LANES, 64, 0)
    if has_ctx:
        k_s[0:past, :] = ck_ref[...].T.astype(BF16)
        cv = cv_ref[...].T
    for half in range(2):
        cols = slice(half * LANES, (half + 1) * LANES)
        q, k, v = dq_ref[:, cols], dk_ref[:, cols], dv_ref[:, cols]
        if has_ctx:
            v_s[2 * half, 0:past, :] = _with_ones(cv[:, cols], lo, True).astype(BF16)
            v_s[2 * half + 1, 0:past, :] = _with_ones(cv[:, cols], lo, False).astype(BF16)
        q_s[:, cols] = (q * (DIFF_SCALE * LOG2E)).astype(BF16)
        k_s[past:past + rows_all, cols] = k.astype(BF16)
        v_s[2 * half, past:past + rows_all, :] = _with_ones(v, lo, True).astype(BF16)
        v_s[2 * half + 1, past:past + rows_all, :] = _with_ones(v, lo, False).astype(BF16)

    lam = lam_ref[...]
    lam_val = (jnp.exp(jnp.sum(lam[0:1] * lam[1:2], axis=-1, keepdims=True))
               - jnp.exp(jnp.sum(lam[2:3] * lam[3:4], axis=-1, keepdims=True)) + lam_init)
    comp_masks = [jnp.where(_lane_group_mask(D_D, DH_D, j), 1.0, 0.0).astype(BF16) for j in range(2 * H_D)]
    sn = sn_ref[...]

    def body(i, carry):
        rows = pl.ds(pl.multiple_of(i * tq, tq), tq)
        qb = q_s[rows, :]
        keys = pl.ds(pl.multiple_of(((i * tq) // seq) * seq, seq), lk)
        k_all = k_s[keys, :]
        score = lambda h: [_dot_nt(qb * comp_masks[2 * h + comp], k_all) for comp in range(2)]
        s_next = score(0)
        o = []
        for h in range(H_D):
            s = s_next
            if h + 1 < H_D:
                s_next = score(h + 1)
            o1, o2 = [_dot(jnp.exp2(sc - jnp.max(sc, axis=-1, keepdims=True)).astype(BF16), v_s[h, keys, :])
                      for sc in s]
            ll = 64 if h % 2 == 0 else 0
            o.append(o1 / o1[:, ll:ll + 1] - lam_val * (o2 / o2[:, ll:ll + 1]))
        for pair in range(H_D // 2):
            acc = jnp.where(lo, o[2 * pair], o[2 * pair + 1])
            sq = acc * acc
            ss_lo = jnp.sum(jnp.where(lo, sq, 0.0), axis=-1, keepdims=True)
            ss_hi = jnp.sum(jnp.where(lo, 0.0, sq), axis=-1, keepdims=True)
            r = jnp.where(lo, lax.rsqrt(ss_lo / (2 * DH_D) + EPS), lax.rsqrt(ss_hi / (2 * DH_D) + EPS))
            yd_ref[rows, pair * LANES:(pair + 1) * LANES] = (acc * r * sn * (1.0 - lam_init)).astype(yd_ref.dtype)
        return carry

    lax.fori_loop(0, rows_all // tq, body, 0, unroll=not has_ctx)


def _diff(dq, dk, dv, lam, sn, layer, batch, seq, lam_init, ctx=None, prev=None):
    has_ctx = ctx is not None
    n_prev = prev[0].shape[1] if prev is not None else 0
    past = ctx[0].shape[3] if has_ctx else 0
    ng = 1
    rows = ng * seq
    keys = past + rows
    tq = min(seq, 256)
    in_specs = [pl.BlockSpec((rows, D_D), lambda b: (b, 0))] * 3
    in_specs += [_layer_block((4, DH_D), layer), _layer_block((1, LANES), layer)]
    args = [dq, dk, dv, lam, sn]
    if has_ctx:
        in_specs += [_ctx_block(D_D, past, layer)] * 2
        args += list(ctx)
    if n_prev:
        in_specs += [pl.BlockSpec((ng, n_prev, D_D, seq), lambda b: (b, 0, 0, 0))] * 2
        args += list(prev)
    out_specs = [pl.BlockSpec((rows, MIX_W), lambda b: (b, 0))]
    out_shape = [jax.ShapeDtypeStruct((batch * seq, MIX_W), MIX_DTYPE)]
    if not has_ctx:
        out_specs += [pl.BlockSpec((ng, n_prev + 1, D_D, seq), lambda b: (b, 0, 0, 0))] * 2
        out_shape += [jax.ShapeDtypeStruct((batch, n_prev + 1, D_D, seq), F32)] * 2
    return pl.pallas_call(
        functools.partial(_diff_kernel, seq=seq, ng=ng, has_ctx=has_ctx, n_prev=n_prev, tq=tq, lam_init=lam_init),
        grid=(batch // ng,),
        in_specs=in_specs,
        out_specs=out_specs,
        out_shape=out_shape,
        scratch_shapes=[
            pltpu.VMEM((rows, D_D), BF16),
            pltpu.VMEM((keys, D_D), BF16),
            pltpu.VMEM((H_D, keys, LANES), BF16),
        ],
        compiler_params=pltpu.CompilerParams(vmem_limit_bytes=VMEM_LIMIT),
        name="diff_attn",
    )(*args)


def _split3(x):
    hi = x.astype(BF16)
    r1 = x - hi.astype(F32)
    mid = r1.astype(BF16)
    lo = (r1 - mid.astype(F32)).astype(BF16)
    return hi, mid, lo


ST_ROWS = DH_C + 8
N_UNITS = 2 * H_C
MLSTM_GROUP_TOKENS = 1024


def _mlstm_kernel(*refs, seq, ng, has_state):
    ck = MLSTM_CHUNK
    nc = seq // ck
    if has_state:
        (zc_ref, zg_ref, gb_ref, hn_ref, c0_ref, n0_ref, m0_ref,
         yc_ref, k_s, qt_s, kt_s, vt_s, rcc_s, ca_s, wi_s, en_s, ws_s, wo_s, st_s, ht_s) = refs
    else:
        (zc_ref, zg_ref, gb_ref, hn_ref,
         yc_ref, cout_ref, nout_ref, mout_ref,
         k_s, qt_s, kt_s, vt_s, rcc_s, ca_s, wi_s, en_s, ws_s, wo_s, st_s, ht_s) = refs
    eye = jnp.where(lax.broadcasted_iota(jnp.int32, (DH_C, DH_C), 0) == lax.broadcasted_iota(jnp.int32, (DH_C, DH_C), 1),
                    1.0, 0.0).astype(BF16)

    def transpose_exact(a):
        return sum(_dot_nt(eye, part) for part in _split3(a))

    def chunk_rows(g, c):
        return slice(g * seq + c * ck, g * seq + (c + 1) * ck)

    for g in range(ng):
        for c in range(nc):
            rows = chunk_rows(g, c)
            qt_s[g, c] = zc_ref[rows, 0:D_C].T.astype(BF16)
            kk = zc_ref[rows, D_C:2 * D_C] * (DH_C ** -0.5)
            k_s[rows, :] = kk.astype(BF16)
            kt_s[g, c] = kk.T.astype(BF16)
            vt_s[g, c] = zc_ref[rows, 2 * D_C:3 * D_C].T.astype(BF16)

    gates = zg_ref[...] + gb_ref[...]
    r_i = lax.broadcasted_iota(jnp.int32, (ck, ck), 0)
    c_i = lax.broadcasted_iota(jnp.int32, (ck, ck), 1)
    sum_fw = jnp.where(r_i <= c_i, 1.0, 0.0).astype(BF16)
    sum_bw = jnp.where(r_i >= c_i, 1.0, 0.0).astype(BF16)
    row8 = lax.broadcasted_iota(jnp.int32, (N_UNITS, ck), 0)
    fw8 = row8 < H_C
    fw81 = fw8[:, 0:1]
    b_steps, r_steps = [], []
    for g in range(ng):
        i_rows, b_rows = [], []
        for c in range(nc):
            rows = chunk_rows(g, c)
            g_t = gates[rows, :].T
            i8 = g_t[0:N_UNITS, :]
            p0, p1, p2 = _split3(_log_sigmoid(g_t[N_UNITS:2 * N_UNITS, :]))
            b_fw = _dot(p0, sum_fw) + _dot(p1, sum_fw) + _dot(p2, sum_fw)
            b_bw = _dot(p0, sum_bw) + _dot(p1, sum_bw) + _dot(p2, sum_bw)
            b8 = jnp.where(fw8, b_fw, b_bw)
            rcc_s[rows, :] = jnp.concatenate([i8 - b8, jnp.zeros((ck - N_UNITS, ck), F32)], axis=0).T
            i_rows.append(i8)
            b_rows.append(b8)
        b_steps.append([jnp.where(fw8, b_rows[i], b_rows[nc - 1 - i]) for i in range(nc)])
        r_steps.append([jnp.where(fw8, i_rows[i], i_rows[nc - 1 - i]) - b_steps[g][i] for i in range(nc)])

    pm = sm = jnp.concatenate([r for g in range(ng) for r in r_steps[g]], axis=0)
    lane_all = lax.broadcasted_iota(jnp.int32, pm.shape, 1)
    sh = 1
    while sh < ck:
        pm = jnp.maximum(pm, jnp.where(lane_all >= sh, pltpu.roll(pm, sh, 1), NEG_INF))
        sm = jnp.maximum(sm, jnp.where(lane_all < ck - sh, pltpu.roll(sm, ck - sh, 1), NEG_INF))
        sh *= 2
    m_last = []
    for g in range(ng):
        m_prev = m0_ref[g, :, 0:1] if has_state else jnp.zeros((N_UNITS, 1), F32)
        for i in range(nc):
            b8, r8 = b_steps[g][i], r_steps[g][i]
            at = (g * nc + i) * N_UNITS
            pm8, sm8 = pm[at:at + N_UNITS, :], sm[at:at + N_UNITS, :]
            cm = jnp.where(fw8, pm8, sm8)
            b_t = jnp.where(fw81, b8[:, ck - 1:ck], b8[:, 0:1])
            cm_end = jnp.where(fw81, pm8[:, ck - 1:ck], sm8[:, 0:1])
            inter = b8 + m_prev
            mt = jnp.maximum(inter, b8 + cm)
            ca_s[g, i] = b8 - mt
            wi_s[g, i] = jnp.exp(inter - mt)
            en_s[g, i] = jnp.exp(-mt)
            m_new = b_t + jnp.maximum(m_prev, cm_end)
            ws_s[g, i] = jnp.exp(b_t + r8 - m_new)
            wo_s[g, i] = jnp.broadcast_to(jnp.exp(b_t + m_prev - m_new), (N_UNITS, ck))
            m_prev = m_new
        m_last.append(m_prev)

    if has_state:
        for g in range(ng):
            for u in range(N_UNITS):
                st_s[g * N_UNITS + u, 0:DH_C, :] = transpose_exact(c0_ref[g, u])
                st_s[g * N_UNITS + u, DH_C:ST_ROWS, :] = jnp.broadcast_to(n0_ref[g, u:u + 1, :],
                                                                         (ST_ROWS - DH_C, DH_C))
    else:
        st_s[...] = jnp.zeros_like(st_s)

    hmask_bf = [jnp.where(_lane_group_mask(D_C, DH_C, h), 1.0, 0.0).astype(BF16) for h in range(H_C)]
    src_ok = (r_i <= c_i, r_i >= c_i)
    hs = [slice(h * DH_C, (h + 1) * DH_C) for h in range(H_C)]
    units = [(g, d, h) for g in range(ng) for d in range(2) for h in range(H_C)]

    def body(i, carry):
        chunks = (i, nc - 1 - i)
        ca, wi, en, ws, wo = [[ref[g, i] for g in range(ng)] for ref in (ca_s, wi_s, en_s, ws_s, wo_s)]
        q_t = [[qt_s[g, c] for c in chunks] for g in range(ng)]
        k_t = [[kt_s[g, c] for c in chunks] for g in range(ng)]
        v_t = [[vt_s[g, c] for c in chunks] for g in range(ng)]
        rows = [[pl.ds(pl.multiple_of(g * seq + c * ck, ck), ck) for c in chunks] for g in range(ng)]
        sts, qcts, states, upds, pts = [], [], [], [], []
        for g, d, h in units:
            sts.append(_dot(k_s[rows[g][d], :] * hmask_bf[h], q_t[g][d]))
            states.append(st_s[g * N_UNITS + d * H_C + h])
            qcts.append(_dot(states[-1].astype(BF16), q_t[g][d][hs[h], :]))
        for g, d, h in units:
            u = d * H_C + h
            w_s = ws[g][u:u + 1, :]
            vw = jnp.concatenate([v_t[g][d][hs[h], :].astype(F32) * w_s, jnp.broadcast_to(w_s, (8, ck))], axis=0)
            upds.append(_dot_nt(vw.astype(BF16), k_t[g][d][hs[h], :]))
        for j, (g, d, h) in enumerate(units):
            u = d * H_C + h
            decay = jnp.exp(rcc_s[rows[g][d], u:u + 1] + ca[g][u:u + 1, :])
            pts.append(sts[j] * jnp.where(src_ok[d], decay, 0.0))
        nums = [_dot(v_t[g][d][hs[h], :], pts[j].astype(BF16)) for j, (g, d, h) in enumerate(units)]
        for j, (g, d, h) in enumerate(units):
            u = d * H_C + h
            w_i = wi[g][u:u + 1, :]
            den = w_i * qcts[j][DH_C:DH_C + 1, :] + jnp.sum(pts[j], axis=0, keepdims=True)
            num = w_i * qcts[j][0:DH_C, :] + nums[j]
            ht_s[g, d, chunks[d], hs[h], :] = num / jnp.maximum(jnp.abs(den), en[g][u:u + 1, :])
            st_s[g * N_UNITS + u] = wo[g][u:u + 1, 0:DH_C] * states[j] + upds[j]
        return carry

    lax.fori_loop(0, nc, body, 0)

    hn = hn_ref[...]
    for g in range(ng):
        for c in range(nc):
            rows = chunk_rows(g, c)
            h_t = ht_s[g, 0, c] + ht_s[g, 1, c]
            parts = []
            for h in range(H_C):
                x = h_t[hs[h], :]
                parts.append(x * lax.rsqrt(jnp.mean(x * x, axis=0, keepdims=True) + EPS))
            h_n = jnp.concatenate(parts, axis=0).T
            yc_ref[rows, :] = (_sigmoid(zc_ref[rows, 3 * D_C:4 * D_C]) * (h_n * hn)).astype(yc_ref.dtype)

    if not has_state:
        for g in range(ng):
            for u in range(N_UNITS):
                cout_ref[g, u] = transpose_exact(st_s[g * N_UNITS + u, 0:DH_C, :])
                nout_ref[g, u:u + 1, :] = st_s[g * N_UNITS + u, DH_C:DH_C + 1, :]
            mout_ref[g] = jnp.broadcast_to(m_last[g], (N_UNITS, LANES))


def _mlstm(zc, zg, gb, hn, layer, batch, seq, state=None):
    has_state = state is not None
    nc = seq // MLSTM_CHUNK
    ng = max(2, MLSTM_GROUP_TOKENS // seq)
    in_specs = [pl.BlockSpec((ng * seq, 4 * D_C), lambda b: (b, 0)), pl.BlockSpec((ng * seq, ZG_W), lambda b: (b, 0)),
                _layer_block((1, ZG_W), layer), _layer_block((1, D_C), layer)]
    args = [zc, zg, gb, hn]
    out_specs = [pl.BlockSpec((ng * seq, MIX_W), lambda b: (b, 0))]
    out_shape = [jax.ShapeDtypeStruct((batch * seq, MIX_W), MIX_DTYPE)]
    if has_state:
        in_specs += [pl.BlockSpec((ng, None, N_UNITS, DH_C, DH_C), lambda b: (b, layer, 0, 0, 0)),
                     pl.BlockSpec((ng, None, N_UNITS, DH_C), lambda b: (b, layer, 0, 0)),
                     pl.BlockSpec((ng, None, N_UNITS, LANES), lambda b: (b, layer, 0, 0))]
        args += list(state)
    else:
        out_specs += [pl.BlockSpec((ng, N_UNITS, DH_C, DH_C), lambda b: (b, 0, 0, 0)),
                      pl.BlockSpec((ng, N_UNITS, DH_C), lambda b: (b, 0, 0)),
                      pl.BlockSpec((ng, N_UNITS, LANES), lambda b: (b, 0, 0))]
        out_shape += [jax.ShapeDtypeStruct((batch, N_UNITS, DH_C, DH_C), F32),
                      jax.ShapeDtypeStruct((batch, N_UNITS, DH_C), F32),
                      jax.ShapeDtypeStruct((batch, N_UNITS, LANES), F32)]
    step = lambda: pltpu.VMEM((ng, nc, N_UNITS, MLSTM_CHUNK), F32)
    transposed = lambda: pltpu.VMEM((ng, nc, D_C, MLSTM_CHUNK), BF16)
    return pl.pallas_call(
        functools.partial(_mlstm_kernel, seq=seq, ng=ng, has_state=has_state),
        grid=(batch // ng,),
        in_specs=in_specs,
        out_specs=out_specs,
        out_shape=out_shape,
        scratch_shapes=[
            pltpu.VMEM((ng * seq, D_C), BF16),
            transposed(), transposed(), transposed(),
            pltpu.VMEM((ng * seq, ZG_W), F32),
            step(), step(), step(), step(), step(),
            pltpu.VMEM((ng * N_UNITS, ST_ROWS, DH_C), F32),
            pltpu.VMEM((ng, 2, nc, D_C, MLSTM_CHUNK), F32),
        ],
        compiler_params=pltpu.CompilerParams(vmem_limit_bytes=VMEM_LIMIT),
        name="mlstm",
    )(*args)


def _merge_kernel(x_ref, mod_ref, g_ref, ya_ref, yb_ref, yc_ref, yd_ref, wm_ref, bm_ref, wb_ref, wo_ref,
                  w1_ref, w2_ref, o_ref):
    x = x_ref[...]
    h = (_rms(x, g_ref[0:1, :]) * (1.0 + mod_ref[1:2, :]) + mod_ref[0:1, :]).astype(BF16)
    acc = jnp.zeros(x.shape, F32)
    for n, y_ref in enumerate((ya_ref, yb_ref, yc_ref, yd_ref)):
        cols = slice(n * D_MODEL, (n + 1) * D_MODEL)
        gate = _sigmoid(_dot(h, wm_ref[:, cols]) + bm_ref[:, cols])
        acc = acc + gate * _dot(y_ref[...].astype(BF16), wb_ref[n])
    y = _dot(acc.astype(BF16), wo_ref[...])
    x = x + mod_ref[2:3, :] * _rms(y, g_ref[1:2, :])
    h2 = (_rms(x, g_ref[2:3, :]) * (1.0 + mod_ref[4:5, :]) + mod_ref[3:4, :]).astype(BF16)
    f = jnp.zeros(x.shape, F32)
    for j in range(D_FF // D_MODEL):
        cols = slice(j * D_MODEL, (j + 1) * D_MODEL)
        a = jnp.maximum(_dot(h2, w1_ref[:, cols]), 0.0)
        f = f + _dot((a * a).astype(BF16), w2_ref[cols, :])
    o_ref[...] = x + mod_ref[5:6, :] * _rms(f, g_ref[3:4, :])


def _merge_ffn(x, mod, g, ys, wm, bm, wb, wo, w1, w2, layer, seq, per_seq):
    t = x.shape[0]
    tm = TOKEN_TILE
    full = lambda shape: _layer_block(shape, layer)
    tok = lambda w_: pl.BlockSpec((tm, w_), lambda i: (i, 0))
    return pl.pallas_call(
        _merge_kernel,
        grid=(t // tm,),
        in_specs=[tok(D_MODEL), _mod_spec(layer, tm, seq, per_seq),
                  full((4, D_MODEL)), tok(MIX_W), tok(MIX_W), tok(MIX_W), tok(MIX_W),
                  full((D_MODEL, N_BRANCH * D_MODEL)), full((1, N_BRANCH * D_MODEL)),
                  full((N_BRANCH, MIX_W, D_MODEL)), full((D_MODEL, D_MODEL)),
                  full((D_MODEL, D_FF)), full((D_FF, D_MODEL))],
        out_specs=tok(D_MODEL),
        out_shape=jax.ShapeDtypeStruct((t, D_MODEL), F32),
        compiler_params=pltpu.CompilerParams(vmem_limit_bytes=VMEM_LIMIT),
        name="merge_ffn",
    )(x, mod, g, *ys, wm, bm, wb, wo, w1, w2)


def _arrange_w_in(w):
    wt = w.T.astype(BF16)
    z = lambda n: jnp.zeros((n, D_MODEL), BF16)
    o_b, o_c = IN_A, IN_A + IN_B
    o_g, o_d = o_c + 4 * D_C, o_c + IN_C
    gates = wt[o_g:o_d].reshape(2, 2, H_C, D_MODEL).transpose(1, 0, 2, 3).reshape(4 * H_C, D_MODEL)
    return jnp.concatenate([
        wt[:Q_LORA + KV_LORA], z(64), wt[Q_LORA + KV_LORA:IN_A], z(32),
        wt[o_b:o_g], gates, z(ZG_W - 4 * H_C), wt[o_d:]], axis=0).T


def _arrange_w_uq(w):
    w = w.reshape(Q_LORA, H_A, NOPE_A + ROPE_A)
    half = ROPE_A // 2
    swapped = jnp.concatenate([jnp.zeros_like(w[:, :, :NOPE_A]), w[:, :, NOPE_A + half:], w[:, :, NOPE_A:NOPE_A + half]],
                              axis=-1)
    pad = lambda a: jnp.pad(a, ((0, 0), (0, 0), (0, LANES - NOPE_A - ROPE_A))).reshape(Q_LORA, H_A * LANES)
    return jnp.concatenate([pad(w), pad(swapped)], axis=1).astype(BF16)


def _arrange_w_ukv(w):
    w = w.reshape(KV_LORA, H_A, NOPE_A + VH_A)
    wk = jnp.pad(w[:, :, :NOPE_A], ((0, 0), (0, 0), (0, LANES - NOPE_A)))
    v = w[:, :, NOPE_A:]
    zero = jnp.zeros_like(v)
    even = jnp.concatenate([v, zero], axis=-1)
    odd = jnp.concatenate([zero, v], axis=-1)
    wv = jnp.where((jnp.arange(H_A) % 2 == 0)[None, :, None], even, odd)
    return jnp.concatenate([wk.reshape(KV_LORA, -1), wv.reshape(KV_LORA, -1)], axis=1).astype(BF16)


def _rope_tables(rows):
    row = np.repeat(np.arange(rows), GRID_W).astype(np.float64)
    col = np.tile(np.arange(GRID_W), rows).astype(np.float64)
    nf = ROPE_A // 4
    inv = np.exp(-math.log(ROPE_BASE) * np.arange(nf, dtype=np.float64) / nf)
    ang = np.concatenate([row[:, None] * inv, col[:, None] * inv], axis=-1)
    cos, sin = np.cos(ang), np.sin(ang)
    n = cos.shape[0]
    ones, zeros = (lambda k: np.ones((n, k))), (lambda k: np.zeros((n, k)))
    mla = (np.concatenate([ones(64), cos, cos, ones(32)], axis=1),
           np.concatenate([zeros(64), -sin, zeros(48)], axis=1),
           np.concatenate([zeros(80), sin, zeros(32)], axis=1))
    dif = (np.tile(np.concatenate([cos, cos], axis=1), (1, 4)),
           np.tile(np.concatenate([-sin, zeros(16)], axis=1), (1, 4)),
           np.tile(np.concatenate([zeros(16), sin], axis=1), (1, 4)))
    as_f32 = lambda ts: tuple(jnp.asarray(t.astype(np.float32)) for t in ts)
    return as_f32(mla), as_f32(dif)


def kernel(x_prompt, x_sample, cache_mla_ckv, cache_mla_krope, cache_diff_k, cache_diff_v, state_mlstm_C,
           state_mlstm_n, state_mlstm_m, c, c_ctx, w_mod, b_mod, norm_g, w_in, mla_q_norm, w_uq, mla_kv_norm,
           w_ukv, gmlp_v_norm, gmlp_w_s, gmlp_b_s, mlstm_gate_bias, mlstm_head_norm, diff_lambda, diff_sub_norm,
           w_branch, w_merge, b_merge, w_out, w_ff1, w_ff2):
    bp, lp, _ = x_prompt.shape
    bs, ls, _ = x_sample.shape
    past = cache_mla_ckv.shape[2]

    cond = jnp.concatenate([c_ctx[None, :], c, jnp.zeros((16 - 1 - bs, D_MODEL), F32)], axis=0)
    mod = _modulation(cond, w_mod, b_mod).reshape(DEPTH, 16, 6, D_MODEL)
    rope_mla, rope_dif = _rope_tables(ls // GRID_W)

    row = lambda a: a[:, None, :]
    w_in_a = jax.vmap(_arrange_w_in)(w_in)
    wuq_a = jax.vmap(_arrange_w_uq)(w_uq)
    wukv_a = jax.vmap(_arrange_w_ukv)(w_ukv)
    qn, kvn, vn, hn, bm = row(mla_q_norm), row(mla_kv_norm), row(gmlp_v_norm), row(mlstm_head_norm), row(b_merge)
    ws = gmlp_w_s.astype(BF16)
    bias = jnp.repeat(jnp.swapaxes(gmlp_b_s, 1, 2), D_B // G_B, axis=2)
    gb = jnp.pad(mlstm_gate_bias.transpose(0, 2, 1, 3).reshape(DEPTH, 1, 4 * H_C),
                 ((0, 0), (0, 0), (0, ZG_W - 4 * H_C)))
    sn = row(jnp.tile(diff_sub_norm, (1, 2)))
    wm, wb, wo = w_merge.astype(BF16), w_branch.astype(BF16), w_out.astype(BF16)
    w1, w2 = w_ff1.astype(BF16), w_ff2.astype(BF16)

    ctx_kr = cache_mla_krope.transpose(0, 1, 3, 2)
    ctx_dk = cache_diff_k.transpose(0, 1, 3, 4, 5, 2).reshape(bs, DEPTH, D_D, past)
    ctx_dv = cache_diff_v.transpose(0, 1, 3, 4, 2).reshape(bs, DEPTH, D_D, past)
    c0 = state_mlstm_C.reshape(bs, DEPTH, N_UNITS, DH_C, DH_C)
    n0 = state_mlstm_n.reshape(bs, DEPTH, N_UNITS, DH_C)
    m0 = jnp.broadcast_to(state_mlstm_m.reshape(bs, DEPTH, N_UNITS, 1), (bs, DEPTH, N_UNITS, LANES))

    xp = x_prompt.reshape(bp * lp, D_MODEL)
    xs = x_sample.reshape(bs * ls, D_MODEL)
    ents = []
    diff_kv = None
    for l in range(DEPTH):
        lam_init = 0.8 - 0.6 * math.exp(-0.3 * l)
        for is_sample in (False, True):
            x = xs if is_sample else xp
            batch, seq = (bs, ls) if is_sample else (bp, lp)
            za, yb, zc, zg, dq, dk, dv = _in_proj(x, mod, norm_g, w_in_a, vn, ws, bias, l, seq, is_sample,
                                                  rope=rope_dif if is_sample else None)
            if is_sample:
                (ya,) = _mla(za, qn, kvn, wuq_a, wukv_a, l, batch, seq, rope=rope_mla, ctx=(cache_mla_ckv, ctx_kr))
                (yd,) = _diff(dq, dk, dv, diff_lambda, sn, l, batch, seq, lam_init, ctx=(ctx_dk, ctx_dv))
                (yc,) = _mlstm(zc, zg, gb, hn, l, batch, seq, state=(c0, n0, m0))
            else:
                ya, ckv, krt = _mla(za, qn, kvn, wuq_a, wukv_a, l, batch, seq)
                yd, *diff_kv = _diff(dq, dk, dv, diff_lambda, sn, l, batch, seq, lam_init, prev=diff_kv)
                yc, c_new, n_new, m_new = _mlstm(zc, zg, gb, hn, l, batch, seq)
                ents.append((
                    ckv.reshape(bp, lp, KV_LORA),
                    krt.transpose(0, 2, 1),
                    c_new.reshape(bp, 2, H_C, DH_C, DH_C),
                    n_new.reshape(bp, 2, H_C, DH_C),
                    m_new[:, :, 0].reshape(bp, 2, H_C)))
            x = _merge_ffn(x, mod, norm_g, (ya, yb, yc, yd), wm, bm, wb, wo, w1, w2, l, seq, is_sample)
            if is_sample:
                xs = x
            else:
                xp = x

    stack = lambda j: jnp.stack([e[j] for e in ents], axis=1)
    dkt, dvt = diff_kv
    new_diff_k = dkt.reshape(bp, DEPTH, H_D, 2, DH_D, lp).transpose(0, 1, 5, 2, 3, 4)
    new_diff_v = dvt.reshape(bp, DEPTH, H_D, 2 * DH_D, lp).transpose(0, 1, 4, 2, 3)
    return (xp.reshape(bp, lp, D_MODEL), xs.reshape(bs, ls, D_MODEL),
            stack(0), stack(1), new_diff_k, new_diff_v, stack(2), stack(3), stack(4))
```

```python
import functools
import math

import jax
import jax.numpy as jnp
import numpy as np
from jax import lax
from jax.experimental import pallas as pl
from jax.experimental.pallas import tpu as pltpu

F32 = jnp.float32
BF16 = jnp.bfloat16

D_MODEL = 1024
DEPTH = 2
GRID_W = 64
N_BRANCH = 4
MIX_W = 256
H_A, NOPE_A, ROPE_A, VH_A = 4, 64, 32, 64
Q_LORA, KV_LORA = 256, 128
D_B, G_B, CHUNK_B = 256, 4, 128
H_C, DH_C = 4, 64
D_C = H_C * DH_C
H_D, DH_D = 4, 32
D_D = H_D * 2 * DH_D
D_FF = 4 * D_MODEL
IN_A = Q_LORA + KV_LORA + ROPE_A
IN_B = 2 * D_B
IN_C = 4 * D_C + 4 * H_C
IN_D = 3 * D_D
ROPE_BASE = 10000.0
EPS = 1e-6
MLA_SCALE = (NOPE_A + ROPE_A) ** -0.5
DIFF_SCALE = DH_D ** -0.5

MIX_DTYPE = BF16
LANES = 128
MLSTM_CHUNK = 128
VMEM_LIMIT = 56 * 1024 * 1024
NEG_INF = float("-inf")

ZA_W = 512
ZG_W = 128
Z_OFF_A = 0
Z_OFF_B = Z_OFF_A + ZA_W
Z_OFF_C = Z_OFF_B + IN_B
Z_OFF_G = Z_OFF_C + 4 * D_C
Z_OFF_D = Z_OFF_G + ZG_W
Z_W = Z_OFF_D + IN_D


def _rms(x, g):
    return x * lax.rsqrt(jnp.mean(x * x, axis=-1, keepdims=True) + EPS) * g


def _sigmoid(x):
    return 1.0 / (1.0 + jnp.exp(-x))


def _log_sigmoid(x):
    return jnp.minimum(x, 0.0) - jnp.log1p(jnp.exp(-jnp.abs(x)))


def _dot(a, b):
    return jnp.dot(a, b, preferred_element_type=F32)


def _dot_nt(a, b):
    return lax.dot_general(a, b, (((1,), (1,)), ((), ())), preferred_element_type=F32)


def _lane_group_mask(width, group, index):
    lane = lax.broadcasted_iota(jnp.int32, (1, width), 1)
    return (lane >= index * group) & (lane < (index + 1) * group)


def _mod_kernel(cond_ref, w_ref, b_ref, o_ref):
    c = cond_ref[...]
    s = c * _sigmoid(c)
    o_ref[...] = _dot(s.astype(BF16), w_ref[...].astype(BF16)) + b_ref[...]


def _modulation(cond, w_mod, b_mod):
    rows = cond.shape[0]
    nb = 2048
    return pl.pallas_call(
        _mod_kernel,
        grid=(DEPTH, 6 * D_MODEL // nb),
        in_specs=[
            pl.BlockSpec((rows, D_MODEL), lambda l, j: (0, 0)),
            pl.BlockSpec((None, D_MODEL, nb), lambda l, j: (l, 0, j)),
            pl.BlockSpec((None, 1, nb), lambda l, j: (l, 0, j)),
        ],
        out_specs=pl.BlockSpec((None, rows, nb), lambda l, j: (l, 0, j)),
        out_shape=jax.ShapeDtypeStruct((DEPTH, rows, 6 * D_MODEL), F32),
        name="modulation",
    )(cond, w_mod, b_mod.reshape(DEPTH, 1, 6 * D_MODEL))


TOKEN_TILE = 512
IN_PROJ_TILE = 1024


def _layer_block(shape, layer):
    shape = tuple(shape)
    return pl.BlockSpec((None,) + shape, lambda *_: (layer,) + (0,) * len(shape), pipeline_mode=pl.Buffered(1))


def _mod_spec(layer, tm, seq, per_seq):
    def index(i):
        return (layer, 1 + (i * tm) // seq if per_seq else 0, 0, 0)
    return pl.BlockSpec((None, None, 6, D_MODEL), index)


def _gmlp_tile(zb, vn, ws_ref, bias):
    tm = zb.shape[0]
    v = _rms(zb[:, D_B:], vn).astype(BF16)
    gmasks = [_lane_group_mask(D_B, D_B // G_B, g) for g in range(G_B)]
    out = []
    for ch in range(tm // CHUNK_B):
        vc = v[ch * CHUNK_B:(ch + 1) * CHUNK_B, :]
        mixed = bias
        for g in range(G_B):
            mixed = mixed + jnp.where(gmasks[g], _dot(ws_ref[g], vc), 0.0)
        out.append(zb[ch * CHUNK_B:(ch + 1) * CHUNK_B, :D_B] * mixed)
    return jnp.concatenate(out, axis=0)


def _in_kernel(*refs, rotary):
    if rotary:
        x_ref, mod_ref, g_ref, w_ref, vn_ref, ws_ref, bias_ref, tc_ref, ts1_ref, ts2_ref = refs[:10]
    else:
        x_ref, mod_ref, g_ref, w_ref, vn_ref, ws_ref, bias_ref = refs[:7]
    za_ref, yb_ref, zc_ref, zg_ref, dq_ref, dk_ref, dv_ref = refs[-7:]
    x = x_ref[...]
    h = _rms(x, g_ref[0:1, :]) * (1.0 + mod_ref[1:2, :]) + mod_ref[0:1, :]
    hb = h.astype(BF16)
    za_ref[...] = _dot(hb, w_ref[:, Z_OFF_A:Z_OFF_B])
    zc_ref[...] = _dot(hb, w_ref[:, Z_OFF_C:Z_OFF_G])
    zg_ref[...] = _dot(hb, w_ref[:, Z_OFF_G:Z_OFF_D])
    for j, d_ref in enumerate((dq_ref, dk_ref, dv_ref)):
        d = _dot(hb, w_ref[:, Z_OFF_D + j * D_D:Z_OFF_D + (j + 1) * D_D])
        if rotary and j < 2:
            tc, ts1, ts2 = tc_ref[...], ts1_ref[...], ts2_ref[...]
            d = jnp.concatenate([_rope128(d[:, c * LANES:(c + 1) * LANES], tc, ts1, ts2) for c in range(D_D // LANES)],
                                axis=1)
        d_ref[...] = d
    yb = _gmlp_tile(_dot(hb, w_ref[:, Z_OFF_B:Z_OFF_C]), vn_ref[...], ws_ref, bias_ref[...])
    yb_ref[...] = yb.astype(yb_ref.dtype)


def _in_proj(x, mod, g, w, vn, ws, bias, layer, seq, per_seq, rope=None):
    t = x.shape[0]
    tm = IN_PROJ_TILE
    widths = (ZA_W, MIX_W, 4 * D_C, ZG_W, D_D, D_D, D_D)
    in_specs = [
        pl.BlockSpec((tm, D_MODEL), lambda i: (i, 0)),
        _mod_spec(layer, tm, seq, per_seq),
        _layer_block((4, D_MODEL), layer),
        _layer_block((D_MODEL, Z_W), layer),
        _layer_block((1, D_B), layer),
        _layer_block((G_B, CHUNK_B, CHUNK_B), layer),
        _layer_block((CHUNK_B, D_B), layer),
    ]
    args = [x, mod, g, w, vn, ws, bias]
    if rope is not None:
        in_specs += [pl.BlockSpec((tm, LANES), lambda i: (i % (seq // tm), 0))] * 3
        args += list(rope)
    return pl.pallas_call(
        functools.partial(_in_kernel, rotary=rope is not None),
        grid=(t // tm,),
        in_specs=in_specs,
        out_specs=[pl.BlockSpec((tm, w_), lambda i: (i, 0)) for w_ in widths],
        out_shape=[jax.ShapeDtypeStruct((t, w_), MIX_DTYPE if j == 1 else F32) for j, w_ in enumerate(widths)],
        compiler_params=pltpu.CompilerParams(vmem_limit_bytes=VMEM_LIMIT),
        name="in_proj",
    )(*args)


LOG2E = math.log2(math.e)
ATTN_GROUP_TOKENS = 1024


def _with_ones(v, lo_mask, keep_lo):
    lane = lax.broadcasted_iota(jnp.int32, (1, LANES), 1)
    if keep_lo:
        return jnp.where(lane == 64, 1.0, jnp.where(lo_mask, v, 0.0))
    return jnp.where(lane == 0, 1.0, jnp.where(lo_mask, 0.0, v))


def _pair_normalise(o_even, o_odd, lo_mask):
    return jnp.where(lo_mask, o_even / o_even[:, 64:65], o_odd / o_odd[:, 0:1])


def _rope128(x, tc, ts1, ts2):
    return x * tc + pltpu.roll(x, LANES - 16, 1) * ts1 + pltpu.roll(x, 16, 1) * ts2


def _mla_kernel(*refs, seq, ng, has_ctx, tq):
    rows_all = ng * seq
    if has_ctx:
        (za_ref, qn_ref, kvn_ref, wuq_ref, wukv_ref, tc_ref, ts1_ref, ts2_ref, cckv_ref, ckr_ref,
         ya_ref, q_s, k_s, v_s, ckv_s, kr_s) = refs
        past = cckv_ref.shape[0]
    else:
        (za_ref, qn_ref, kvn_ref, wuq_ref, wukv_ref, ya_ref, ckv_ref, krt_ref, q_s, k_s, v_s, ckv_s, kr_s) = refs
        past = 0
    lk = past + seq
    za = za_ref[...]
    cq = _rms(za[:, :Q_LORA], qn_ref[...])
    qh = _dot(cq.astype(BF16), wuq_ref[:, 0:(2 if has_ctx else 1) * H_A * LANES])
    ckv = _rms(za[:, Q_LORA:Q_LORA + KV_LORA], kvn_ref[...])
    kr = za[:, Q_LORA + KV_LORA:ZA_W]
    if has_ctx:
        tc, ts1, ts2 = tc_ref[...], ts1_ref[...], ts2_ref[...]
        kr = _rope128(kr, tc, ts1, ts2)
        ckv_s[0:past, :] = cckv_ref[...]
        krt = jnp.concatenate([jnp.zeros((64, past), F32), ckr_ref[...], jnp.zeros((LANES - 64 - ROPE_A, past), F32)],
                              axis=0)
        kr_s[0:past, :] = krt.T
    else:
        ckv_ref[...] = ckv
        for g in range(ng):
            krt_ref[g] = kr[g * seq:(g + 1) * seq, :].T[64:64 + ROPE_A, :]
    ckv_s[past:past + rows_all, :] = ckv
    kr_s[past:past + rows_all, :] = kr
    kv = _dot(ckv_s[...].astype(BF16), wukv_ref[...])
    kr_all = kr_s[...]
    lo = _lane_group_mask(LANES, 64, 0)
    for h in range(H_A):
        qg = qh[:, h * LANES:(h + 1) * LANES]
        if has_ctx:
            qg = qg * tc + qh[:, (H_A + h) * LANES:(H_A + h + 1) * LANES] * (ts1 + ts2)
        q_s[h] = (qg * (MLA_SCALE * LOG2E)).astype(BF16)
        k_s[h] = (kv[:, h * LANES:(h + 1) * LANES] + kr_all).astype(BF16)
        v_s[h] = _with_ones(kv[:, (H_A + h) * LANES:(H_A + h + 1) * LANES], lo, h % 2 == 0).astype(BF16)

    def body(i, carry):
        rows = pl.ds(pl.multiple_of(i * tq, tq), tq)
        keys = pl.ds(pl.multiple_of(((i * tq) // seq) * seq, seq), lk)
        s = [_dot_nt(q_s[h, rows, :], k_s[h, keys, :]) for h in range(H_A)]
        o = [_dot(jnp.exp2(s[h] - jnp.max(s[h], axis=-1, keepdims=True)).astype(BF16), v_s[h, keys, :])
             for h in range(H_A)]
        for pair in range(H_A // 2):
            ya_ref[rows, pair * LANES:(pair + 1) * LANES] = _pair_normalise(o[2 * pair], o[2 * pair + 1],
                                                                            lo).astype(ya_ref.dtype)
        return carry

    lax.fori_loop(0, rows_all // tq, body, 0, unroll=True)


def _ctx_block(rows, cols, layer):
    return pl.BlockSpec((None, None, rows, cols), lambda b: (b, layer, 0, 0))


def _mla(za, qn, kvn, wuq, wukv, layer, batch, seq, rope=None, ctx=None):
    has_ctx = ctx is not None
    past = ctx[0].shape[2] if has_ctx else 0
    ng = 1 if has_ctx else ATTN_GROUP_TOKENS // seq
    rows = ng * seq
    keys = past + rows
    tq = min(seq, 256)
    full = lambda shape: pl.BlockSpec(shape, lambda b: (0,) * len(shape))
    in_specs = [
        pl.BlockSpec((rows, ZA_W), lambda b: (b, 0)),
        _layer_block((1, Q_LORA), layer), _layer_block((1, KV_LORA), layer),
        _layer_block((Q_LORA, 2 * H_A * LANES), layer), _layer_block((KV_LORA, 2 * H_A * LANES), layer),
    ]
    args = [za, qn, kvn, wuq, wukv]
    out_specs = [pl.BlockSpec((rows, MIX_W), lambda b: (b, 0))]
    out_shape = [jax.ShapeDtypeStruct((batch * seq, MIX_W), MIX_DTYPE)]
    if has_ctx:
        in_specs += [full((seq, LANES))] * 3
        in_specs += [_ctx_block(past, KV_LORA, layer), _ctx_block(ROPE_A, past, layer)]
        args += list(rope) + list(ctx)
    else:
        out_specs += [pl.BlockSpec((rows, KV_LORA), lambda b: (b, 0)),
                      pl.BlockSpec((ng, ROPE_A, seq), lambda b: (b, 0, 0))]
        out_shape += [jax.ShapeDtypeStruct((batch * seq, KV_LORA), F32),
                      jax.ShapeDtypeStruct((batch, ROPE_A, seq), F32)]
    return pl.pallas_call(
        functools.partial(_mla_kernel, seq=seq, ng=ng, has_ctx=has_ctx, tq=tq),
        grid=(batch // ng,),
        in_specs=in_specs,
        out_specs=out_specs,
        out_shape=out_shape,
        scratch_shapes=[
            pltpu.VMEM((H_A, rows, LANES), BF16),
            pltpu.VMEM((H_A, keys, LANES), BF16),
            pltpu.VMEM((H_A, keys, LANES), BF16),
            pltpu.VMEM((keys, LANES), F32),
            pltpu.VMEM((keys, LANES), F32),
        ],
        compiler_params=pltpu.CompilerParams(vmem_limit_bytes=VMEM_LIMIT),
        name="mla",
    )(*args)


def _diff_kernel(*refs, seq, ng, has_ctx, n_prev, tq, lam_init):
    rows_all = ng * seq
    if has_ctx:
        (dq_ref, dk_ref, dv_ref, lam_ref, sn_ref, ck_ref, cv_ref, yd_ref, q_s, k_s, v_s) = refs
        past = ck_ref.shape[1]
    else:
        dq_ref, dk_ref, dv_ref, lam_ref, sn_ref = refs[:5]
        yd_ref, dkt_ref, dvt_ref, q_s, k_s, v_s = refs[-6:]
        past = 0
        for g in range(ng):
            if n_prev:
                dkt_ref[g, 0:n_prev] = refs[5][g]
                dvt_ref[g, 0:n_prev] = refs[6][g]
            dkt_ref[g, n_prev] = dk_ref[g * seq:(g + 1) * seq, :].T
            dvt_ref[g, n_prev] = dv_ref[g * seq:(g + 1) * seq, :].T
    lk = past + seq
    lo = _lane_group_mask(LANES, 64, 0)
    if has_ctx:
        k_s[0:past, :] = ck_ref[...].T.astype(BF16)
        cv = cv_ref[...].T
    for half in range(2):
        cols = slice(half * LANES, (half + 1) * LANES)
        q, k, v = dq_ref[:, cols], dk_ref[:, cols], dv_ref[:, cols]
        if has_ctx:
            v_s[2 * half, 0:past, :] = _with_ones(cv[:, cols], lo, True).astype(BF16)
            v_s[2 * half + 1, 0:past, :] = _with_ones(cv[:, cols], lo, False).astype(BF16)
        q_s[:, cols] = (q * (DIFF_SCALE * LOG2E)).astype(BF16)
        k_s[past:past + rows_all, cols] = k.astype(BF16)
        v_s[2 * half, past:past + rows_all, :] = _with_ones(v, lo, True).astype(BF16)
        v_s[2 * half + 1, past:past + rows_all, :] = _with_ones(v, lo, False).astype(BF16)

    lam = lam_ref[...]
    lam_val = (jnp.exp(jnp.sum(lam[0:1] * lam[1:2], axis=-1, keepdims=True))
               - jnp.exp(jnp.sum(lam[2:3] * lam[3:4], axis=-1, keepdims=True)) + lam_init)
    comp_masks = [jnp.where(_lane_group_mask(D_D, DH_D, j), 1.0, 0.0).astype(BF16) for j in range(2 * H_D)]
    sn = sn_ref[...]

    def body(i, carry):
        rows = pl.ds(pl.multiple_of(i * tq, tq), tq)
        qb = q_s[rows, :]
        keys = pl.ds(pl.multiple_of(((i * tq) // seq) * seq, seq), lk)
        k_all = k_s[keys, :]
        score = lambda h: [_dot_nt(qb * comp_masks[2 * h + comp], k_all) for comp in range(2)]
        s_next = score(0)
        o = []
        for h in range(H_D):
            s = s_next
            if h + 1 < H_D:
                s_next = score(h + 1)
            o1, o2 = [_dot(jnp.exp2(sc - jnp.max(sc, axis=-1, keepdims=True)).astype(BF16), v_s[h, keys, :])
                      for sc in s]
            ll = 64 if h % 2 == 0 else 0
            o.append(o1 / o1[:, ll:ll + 1] - lam_val * (o2 / o2[:, ll:ll + 1]))
        for pair in range(H_D // 2):
            acc = jnp.where(lo, o[2 * pair], o[2 * pair + 1])
            sq = acc * acc
            ss_lo = jnp.sum(jnp.where(lo, sq, 0.0), axis=-1, keepdims=True)
            ss_hi = jnp.sum(jnp.where(lo, 0.0, sq), axis=-1, keepdims=True)
            r = jnp.where(lo, lax.rsqrt(ss_lo / (2 * DH_D) + EPS), lax.rsqrt(ss_hi / (2 * DH_D) + EPS))
            yd_ref[rows, pair * LANES:(pair + 1) * LANES] = (acc * r * sn * (1.0 - lam_init)).astype(yd_ref.dtype)
        return carry

    lax.fori_loop(0, rows_all // tq, body, 0, unroll=True)


def _diff(dq, dk, dv, lam, sn, layer, batch, seq, lam_init, ctx=None, prev=None):
    has_ctx = ctx is not None
    n_prev = prev[0].shape[1] if prev is not None else 0
    past = ctx[0].shape[3] if has_ctx else 0
    ng = 1
    rows = ng * seq
    keys = past + rows
    tq = min(seq, 256)
    in_specs = [pl.BlockSpec((rows, D_D), lambda b: (b, 0))] * 3
    in_specs += [_layer_block((4, DH_D), layer), _layer_block((1, LANES), layer)]
    args = [dq, dk, dv, lam, sn]
    if has_ctx:
        in_specs += [_ctx_block(D_D, past, layer)] * 2
        args += list(ctx)
    if n_prev:
        in_specs += [pl.BlockSpec((ng, n_prev, D_D, seq), lambda b: (b, 0, 0, 0))] * 2
        args += list(prev)
    out_specs = [pl.BlockSpec((rows, MIX_W), lambda b: (b, 0))]
    out_shape = [jax.ShapeDtypeStruct((batch * seq, MIX_W), MIX_DTYPE)]
    if not has_ctx:
        out_specs += [pl.BlockSpec((ng, n_prev + 1, D_D, seq), lambda b: (b, 0, 0, 0))] * 2
        out_shape += [jax.ShapeDtypeStruct((batch, n_prev + 1, D_D, seq), F32)] * 2
    return pl.pallas_call(
        functools.partial(_diff_kernel, seq=seq, ng=ng, has_ctx=has_ctx, n_prev=n_prev, tq=tq, lam_init=lam_init),
        grid=(batch // ng,),
        in_specs=in_specs,
        out_specs=out_specs,
        out_shape=out_shape,
        scratch_shapes=[
            pltpu.VMEM((rows, D_D), BF16),
            pltpu.VMEM((keys, D_D), BF16),
            pltpu.VMEM((H_D, keys, LANES), BF16),
        ],
        compiler_params=pltpu.CompilerParams(vmem_limit_bytes=VMEM_LIMIT),
        name="diff_attn",
    )(*args)


def _split3(x):
    hi = x.astype(BF16)
    r1 = x - hi.astype(F32)
    mid = r1.astype(BF16)
    lo = (r1 - mid.astype(F32)).astype(BF16)
    return hi, mid, lo


ST_ROWS = DH_C + 8
N_UNITS = 2 * H_C
MLSTM_GROUP_TOKENS = 1024


def _mlstm_kernel(*refs, seq, ng, has_state):
    ck = MLSTM_CHUNK
    nc = seq // ck
    if has_state:
        (zc_ref, zg_ref, gb_ref, hn_ref, c0_ref, n0_ref, m0_ref,
         yc_ref, k_s, qt_s, kt_s, vt_s, rcc_s, ca_s, wi_s, en_s, ws_s, wo_s, st_s, ht_s) = refs
    else:
        (zc_ref, zg_ref, gb_ref, hn_ref,
         yc_ref, cout_ref, nout_ref, mout_ref,
         k_s, qt_s, kt_s, vt_s, rcc_s, ca_s, wi_s, en_s, ws_s, wo_s, st_s, ht_s) = refs
    eye = jnp.where(lax.broadcasted_iota(jnp.int32, (DH_C, DH_C), 0) == lax.broadcasted_iota(jnp.int32, (DH_C, DH_C), 1),
                    1.0, 0.0).astype(BF16)

    def transpose_exact(a):
        return sum(_dot_nt(eye, part) for part in _split3(a))

    def chunk_rows(g, c):
        return slice(g * seq + c * ck, g * seq + (c + 1) * ck)

    for g in range(ng):
        for c in range(nc):
            rows = chunk_rows(g, c)
            qt_s[g, c] = zc_ref[rows, 0:D_C].T.astype(BF16)
            kk = zc_ref[rows, D_C:2 * D_C] * (DH_C ** -0.5)
            k_s[rows, :] = kk.astype(BF16)
            kt_s[g, c] = kk.T.astype(BF16)
            vt_s[g, c] = zc_ref[rows, 2 * D_C:3 * D_C].T.astype(BF16)

    gates = zg_ref[...] + gb_ref[...]
    r_i = lax.broadcasted_iota(jnp.int32, (ck, ck), 0)
    c_i = lax.broadcasted_iota(jnp.int32, (ck, ck), 1)
    sum_fw = jnp.where(r_i <= c_i, 1.0, 0.0).astype(BF16)
    sum_bw = jnp.where(r_i >= c_i, 1.0, 0.0).astype(BF16)
    row8 = lax.broadcasted_iota(jnp.int32, (N_UNITS, ck), 0)
    fw8 = row8 < H_C
    fw81 = fw8[:, 0:1]
    b_steps, r_steps = [], []
    for g in range(ng):
        i_rows, b_rows = [], []
        for c in range(nc):
            rows = chunk_rows(g, c)
            g_t = gates[rows, :].T
            i8 = g_t[0:N_UNITS, :]
            p0, p1, p2 = _split3(_log_sigmoid(g_t[N_UNITS:2 * N_UNITS, :]))
            b_fw = _dot(p0, sum_fw) + _dot(p1, sum_fw) + _dot(p2, sum_fw)
            b_bw = _dot(p0, sum_bw) + _dot(p1, sum_bw) + _dot(p2, sum_bw)
            b8 = jnp.where(fw8, b_fw, b_bw)
            rcc_s[rows, :] = jnp.concatenate([i8 - b8, jnp.zeros((ck - N_UNITS, ck), F32)], axis=0).T
            i_rows.append(i8)
            b_rows.append(b8)
        b_steps.append([jnp.where(fw8, b_rows[i], b_rows[nc - 1 - i]) for i in range(nc)])
        r_steps.append([jnp.where(fw8, i_rows[i], i_rows[nc - 1 - i]) - b_steps[g][i] for i in range(nc)])

    pm = sm = jnp.concatenate([r for g in range(ng) for r in r_steps[g]], axis=0)
    lane_all = lax.broadcasted_iota(jnp.int32, pm.shape, 1)
    sh = 1
    while sh < ck:
        pm = jnp.maximum(pm, jnp.where(lane_all >= sh, pltpu.roll(pm, sh, 1), NEG_INF))
        sm = jnp.maximum(sm, jnp.where(lane_all < ck - sh, pltpu.roll(sm, ck - sh, 1), NEG_INF))
        sh *= 2
    m_last = []
    for g in range(ng):
        m_prev = m0_ref[g, :, 0:1] if has_state else jnp.zeros((N_UNITS, 1), F32)
        for i in range(nc):
            b8, r8 = b_steps[g][i], r_steps[g][i]
            at = (g * nc + i) * N_UNITS
            pm8, sm8 = pm[at:at + N_UNITS, :], sm[at:at + N_UNITS, :]
            cm = jnp.where(fw8, pm8, sm8)
            b_t = jnp.where(fw81, b8[:, ck - 1:ck], b8[:, 0:1])
            cm_end = jnp.where(fw81, pm8[:, ck - 1:ck], sm8[:, 0:1])
            inter = b8 + m_prev
            mt = jnp.maximum(inter, b8 + cm)
            ca_s[g, i] = b8 - mt
            wi_s[g, i] = jnp.exp(inter - mt)
            en_s[g, i] = jnp.exp(-mt)
            m_new = b_t + jnp.maximum(m_prev, cm_end)
            ws_s[g, i] = jnp.exp(b_t + r8 - m_new)
            wo_s[g, i] = jnp.broadcast_to(jnp.exp(b_t + m_prev - m_new), (N_UNITS, ck))
            m_prev = m_new
        m_last.append(m_prev)

    if has_state:
        for g in range(ng):
            for u in range(N_UNITS):
                st_s[g * N_UNITS + u, 0:DH_C, :] = transpose_exact(c0_ref[g, u])
                st_s[g * N_UNITS + u, DH_C:ST_ROWS, :] = jnp.broadcast_to(n0_ref[g, u:u + 1, :],
                                                                         (ST_ROWS - DH_C, DH_C))
    else:
        st_s[...] = jnp.zeros_like(st_s)

    hmask_bf = [jnp.where(_lane_group_mask(D_C, DH_C, h), 1.0, 0.0).astype(BF16) for h in range(H_C)]
    src_ok = (r_i <= c_i, r_i >= c_i)
    hs = [slice(h * DH_C, (h + 1) * DH_C) for h in range(H_C)]
    units = [(g, d, h) for g in range(ng) for d in range(2) for h in range(H_C)]

    def body(i, carry):
        chunks = (i, nc - 1 - i)
        ca, wi, en, ws, wo = [[ref[g, i] for g in range(ng)] for ref in (ca_s, wi_s, en_s, ws_s, wo_s)]
        q_t = [[qt_s[g, c] for c in chunks] for g in range(ng)]
        k_t = [[kt_s[g, c] for c in chunks] for g in range(ng)]
        v_t = [[vt_s[g, c] for c in chunks] for g in range(ng)]
        rows = [[pl.ds(pl.multiple_of(g * seq + c * ck, ck), ck) for c in chunks] for g in range(ng)]
        sts, qcts, states, upds, pts = [], [], [], [], []
        for g, d, h in units:
            sts.append(_dot(k_s[rows[g][d], :] * hmask_bf[h], q_t[g][d]))
            states.append(st_s[g * N_UNITS + d * H_C + h])
            qcts.append(_dot(states[-1].astype(BF16), q_t[g][d][hs[h], :]))
        for g, d, h in units:
            u = d * H_C + h
            w_s = ws[g][u:u + 1, :]
            vw = jnp.concatenate([v_t[g][d][hs[h], :].astype(F32) * w_s, jnp.broadcast_to(w_s, (8, ck))], axis=0)
            upds.append(_dot_nt(vw.astype(BF16), k_t[g][d][hs[h], :]))
        for j, (g, d, h) in enumerate(units):
            u = d * H_C + h
            decay = jnp.exp(rcc_s[rows[g][d], u:u + 1] + ca[g][u:u + 1, :])
            pts.append(sts[j] * jnp.where(src_ok[d], decay, 0.0))
        nums = [_dot(v_t[g][d][hs[h], :], pts[j].astype(BF16)) for j, (g, d, h) in enumerate(units)]
        for j, (g, d, h) in enumerate(units):
            u = d * H_C + h
            w_i = wi[g][u:u + 1, :]
            den = w_i * qcts[j][DH_C:DH_C + 1, :] + jnp.sum(pts[j], axis=0, keepdims=True)
            num = w_i * qcts[j][0:DH_C, :] + nums[j]
            ht_s[g, d, chunks[d], hs[h], :] = num / jnp.maximum(jnp.abs(den), en[g][u:u + 1, :])
            st_s[g * N_UNITS + u] = wo[g][u:u + 1, 0:DH_C] * states[j] + upds[j]
        return carry

    lax.fori_loop(0, nc, body, 0)

    hn = hn_ref[...]
    for g in range(ng):
        for c in range(nc):
            rows = chunk_rows(g, c)
            h_t = ht_s[g, 0, c] + ht_s[g, 1, c]
            parts = []
            for h in range(H_C):
                x = h_t[hs[h], :]
                parts.append(x * lax.rsqrt(jnp.mean(x * x, axis=0, keepdims=True) + EPS))
            h_n = jnp.concatenate(parts, axis=0).T
            yc_ref[rows, :] = (_sigmoid(zc_ref[rows, 3 * D_C:4 * D_C]) * (h_n * hn)).astype(yc_ref.dtype)

    if not has_state:
        for g in range(ng):
            for u in range(N_UNITS):
                cout_ref[g, u] = transpose_exact(st_s[g * N_UNITS + u, 0:DH_C, :])
                nout_ref[g, u:u + 1, :] = st_s[g * N_UNITS + u, DH_C:DH_C + 1, :]
            mout_ref[g] = jnp.broadcast_to(m_last[g], (N_UNITS, LANES))


def _mlstm(zc, zg, gb, hn, layer, batch, seq, state=None):
    has_state = state is not None
    nc = seq // MLSTM_CHUNK
    ng = max(2, MLSTM_GROUP_TOKENS // seq)
    in_specs = [pl.BlockSpec((ng * seq, 4 * D_C), lambda b: (b, 0)), pl.BlockSpec((ng * seq, ZG_W), lambda b: (b, 0)),
                _layer_block((1, ZG_W), layer), _layer_block((1, D_C), layer)]
    args = [zc, zg, gb, hn]
    out_specs = [pl.BlockSpec((ng * seq, MIX_W), lambda b: (b, 0))]
    out_shape = [jax.ShapeDtypeStruct((batch * seq, MIX_W), MIX_DTYPE)]
    if has_state:
        in_specs += [pl.BlockSpec((ng, None, N_UNITS, DH_C, DH_C), lambda b: (b, layer, 0, 0, 0)),
                     pl.BlockSpec((ng, None, N_UNITS, DH_C), lambda b: (b, layer, 0, 0)),
                     pl.BlockSpec((ng, None, N_UNITS, LANES), lambda b: (b, layer, 0, 0))]
        args += list(state)
    else:
        out_specs += [pl.BlockSpec((ng, N_UNITS, DH_C, DH_C), lambda b: (b, 0, 0, 0)),
                      pl.BlockSpec((ng, N_UNITS, DH_C), lambda b: (b, 0, 0)),
                      pl.BlockSpec((ng, N_UNITS, LANES), lambda b: (b, 0, 0))]
        out_shape += [jax.ShapeDtypeStruct((batch, N_UNITS, DH_C, DH_C), F32),
                      jax.ShapeDtypeStruct((batch, N_UNITS, DH_C), F32),
                      jax.ShapeDtypeStruct((batch, N_UNITS, LANES), F32)]
    step = lambda: pltpu.VMEM((ng, nc, N_UNITS, MLSTM_CHUNK), F32)
    transposed = lambda: pltpu.VMEM((ng, nc, D_C, MLSTM_CHUNK), BF16)
    return pl.pallas_call(
        functools.partial(_mlstm_kernel, seq=seq, ng=ng, has_state=has_state),
        grid=(batch // ng,),
        in_specs=in_specs,
        out_specs=out_specs,
        out_shape=out_shape,
        scratch_shapes=[
            pltpu.VMEM((ng * seq, D_C), BF16),
            transposed(), transposed(), transposed(),
            pltpu.VMEM((ng * seq, ZG_W), F32),
            step(), step(), step(), step(), step(),
            pltpu.VMEM((ng * N_UNITS, ST_ROWS, DH_C), F32),
            pltpu.VMEM((ng, 2, nc, D_C, MLSTM_CHUNK), F32),
        ],
        compiler_params=pltpu.CompilerParams(vmem_limit_bytes=VMEM_LIMIT),
        name="mlstm",
    )(*args)


def _merge_kernel(x_ref, mod_ref, g_ref, ya_ref, yb_ref, yc_ref, yd_ref, wm_ref, bm_ref, wb_ref, wo_ref,
                  w1_ref, w2_ref, o_ref):
    x = x_ref[...]
    h = (_rms(x, g_ref[0:1, :]) * (1.0 + mod_ref[1:2, :]) + mod_ref[0:1, :]).astype(BF16)
    acc = jnp.zeros(x.shape, F32)
    for n, y_ref in enumerate((ya_ref, yb_ref, yc_ref, yd_ref)):
        cols = slice(n * D_MODEL, (n + 1) * D_MODEL)
        gate = _sigmoid(_dot(h, wm_ref[:, cols]) + bm_ref[:, cols])
        acc = acc + gate * _dot(y_ref[...].astype(BF16), wb_ref[n])
    y = _dot(acc.astype(BF16), wo_ref[...])
    x = x + mod_ref[2:3, :] * _rms(y, g_ref[1:2, :])
    h2 = (_rms(x, g_ref[2:3, :]) * (1.0 + mod_ref[4:5, :]) + mod_ref[3:4, :]).astype(BF16)
    f = jnp.zeros(x.shape, F32)
    for j in range(D_FF // D_MODEL):
        cols = slice(j * D_MODEL, (j + 1) * D_MODEL)
        a = jnp.maximum(_dot(h2, w1_ref[:, cols]), 0.0)
        f = f + _dot((a * a).astype(BF16), w2_ref[cols, :])
    o_ref[...] = x + mod_ref[5:6, :] * _rms(f, g_ref[3:4, :])


def _merge_ffn(x, mod, g, ys, wm, bm, wb, wo, w1, w2, layer, seq, per_seq):
    t = x.shape[0]
    tm = TOKEN_TILE
    full = lambda shape: _layer_block(shape, layer)
    tok = lambda w_: pl.BlockSpec((tm, w_), lambda i: (i, 0))
    return pl.pallas_call(
        _merge_kernel,
        grid=(t // tm,),
        in_specs=[tok(D_MODEL), _mod_spec(layer, tm, seq, per_seq),
                  full((4, D_MODEL)), tok(MIX_W), tok(MIX_W), tok(MIX_W), tok(MIX_W),
                  full((D_MODEL, N_BRANCH * D_MODEL)), full((1, N_BRANCH * D_MODEL)),
                  full((N_BRANCH, MIX_W, D_MODEL)), full((D_MODEL, D_MODEL)),
                  full((D_MODEL, D_FF)), full((D_FF, D_MODEL))],
        out_specs=tok(D_MODEL),
        out_shape=jax.ShapeDtypeStruct((t, D_MODEL), F32),
        compiler_params=pltpu.CompilerParams(vmem_limit_bytes=VMEM_LIMIT),
        name="merge_ffn",
    )(x, mod, g, *ys, wm, bm, wb, wo, w1, w2)


def _arrange_w_in(w):
    wt = w.T.astype(BF16)
    z = lambda n: jnp.zeros((n, D_MODEL), BF16)
    o_b, o_c = IN_A, IN_A + IN_B
    o_g, o_d = o_c + 4 * D_C, o_c + IN_C
    gates = wt[o_g:o_d].reshape(2, 2, H_C, D_MODEL).transpose(1, 0, 2, 3).reshape(4 * H_C, D_MODEL)
    return jnp.concatenate([
        wt[:Q_LORA + KV_LORA], z(64), wt[Q_LORA + KV_LORA:IN_A], z(32),
        wt[o_b:o_g], gates, z(ZG_W - 4 * H_C), wt[o_d:]], axis=0).T


def _arrange_w_uq(w):
    w = w.reshape(Q_LORA, H_A, NOPE_A + ROPE_A)
    half = ROPE_A // 2
    swapped = jnp.concatenate([jnp.zeros_like(w[:, :, :NOPE_A]), w[:, :, NOPE_A + half:], w[:, :, NOPE_A:NOPE_A + half]],
                              axis=-1)
    pad = lambda a: jnp.pad(a, ((0, 0), (0, 0), (0, LANES - NOPE_A - ROPE_A))).reshape(Q_LORA, H_A * LANES)
    return jnp.concatenate([pad(w), pad(swapped)], axis=1).astype(BF16)


def _arrange_w_ukv(w):
    w = w.reshape(KV_LORA, H_A, NOPE_A + VH_A)
    wk = jnp.pad(w[:, :, :NOPE_A], ((0, 0), (0, 0), (0, LANES - NOPE_A)))
    v = w[:, :, NOPE_A:]
    zero = jnp.zeros_like(v)
    even = jnp.concatenate([v, zero], axis=-1)
    odd = jnp.concatenate([zero, v], axis=-1)
    wv = jnp.where((jnp.arange(H_A) % 2 == 0)[None, :, None], even, odd)
    return jnp.concatenate([wk.reshape(KV_LORA, -1), wv.reshape(KV_LORA, -1)], axis=1).astype(BF16)


def _rope_tables(rows):
    row = np.repeat(np.arange(rows), GRID_W).astype(np.float64)
    col = np.tile(np.arange(GRID_W), rows).astype(np.float64)
    nf = ROPE_A // 4
    inv = np.exp(-math.log(ROPE_BASE) * np.arange(nf, dtype=np.float64) / nf)
    ang = np.concatenate([row[:, None] * inv, col[:, None] * inv], axis=-1)
    cos, sin = np.cos(ang), np.sin(ang)
    n = cos.shape[0]
    ones, zeros = (lambda k: np.ones((n, k))), (lambda k: np.zeros((n, k)))
    mla = (np.concatenate([ones(64), cos, cos, ones(32)], axis=1),
           np.concatenate([zeros(64), -sin, zeros(48)], axis=1),
           np.concatenate([zeros(80), sin, zeros(32)], axis=1))
    dif = (np.tile(np.concatenate([cos, cos], axis=1), (1, 4)),
           np.tile(np.concatenate([-sin, zeros(16)], axis=1), (1, 4)),
           np.tile(np.concatenate([zeros(16), sin], axis=1), (1, 4)))
    as_f32 = lambda ts: tuple(jnp.asarray(t.astype(np.float32)) for t in ts)
    return as_f32(mla), as_f32(dif)


def kernel(x_prompt, x_sample, cache_mla_ckv, cache_mla_krope, cache_diff_k, cache_diff_v, state_mlstm_C,
           state_mlstm_n, state_mlstm_m, c, c_ctx, w_mod, b_mod, norm_g, w_in, mla_q_norm, w_uq, mla_kv_norm,
           w_ukv, gmlp_v_norm, gmlp_w_s, gmlp_b_s, mlstm_gate_bias, mlstm_head_norm, diff_lambda, diff_sub_norm,
           w_branch, w_merge, b_merge, w_out, w_ff1, w_ff2):
    bp, lp, _ = x_prompt.shape
    bs, ls, _ = x_sample.shape
    past = cache_mla_ckv.shape[2]

    cond = jnp.concatenate([c_ctx[None, :], c, jnp.zeros((16 - 1 - bs, D_MODEL), F32)], axis=0)
    mod = _modulation(cond, w_mod, b_mod).reshape(DEPTH, 16, 6, D_MODEL)
    rope_mla, rope_dif = _rope_tables(ls // GRID_W)

    row = lambda a: a[:, None, :]
    w_in_a = jax.vmap(_arrange_w_in)(w_in)
    wuq_a = jax.vmap(_arrange_w_uq)(w_uq)
    wukv_a = jax.vmap(_arrange_w_ukv)(w_ukv)
    qn, kvn, vn, hn, bm = row(mla_q_norm), row(mla_kv_norm), row(gmlp_v_norm), row(mlstm_head_norm), row(b_merge)
    ws = gmlp_w_s.astype(BF16)
    bias = jnp.repeat(jnp.swapaxes(gmlp_b_s, 1, 2), D_B // G_B, axis=2)
    gb = jnp.pad(mlstm_gate_bias.transpose(0, 2, 1, 3).reshape(DEPTH, 1, 4 * H_C),
                 ((0, 0), (0, 0), (0, ZG_W - 4 * H_C)))
    sn = row(jnp.tile(diff_sub_norm, (1, 2)))
    wm, wb, wo = w_merge.astype(BF16), w_branch.astype(BF16), w_out.astype(BF16)
    w1, w2 = w_ff1.astype(BF16), w_ff2.astype(BF16)

    ctx_kr = cache_mla_krope.transpose(0, 1, 3, 2)
    ctx_dk = cache_diff_k.transpose(0, 1, 3, 4, 5, 2).reshape(bs, DEPTH, D_D, past)
    ctx_dv = cache_diff_v.transpose(0, 1, 3, 4, 2).reshape(bs, DEPTH, D_D, past)
    c0 = state_mlstm_C.reshape(bs, DEPTH, N_UNITS, DH_C, DH_C)
    n0 = state_mlstm_n.reshape(bs, DEPTH, N_UNITS, DH_C)
    m0 = jnp.broadcast_to(state_mlstm_m.reshape(bs, DEPTH, N_UNITS, 1), (bs, DEPTH, N_UNITS, LANES))

    xp = x_prompt.reshape(bp * lp, D_MODEL)
    xs = x_sample.reshape(bs * ls, D_MODEL)
    ents = []
    diff_kv = None
    for l in range(DEPTH):
        lam_init = 0.8 - 0.6 * math.exp(-0.3 * l)
        for is_sample in (False, True):
            x = xs if is_sample else xp
            batch, seq = (bs, ls) if is_sample else (bp, lp)
            za, yb, zc, zg, dq, dk, dv = _in_proj(x, mod, norm_g, w_in_a, vn, ws, bias, l, seq, is_sample,
                                                  rope=rope_dif if is_sample else None)
            if is_sample:
                (ya,) = _mla(za, qn, kvn, wuq_a, wukv_a, l, batch, seq, rope=rope_mla, ctx=(cache_mla_ckv, ctx_kr))
                (yd,) = _diff(dq, dk, dv, diff_lambda, sn, l, batch, seq, lam_init, ctx=(ctx_dk, ctx_dv))
                (yc,) = _mlstm(zc, zg, gb, hn, l, batch, seq, state=(c0, n0, m0))
            else:
                ya, ckv, krt = _mla(za, qn, kvn, wuq_a, wukv_a, l, batch, seq)
                yd, *diff_kv = _diff(dq, dk, dv, diff_lambda, sn, l, batch, seq, lam_init, prev=diff_kv)
                yc, c_new, n_new, m_new = _mlstm(zc, zg, gb, hn, l, batch, seq)
                ents.append((
                    ckv.reshape(bp, lp, KV_LORA),
                    krt.transpose(0, 2, 1),
                    c_new.reshape(bp, 2, H_C, DH_C, DH_C),
                    n_new.reshape(bp, 2, H_C, DH_C),
                    m_new[:, :, 0].reshape(bp, 2, H_C)))
            x = _merge_ffn(x, mod, norm_g, (ya, yb, yc, yd), wm, bm, wb, wo, w1, w2, l, seq, is_sample)
            if is_sample:
                xs = x
            else:
                xp = x

    stack = lambda j: jnp.stack([e[j] for e in ents], axis=1)
    dkt, dvt = diff_kv
    new_diff_k = dkt.reshape(bp, DEPTH, H_D, 2, DH_D, lp).transpose(0, 1, 5, 2, 3, 4)
    new_diff_v = dvt.reshape(bp, DEPTH, H_D, 2 * DH_D, lp).transpose(0, 1, 4, 2, 3)
    return (xp.reshape(bp, lp, D_MODEL), xs.reshape(bs, ls, D_MODEL),
            stack(0), stack(1), new_diff_k, new_diff_v, stack(2), stack(3), stack(4))
```

```python
import functools
import math

import jax
import jax.numpy as jnp
import numpy as np
from jax import lax
from jax.experimental import pallas as pl
from jax.experimental.pallas import tpu as pltpu

F32 = jnp.float32
BF16 = jnp.bfloat16

D_MODEL = 1024
DEPTH = 2
GRID_W = 64
N_BRANCH = 4
MIX_W = 256
H_A, NOPE_A, ROPE_A, VH_A = 4, 64, 32, 64
Q_LORA, KV_LORA = 256, 128
D_B, G_B, CHUNK_B = 256, 4, 128
H_C, DH_C = 4, 64
D_C = H_C * DH_C
H_D, DH_D = 4, 32
D_D = H_D * 2 * DH_D
D_FF = 4 * D_MODEL
IN_A = Q_LORA + KV_LORA + ROPE_A
IN_B = 2 * D_B
IN_C = 4 * D_C + 4 * H_C
IN_D = 3 * D_D
ROPE_BASE = 10000.0
EPS = 1e-6
MLA_SCALE = (NOPE_A + ROPE_A) ** -0.5
DIFF_SCALE = DH_D ** -0.5

MIX_DTYPE = BF16
LANES = 128
MLSTM_CHUNK = 128
VMEM_LIMIT = 56 * 1024 * 1024
NEG_INF = float("-inf")

ZA_W = 512
ZG_W = 128
Z_OFF_A = 0
Z_OFF_B = Z_OFF_A + ZA_W
Z_OFF_C = Z_OFF_B + IN_B
Z_OFF_G = Z_OFF_C + 4 * D_C
Z_OFF_D = Z_OFF_G + ZG_W
Z_W = Z_OFF_D + IN_D


def _rms(x, g):
    return x * lax.rsqrt(jnp.mean(x * x, axis=-1, keepdims=True) + EPS) * g


def _sigmoid(x):
    return 1.0 / (1.0 + jnp.exp(-x))


def _log_sigmoid(x):
    return jnp.minimum(x, 0.0) - jnp.log1p(jnp.exp(-jnp.abs(x)))


def _dot(a, b):
    return jnp.dot(a, b, preferred_element_type=F32)


def _dot_nt(a, b):
    return lax.dot_general(a, b, (((1,), (1,)), ((), ())), preferred_element_type=F32)


def _lane_group_mask(width, group, index):
    lane = lax.broadcasted_iota(jnp.int32, (1, width), 1)
    return (lane >= index * group) & (lane < (index + 1) * group)


def _mod_kernel(cond_ref, w_ref, b_ref, o_ref):
    c = cond_ref[...]
    s = c * _sigmoid(c)
    o_ref[...] = _dot(s.astype(BF16), w_ref[...].astype(BF16)) + b_ref[...]


def _modulation(cond, w_mod, b_mod):
    rows = cond.shape[0]
    nb = 2048
    return pl.pallas_call(
        _mod_kernel,
        grid=(DEPTH, 6 * D_MODEL // nb),
        in_specs=[
            pl.BlockSpec((rows, D_MODEL), lambda l, j: (0, 0)),
            pl.BlockSpec((None, D_MODEL, nb), lambda l, j: (l, 0, j)),
            pl.BlockSpec((None, 1, nb), lambda l, j: (l, 0, j)),
        ],
        out_specs=pl.BlockSpec((None, rows, nb), lambda l, j: (l, 0, j)),
        out_shape=jax.ShapeDtypeStruct((DEPTH, rows, 6 * D_MODEL), F32),
        name="modulation",
    )(cond, w_mod, b_mod.reshape(DEPTH, 1, 6 * D_MODEL))


TOKEN_TILE = 512
IN_PROJ_TILE = 1024


def _layer_block(shape, layer):
    shape = tuple(shape)
    return pl.BlockSpec((None,) + shape, lambda *_: (layer,) + (0,) * len(shape), pipeline_mode=pl.Buffered(1))


def _mod_spec(layer, tm, seq, per_seq):
    def index(i):
        return (layer, 1 + (i * tm) // seq if per_seq else 0, 0, 0)
    return pl.BlockSpec((None, None, 6, D_MODEL), index)


def _gmlp_tile(zb, vn, ws_ref, bias):
    tm = zb.shape[0]
    v = _rms(zb[:, D_B:], vn).astype(BF16)
    gmasks = [_lane_group_mask(D_B, D_B // G_B, g) for g in range(G_B)]
    out = []
    for ch in range(tm // CHUNK_B):
        vc = v[ch * CHUNK_B:(ch + 1) * CHUNK_B, :]
        mixed = bias
        for g in range(G_B):
            mixed = mixed + jnp.where(gmasks[g], _dot(ws_ref[g], vc), 0.0)
        out.append(zb[ch * CHUNK_B:(ch + 1) * CHUNK_B, :D_B] * mixed)
    return jnp.concatenate(out, axis=0)


def _in_kernel(*refs, rotary):
    if rotary:
        x_ref, mod_ref, g_ref, w_ref, vn_ref, ws_ref, bias_ref, tc_ref, ts1_ref, ts2_ref = refs[:10]
    else:
        x_ref, mod_ref, g_ref, w_ref, vn_ref, ws_ref, bias_ref = refs[:7]
    za_ref, yb_ref, zc_ref, zg_ref, dq_ref, dk_ref, dv_ref = refs[-7:]
    x = x_ref[...]
    h = _rms(x, g_ref[0:1, :]) * (1.0 + mod_ref[1:2, :]) + mod_ref[0:1, :]
    hb = h.astype(BF16)
    za_ref[...] = _dot(hb, w_ref[:, Z_OFF_A:Z_OFF_B])
    zc_ref[...] = _dot(hb, w_ref[:, Z_OFF_C:Z_OFF_G])
    zg_ref[...] = _dot(hb, w_ref[:, Z_OFF_G:Z_OFF_D])
    for j, d_ref in enumerate((dq_ref, dk_ref, dv_ref)):
        d = _dot(hb, w_ref[:, Z_OFF_D + j * D_D:Z_OFF_D + (j + 1) * D_D])
        if rotary and j < 2:
            tc, ts1, ts2 = tc_ref[...], ts1_ref[...], ts2_ref[...]
            d = jnp.concatenate([_rope128(d[:, c * LANES:(c + 1) * LANES], tc, ts1, ts2) for c in range(D_D // LANES)],
                                axis=1)
        d_ref[...] = d
    yb = _gmlp_tile(_dot(hb, w_ref[:, Z_OFF_B:Z_OFF_C]), vn_ref[...], ws_ref, bias_ref[...])
    yb_ref[...] = yb.astype(yb_ref.dtype)


def _in_proj(x, mod, g, w, vn, ws, bias, layer, seq, per_seq, rope=None):
    t = x.shape[0]
    tm = IN_PROJ_TILE
    widths = (ZA_W, MIX_W, 4 * D_C, ZG_W, D_D, D_D, D_D)
    in_specs = [
        pl.BlockSpec((tm, D_MODEL), lambda i: (i, 0)),
        _mod_spec(layer, tm, seq, per_seq),
        _layer_block((4, D_MODEL), layer),
        _layer_block((D_MODEL, Z_W), layer),
        _layer_block((1, D_B), layer),
        _layer_block((G_B, CHUNK_B, CHUNK_B), layer),
        _layer_block((CHUNK_B, D_B), layer),
    ]
    args = [x, mod, g, w, vn, ws, bias]
    if rope is not None:
        in_specs += [pl.BlockSpec((tm, LANES), lambda i: (i % (seq // tm), 0))] * 3
        args += list(rope)
    return pl.pallas_call(
        functools.partial(_in_kernel, rotary=rope is not None),
        grid=(t // tm,),
        in_specs=in_specs,
        out_specs=[pl.BlockSpec((tm, w_), lambda i: (i, 0)) for w_ in widths],
        out_shape=[jax.ShapeDtypeStruct((t, w_), MIX_DTYPE if j == 1 else F32) for j, w_ in enumerate(widths)],
        compiler_params=pltpu.CompilerParams(vmem_limit_bytes=VMEM_LIMIT),
        name="in_proj",
    )(*args)


LOG2E = math.log2(math.e)
ATTN_GROUP_TOKENS = 1024


def _with_ones(v, lo_mask, keep_lo):
    lane = lax.broadcasted_iota(jnp.int32, (1, LANES), 1)
    if keep_lo:
        return jnp.where(lane == 64, 1.0, jnp.where(lo_mask, v, 0.0))
    return jnp.where(lane == 0, 1.0, jnp.where(lo_mask, 0.0, v))


def _pair_normalise(o_even, o_odd, lo_mask):
    return jnp.where(lo_mask, o_even / o_even[:, 64:65], o_odd / o_odd[:, 0:1])


def _rope128(x, tc, ts1, ts2):
    return x * tc + pltpu.roll(x, LANES - 16, 1) * ts1 + pltpu.roll(x, 16, 1) * ts2


def _mla_kernel(*refs, seq, ng, has_ctx, tq):
    rows_all = ng * seq
    if has_ctx:
        (za_ref, qn_ref, kvn_ref, wuq_ref, wukv_ref, tc_ref, ts1_ref, ts2_ref, cckv_ref, ckr_ref,
         ya_ref, q_s, k_s, v_s, ckv_s, kr_s) = refs
        past = cckv_ref.shape[0]
    else:
        (za_ref, qn_ref, kvn_ref, wuq_ref, wukv_ref, ya_ref, ckv_ref, krt_ref, q_s, k_s, v_s, ckv_s, kr_s) = refs
        past = 0
    lk = past + seq
    za = za_ref[...]
    cq = _rms(za[:, :Q_LORA], qn_ref[...])
    qh = _dot(cq.astype(BF16), wuq_ref[:, 0:(2 if has_ctx else 1) * H_A * LANES])
    ckv = _rms(za[:, Q_LORA:Q_LORA + KV_LORA], kvn_ref[...])
    kr = za[:, Q_LORA + KV_LORA:ZA_W]
    if has_ctx:
        tc, ts1, ts2 = tc_ref[...], ts1_ref[...], ts2_ref[...]
        kr = _rope128(kr, tc, ts1, ts2)
        ckv_s[0:past, :] = cckv_ref[...]
        krt = jnp.concatenate([jnp.zeros((64, past), F32), ckr_ref[...], jnp.zeros((LANES - 64 - ROPE_A, past), F32)],
                              axis=0)
        kr_s[0:past, :] = krt.T
    else:
        ckv_ref[...] = ckv
        for g in range(ng):
            krt_ref[g] = kr[g * seq:(g + 1) * seq, :].T[64:64 + ROPE_A, :]
    ckv_s[past:past + rows_all, :] = ckv
    kr_s[past:past + rows_all, :] = kr
    kv = _dot(ckv_s[...].astype(BF16), wukv_ref[...])
    kr_all = kr_s[...]
    lo = _lane_group_mask(LANES, 64, 0)
    for h in range(H_A):
        qg = qh[:, h * LANES:(h + 1) * LANES]
        if has_ctx:
            qg = qg * tc + qh[:, (H_A + h) * LANES:(H_A + h + 1) * LANES] * (ts1 + ts2)
        q_s[h] = (qg * (MLA_SCALE * LOG2E)).astype(BF16)
        k_s[h] = (kv[:, h * LANES:(h + 1) * LANES] + kr_all).astype(BF16)
        v_s[h] = _with_ones(kv[:, (H_A + h) * LANES:(H_A + h + 1) * LANES], lo, h % 2 == 0).astype(BF16)

    def body(i, carry):
        rows = pl.ds(pl.multiple_of(i * tq, tq), tq)
        keys = pl.ds(pl.multiple_of(((i * tq) // seq) * seq, seq), lk)
        s = [_dot_nt(q_s[h, rows, :], k_s[h, keys, :]) for h in range(H_A)]
        o = [_dot(jnp.exp2(s[h] - jnp.max(s[h], axis=-1, keepdims=True)).astype(BF16), v_s[h, keys, :])
             for h in range(H_A)]
        for pair in range(H_A // 2):
            ya_ref[rows, pair * LANES:(pair + 1) * LANES] = _pair_normalise(o[2 * pair], o[2 * pair + 1],
                                                                            lo).astype(ya_ref.dtype)
        return carry

    lax.fori_loop(0, rows_all // tq, body, 0, unroll=True)


def _ctx_block(rows, cols, layer):
    return pl.BlockSpec((None, None, rows, cols), lambda b: (b, layer, 0, 0))


def _mla(za, qn, kvn, wuq, wukv, layer, batch, seq, rope=None, ctx=None):
    has_ctx = ctx is not None
    past = ctx[0].shape[2] if has_ctx else 0
    ng = 1 if has_ctx else ATTN_GROUP_TOKENS // seq
    rows = ng * seq
    keys = past + rows
    tq = min(seq, 256)
    full = lambda shape: pl.BlockSpec(shape, lambda b: (0,) * len(shape))
    in_specs = [
        pl.BlockSpec((rows, ZA_W), lambda b: (b, 0)),
        _layer_block((1, Q_LORA), layer), _layer_block((1, KV_LORA), layer),
        _layer_block((Q_LORA, 2 * H_A * LANES), layer), _layer_block((KV_LORA, 2 * H_A * LANES), layer),
    ]
    args = [za, qn, kvn, wuq, wukv]
    out_specs = [pl.BlockSpec((rows, MIX_W), lambda b: (b, 0))]
    out_shape = [jax.ShapeDtypeStruct((batch * seq, MIX_W), MIX_DTYPE)]
    if has_ctx:
        in_specs += [full((seq, LANES))] * 3
        in_specs += [_ctx_block(past, KV_LORA, layer), _ctx_block(ROPE_A, past, layer)]
        args += list(rope) + list(ctx)
    else:
        out_specs += [pl.BlockSpec((rows, KV_LORA), lambda b: (b, 0)),
                      pl.BlockSpec((ng, ROPE_A, seq), lambda b: (b, 0, 0))]
        out_shape += [jax.ShapeDtypeStruct((batch * seq, KV_LORA), F32),
                      jax.ShapeDtypeStruct((batch, ROPE_A, seq), F32)]
    return pl.pallas_call(
        functools.partial(_mla_kernel, seq=seq, ng=ng, has_ctx=has_ctx, tq=tq),
        grid=(batch // ng,),
        in_specs=in_specs,
        out_specs=out_specs,
        out_shape=out_shape,
        scratch_shapes=[
            pltpu.VMEM((H_A, rows, LANES), BF16),
            pltpu.VMEM((H_A, keys, LANES), BF16),
            pltpu.VMEM((H_A, keys, LANES), BF16),
            pltpu.VMEM((keys, LANES), F32),
            pltpu.VMEM((keys, LANES), F32),
        ],
        compiler_params=pltpu.CompilerParams(vmem_limit_bytes=VMEM_LIMIT),
        name="mla",
    )(*args)


def _diff_kernel(*refs, seq, ng, has_ctx, n_prev, tq, lam_init):
    rows_all = ng * seq
    if has_ctx:
        (dq_ref, dk_ref, dv_ref, lam_ref, sn_ref, ck_ref, cv_ref, yd_ref, q_s, k_s, v_s) = refs
        past = ck_ref.shape[1]
    else:
        dq_ref, dk_ref, dv_ref, lam_ref, sn_ref = refs[:5]
        yd_ref, dkt_ref, dvt_ref, q_s, k_s, v_s = refs[-6:]
        past = 0
        for g in range(ng):
            if n_prev:
                dkt_ref[g, 0:n_prev] = refs[5][g]
                dvt_ref[g, 0:n_prev] = refs[6][g]
            dkt_ref[g, n_prev] = dk_ref[g * seq:(g + 1) * seq, :].T
            dvt_ref[g, n_prev] = dv_ref[g * seq:(g + 1) * seq, :].T
    lk = past + seq
    lo = _lane_group_mask(LANES, 64, 0)
    if has_ctx:
        k_s[0:past, :] = ck_ref[...].T.astype(BF16)
        cv = cv_ref[...].T
    for half in range(2):
        cols = slice(half * LANES, (half + 1) * LANES)
        q, k, v = dq_ref[:, cols], dk_ref[:, cols], dv_ref[:, cols]
        if has_ctx:
            v_s[2 * half, 0:past, :] = _with_ones(cv[:, cols], lo, True).astype(BF16)
            v_s[2 * half + 1, 0:past, :] = _with_ones(cv[:, cols], lo, False).astype(BF16)
        q_s[:, cols] = (q * (DIFF_SCALE * LOG2E)).astype(BF16)
        k_s[past:past + rows_all, cols] = k.astype(BF16)
        v_s[2 * half, past:past + rows_all, :] = _with_ones(v, lo, True).astype(BF16)
        v_s[2 * half + 1, past:past + rows_all, :] = _with_ones(v, lo, False).astype(BF16)

    lam = lam_ref[...]
    lam_val = (jnp.exp(jnp.sum(lam[0:1] * lam[1:2], axis=-1, keepdims=True))
               - jnp.exp(jnp.sum(lam[2:3] * lam[3:4], axis=-1, keepdims=True)) + lam_init)
    comp_masks = [jnp.where(_lane_group_mask(D_D, DH_D, j), 1.0, 0.0).astype(BF16) for j in range(2 * H_D)]
    sn = sn_ref[...]

    def body(i, carry):
        rows = pl.ds(pl.multiple_of(i * tq, tq), tq)
        qb = q_s[rows, :]
        keys = pl.ds(pl.multiple_of(((i * tq) // seq) * seq, seq), lk)
        k_all = k_s[keys, :]
        score = lambda h: [_dot_nt(qb * comp_masks[2 * h + comp], k_all) for comp in range(2)]
        s_next = score(0)
        o = []
        for h in range(H_D):
            s = s_next
            if h + 1 < H_D:
                s_next = score(h + 1)
            o1, o2 = [_dot(jnp.exp2(sc - jnp.max(sc, axis=-1, keepdims=True)).astype(BF16), v_s[h, keys, :])
                      for sc in s]
            ll = 64 if h % 2 == 0 else 0
            o.append(o1 / o1[:, ll:ll + 1] - lam_val * (o2 / o2[:, ll:ll + 1]))
        for pair in range(H_D // 2):
            acc = jnp.where(lo, o[2 * pair], o[2 * pair + 1])
            sq = acc * acc
            ss_lo = jnp.sum(jnp.where(lo, sq, 0.0), axis=-1, keepdims=True)
            ss_hi = jnp.sum(jnp.where(lo, 0.0, sq), axis=-1, keepdims=True)
            r = jnp.where(lo, lax.rsqrt(ss_lo / (2 * DH_D) + EPS), lax.rsqrt(ss_hi / (2 * DH_D) + EPS))
            yd_ref[rows, pair * LANES:(pair + 1) * LANES] = (acc * r * sn * (1.0 - lam_init)).astype(yd_ref.dtype)
        return carry

    lax.fori_loop(0, rows_all // tq, body, 0, unroll=True)


def _diff(dq, dk, dv, lam, sn, layer, batch, seq, lam_init, ctx=None, prev=None):
    has_ctx = ctx is not None
    n_prev = prev[0].shape[1] if prev is not None else 0
    past = ctx[0].shape[3] if has_ctx else 0
    ng = 1
    rows = ng * seq
    keys = past + rows
    tq = min(seq, 256)
    in_specs = [pl.BlockSpec((rows, D_D), lambda b: (b, 0))] * 3
    in_specs += [_layer_block((4, DH_D), layer), _layer_block((1, LANES), layer)]
    args = [dq, dk, dv, lam, sn]
    if has_ctx:
        in_specs += [_ctx_block(D_D, past, layer)] * 2
        args += list(ctx)
    if n_prev:
        in_specs += [pl.BlockSpec((ng, n_prev, D_D, seq), lambda b: (b, 0, 0, 0))] * 2
        args += list(prev)
    out_specs = [pl.BlockSpec((rows, MIX_W), lambda b: (b, 0))]
    out_shape = [jax.ShapeDtypeStruct((batch * seq, MIX_W), MIX_DTYPE)]
    if not has_ctx:
        out_specs += [pl.BlockSpec((ng, n_prev + 1, D_D, seq), lambda b: (b, 0, 0, 0))] * 2
        out_shape += [jax.ShapeDtypeStruct((batch, n_prev + 1, D_D, seq), F32)] * 2
    return pl.pallas_call(
        functools.partial(_diff_kernel, seq=seq, ng=ng, has_ctx=has_ctx, n_prev=n_prev, tq=tq, lam_init=lam_init),
        grid=(batch // ng,),
        in_specs=in_specs,
        out_specs=out_specs,
        out_shape=out_shape,
        scratch_shapes=[
            pltpu.VMEM((rows, D_D), BF16),
            pltpu.VMEM((keys, D_D), BF16),
            pltpu.VMEM((H_D, keys, LANES), BF16),
        ],
        compiler_params=pltpu.CompilerParams(vmem_limit_bytes=VMEM_LIMIT),
        name="diff_attn",
    )(*args)


def _split3(x):
    hi = x.astype(BF16)
    r1 = x - hi.astype(F32)
    mid = r1.astype(BF16)
    lo = (r1 - mid.astype(F32)).astype(BF16)
    return hi, mid, lo


ST_ROWS = DH_C + 8
N_UNITS = 2 * H_C
MLSTM_GROUP_TOKENS = 1024


def _mlstm_kernel(*refs, seq, ng, has_state):
    ck = MLSTM_CHUNK
    nc = seq // ck
    if has_state:
        (zc_ref, zg_ref, gb_ref, hn_ref, c0_ref, n0_ref, m0_ref,
         yc_ref, k_s, qt_s, kt_s, vt_s, rcc_s, ca_s, wi_s, en_s, ws_s, wo_s, st_s, ht_s) = refs
    else:
        (zc_ref, zg_ref, gb_ref, hn_ref,
         yc_ref, cout_ref, nout_ref, mout_ref,
         k_s, qt_s, kt_s, vt_s, rcc_s, ca_s, wi_s, en_s, ws_s, wo_s, st_s, ht_s) = refs
    eye = jnp.where(lax.broadcasted_iota(jnp.int32, (DH_C, DH_C), 0) == lax.broadcasted_iota(jnp.int32, (DH_C, DH_C), 1),
                    1.0, 0.0).astype(BF16)

    def transpose_exact(a):
        return sum(_dot_nt(eye, part) for part in _split3(a))

    def chunk_rows(g, c):
        return slice(g * seq + c * ck, g * seq + (c + 1) * ck)

    for g in range(ng):
        for c in range(nc):
            rows = chunk_rows(g, c)
            qt_s[g, c] = zc_ref[rows, 0:D_C].T.astype(BF16)
            kk = zc_ref[rows, D_C:2 * D_C] * (DH_C ** -0.5)
            k_s[rows, :] = kk.astype(BF16)
            kt_s[g, c] = kk.T.astype(BF16)
            vt_s[g, c] = zc_ref[rows, 2 * D_C:3 * D_C].T.astype(BF16)

    gates = zg_ref[...] + gb_ref[...]
    r_i = lax.broadcasted_iota(jnp.int32, (ck, ck), 0)
    c_i = lax.broadcasted_iota(jnp.int32, (ck, ck), 1)
    sum_fw = jnp.where(r_i <= c_i, 1.0, 0.0).astype(BF16)
    sum_bw = jnp.where(r_i >= c_i, 1.0, 0.0).astype(BF16)
    row8 = lax.broadcasted_iota(jnp.int32, (N_UNITS, ck), 0)
    fw8 = row8 < H_C
    fw81 = fw8[:, 0:1]
    b_steps, r_steps = [], []
    for g in range(ng):
        i_rows, b_rows = [], []
        for c in range(nc):
            rows = chunk_rows(g, c)
            g_t = gates[rows, :].T
            i8 = g_t[0:N_UNITS, :]
            p0, p1, p2 = _split3(_log_sigmoid(g_t[N_UNITS:2 * N_UNITS, :]))
            b_fw = _dot(p0, sum_fw) + _dot(p1, sum_fw) + _dot(p2, sum_fw)
            b_bw = _dot(p0, sum_bw) + _dot(p1, sum_bw) + _dot(p2, sum_bw)
            b8 = jnp.where(fw8, b_fw, b_bw)
            rcc_s[rows, :] = jnp.concatenate([i8 - b8, jnp.zeros((ck - N_UNITS, ck), F32)], axis=0).T
            i_rows.append(i8)
            b_rows.append(b8)
        b_steps.append([jnp.where(fw8, b_rows[i], b_rows[nc - 1 - i]) for i in range(nc)])
        r_steps.append([jnp.where(fw8, i_rows[i], i_rows[nc - 1 - i]) - b_steps[g][i] for i in range(nc)])

    pm = sm = jnp.concatenate([r for g in range(ng) for r in r_steps[g]], axis=0)
    lane_all = lax.broadcasted_iota(jnp.int32, pm.shape, 1)
    sh = 1
    while sh < ck:
        pm = jnp.maximum(pm, jnp.where(lane_all >= sh, pltpu.roll(pm, sh, 1), NEG_INF))
        sm = jnp.maximum(sm, jnp.where(lane_all < ck - sh, pltpu.roll(sm, ck - sh, 1), NEG_INF))
        sh *= 2
    m_last = []
    for g in range(ng):
        m_prev = m0_ref[g, :, 0:1] if has_state else jnp.zeros((N_UNITS, 1), F32)
        for i in range(nc):
            b8, r8 = b_steps[g][i], r_steps[g][i]
            at = (g * nc + i) * N_UNITS
            pm8, sm8 = pm[at:at + N_UNITS, :], sm[at:at + N_UNITS, :]
            cm = jnp.where(fw8, pm8, sm8)
            b_t = jnp.where(fw81, b8[:, ck - 1:ck], b8[:, 0:1])
            cm_end = jnp.where(fw81, pm8[:, ck - 1:ck], sm8[:, 0:1])
            inter = b8 + m_prev
            mt = jnp.maximum(inter, b8 + cm)
            ca_s[g, i] = b8 - mt
            wi_s[g, i] = jnp.exp(inter - mt)
            en_s[g, i] = jnp.exp(-mt)
            m_new = b_t + jnp.maximum(m_prev, cm_end)
            ws_s[g, i] = jnp.exp(b_t + r8 - m_new)
            wo_s[g, i] = jnp.broadcast_to(jnp.exp(b_t + m_prev - m_new), (N_UNITS, ck))
            m_prev = m_new
        m_last.append(m_prev)

    if has_state:
        for g in range(ng):
            for u in range(N_UNITS):
                st_s[g * N_UNITS + u, 0:DH_C, :] = transpose_exact(c0_ref[g, u])
                st_s[g * N_UNITS + u, DH_C:ST_ROWS, :] = jnp.broadcast_to(n0_ref[g, u:u + 1, :],
                                                                         (ST_ROWS - DH_C, DH_C))
    else:
        st_s[...] = jnp.zeros_like(st_s)

    hmask_bf = [jnp.where(_lane_group_mask(D_C, DH_C, h), 1.0, 0.0).astype(BF16) for h in range(H_C)]
    src_ok = (r_i <= c_i, r_i >= c_i)
    hs = [slice(h * DH_C, (h + 1) * DH_C) for h in range(H_C)]
    units = [(g, d, h) for g in range(ng) for d in range(2) for h in range(H_C)]

    def body(i, carry):
        chunks = (i, nc - 1 - i)
        ca, wi, en, ws, wo = [[ref[g, i] for g in range(ng)] for ref in (ca_s, wi_s, en_s, ws_s, wo_s)]
        q_t = [[qt_s[g, c] for c in chunks] for g in range(ng)]
        k_t = [[kt_s[g, c] for c in chunks] for g in range(ng)]
        v_t = [[vt_s[g, c] for c in chunks] for g in range(ng)]
        rows = [[pl.ds(pl.multiple_of(g * seq + c * ck, ck), ck) for c in chunks] for g in range(ng)]
        sts, qcts, states, upds, pts = [], [], [], [], []
        for g, d, h in units:
            sts.append(_dot(k_s[rows[g][d], :] * hmask_bf[h], q_t[g][d]))
            states.append(st_s[g * N_UNITS + d * H_C + h])
            qcts.append(_dot(states[-1].astype(BF16), q_t[g][d][hs[h], :]))
        for g, d, h in units:
            u = d * H_C + h
            w_s = ws[g][u:u + 1, :]
            vw = jnp.concatenate([v_t[g][d][hs[h], :].astype(F32) * w_s, jnp.broadcast_to(w_s, (8, ck))], axis=0)
            upds.append(_dot_nt(vw.astype(BF16), k_t[g][d][hs[h], :]))
        for j, (g, d, h) in enumerate(units):
            u = d * H_C + h
            decay = jnp.exp(rcc_s[rows[g][d], u:u + 1] + ca[g][u:u + 1, :])
            pts.append(sts[j] * jnp.where(src_ok[d], decay, 0.0))
        nums = [_dot(v_t[g][d][hs[h], :], pts[j].astype(BF16)) for j, (g, d, h) in enumerate(units)]
        for j, (g, d, h) in enumerate(units):
            u = d * H_C + h
            w_i = wi[g][u:u + 1, :]
            den = w_i * qcts[j][DH_C:DH_C + 1, :] + jnp.sum(pts[j], axis=0, keepdims=True)
            num = w_i * qcts[j][0:DH_C, :] + nums[j]
            ht_s[g, d, chunks[d], hs[h], :] = num / jnp.maximum(jnp.abs(den), en[g][u:u + 1, :])
            st_s[g * N_UNITS + u] = wo[g][u:u + 1, 0:DH_C] * states[j] + upds[j]
        return carry

    lax.fori_loop(0, nc, body, 0, unroll=2)

    hn = hn_ref[...]
    for g in range(ng):
        for c in range(nc):
            rows = chunk_rows(g, c)
            h_t = ht_s[g, 0, c] + ht_s[g, 1, c]
            parts = []
            for h in range(H_C):
                x = h_t[hs[h], :]
                parts.append(x * lax.rsqrt(jnp.mean(x * x, axis=0, keepdims=True) + EPS))
            h_n = jnp.concatenate(parts, axis=0).T
            yc_ref[rows, :] = (_sigmoid(zc_ref[rows, 3 * D_C:4 * D_C]) * (h_n * hn)).astype(yc_ref.dtype)

    if not has_state:
        for g in range(ng):
            for u in range(N_UNITS):
                cout_ref[g, u] = transpose_exact(st_s[g * N_UNITS + u, 0:DH_C, :])
                nout_ref[g, u:u + 1, :] = st_s[g * N_UNITS + u, DH_C:DH_C + 1, :]
            mout_ref[g] = jnp.broadcast_to(m_last[g], (N_UNITS, LANES))


def _mlstm(zc, zg, gb, hn, layer, batch, seq, state=None):
    has_state = state is not None
    nc = seq // MLSTM_CHUNK
    ng = max(2, MLSTM_GROUP_TOKENS // seq)
    in_specs = [pl.BlockSpec((ng * seq, 4 * D_C), lambda b: (b, 0)), pl.BlockSpec((ng * seq, ZG_W), lambda b: (b, 0)),
                _layer_block((1, ZG_W), layer), _layer_block((1, D_C), layer)]
    args = [zc, zg, gb, hn]
    out_specs = [pl.BlockSpec((ng * seq, MIX_W), lambda b: (b, 0))]
    out_shape = [jax.ShapeDtypeStruct((batch * seq, MIX_W), MIX_DTYPE)]
    if has_state:
        in_specs += [pl.BlockSpec((ng, None, N_UNITS, DH_C, DH_C), lambda b: (b, layer, 0, 0, 0)),
                     pl.BlockSpec((ng, None, N_UNITS, DH_C), lambda b: (b, layer, 0, 0)),
                     pl.BlockSpec((ng, None, N_UNITS, LANES), lambda b: (b, layer, 0, 0))]
        args += list(state)
    else:
        out_specs += [pl.BlockSpec((ng, N_UNITS, DH_C, DH_C), lambda b: (b, 0, 0, 0)),
                      pl.BlockSpec((ng, N_UNITS, DH_C), lambda b: (b, 0, 0)),
                      pl.BlockSpec((ng, N_UNITS, LANES), lambda b: (b, 0, 0))]
        out_shape += [jax.ShapeDtypeStruct((batch, N_UNITS, DH_C, DH_C), F32),
                      jax.ShapeDtypeStruct((batch, N_UNITS, DH_C), F32),
                      jax.ShapeDtypeStruct((batch, N_UNITS, LANES), F32)]
    step = lambda: pltpu.VMEM((ng, nc, N_UNITS, MLSTM_CHUNK), F32)
    transposed = lambda: pltpu.VMEM((ng, nc, D_C, MLSTM_CHUNK), BF16)
    return pl.pallas_call(
        functools.partial(_mlstm_kernel, seq=seq, ng=ng, has_state=has_state),
        grid=(batch // ng,),
        in_specs=in_specs,
        out_specs=out_specs,
        out_shape=out_shape,
        scratch_shapes=[
            pltpu.VMEM((ng * seq, D_C), BF16),
            transposed(), transposed(), transposed(),
            pltpu.VMEM((ng * seq, ZG_W), F32),
            step(), step(), step(), step(), step(),
            pltpu.VMEM((ng * N_UNITS, ST_ROWS, DH_C), F32),
            pltpu.VMEM((ng, 2, nc, D_C, MLSTM_CHUNK), F32),
        ],
        compiler_params=pltpu.CompilerParams(vmem_limit_bytes=VMEM_LIMIT),
        name="mlstm",
    )(*args)


def _merge_kernel(x_ref, mod_ref, g_ref, ya_ref, yb_ref, yc_ref, yd_ref, wm_ref, bm_ref, wb_ref, wo_ref,
                  w1_ref, w2_ref, o_ref):
    x = x_ref[...]
    h = (_rms(x, g_ref[0:1, :]) * (1.0 + mod_ref[1:2, :]) + mod_ref[0:1, :]).astype(BF16)
    acc = jnp.zeros(x.shape, F32)
    for n, y_ref in enumerate((ya_ref, yb_ref, yc_ref, yd_ref)):
        cols = slice(n * D_MODEL, (n + 1) * D_MODEL)
        gate = _sigmoid(_dot(h, wm_ref[:, cols]) + bm_ref[:, cols])
        acc = acc + gate * _dot(y_ref[...].astype(BF16), wb_ref[n])
    y = _dot(acc.astype(BF16), wo_ref[...])
    x = x + mod_ref[2:3, :] * _rms(y, g_ref[1:2, :])
    h2 = (_rms(x, g_ref[2:3, :]) * (1.0 + mod_ref[4:5, :]) + mod_ref[3:4, :]).astype(BF16)
    f = jnp.zeros(x.shape, F32)
    for j in range(D_FF // D_MODEL):
        cols = slice(j * D_MODEL, (j + 1) * D_MODEL)
        a = jnp.maximum(_dot(h2, w1_ref[:, cols]), 0.0)
        f = f + _dot((a * a).astype(BF16), w2_ref[cols, :])
    o_ref[...] = x + mod_ref[5:6, :] * _rms(f, g_ref[3:4, :])


def _merge_ffn(x, mod, g, ys, wm, bm, wb, wo, w1, w2, layer, seq, per_seq):
    t = x.shape[0]
    tm = TOKEN_TILE
    full = lambda shape: _layer_block(shape, layer)
    tok = lambda w_: pl.BlockSpec((tm, w_), lambda i: (i, 0))
    return pl.pallas_call(
        _merge_kernel,
        grid=(t // tm,),
        in_specs=[tok(D_MODEL), _mod_spec(layer, tm, seq, per_seq),
                  full((4, D_MODEL)), tok(MIX_W), tok(MIX_W), tok(MIX_W), tok(MIX_W),
                  full((D_MODEL, N_BRANCH * D_MODEL)), full((1, N_BRANCH * D_MODEL)),
                  full((N_BRANCH, MIX_W, D_MODEL)), full((D_MODEL, D_MODEL)),
                  full((D_MODEL, D_FF)), full((D_FF, D_MODEL))],
        out_specs=tok(D_MODEL),
        out_shape=jax.ShapeDtypeStruct((t, D_MODEL), F32),
        compiler_params=pltpu.CompilerParams(vmem_limit_bytes=VMEM_LIMIT),
        name="merge_ffn",
    )(x, mod, g, *ys, wm, bm, wb, wo, w1, w2)


def _arrange_w_in(w):
    wt = w.T.astype(BF16)
    z = lambda n: jnp.zeros((n, D_MODEL), BF16)
    o_b, o_c = IN_A, IN_A + IN_B
    o_g, o_d = o_c + 4 * D_C, o_c + IN_C
    gates = wt[o_g:o_d].reshape(2, 2, H_C, D_MODEL).transpose(1, 0, 2, 3).reshape(4 * H_C, D_MODEL)
    return jnp.concatenate([
        wt[:Q_LORA + KV_LORA], z(64), wt[Q_LORA + KV_LORA:IN_A], z(32),
        wt[o_b:o_g], gates, z(ZG_W - 4 * H_C), wt[o_d:]], axis=0).T


def _arrange_w_uq(w):
    w = w.reshape(Q_LORA, H_A, NOPE_A + ROPE_A)
    half = ROPE_A // 2
    swapped = jnp.concatenate([jnp.zeros_like(w[:, :, :NOPE_A]), w[:, :, NOPE_A + half:], w[:, :, NOPE_A:NOPE_A + half]],
                              axis=-1)
    pad = lambda a: jnp.pad(a, ((0, 0), (0, 0), (0, LANES - NOPE_A - ROPE_A))).reshape(Q_LORA, H_A * LANES)
    return jnp.concatenate([pad(w), pad(swapped)], axis=1).astype(BF16)


def _arrange_w_ukv(w):
    w = w.reshape(KV_LORA, H_A, NOPE_A + VH_A)
    wk = jnp.pad(w[:, :, :NOPE_A], ((0, 0), (0, 0), (0, LANES - NOPE_A)))
    v = w[:, :, NOPE_A:]
    zero = jnp.zeros_like(v)
    even = jnp.concatenate([v, zero], axis=-1)
    odd = jnp.concatenate([zero, v], axis=-1)
    wv = jnp.where((jnp.arange(H_A) % 2 == 0)[None, :, None], even, odd)
    return jnp.concatenate([wk.reshape(KV_LORA, -1), wv.reshape(KV_LORA, -1)], axis=1).astype(BF16)


def _rope_tables(rows):
    row = np.repeat(np.arange(rows), GRID_W).astype(np.float64)
    col = np.tile(np.arange(GRID_W), rows).astype(np.float64)
    nf = ROPE_A // 4
    inv = np.exp(-math.log(ROPE_BASE) * np.arange(nf, dtype=np.float64) / nf)
    ang = np.concatenate([row[:, None] * inv, col[:, None] * inv], axis=-1)
    cos, sin = np.cos(ang), np.sin(ang)
    n = cos.shape[0]
    ones, zeros = (lambda k: np.ones((n, k))), (lambda k: np.zeros((n, k)))
    mla = (np.concatenate([ones(64), cos, cos, ones(32)], axis=1),
           np.concatenate([zeros(64), -sin, zeros(48)], axis=1),
           np.concatenate([zeros(80), sin, zeros(32)], axis=1))
    dif = (np.tile(np.concatenate([cos, cos], axis=1), (1, 4)),
           np.tile(np.concatenate([-sin, zeros(16)], axis=1), (1, 4)),
           np.tile(np.concatenate([zeros(16), sin], axis=1), (1, 4)))
    as_f32 = lambda ts: tuple(jnp.asarray(t.astype(np.float32)) for t in ts)
    return as_f32(mla), as_f32(dif)


def kernel(x_prompt, x_sample, cache_mla_ckv, cache_mla_krope, cache_diff_k, cache_diff_v, state_mlstm_C,
           state_mlstm_n, state_mlstm_m, c, c_ctx, w_mod, b_mod, norm_g, w_in, mla_q_norm, w_uq, mla_kv_norm,
           w_ukv, gmlp_v_norm, gmlp_w_s, gmlp_b_s, mlstm_gate_bias, mlstm_head_norm, diff_lambda, diff_sub_norm,
           w_branch, w_merge, b_merge, w_out, w_ff1, w_ff2):
    bp, lp, _ = x_prompt.shape
    bs, ls, _ = x_sample.shape
    past = cache_mla_ckv.shape[2]

    cond = jnp.concatenate([c_ctx[None, :], c, jnp.zeros((16 - 1 - bs, D_MODEL), F32)], axis=0)
    mod = _modulation(cond, w_mod, b_mod).reshape(DEPTH, 16, 6, D_MODEL)
    rope_mla, rope_dif = _rope_tables(ls // GRID_W)

    row = lambda a: a[:, None, :]
    w_in_a = jax.vmap(_arrange_w_in)(w_in)
    wuq_a = jax.vmap(_arrange_w_uq)(w_uq)
    wukv_a = jax.vmap(_arrange_w_ukv)(w_ukv)
    qn, kvn, vn, hn, bm = row(mla_q_norm), row(mla_kv_norm), row(gmlp_v_norm), row(mlstm_head_norm), row(b_merge)
    ws = gmlp_w_s.astype(BF16)
    bias = jnp.repeat(jnp.swapaxes(gmlp_b_s, 1, 2), D_B // G_B, axis=2)
    gb = jnp.pad(mlstm_gate_bias.transpose(0, 2, 1, 3).reshape(DEPTH, 1, 4 * H_C),
                 ((0, 0), (0, 0), (0, ZG_W - 4 * H_C)))
    sn = row(jnp.tile(diff_sub_norm, (1, 2)))
    wm, wb, wo = w_merge.astype(BF16), w_branch.astype(BF16), w_out.astype(BF16)
    w1, w2 = w_ff1.astype(BF16), w_ff2.astype(BF16)

    ctx_kr = cache_mla_krope.transpose(0, 1, 3, 2)
    ctx_dk = cache_diff_k.transpose(0, 1, 3, 4, 5, 2).reshape(bs, DEPTH, D_D, past)
    ctx_dv = cache_diff_v.transpose(0, 1, 3, 4, 2).reshape(bs, DEPTH, D_D, past)
    c0 = state_mlstm_C.reshape(bs, DEPTH, N_UNITS, DH_C, DH_C)
    n0 = state_mlstm_n.reshape(bs, DEPTH, N_UNITS, DH_C)
    m0 = jnp.broadcast_to(state_mlstm_m.reshape(bs, DEPTH, N_UNITS, 1), (bs, DEPTH, N_UNITS, LANES))

    xp = x_prompt.reshape(bp * lp, D_MODEL)
    xs = x_sample.reshape(bs * ls, D_MODEL)
    ents = []
    diff_kv = None
    for l in range(DEPTH):
        lam_init = 0.8 - 0.6 * math.exp(-0.3 * l)
        for is_sample in (False, True):
            x = xs if is_sample else xp
            batch, seq = (bs, ls) if is_sample else (bp, lp)
            za, yb, zc, zg, dq, dk, dv = _in_proj(x, mod, norm_g, w_in_a, vn, ws, bias, l, seq, is_sample,
                                                  rope=rope_dif if is_sample else None)
            if is_sample:
                (ya,) = _mla(za, qn, kvn, wuq_a, wukv_a, l, batch, seq, rope=rope_mla, ctx=(cache_mla_ckv, ctx_kr))
                (yd,) = _diff(dq, dk, dv, diff_lambda, sn, l, batch, seq, lam_init, ctx=(ctx_dk, ctx_dv))
                (yc,) = _mlstm(zc, zg, gb, hn, l, batch, seq, state=(c0, n0, m0))
            else:
                ya, ckv, krt = _mla(za, qn, kvn, wuq_a, wukv_a, l, batch, seq)
                yd, *diff_kv = _diff(dq, dk, dv, diff_lambda, sn, l, batch, seq, lam_init, prev=diff_kv)
                yc, c_new, n_new, m_new = _mlstm(zc, zg, gb, hn, l, batch, seq)
                ents.append((
                    ckv.reshape(bp, lp, KV_LORA),
                    krt.transpose(0, 2, 1),
                    c_new.reshape(bp, 2, H_C, DH_C, DH_C),
                    n_new.reshape(bp, 2, H_C, DH_C),
                    m_new[:, :, 0].reshape(bp, 2, H_C)))
            x = _merge_ffn(x, mod, norm_g, (ya, yb, yc, yd), wm, bm, wb, wo, w1, w2, l, seq, is_sample)
            if is_sample:
                xs = x
            else:
                xp = x

    stack = lambda j: jnp.stack([e[j] for e in ents], axis=1)
    dkt, dvt = diff_kv
    new_diff_k = dkt.reshape(bp, DEPTH, H_D, 2, DH_D, lp).transpose(0, 1, 5, 2, 3, 4)
    new_diff_v = dvt.reshape(bp, DEPTH, H_D, 2 * DH_D, lp).transpose(0, 1, 4, 2, 3)
    return (xp.reshape(bp, lp, D_MODEL), xs.reshape(bs, ls, D_MODEL),
            stack(0), stack(1), new_diff_k, new_diff_v, stack(2), stack(3), stack(4))
```

```python
import functools
import math

import jax
import jax.numpy as jnp
import numpy as np
from jax import lax
from jax.experimental import pallas as pl
from jax.experimental.pallas import tpu as pltpu

F32 = jnp.float32
BF16 = jnp.bfloat16

D_MODEL = 1024
DEPTH = 2
GRID_W = 64
N_BRANCH = 4
MIX_W = 256
H_A, NOPE_A, ROPE_A, VH_A = 4, 64, 32, 64
Q_LORA, KV_LORA = 256, 128
D_B, G_B, CHUNK_B = 256, 4, 128
H_C, DH_C = 4, 64
D_C = H_C * DH_C
H_D, DH_D = 4, 32
D_D = H_D * 2 * DH_D
D_FF = 4 * D_MODEL
IN_A = Q_LORA + KV_LORA + ROPE_A
IN_B = 2 * D_B
IN_C = 4 * D_C + 4 * H_C
IN_D = 3 * D_D
ROPE_BASE = 10000.0
EPS = 1e-6
MLA_SCALE = (NOPE_A + ROPE_A) ** -0.5
DIFF_SCALE = DH_D ** -0.5

MIX_DTYPE = BF16
LANES = 128
MLSTM_CHUNK = 128
VMEM_LIMIT = 56 * 1024 * 1024
NEG_INF = float("-inf")

ZA_W = 512
ZG_W = 128
Z_OFF_A = 0
Z_OFF_B = Z_OFF_A + ZA_W
Z_OFF_C = Z_OFF_B + IN_B
Z_OFF_G = Z_OFF_C + 4 * D_C
Z_OFF_D = Z_OFF_G + ZG_W
Z_W = Z_OFF_D + IN_D


def _rms(x, g):
    return x * lax.rsqrt(jnp.mean(x * x, axis=-1, keepdims=True) + EPS) * g


def _sigmoid(x):
    return 1.0 / (1.0 + jnp.exp(-x))


def _log_sigmoid(x):
    return jnp.minimum(x, 0.0) - jnp.log1p(jnp.exp(-jnp.abs(x)))


def _dot(a, b):
    return jnp.dot(a, b, preferred_element_type=F32)


def _dot_nt(a, b):
    return lax.dot_general(a, b, (((1,), (1,)), ((), ())), preferred_element_type=F32)


def _lane_group_mask(width, group, index):
    lane = lax.broadcasted_iota(jnp.int32, (1, width), 1)
    return (lane >= index * group) & (lane < (index + 1) * group)


def _mod_kernel(cond_ref, w_ref, b_ref, o_ref):
    c = cond_ref[...]
    s = c * _sigmoid(c)
    o_ref[...] = _dot(s.astype(BF16), w_ref[...].astype(BF16)) + b_ref[...]


def _modulation(cond, w_mod, b_mod):
    rows = cond.shape[0]
    nb = 2048
    return pl.pallas_call(
        _mod_kernel,
        grid=(DEPTH, 6 * D_MODEL // nb),
        in_specs=[
            pl.BlockSpec((rows, D_MODEL), lambda l, j: (0, 0)),
            pl.BlockSpec((None, D_MODEL, nb), lambda l, j: (l, 0, j)),
            pl.BlockSpec((None, 1, nb), lambda l, j: (l, 0, j)),
        ],
        out_specs=pl.BlockSpec((None, rows, nb), lambda l, j: (l, 0, j)),
        out_shape=jax.ShapeDtypeStruct((DEPTH, rows, 6 * D_MODEL), F32),
        name="modulation",
    )(cond, w_mod, b_mod.reshape(DEPTH, 1, 6 * D_MODEL))


TOKEN_TILE = 512
IN_PROJ_TILE = 1024


def _layer_block(shape, layer):
    shape = tuple(shape)
    return pl.BlockSpec((None,) + shape, lambda *_: (layer,) + (0,) * len(shape), pipeline_mode=pl.Buffered(1))


def _mod_spec(layer, tm, seq, per_seq):
    def index(i):
        return (layer, 1 + (i * tm) // seq if per_seq else 0, 0, 0)
    return pl.BlockSpec((None, None, 6, D_MODEL), index)


def _gmlp_tile(zb, vn, ws_ref, bias):
    tm = zb.shape[0]
    v = _rms(zb[:, D_B:], vn).astype(BF16)
    gmasks = [_lane_group_mask(D_B, D_B // G_B, g) for g in range(G_B)]
    out = []
    for ch in range(tm // CHUNK_B):
        vc = v[ch * CHUNK_B:(ch + 1) * CHUNK_B, :]
        mixed = bias
        for g in range(G_B):
            mixed = mixed + jnp.where(gmasks[g], _dot(ws_ref[g], vc), 0.0)
        out.append(zb[ch * CHUNK_B:(ch + 1) * CHUNK_B, :D_B] * mixed)
    return jnp.concatenate(out, axis=0)


def _in_kernel(*refs, rotary):
    if rotary:
        x_ref, mod_ref, g_ref, w_ref, vn_ref, ws_ref, bias_ref, tc_ref, ts1_ref, ts2_ref = refs[:10]
    else:
        x_ref, mod_ref, g_ref, w_ref, vn_ref, ws_ref, bias_ref = refs[:7]
    za_ref, yb_ref, zc_ref, zg_ref, dq_ref, dk_ref, dv_ref = refs[-7:]
    x = x_ref[...]
    h = _rms(x, g_ref[0:1, :]) * (1.0 + mod_ref[1:2, :]) + mod_ref[0:1, :]
    hb = h.astype(BF16)
    za_ref[...] = _dot(hb, w_ref[:, Z_OFF_A:Z_OFF_B])
    zc_ref[...] = _dot(hb, w_ref[:, Z_OFF_C:Z_OFF_G])
    zg_ref[...] = _dot(hb, w_ref[:, Z_OFF_G:Z_OFF_D])
    for j, d_ref in enumerate((dq_ref, dk_ref, dv_ref)):
        d = _dot(hb, w_ref[:, Z_OFF_D + j * D_D:Z_OFF_D + (j + 1) * D_D])
        if rotary and j < 2:
            tc, ts1, ts2 = tc_ref[...], ts1_ref[...], ts2_ref[...]
            d = jnp.concatenate([_rope128(d[:, c * LANES:(c + 1) * LANES], tc, ts1, ts2) for c in range(D_D // LANES)],
                                axis=1)
        d_ref[...] = d
    yb = _gmlp_tile(_dot(hb, w_ref[:, Z_OFF_B:Z_OFF_C]), vn_ref[...], ws_ref, bias_ref[...])
    yb_ref[...] = yb.astype(yb_ref.dtype)


def _in_proj(x, mod, g, w, vn, ws, bias, layer, seq, per_seq, rope=None):
    t = x.shape[0]
    tm = IN_PROJ_TILE
    widths = (ZA_W, MIX_W, 4 * D_C, ZG_W, D_D, D_D, D_D)
    in_specs = [
        pl.BlockSpec((tm, D_MODEL), lambda i: (i, 0)),
        _mod_spec(layer, tm, seq, per_seq),
        _layer_block((4, D_MODEL), layer),
        _layer_block((D_MODEL, Z_W), layer),
        _layer_block((1, D_B), layer),
        _layer_block((G_B, CHUNK_B, CHUNK_B), layer),
        _layer_block((CHUNK_B, D_B), layer),
    ]
    args = [x, mod, g, w, vn, ws, bias]
    if rope is not None:
        in_specs += [pl.BlockSpec((tm, LANES), lambda i: (i % (seq // tm), 0))] * 3
        args += list(rope)
    return pl.pallas_call(
        functools.partial(_in_kernel, rotary=rope is not None),
        grid=(t // tm,),
        in_specs=in_specs,
        out_specs=[pl.BlockSpec((tm, w_), lambda i: (i, 0)) for w_ in widths],
        out_shape=[jax.ShapeDtypeStruct((t, w_), MIX_DTYPE if j == 1 else F32) for j, w_ in enumerate(widths)],
        compiler_params=pltpu.CompilerParams(vmem_limit_bytes=VMEM_LIMIT),
        name="in_proj",
    )(*args)


LOG2E = math.log2(math.e)
ATTN_GROUP_TOKENS = 1024


def _with_ones(v, lo_mask, keep_lo):
    lane = lax.broadcasted_iota(jnp.int32, (1, LANES), 1)
    if keep_lo:
        return jnp.where(lane == 64, 1.0, jnp.where(lo_mask, v, 0.0))
    return jnp.where(lane == 0, 1.0, jnp.where(lo_mask, 0.0, v))


def _pair_normalise(o_even, o_odd, lo_mask):
    return jnp.where(lo_mask, o_even / o_even[:, 64:65], o_odd / o_odd[:, 0:1])


def _rope128(x, tc, ts1, ts2):
    return x * tc + pltpu.roll(x, LANES - 16, 1) * ts1 + pltpu.roll(x, 16, 1) * ts2


def _mla_kernel(*refs, seq, ng, has_ctx, tq):
    rows_all = ng * seq
    if has_ctx:
        (za_ref, qn_ref, kvn_ref, wuq_ref, wukv_ref, tc_ref, ts1_ref, ts2_ref, cckv_ref, ckr_ref,
         ya_ref, q_s, k_s, v_s, ckv_s, kr_s) = refs
        past = cckv_ref.shape[0]
    else:
        (za_ref, qn_ref, kvn_ref, wuq_ref, wukv_ref, ya_ref, ckv_ref, krt_ref, q_s, k_s, v_s, ckv_s, kr_s) = refs
        past = 0
    lk = past + seq
    za = za_ref[...]
    cq = _rms(za[:, :Q_LORA], qn_ref[...])
    qh = _dot(cq.astype(BF16), wuq_ref[:, 0:(2 if has_ctx else 1) * H_A * LANES])
    ckv = _rms(za[:, Q_LORA:Q_LORA + KV_LORA], kvn_ref[...])
    kr = za[:, Q_LORA + KV_LORA:ZA_W]
    if has_ctx:
        tc, ts1, ts2 = tc_ref[...], ts1_ref[...], ts2_ref[...]
        kr = _rope128(kr, tc, ts1, ts2)
        ckv_s[0:past, :] = cckv_ref[...]
        krt = jnp.concatenate([jnp.zeros((64, past), F32), ckr_ref[...], jnp.zeros((LANES - 64 - ROPE_A, past), F32)],
                              axis=0)
        kr_s[0:past, :] = krt.T
    else:
        ckv_ref[...] = ckv
        for g in range(ng):
            krt_ref[g] = kr[g * seq:(g + 1) * seq, :].T[64:64 + ROPE_A, :]
    ckv_s[past:past + rows_all, :] = ckv
    kr_s[past:past + rows_all, :] = kr
    kv = _dot(ckv_s[...].astype(BF16), wukv_ref[...])
    kr_all = kr_s[...]
    lo = _lane_group_mask(LANES, 64, 0)
    for h in range(H_A):
        qg = qh[:, h * LANES:(h + 1) * LANES]
        if has_ctx:
            qg = qg * tc + qh[:, (H_A + h) * LANES:(H_A + h + 1) * LANES] * (ts1 + ts2)
        q_s[h] = (qg * (MLA_SCALE * LOG2E)).astype(BF16)
        k_s[h] = (kv[:, h * LANES:(h + 1) * LANES] + kr_all).astype(BF16)
        v_s[h] = _with_ones(kv[:, (H_A + h) * LANES:(H_A + h + 1) * LANES], lo, h % 2 == 0).astype(BF16)

    def body(i, carry):
        rows = pl.ds(pl.multiple_of(i * tq, tq), tq)
        keys = pl.ds(pl.multiple_of(((i * tq) // seq) * seq, seq), lk)
        s = [_dot_nt(q_s[h, rows, :], k_s[h, keys, :]) for h in range(H_A)]
        o = [_dot(jnp.exp2(s[h] - jnp.max(s[h], axis=-1, keepdims=True)).astype(BF16), v_s[h, keys, :])
             for h in range(H_A)]
        for pair in range(H_A // 2):
            ya_ref[rows, pair * LANES:(pair + 1) * LANES] = _pair_normalise(o[2 * pair], o[2 * pair + 1],
                                                                            lo).astype(ya_ref.dtype)
        return carry

    lax.fori_loop(0, rows_all // tq, body, 0, unroll=True)


def _ctx_block(rows, cols, layer):
    return pl.BlockSpec((None, None, rows, cols), lambda b: (b, layer, 0, 0))


def _mla(za, qn, kvn, wuq, wukv, layer, batch, seq, rope=None, ctx=None):
    has_ctx = ctx is not None
    past = ctx[0].shape[2] if has_ctx else 0
    ng = 1 if has_ctx else ATTN_GROUP_TOKENS // seq
    rows = ng * seq
    keys = past + rows
    tq = min(seq, 256)
    full = lambda shape: pl.BlockSpec(shape, lambda b: (0,) * len(shape))
    in_specs = [
        pl.BlockSpec((rows, ZA_W), lambda b: (b, 0)),
        _layer_block((1, Q_LORA), layer), _layer_block((1, KV_LORA), layer),
        _layer_block((Q_LORA, 2 * H_A * LANES), layer), _layer_block((KV_LORA, 2 * H_A * LANES), layer),
    ]
    args = [za, qn, kvn, wuq, wukv]
    out_specs = [pl.BlockSpec((rows, MIX_W), lambda b: (b, 0))]
    out_shape = [jax.ShapeDtypeStruct((batch * seq, MIX_W), MIX_DTYPE)]
    if has_ctx:
        in_specs += [full((seq, LANES))] * 3
        in_specs += [_ctx_block(past, KV_LORA, layer), _ctx_block(ROPE_A, past, layer)]
        args += list(rope) + list(ctx)
    else:
        out_specs += [pl.BlockSpec((rows, KV_LORA), lambda b: (b, 0)),
                      pl.BlockSpec((ng, ROPE_A, seq), lambda b: (b, 0, 0))]
        out_shape += [jax.ShapeDtypeStruct((batch * seq, KV_LORA), F32),
                      jax.ShapeDtypeStruct((batch, ROPE_A, seq), F32)]
    return pl.pallas_call(
        functools.partial(_mla_kernel, seq=seq, ng=ng, has_ctx=has_ctx, tq=tq),
        grid=(batch // ng,),
        in_specs=in_specs,
        out_specs=out_specs,
        out_shape=out_shape,
        scratch_shapes=[
            pltpu.VMEM((H_A, rows, LANES), BF16),
            pltpu.VMEM((H_A, keys, LANES), BF16),
            pltpu.VMEM((H_A, keys, LANES), BF16),
            pltpu.VMEM((keys, LANES), F32),
            pltpu.VMEM((keys, LANES), F32),
        ],
        compiler_params=pltpu.CompilerParams(vmem_limit_bytes=VMEM_LIMIT),
        name="mla",
    )(*args)


def _diff_kernel(*refs, seq, ng, has_ctx, n_prev, tq, lam_init):
    rows_all = ng * seq
    if has_ctx:
        (dq_ref, dk_ref, dv_ref, lam_ref, sn_ref, ck_ref, cv_ref, yd_ref, q_s, k_s, v_s) = refs
        past = ck_ref.shape[1]
    else:
        dq_ref, dk_ref, dv_ref, lam_ref, sn_ref = refs[:5]
        yd_ref, dkt_ref, dvt_ref, q_s, k_s, v_s = refs[-6:]
        past = 0
        for g in range(ng):
            if n_prev:
                dkt_ref[g, 0:n_prev] = refs[5][g]
                dvt_ref[g, 0:n_prev] = refs[6][g]
            dkt_ref[g, n_prev] = dk_ref[g * seq:(g + 1) * seq, :].T
            dvt_ref[g, n_prev] = dv_ref[g * seq:(g + 1) * seq, :].T
    lk = past + seq
    lo = _lane_group_mask(LANES, 64, 0)
    if has_ctx:
        k_s[0:past, :] = ck_ref[...].T.astype(BF16)
        cv = cv_ref[...].T
    for half in range(2):
        cols = slice(half * LANES, (half + 1) * LANES)
        q, k, v = dq_ref[:, cols], dk_ref[:, cols], dv_ref[:, cols]
        if has_ctx:
            v_s[2 * half, 0:past, :] = _with_ones(cv[:, cols], lo, True).astype(BF16)
            v_s[2 * half + 1, 0:past, :] = _with_ones(cv[:, cols], lo, False).astype(BF16)
        q_s[:, cols] = (q * (DIFF_SCALE * LOG2E)).astype(BF16)
        k_s[past:past + rows_all, cols] = k.astype(BF16)
        v_s[2 * half, past:past + rows_all, :] = _with_ones(v, lo, True).astype(BF16)
        v_s[2 * half + 1, past:past + rows_all, :] = _with_ones(v, lo, False).astype(BF16)

    lam = lam_ref[...]
    lam_val = (jnp.exp(jnp.sum(lam[0:1] * lam[1:2], axis=-1, keepdims=True))
               - jnp.exp(jnp.sum(lam[2:3] * lam[3:4], axis=-1, keepdims=True)) + lam_init)
    comp_masks = [jnp.where(_lane_group_mask(D_D, DH_D, j), 1.0, 0.0).astype(BF16) for j in range(2 * H_D)]
    sn = sn_ref[...]

    def body(i, carry):
        rows = pl.ds(pl.multiple_of(i * tq, tq), tq)
        qb = q_s[rows, :]
        keys = pl.ds(pl.multiple_of(((i * tq) // seq) * seq, seq), lk)
        k_all = k_s[keys, :]
        score = lambda h: [_dot_nt(qb * comp_masks[2 * h + comp], k_all) for comp in range(2)]
        s_next = score(0)
        o = []
        for h in range(H_D):
            s = s_next
            if h + 1 < H_D:
                s_next = score(h + 1)
            o1, o2 = [_dot(jnp.exp2(sc - jnp.max(sc, axis=-1, keepdims=True)).astype(BF16), v_s[h, keys, :])
                      for sc in s]
            ll = 64 if h % 2 == 0 else 0
            o.append(o1 / o1[:, ll:ll + 1] - lam_val * (o2 / o2[:, ll:ll + 1]))
        for pair in range(H_D // 2):
            acc = jnp.where(lo, o[2 * pair], o[2 * pair + 1])
            sq = acc * acc
            ss_lo = jnp.sum(jnp.where(lo, sq, 0.0), axis=-1, keepdims=True)
            ss_hi = jnp.sum(jnp.where(lo, 0.0, sq), axis=-1, keepdims=True)
            r = jnp.where(lo, lax.rsqrt(ss_lo / (2 * DH_D) + EPS), lax.rsqrt(ss_hi / (2 * DH_D) + EPS))
            yd_ref[rows, pair * LANES:(pair + 1) * LANES] = (acc * r * sn * (1.0 - lam_init)).astype(yd_ref.dtype)
        return carry

    lax.fori_loop(0, rows_all // tq, body, 0, unroll=True)


def _diff(dq, dk, dv, lam, sn, layer, batch, seq, lam_init, ctx=None, prev=None):
    has_ctx = ctx is not None
    n_prev = prev[0].shape[1] if prev is not None else 0
    past = ctx[0].shape[3] if has_ctx else 0
    ng = 1
    rows = ng * seq
    keys = past + rows
    tq = min(seq, 256)
    in_specs = [pl.BlockSpec((rows, D_D), lambda b: (b, 0))] * 3
    in_specs += [_layer_block((4, DH_D), layer), _layer_block((1, LANES), layer)]
    args = [dq, dk, dv, lam, sn]
    if has_ctx:
        in_specs += [_ctx_block(D_D, past, layer)] * 2
        args += list(ctx)
    if n_prev:
        in_specs += [pl.BlockSpec((ng, n_prev, D_D, seq), lambda b: (b, 0, 0, 0))] * 2
        args += list(prev)
    out_specs = [pl.BlockSpec((rows, MIX_W), lambda b: (b, 0))]
    out_shape = [jax.ShapeDtypeStruct((batch * seq, MIX_W), MIX_DTYPE)]
    if not has_ctx:
        out_specs += [pl.BlockSpec((ng, n_prev + 1, D_D, seq), lambda b: (b, 0, 0, 0))] * 2
        out_shape += [jax.ShapeDtypeStruct((batch, n_prev + 1, D_D, seq), F32)] * 2
    return pl.pallas_call(
        functools.partial(_diff_kernel, seq=seq, ng=ng, has_ctx=has_ctx, n_prev=n_prev, tq=tq, lam_init=lam_init),
        grid=(batch // ng,),
        in_specs=in_specs,
        out_specs=out_specs,
        out_shape=out_shape,
        scratch_shapes=[
            pltpu.VMEM((rows, D_D), BF16),
            pltpu.VMEM((keys, D_D), BF16),
            pltpu.VMEM((H_D, keys, LANES), BF16),
        ],
        compiler_params=pltpu.CompilerParams(vmem_limit_bytes=VMEM_LIMIT),
        name="diff_attn",
    )(*args)


def _split3(x):
    hi = x.astype(BF16)
    r1 = x - hi.astype(F32)
    mid = r1.astype(BF16)
    lo = (r1 - mid.astype(F32)).astype(BF16)
    return hi, mid, lo


ST_ROWS = DH_C + 8
N_UNITS = 2 * H_C
MLSTM_GROUP_TOKENS = 1024


def _mlstm_kernel(*refs, seq, ng, has_state):
    ck = MLSTM_CHUNK
    nc = seq // ck
    if has_state:
        (zc_ref, zg_ref, gb_ref, hn_ref, c0_ref, n0_ref, m0_ref,
         yc_ref, k_s, qt_s, kt_s, vt_s, rcc_s, ca_s, wi_s, en_s, ws_s, wo_s, st_s, ht_s) = refs
    else:
        (zc_ref, zg_ref, gb_ref, hn_ref,
         yc_ref, cout_ref, nout_ref, mout_ref,
         k_s, qt_s, kt_s, vt_s, rcc_s, ca_s, wi_s, en_s, ws_s, wo_s, st_s, ht_s) = refs
    eye = jnp.where(lax.broadcasted_iota(jnp.int32, (DH_C, DH_C), 0) == lax.broadcasted_iota(jnp.int32, (DH_C, DH_C), 1),
                    1.0, 0.0).astype(BF16)

    def transpose_exact(a):
        return sum(_dot_nt(eye, part) for part in _split3(a))

    def chunk_rows(g, c):
        return slice(g * seq + c * ck, g * seq + (c + 1) * ck)

    for g in range(ng):
        for c in range(nc):
            rows = chunk_rows(g, c)
            qt_s[g, c] = zc_ref[rows, 0:D_C].T.astype(BF16)
            kk = zc_ref[rows, D_C:2 * D_C] * (DH_C ** -0.5)
            k_s[rows, :] = kk.astype(BF16)
            kt_s[g, c] = kk.T.astype(BF16)
            vt_s[g, c] = zc_ref[rows, 2 * D_C:3 * D_C].T.astype(BF16)

    gates = zg_ref[...] + gb_ref[...]
    r_i = lax.broadcasted_iota(jnp.int32, (ck, ck), 0)
    c_i = lax.broadcasted_iota(jnp.int32, (ck, ck), 1)
    sum_fw = jnp.where(r_i <= c_i, 1.0, 0.0).astype(BF16)
    sum_bw = jnp.where(r_i >= c_i, 1.0, 0.0).astype(BF16)
    row8 = lax.broadcasted_iota(jnp.int32, (N_UNITS, ck), 0)
    fw8 = row8 < H_C
    fw81 = fw8[:, 0:1]
    b_steps, r_steps = [], []
    for g in range(ng):
        i_rows, b_rows = [], []
        for c in range(nc):
            rows = chunk_rows(g, c)
            g_t = gates[rows, :].T
            i8 = g_t[0:N_UNITS, :]
            p0, p1, p2 = _split3(_log_sigmoid(g_t[N_UNITS:2 * N_UNITS, :]))
            b_fw = _dot(p0, sum_fw) + _dot(p1, sum_fw) + _dot(p2, sum_fw)
            b_bw = _dot(p0, sum_bw) + _dot(p1, sum_bw) + _dot(p2, sum_bw)
            b8 = jnp.where(fw8, b_fw, b_bw)
            rcc_s[rows, :] = jnp.concatenate([i8 - b8, jnp.zeros((ck - N_UNITS, ck), F32)], axis=0).T
            i_rows.append(i8)
            b_rows.append(b8)
        b_steps.append([jnp.where(fw8, b_rows[i], b_rows[nc - 1 - i]) for i in range(nc)])
        r_steps.append([jnp.where(fw8, i_rows[i], i_rows[nc - 1 - i]) - b_steps[g][i] for i in range(nc)])

    pm = sm = jnp.concatenate([r for g in range(ng) for r in r_steps[g]], axis=0)
    lane_all = lax.broadcasted_iota(jnp.int32, pm.shape, 1)
    sh = 1
    while sh < ck:
        pm = jnp.maximum(pm, jnp.where(lane_all >= sh, pltpu.roll(pm, sh, 1), NEG_INF))
        sm = jnp.maximum(sm, jnp.where(lane_all < ck - sh, pltpu.roll(sm, ck - sh, 1), NEG_INF))
        sh *= 2
    m_last = []
    for g in range(ng):
        m_prev = m0_ref[g, :, 0:1] if has_state else jnp.zeros((N_UNITS, 1), F32)
        for i in range(nc):
            b8, r8 = b_steps[g][i], r_steps[g][i]
            at = (g * nc + i) * N_UNITS
            pm8, sm8 = pm[at:at + N_UNITS, :], sm[at:at + N_UNITS, :]
            cm = jnp.where(fw8, pm8, sm8)
            b_t = jnp.where(fw81, b8[:, ck - 1:ck], b8[:, 0:1])
            cm_end = jnp.where(fw81, pm8[:, ck - 1:ck], sm8[:, 0:1])
            inter = b8 + m_prev
            mt = jnp.maximum(inter, b8 + cm)
            ca_s[g, i] = b8 - mt
            wi_s[g, i] = jnp.exp(inter - mt)
            en_s[g, i] = jnp.exp(-mt)
            m_new = b_t + jnp.maximum(m_prev, cm_end)
            ws_s[g, i] = jnp.exp(b_t + r8 - m_new)
            wo_s[g, i] = jnp.broadcast_to(jnp.exp(b_t + m_prev - m_new), (N_UNITS, ck))
            m_prev = m_new
        m_last.append(m_prev)

    if has_state:
        for g in range(ng):
            for u in range(N_UNITS):
                st_s[g * N_UNITS + u, 0:DH_C, :] = transpose_exact(c0_ref[g, u])
                st_s[g * N_UNITS + u, DH_C:ST_ROWS, :] = jnp.broadcast_to(n0_ref[g, u:u + 1, :],
                                                                         (ST_ROWS - DH_C, DH_C))
    else:
        st_s[...] = jnp.zeros_like(st_s)

    hmask_bf = [jnp.where(_lane_group_mask(D_C, DH_C, h), 1.0, 0.0).astype(BF16) for h in range(H_C)]
    src_ok = (r_i <= c_i, r_i >= c_i)
    hs = [slice(h * DH_C, (h + 1) * DH_C) for h in range(H_C)]
    units = [(g, d, h) for g in range(ng) for d in range(2) for h in range(H_C)]

    def body(i, carry):
        chunks = (i, nc - 1 - i)
        ca, wi, en, ws, wo = [[ref[g, i] for g in range(ng)] for ref in (ca_s, wi_s, en_s, ws_s, wo_s)]
        q_t = [[qt_s[g, c] for c in chunks] for g in range(ng)]
        k_t = [[kt_s[g, c] for c in chunks] for g in range(ng)]
        v_t = [[vt_s[g, c] for c in chunks] for g in range(ng)]
        rows = [[pl.ds(pl.multiple_of(g * seq + c * ck, ck), ck) for c in chunks] for g in range(ng)]
        sts, qcts, states, upds, pts = [], [], [], [], []
        for g, d, h in units:
            sts.append(_dot(k_s[rows[g][d], :] * hmask_bf[h], q_t[g][d]))
            states.append(st_s[g * N_UNITS + d * H_C + h])
            qcts.append(_dot(states[-1].astype(BF16), q_t[g][d][hs[h], :]))
        for g, d, h in units:
            u = d * H_C + h
            w_s = ws[g][u:u + 1, :]
            vw = jnp.concatenate([v_t[g][d][hs[h], :].astype(F32) * w_s, jnp.broadcast_to(w_s, (8, ck))], axis=0)
            upds.append(_dot_nt(vw.astype(BF16), k_t[g][d][hs[h], :]))
        for j, (g, d, h) in enumerate(units):
            u = d * H_C + h
            decay = jnp.exp(rcc_s[rows[g][d], u:u + 1] + ca[g][u:u + 1, :])
            pts.append(sts[j] * jnp.where(src_ok[d], decay, 0.0))
        nums = [_dot(v_t[g][d][hs[h], :], pts[j].astype(BF16)) for j, (g, d, h) in enumerate(units)]
        for j, (g, d, h) in enumerate(units):
            u = d * H_C + h
            w_i = wi[g][u:u + 1, :]
            den = w_i * qcts[j][DH_C:DH_C + 1, :] + jnp.sum(pts[j], axis=0, keepdims=True)
            num = w_i * qcts[j][0:DH_C, :] + nums[j]
            ht_s[g, d, chunks[d], hs[h], :] = num / jnp.maximum(jnp.abs(den), en[g][u:u + 1, :])
            st_s[g * N_UNITS + u] = wo[g][u:u + 1, 0:DH_C] * states[j] + upds[j]
        return carry

    lax.fori_loop(0, nc, body, 0, unroll=2)

    hn = hn_ref[...]
    for g in range(ng):
        for c in range(nc):
            rows = chunk_rows(g, c)
            h_t = ht_s[g, 0, c] + ht_s[g, 1, c]
            parts = []
            for h in range(H_C):
                x = h_t[hs[h], :]
                parts.append(x * lax.rsqrt(jnp.mean(x * x, axis=0, keepdims=True) + EPS))
            h_n = jnp.concatenate(parts, axis=0).T
            yc_ref[rows, :] = (_sigmoid(zc_ref[rows, 3 * D_C:4 * D_C]) * (h_n * hn)).astype(yc_ref.dtype)

    if not has_state:
        for g in range(ng):
            for u in range(N_UNITS):
                cout_ref[g, u] = transpose_exact(st_s[g * N_UNITS + u, 0:DH_C, :])
                nout_ref[g, u:u + 1, :] = st_s[g * N_UNITS + u, DH_C:DH_C + 1, :]
            mout_ref[g] = jnp.broadcast_to(m_last[g], (N_UNITS, LANES))


def _mlstm(zc, zg, gb, hn, layer, batch, seq, state=None):
    has_state = state is not None
    nc = seq // MLSTM_CHUNK
    ng = max(2, MLSTM_GROUP_TOKENS // seq)
    in_specs = [pl.BlockSpec((ng * seq, 4 * D_C), lambda b: (b, 0)), pl.BlockSpec((ng * seq, ZG_W), lambda b: (b, 0)),
                _layer_block((1, ZG_W), layer), _layer_block((1, D_C), layer)]
    args = [zc, zg, gb, hn]
    out_specs = [pl.BlockSpec((ng * seq, MIX_W), lambda b: (b, 0))]
    out_shape = [jax.ShapeDtypeStruct((batch * seq, MIX_W), MIX_DTYPE)]
    if has_state:
        in_specs += [pl.BlockSpec((ng, None, N_UNITS, DH_C, DH_C), lambda b: (b, layer, 0, 0, 0)),
                     pl.BlockSpec((ng, None, N_UNITS, DH_C), lambda b: (b, layer, 0, 0)),
                     pl.BlockSpec((ng, None, N_UNITS, LANES), lambda b: (b, layer, 0, 0))]
        args += list(state)
    else:
        out_specs += [pl.BlockSpec((ng, N_UNITS, DH_C, DH_C), lambda b: (b, 0, 0, 0)),
                      pl.BlockSpec((ng, N_UNITS, DH_C), lambda b: (b, 0, 0)),
                      pl.BlockSpec((ng, N_UNITS, LANES), lambda b: (b, 0, 0))]
        out_shape += [jax.ShapeDtypeStruct((batch, N_UNITS, DH_C, DH_C), F32),
                      jax.ShapeDtypeStruct((batch, N_UNITS, DH_C), F32),
                      jax.ShapeDtypeStruct((batch, N_UNITS, LANES), F32)]
    step = lambda: pltpu.VMEM((ng, nc, N_UNITS, MLSTM_CHUNK), F32)
    transposed = lambda: pltpu.VMEM((ng, nc, D_C, MLSTM_CHUNK), BF16)
    return pl.pallas_call(
        functools.partial(_mlstm_kernel, seq=seq, ng=ng, has_state=has_state),
        grid=(batch // ng,),
        in_specs=in_specs,
        out_specs=out_specs,
        out_shape=out_shape,
        scratch_shapes=[
            pltpu.VMEM((ng * seq, D_C), BF16),
            transposed(), transposed(), transposed(),
            pltpu.VMEM((ng * seq, ZG_W), F32),
            step(), step(), step(), step(), step(),
            pltpu.VMEM((ng * N_UNITS, ST_ROWS, DH_C), F32),
            pltpu.VMEM((ng, 2, nc, D_C, MLSTM_CHUNK), F32),
        ],
        compiler_params=pltpu.CompilerParams(vmem_limit_bytes=VMEM_LIMIT),
        name="mlstm",
    )(*args)


def _mlp_weight_copies(w1_hbm, w2_hbm, w1_ref, w2_ref, sem, layer):
    return (pltpu.make_async_copy(w1_hbm.at[layer], w1_ref, sem.at[0]),
            pltpu.make_async_copy(w2_hbm.at[layer], w2_ref, sem.at[1]))


def _merge_kernel(x_ref, mod_ref, g_ref, ya_ref, yb_ref, yc_ref, yd_ref, wm_ref, bm_ref, wb_ref, wo_ref,
                  w1_hbm, w2_hbm, o_ref, w1_ref, w2_ref, sem, *, layer):
    first = pl.program_id(0) == 0

    @pl.when(first)
    def _start():
        for copy in _mlp_weight_copies(w1_hbm, w2_hbm, w1_ref, w2_ref, sem, layer):
            copy.start()

    x = x_ref[...]
    h = (_rms(x, g_ref[0:1, :]) * (1.0 + mod_ref[1:2, :]) + mod_ref[0:1, :]).astype(BF16)
    acc = jnp.zeros(x.shape, F32)
    for n, y_ref in enumerate((ya_ref, yb_ref, yc_ref, yd_ref)):
        cols = slice(n * D_MODEL, (n + 1) * D_MODEL)
        gate = _sigmoid(_dot(h, wm_ref[:, cols]) + bm_ref[:, cols])
        acc = acc + gate * _dot(y_ref[...].astype(BF16), wb_ref[n])
    y = _dot(acc.astype(BF16), wo_ref[...])
    x = x + mod_ref[2:3, :] * _rms(y, g_ref[1:2, :])
    h2 = (_rms(x, g_ref[2:3, :]) * (1.0 + mod_ref[4:5, :]) + mod_ref[3:4, :]).astype(BF16)

    @pl.when(first)
    def _wait():
        for copy in _mlp_weight_copies(w1_hbm, w2_hbm, w1_ref, w2_ref, sem, layer):
            copy.wait()

    f = jnp.zeros(x.shape, F32)
    for j in range(D_FF // D_MODEL):
        cols = slice(j * D_MODEL, (j + 1) * D_MODEL)
        a = jnp.maximum(_dot(h2, w1_ref[:, cols]), 0.0)
        f = f + _dot((a * a).astype(BF16), w2_ref[cols, :])
    o_ref[...] = x + mod_ref[5:6, :] * _rms(f, g_ref[3:4, :])


def _merge_ffn(x, mod, g, ys, wm, bm, wb, wo, w1, w2, layer, seq, per_seq):
    t = x.shape[0]
    tm = TOKEN_TILE
    full = lambda shape: _layer_block(shape, layer)
    tok = lambda w_: pl.BlockSpec((tm, w_), lambda i: (i, 0))
    hbm = pl.BlockSpec(memory_space=pl.ANY)
    return pl.pallas_call(
        functools.partial(_merge_kernel, layer=layer),
        grid=(t // tm,),
        in_specs=[tok(D_MODEL), _mod_spec(layer, tm, seq, per_seq),
                  full((4, D_MODEL)), tok(MIX_W), tok(MIX_W), tok(MIX_W), tok(MIX_W),
                  full((D_MODEL, N_BRANCH * D_MODEL)), full((1, N_BRANCH * D_MODEL)),
                  full((N_BRANCH, MIX_W, D_MODEL)), full((D_MODEL, D_MODEL)),
                  hbm, hbm],
        out_specs=tok(D_MODEL),
        out_shape=jax.ShapeDtypeStruct((t, D_MODEL), F32),
        scratch_shapes=[pltpu.VMEM((D_MODEL, D_FF), BF16), pltpu.VMEM((D_FF, D_MODEL), BF16),
                        pltpu.SemaphoreType.DMA((2,))],
        compiler_params=pltpu.CompilerParams(vmem_limit_bytes=VMEM_LIMIT, dimension_semantics=("arbitrary",)),
        name="merge_ffn",
    )(x, mod, g, *ys, wm, bm, wb, wo, w1, w2)


def _arrange_w_in(w):
    wt = w.T.astype(BF16)
    z = lambda n: jnp.zeros((n, D_MODEL), BF16)
    o_b, o_c = IN_A, IN_A + IN_B
    o_g, o_d = o_c + 4 * D_C, o_c + IN_C
    gates = wt[o_g:o_d].reshape(2, 2, H_C, D_MODEL).transpose(1, 0, 2, 3).reshape(4 * H_C, D_MODEL)
    return jnp.concatenate([
        wt[:Q_LORA + KV_LORA], z(64), wt[Q_LORA + KV_LORA:IN_A], z(32),
        wt[o_b:o_g], gates, z(ZG_W - 4 * H_C), wt[o_d:]], axis=0).T


def _arrange_w_uq(w):
    w = w.reshape(Q_LORA, H_A, NOPE_A + ROPE_A)
    half = ROPE_A // 2
    swapped = jnp.concatenate([jnp.zeros_like(w[:, :, :NOPE_A]), w[:, :, NOPE_A + half:], w[:, :, NOPE_A:NOPE_A + half]],
                              axis=-1)
    pad = lambda a: jnp.pad(a, ((0, 0), (0, 0), (0, LANES - NOPE_A - ROPE_A))).reshape(Q_LORA, H_A * LANES)
    return jnp.concatenate([pad(w), pad(swapped)], axis=1).astype(BF16)


def _arrange_w_ukv(w):
    w = w.reshape(KV_LORA, H_A, NOPE_A + VH_A)
    wk = jnp.pad(w[:, :, :NOPE_A], ((0, 0), (0, 0), (0, LANES - NOPE_A)))
    v = w[:, :, NOPE_A:]
    zero = jnp.zeros_like(v)
    even = jnp.concatenate([v, zero], axis=-1)
    odd = jnp.concatenate([zero, v], axis=-1)
    wv = jnp.where((jnp.arange(H_A) % 2 == 0)[None, :, None], even, odd)
    return jnp.concatenate([wk.reshape(KV_LORA, -1), wv.reshape(KV_LORA, -1)], axis=1).astype(BF16)


def _rope_tables(rows):
    row = np.repeat(np.arange(rows), GRID_W).astype(np.float64)
    col = np.tile(np.arange(GRID_W), rows).astype(np.float64)
    nf = ROPE_A // 4
    inv = np.exp(-math.log(ROPE_BASE) * np.arange(nf, dtype=np.float64) / nf)
    ang = np.concatenate([row[:, None] * inv, col[:, None] * inv], axis=-1)
    cos, sin = np.cos(ang), np.sin(ang)
    n = cos.shape[0]
    ones, zeros = (lambda k: np.ones((n, k))), (lambda k: np.zeros((n, k)))
    mla = (np.concatenate([ones(64), cos, cos, ones(32)], axis=1),
           np.concatenate([zeros(64), -sin, zeros(48)], axis=1),
           np.concatenate([zeros(80), sin, zeros(32)], axis=1))
    dif = (np.tile(np.concatenate([cos, cos], axis=1), (1, 4)),
           np.tile(np.concatenate([-sin, zeros(16)], axis=1), (1, 4)),
           np.tile(np.concatenate([zeros(16), sin], axis=1), (1, 4)))
    as_f32 = lambda ts: tuple(jnp.asarray(t.astype(np.float32)) for t in ts)
    return as_f32(mla), as_f32(dif)


def kernel(x_prompt, x_sample, cache_mla_ckv, cache_mla_krope, cache_diff_k, cache_diff_v, state_mlstm_C,
           state_mlstm_n, state_mlstm_m, c, c_ctx, w_mod, b_mod, norm_g, w_in, mla_q_norm, w_uq, mla_kv_norm,
           w_ukv, gmlp_v_norm, gmlp_w_s, gmlp_b_s, mlstm_gate_bias, mlstm_head_norm, diff_lambda, diff_sub_norm,
           w_branch, w_merge, b_merge, w_out, w_ff1, w_ff2):
    bp, lp, _ = x_prompt.shape
    bs, ls, _ = x_sample.shape
    past = cache_mla_ckv.shape[2]

    cond = jnp.concatenate([c_ctx[None, :], c, jnp.zeros((16 - 1 - bs, D_MODEL), F32)], axis=0)
    mod = _modulation(cond, w_mod, b_mod).reshape(DEPTH, 16, 6, D_MODEL)
    rope_mla, rope_dif = _rope_tables(ls // GRID_W)

    row = lambda a: a[:, None, :]
    w_in_a = jax.vmap(_arrange_w_in)(w_in)
    wuq_a = jax.vmap(_arrange_w_uq)(w_uq)
    wukv_a = jax.vmap(_arrange_w_ukv)(w_ukv)
    qn, kvn, vn, hn, bm = row(mla_q_norm), row(mla_kv_norm), row(gmlp_v_norm), row(mlstm_head_norm), row(b_merge)
    ws = gmlp_w_s.astype(BF16)
    bias = jnp.repeat(jnp.swapaxes(gmlp_b_s, 1, 2), D_B // G_B, axis=2)
    gb = jnp.pad(mlstm_gate_bias.transpose(0, 2, 1, 3).reshape(DEPTH, 1, 4 * H_C),
                 ((0, 0), (0, 0), (0, ZG_W - 4 * H_C)))
    sn = row(jnp.tile(diff_sub_norm, (1, 2)))
    wm, wb, wo = w_merge.astype(BF16), w_branch.astype(BF16), w_out.astype(BF16)
    w1, w2 = w_ff1.astype(BF16), w_ff2.astype(BF16)

    ctx_kr = cache_mla_krope.transpose(0, 1, 3, 2)
    ctx_dk = cache_diff_k.transpose(0, 1, 3, 4, 5, 2).reshape(bs, DEPTH, D_D, past)
    ctx_dv = cache_diff_v.transpose(0, 1, 3, 4, 2).reshape(bs, DEPTH, D_D, past)
    c0 = state_mlstm_C.reshape(bs, DEPTH, N_UNITS, DH_C, DH_C)
    n0 = state_mlstm_n.reshape(bs, DEPTH, N_UNITS, DH_C)
    m0 = jnp.broadcast_to(state_mlstm_m.reshape(bs, DEPTH, N_UNITS, 1), (bs, DEPTH, N_UNITS, LANES))

    xp = x_prompt.reshape(bp * lp, D_MODEL)
    xs = x_sample.reshape(bs * ls, D_MODEL)
    ents = []
    diff_kv = None
    for l in range(DEPTH):
        lam_init = 0.8 - 0.6 * math.exp(-0.3 * l)
        for is_sample in (False, True):
            x = xs if is_sample else xp
            batch, seq = (bs, ls) if is_sample else (bp, lp)
            za, yb, zc, zg, dq, dk, dv = _in_proj(x, mod, norm_g, w_in_a, vn, ws, bias, l, seq, is_sample,
                                                  rope=rope_dif if is_sample else None)
            if is_sample:
                (ya,) = _mla(za, qn, kvn, wuq_a, wukv_a, l, batch, seq, rope=rope_mla, ctx=(cache_mla_ckv, ctx_kr))
                (yd,) = _diff(dq, dk, dv, diff_lambda, sn, l, batch, seq, lam_init, ctx=(ctx_dk, ctx_dv))
                (yc,) = _mlstm(zc, zg, gb, hn, l, batch, seq, state=(c0, n0, m0))
            else:
                ya, ckv, krt = _mla(za, qn, kvn, wuq_a, wukv_a, l, batch, seq)
                yd, *diff_kv = _diff(dq, dk, dv, diff_lambda, sn, l, batch, seq, lam_init, prev=diff_kv)
                yc, c_new, n_new, m_new = _mlstm(zc, zg, gb, hn, l, batch, seq)
                ents.append((
                    ckv.reshape(bp, lp, KV_LORA),
                    krt.transpose(0, 2, 1),
                    c_new.reshape(bp, 2, H_C, DH_C, DH_C),
                    n_new.reshape(bp, 2, H_C, DH_C),
                    m_new[:, :, 0].reshape(bp, 2, H_C)))
            x = _merge_ffn(x, mod, norm_g, (ya, yb, yc, yd), wm, bm, wb, wo, w1, w2, l, seq, is_sample)
            if is_sample:
                xs = x
            else:
                xp = x

    stack = lambda j: jnp.stack([e[j] for e in ents], axis=1)
    dkt, dvt = diff_kv
    new_diff_k = dkt.reshape(bp, DEPTH, H_D, 2, DH_D, lp).transpose(0, 1, 5, 2, 3, 4)
    new_diff_v = dvt.reshape(bp, DEPTH, H_D, 2 * DH_D, lp).transpose(0, 1, 4, 2, 3)
    return (xp.reshape(bp, lp, D_MODEL), xs.reshape(bs, ls, D_MODEL),
            stack(0), stack(1), new_diff_k, new_diff_v, stack(2), stack(3), stack(4))
```
